```python
import math
import jax, jax.numpy as jnp
from jax import lax
import numpy as np

D_MODEL = 1024
BATCH = 32
SEQ = 256
DEPTH = 2
DEC_BATCH = 8
DEC_SEQ = 2048
PAST_LEN = 512

GRID_W = 64
Q_BLOCK = 128
N_EVEN = (DEPTH + 1) // 2
N_ODD = DEPTH // 2
MIX_HALF = D_MODEL // 2
S5_WIDTH = MIX_HALF
S5_GROUP = 16
S5_GROUPS = S5_WIDTH // S5_GROUP
S5_STATE = 64
S5_DT_MIN = 0.001
S5_DT_MAX = 0.1
DIFF_HEADS = 4
DIFF_DK = MIX_HALF // (2 * DIFF_HEADS)
DIFF_DV = 2 * DIFF_DK
DIFF_QK = 2 * DIFF_HEADS * DIFF_DK
WIN_HEADS = 8
WIN_KV_HEADS = 2
WIN_GROUP = WIN_HEADS // WIN_KV_HEADS
WIN_DH = MIX_HALF // WIN_HEADS
WINDOW = 128
POOL_WINDOWS = (2, 4, 8, 16)
POOL_GROUP = MIX_HALF // len(POOL_WINDOWS)
PEER_HEADS = 8
PEER_NKEYS = 128
PEER_EXPERTS = PEER_NKEYS * PEER_NKEYS
PEER_TOPK = 16
PEER_DK = 256
PEER_DK_HALF = PEER_DK // 2
PEER_CHUNK = 128

EVEN_IN = S5_WIDTH + 2 * DIFF_QK + DIFF_HEADS * DIFF_DV
ODD_IN = WIN_HEADS * WIN_DH + 2 * WIN_KV_HEADS * WIN_DH + MIX_HALF
ROPE_BASE = 10000.0
EPS = 1e-6
NEG_INF = -1e30
F32 = jnp.float32

kernel_name = 'hybrid_diffusion_prefix_step'


def rms_norm(x, g):
    xf = x.astype(F32)
    y = xf * lax.rsqrt(jnp.mean(xf * xf, axis=-1, keepdims=True) + EPS)
    return (y * g.astype(F32)).astype(x.dtype)


def axial_rope_tables(n_rows, dim):
    q4 = dim // 4
    row = jnp.repeat(jnp.arange(n_rows), GRID_W).astype(F32)
    col = jnp.tile(jnp.arange(GRID_W), n_rows).astype(F32)
    freqs = ROPE_BASE ** (-jnp.arange(q4, dtype=F32) / q4)
    ang = jnp.stack([row[:, None] * freqs, col[:, None] * freqs], axis=1)
    return jnp.cos(ang), jnp.sin(ang)


def apply_axial_rope(x, cos, sin):
    shp = x.shape
    q4 = shp[-1] // 4
    nh = x.ndim - 3
    xr = x.astype(F32).reshape(shp[:-1] + (2, 2, q4))
    c = cos.reshape((1, shp[1]) + (1,) * nh + (2, q4))
    s = sin.reshape((1, shp[1]) + (1,) * nh + (2, q4))
    x1, x2 = xr[..., 0, :], xr[..., 1, :]
    out = jnp.stack([x1 * c - x2 * s, x2 * c + x1 * s], axis=-2)
    return out.reshape(shp).astype(x.dtype)


def sweep_blocks(fn, q):
    b, l = q.shape[:2]
    nb = l // Q_BLOCK
    qb = jnp.moveaxis(q.reshape((b, nb, Q_BLOCK) + q.shape[2:]), 1, 0)
    out = lax.map(lambda a: fn(a[0], a[1]), (qb, jnp.arange(nb)))
    o = jnp.moveaxis(out, 0, 1)
    return o.reshape((b, l) + o.shape[3:])


def _ssm_combine(e1, e2):
    a1, b1 = e1
    a2, b2 = e2
    return a1 * a2, a2 * b1 + b2


def s5_mixer(u, P, i, h0):
    bn, l, _ = u.shape
    uf = u.astype(F32)
    ug = uf.reshape(bn, l, S5_GROUPS, S5_GROUP).astype(jnp.complex64)
    y = P['s5_d'][i].astype(F32) * uf
    finals = []
    for d in range(2):
        rev = d == 1
        lam = lax.complex(P['s5_lam_re'][i, d].astype(F32), P['s5_lam_im'][i, d].astype(F32))
        step = jnp.exp(P['s5_log_step'][i, d].astype(F32))[:, None]
        a_bar = jnp.exp(lam * step)
        b_mat = lax.complex(P['s5_b_re'][i, d].astype(F32), P['s5_b_im'][i, d].astype(F32))
        b_bar = ((a_bar - 1.0) / lam)[..., None] * b_mat
        bu = jnp.einsum('blgc,gpc->blgp', ug, b_bar)
        a_cum, hs = lax.associative_scan(_ssm_combine, (jnp.broadcast_to(a_bar, bu.shape), bu),
                                         reverse=rev, axis=1)
        if h0 is not None:
            init = lax.complex(h0[:, d, 0].astype(F32), h0[:, d, 1].astype(F32))
            hs = hs + a_cum * init[:, None]
        else:
            hf = hs[:, 0] if rev else hs[:, -1]
            finals.append(jnp.stack([hf.real, hf.imag], axis=1))
        c_mat = lax.complex(P['s5_c_re'][i, d].astype(F32), P['s5_c_im'][i, d].astype(F32))
        y = y + jnp.einsum('blgp,gcp->blgc', hs, c_mat).real.reshape(bn, l, S5_WIDTH)
    g = jax.nn.gelu(y)
    out = (g * jax.nn.sigmoid(g @ P['s5_w_glu'][i].astype(F32))).astype(u.dtype)
    final = jnp.stack(finals, axis=1) if h0 is None else None
    return out, final


def diff_attention(q, k, v, lam):
    scale = DIFF_DK ** -0.5

    def block(qb, j):
        s = jnp.einsum('bqmhd,bkmhd->bmhqk', qb, k, preferred_element_type=F32) * scale
        p = jax.nn.softmax(s, axis=-1)
        a = p[:, 0] - lam * p[:, 1]
        return jnp.einsum('bhqk,bkhe->bqhe', a.astype(v.dtype), v)

    return sweep_blocks(block, q)


def sink_attention(q, k_ctx, v_ctx, sink, k_lat=None, v_lat=None):
    scale = WIN_DH ** -0.5
    l = q.shape[1]
    n_ctx = k_ctx.shape[1]
    local = k_lat is not None
    span = Q_BLOCK + 2 * WINDOW
    if local:
        pad = ((0, 0), (WINDOW, WINDOW), (0, 0), (0, 0))
        kp = jnp.pad(k_lat, pad)
        vp = jnp.pad(v_lat, pad)

    def block(qb, j):
        s = [jnp.einsum('bqngd,bknd->bngqk', qb, k_ctx, preferred_element_type=F32) * scale]
        if local:
            start = j * Q_BLOCK
            kl = lax.dynamic_slice_in_dim(kp, start, span, axis=1)
            vl = lax.dynamic_slice_in_dim(vp, start, span, axis=1)
            qpos = start + jnp.arange(Q_BLOCK)
            kpos = start - WINDOW + jnp.arange(span)
            ok = (jnp.abs(qpos[:, None] - kpos[None, :]) <= WINDOW) & (kpos >= 0)[None, :] & (kpos < l)[None, :]
            sl = jnp.einsum('bqngd,bknd->bngqk', qb, kl, preferred_element_type=F32) * scale
            s.append(jnp.where(ok, sl, NEG_INF))
        sk = jnp.broadcast_to(sink.astype(F32)[None, :, :, None, None], s[0].shape[:-1] + (1,))
        p = jax.nn.softmax(jnp.concatenate(s + [sk], axis=-1), axis=-1).astype(v_ctx.dtype)
        o = jnp.einsum('bngqk,bknd->bqngd', p[..., :n_ctx], v_ctx)
        if local:
            o = o + jnp.einsum('bngqk,bknd->bqngd', p[..., n_ctx:n_ctx + span], vl)
        return o

    return sweep_blocks(block, q)


def pool_mixer(u, w, scale):
    bn, l, _ = u.shape
    uf = u.astype(F32)
    cs = jnp.pad(jnp.cumsum(uf, axis=1), ((0, 0), (1, 0), (0, 0)))
    t = jnp.arange(l)
    parts = []
    for gi, wl in enumerate(POOL_WINDOWS):
        lo = jnp.clip(t - wl // 2, 0, l)
        hi = jnp.clip(t + wl - wl // 2, 0, l)
        csg = cs[:, :, gi * POOL_GROUP:(gi + 1) * POOL_GROUP]
        mean = (csg[:, hi] - csg[:, lo]) / (hi - lo).astype(F32)[None, :, None]
        parts.append(mean - uf[:, :, gi * POOL_GROUP:(gi + 1) * POOL_GROUP])
    pooled = jnp.stack(parts, axis=2)
    y = jnp.einsum('blgc,gcd->blgd', pooled, w.astype(F32)).reshape(bn, l, MIX_HALF) * scale.astype(F32)
    return y.astype(u.dtype)


def peer(h, w_q, sub_keys, u_tab, v_tab):
    bn, l, d = h.shape
    t = bn * l
    x = h.reshape(t, d)
    q = (x @ w_q).reshape(t, PEER_HEADS, 2, PEER_DK_HALF)
    s = jnp.einsum('thcd,hckd->thck', q, sub_keys, preferred_element_type=F32)
    s1, i1 = lax.top_k(s[:, :, 0], PEER_TOPK)
    s2, i2 = lax.top_k(s[:, :, 1], PEER_TOPK)
    cand = (s1[..., :, None] + s2[..., None, :]).reshape(t, PEER_HEADS, PEER_TOPK * PEER_TOPK)
    cidx = (i1[..., :, None] * PEER_NKEYS + i2[..., None, :]).reshape(t, PEER_HEADS, PEER_TOPK * PEER_TOPK)
    best, pos = lax.top_k(cand, PEER_TOPK)
    idx = jnp.take_along_axis(cidx, pos, axis=-1)
    gate = jax.nn.softmax(best, axis=-1)
    nc = t // PEER_CHUNK

    def chunk(args):
        xc, ic, gc = args
        a = jax.nn.gelu(jnp.einsum('td,thkd->thk', xc, jnp.take(u_tab, ic, axis=0), preferred_element_type=F32))
        return jnp.einsum('thk,thkd->td', (gc * a).astype(v_tab.dtype), jnp.take(v_tab, ic, axis=0))

    out = lax.map(chunk, (x.reshape(nc, PEER_CHUNK, d),
                          idx.reshape(nc, PEER_CHUNK, PEER_HEADS, PEER_TOPK),
                          gate.reshape(nc, PEER_CHUNK, PEER_HEADS, PEER_TOPK)))
    return out.reshape(bn, l, d).astype(h.dtype)


def even_mixer(h, i, lam_init, P, rope, cache):
    bn, l, _ = h.shape
    z = h @ P['even_w_in'][i]
    u, q, k, v = jnp.split(z, [S5_WIDTH, S5_WIDTH + DIFF_QK, S5_WIDTH + 2 * DIFF_QK], axis=-1)
    q = rms_norm(q.reshape(bn, l, 2, DIFF_HEADS, DIFF_DK), P['diff_q_norm'][i])
    k = rms_norm(k.reshape(bn, l, 2, DIFF_HEADS, DIFF_DK), P['diff_k_norm'][i])
    v = v.reshape(bn, l, DIFF_HEADS, DIFF_DV)
    lv = P['diff_lambda'][i].astype(F32)
    lam = jnp.exp(jnp.sum(lv[0] * lv[1])) - jnp.exp(jnp.sum(lv[2] * lv[3])) + lam_init
    if cache is None:
        k_all, v_all, h0 = k, v, None
    else:
        cos, sin = rope['diff']
        q = apply_axial_rope(q, cos, sin)
        k_all = jnp.concatenate([cache['diff_k'][:, i].astype(k.dtype), apply_axial_rope(k, cos, sin)], axis=1)
        v_all = jnp.concatenate([cache['diff_v'][:, i].astype(v.dtype), v], axis=1)
        h0 = cache['s5'][:, i]
    o_b = rms_norm(diff_attention(q, k_all, v_all, lam), P['diff_sub_norm'][i]) * (1.0 - lam_init)
    o_a, s5_final = s5_mixer(u, P, i, h0)
    out = jnp.concatenate([o_a, o_b.reshape(bn, l, MIX_HALF).astype(o_a.dtype)], axis=-1) @ P['even_w_out'][i]
    state = (k, v, s5_final) if cache is None else None
    return out, state


def odd_mixer(h, i, P, rope, cache):
    bn, l, _ = h.shape
    z = h @ P['odd_w_in'][i]
    nq = WIN_HEADS * WIN_DH
    nkv = WIN_KV_HEADS * WIN_DH
    q, k, v, u = jnp.split(z, [nq, nq + nkv, nq + 2 * nkv], axis=-1)
    q = rms_norm(q.reshape(bn, l, WIN_KV_HEADS, WIN_GROUP, WIN_DH), P['win_q_norm'][i])
    k = rms_norm(k.reshape(bn, l, WIN_KV_HEADS, WIN_DH), P['win_k_norm'][i])
    v = v.reshape(bn, l, WIN_KV_HEADS, WIN_DH)
    sink = P['win_sink'][i].reshape(WIN_KV_HEADS, WIN_GROUP)
    if cache is None:
        o_c = sink_attention(q, k, v, sink)
    else:
        cos, sin = rope['win']
        o_c = sink_attention(apply_axial_rope(q, cos, sin), cache['win_k'][:, i].astype(k.dtype),
                             cache['win_v'][:, i].astype(v.dtype), sink, apply_axial_rope(k, cos, sin), v)
    o_d = pool_mixer(u, P['pool_w'][i], P['pool_scale'][i])
    out = jnp.concatenate([o_c.reshape(bn, l, MIX_HALF).astype(o_d.dtype), o_d], axis=-1) @ P['odd_w_out'][i]
    state = (k, v) if cache is None else None
    return out, state


def trunk_layer(x, l, mod, P, rope, cache):
    shift_m, scale_m, gate_m, shift_f, scale_f, gate_f = jnp.split(mod, 6, axis=-1)
    h = rms_norm(x, P['norm_mix'][l]) * (1.0 + scale_m) + shift_m
    if l % 2 == 0:
        o, st = even_mixer(h, l // 2, 0.8 - 0.6 * math.exp(-0.3 * l), P, rope, cache)
    else:
        o, st = odd_mixer(h, l // 2, P, rope, cache)
    x = x + gate_m * o
    h = rms_norm(x, P['norm_ffn'][l]) * (1.0 + scale_f) + shift_f
    x = x + gate_f * peer(h, P['peer_w_q'][l], P['peer_sub_keys'][l], P['peer_u'][l], P['peer_v'][l])
    return x, st


def setup_inputs(seed: int = 0) -> dict:
    key = jax.random.key(seed)
    keys = jax.random.split(key, 64)
    cnt = [0]

    def nk():
        cnt[0] += 1
        return keys[cnt[0] - 1]

    def nrm(shape, s=1.0):
        return jax.random.normal(nk(), shape, F32) * s

    def gain(shape):
        return 1.0 + 0.02 * jax.random.normal(nk(), shape, F32)

    d = D_MODEL
    inp = {}
    inp['x_prompt'] = nrm((BATCH, SEQ, d))
    inp['x_sample'] = nrm((DEC_BATCH, DEC_SEQ, d))
    inp['cache_diff_k'] = nrm((DEC_BATCH, N_EVEN, PAST_LEN, 2, DIFF_HEADS, DIFF_DK))
    inp['cache_diff_v'] = nrm((DEC_BATCH, N_EVEN, PAST_LEN, DIFF_HEADS, DIFF_DV))
    inp['state_s5'] = nrm((DEC_BATCH, N_EVEN, 2, 2, S5_GROUPS, S5_STATE), 0.3)
    inp['cache_win_k'] = nrm((DEC_BATCH, N_ODD, PAST_LEN, WIN_KV_HEADS, WIN_DH))
    inp['cache_win_v'] = nrm((DEC_BATCH, N_ODD, PAST_LEN, WIN_KV_HEADS, WIN_DH))
    inp['c'] = nrm((DEC_BATCH, d))
    inp['c_ctx'] = nrm((d,))
    inp['ada_w'] = nrm((DEPTH, d, 6 * d), 0.5 * d ** -0.5)
    inp['ada_b'] = nrm((DEPTH, 6 * d), 0.02)
    inp['norm_mix'] = gain((DEPTH, d))
    inp['norm_ffn'] = gain((DEPTH, d))
    inp['even_w_in'] = nrm((N_EVEN, d, EVEN_IN), d ** -0.5)
    inp['even_w_out'] = nrm((N_EVEN, d, d), d ** -0.5)
    inp['s5_lam_re'] = -0.5 + nrm((N_EVEN, 2, S5_GROUPS, S5_STATE), 0.01)
    inp['s5_lam_im'] = jnp.pi * jnp.arange(S5_STATE, dtype=F32) + nrm((N_EVEN, 2, S5_GROUPS, S5_STATE), 0.01)
    inp['s5_log_step'] = jax.random.uniform(nk(), (N_EVEN, 2, S5_GROUPS), F32,
                                            minval=math.log(S5_DT_MIN), maxval=math.log(S5_DT_MAX))
    inp['s5_b_re'] = nrm((N_EVEN, 2, S5_GROUPS, S5_STATE, S5_GROUP), (2 * S5_GROUP) ** -0.5)
    inp['s5_b_im'] = nrm((N_EVEN, 2, S5_GROUPS, S5_STATE, S5_GROUP), (2 * S5_GROUP) ** -0.5)
    inp['s5_c_re'] = nrm((N_EVEN, 2, S5_GROUPS, S5_GROUP, S5_STATE), S5_STATE ** -0.5)
    inp['s5_c_im'] = nrm((N_EVEN, 2, S5_GROUPS, S5_GROUP, S5_STATE), S5_STATE ** -0.5)
    inp['s5_d'] = nrm((N_EVEN, S5_WIDTH))
    inp['s5_w_glu'] = nrm((N_EVEN, S5_WIDTH, S5_WIDTH), S5_WIDTH ** -0.5)
    inp['diff_q_norm'] = gain((N_EVEN, DIFF_DK))
    inp['diff_k_norm'] = gain((N_EVEN, DIFF_DK))
    inp['diff_lambda'] = nrm((N_EVEN, 4, DIFF_DK), 0.1)
    inp['diff_sub_norm'] = gain((N_EVEN, DIFF_DV))
    inp['odd_w_in'] = nrm((N_ODD, d, ODD_IN), d ** -0.5)
    inp['odd_w_out'] = nrm((N_ODD, d, d), d ** -0.5)
    inp['win_q_norm'] = gain((N_ODD, WIN_DH))
    inp['win_k_norm'] = gain((N_ODD, WIN_DH))
    inp['win_sink'] = nrm((N_ODD, WIN_HEADS), 0.5)
    inp['pool_w'] = nrm((N_ODD, len(POOL_WINDOWS), POOL_GROUP, POOL_GROUP), POOL_GROUP ** -0.5)
    inp['pool_scale'] = 1.0 + nrm((N_ODD, MIX_HALF), 0.1)
    inp['peer_w_q'] = nrm((DEPTH, d, PEER_HEADS * PEER_DK), d ** -0.5)
    inp['peer_sub_keys'] = nrm((DEPTH, PEER_HEADS, 2, PEER_NKEYS, PEER_DK_HALF), PEER_DK_HALF ** -0.5)
    inp['peer_u'] = nrm((DEPTH, PEER_EXPERTS, d), d ** -0.5)
    inp['peer_v'] = nrm((DEPTH, PEER_EXPERTS, d), PEER_HEADS ** -0.5)
    return inp


def reference(x_prompt, x_sample, cache_diff_k, cache_diff_v, state_s5, cache_win_k, cache_win_v, c,
              c_ctx, ada_w, ada_b, norm_mix, norm_ffn, even_w_in, even_w_out,
              s5_lam_re, s5_lam_im, s5_log_step, s5_b_re, s5_b_im, s5_c_re, s5_c_im, s5_d, s5_w_glu,
              diff_q_norm, diff_k_norm, diff_lambda, diff_sub_norm, odd_w_in, odd_w_out,
              win_q_norm, win_k_norm, win_sink, pool_w, pool_scale,
              peer_w_q, peer_sub_keys, peer_u, peer_v):
    P = dict(ada_w=ada_w, ada_b=ada_b, norm_mix=norm_mix, norm_ffn=norm_ffn,
             even_w_in=even_w_in, even_w_out=even_w_out,
             s5_lam_re=s5_lam_re, s5_lam_im=s5_lam_im, s5_log_step=s5_log_step,
             s5_b_re=s5_b_re, s5_b_im=s5_b_im, s5_c_re=s5_c_re, s5_c_im=s5_c_im,
             s5_d=s5_d, s5_w_glu=s5_w_glu,
             diff_q_norm=diff_q_norm, diff_k_norm=diff_k_norm, diff_lambda=diff_lambda,
             diff_sub_norm=diff_sub_norm, odd_w_in=odd_w_in, odd_w_out=odd_w_out,
             win_q_norm=win_q_norm, win_k_norm=win_k_norm, win_sink=win_sink,
             pool_w=pool_w, pool_scale=pool_scale,
             peer_w_q=peer_w_q, peer_sub_keys=peer_sub_keys, peer_u=peer_u, peer_v=peer_v)

    y = x_prompt
    diff_k, diff_v, s5_st, win_k, win_v = [], [], [], [], []
    for l in range(DEPTH):
        mod = (jax.nn.silu(c_ctx) @ ada_w[l] + ada_b[l])[None, None, :]
        y, st = trunk_layer(y, l, mod, P, None, None)
        if l % 2 == 0:
            diff_k.append(st[0])
            diff_v.append(st[1])
            s5_st.append(st[2])
        else:
            win_k.append(st[0])
            win_v.append(st[1])
    y_prompt = y

    n_lat = x_sample.shape[1]
    n_rows = n_lat // GRID_W
    rope = {'diff': axial_rope_tables(n_rows, DIFF_DK), 'win': axial_rope_tables(n_rows, WIN_DH)}
    cache = {'diff_k': cache_diff_k, 'diff_v': cache_diff_v, 's5': state_s5,
             'win_k': cache_win_k, 'win_v': cache_win_v}
    y = x_sample
    for l in range(DEPTH):
        mod = (jax.nn.silu(c) @ ada_w[l] + ada_b[l])[:, None, :]
        y, _ = trunk_layer(y, l, mod, P, rope, cache)
    y_sample = y

    new_diff_k = jnp.stack(diff_k, axis=1)
    new_diff_v = jnp.stack(diff_v, axis=1)
    new_s5 = jnp.stack(s5_st, axis=1)
    new_win_k = jnp.stack(win_k, axis=1)
    new_win_v = jnp.stack(win_v, axis=1)
    return (y_prompt, y_sample, new_diff_k, new_diff_v, new_s5, new_win_k, new_win_v)
```

```python
import functools
import math

import jax
import jax.numpy as jnp
from jax import lax
from jax.experimental import pallas as pl
from jax.experimental.pallas import tpu as pltpu

F32 = jnp.float32
BF16 = jnp.bfloat16

D_MODEL = 1024
MIX_HALF = 512
GRID_W = 64
EPS = 1e-6
NEG_INF = -1e30
ROPE_BASE = 10000.0

S5_GROUPS = 32
S5_GROUP = 16
S5_STATE = 64
S5_JB = 4
S5_BLK = 1024
S5_CHUNK = 32
S5_BATCH = 8

DIFF_HEADS = 4
DIFF_DK = 64
DIFF_DV = 128
WIN_KV_HEADS = 2
WIN_GROUP = 4
WIN_DH = 64
WINDOW = 128
POOL_WINDOWS = (2, 4, 8, 16)
POOL_PAD = 8

PEER_HEADS = 8
PEER_NKEYS = 128
PEER_TOPK = 16
PEER_EXPERT_BLOCK = 512

TOKEN_TILE = 256
VMEM_LIMIT = 56 * 1024 * 1024


def _params(*sem):
    return pltpu.CompilerParams(dimension_semantics=sem, vmem_limit_bytes=VMEM_LIMIT)


def _dot(a, b):
    return jnp.dot(a, b, preferred_element_type=F32)


def _dot_nt(a, b):
    return lax.dot_general(a, b, (((1,), (1,)), ((), ())), preferred_element_type=F32)


def _sigmoid(x):
    return 1.0 / (1.0 + jnp.exp(-x))


def _gelu(x):
    return 0.5 * x * (1.0 + jnp.tanh(math.sqrt(2.0 / math.pi) * (x + 0.044715 * (x * x * x))))


def _modulate(x, gain, shift, scale):
    ms = jnp.mean(x * x, axis=-1, keepdims=True)
    return (x * lax.rsqrt(ms + EPS) * gain) * (1.0 + scale) + shift


def _group_rms(z, ones_bd, gain):
    ms = _dot((z * z).astype(BF16), ones_bd)
    return z * lax.rsqrt(ms + EPS) * gain


def _rope(z, cos_t, sin_t):
    lane = lax.broadcasted_iota(jnp.int32, (1, 128), 1) % 32
    first = lane < 16
    parts = []
    for j in range(z.shape[1] // 128):
        c = z[:, 128 * j:128 * (j + 1)]
        parts.append(jnp.where(first, pltpu.roll(c, 112, axis=1), pltpu.roll(c, 16, axis=1)))
    swapped = parts[0] if len(parts) == 1 else jnp.concatenate(parts, axis=1)
    return z * cos_t + swapped * sin_t


def _mod_kernel(c_ref, w_ref, b_ref, o_ref):
    c = c_ref[...]
    s = c * _sigmoid(c)
    o_ref[0] = jnp.dot(s, w_ref[0], precision=lax.Precision.HIGHEST,
                       preferred_element_type=F32) + b_ref[0]


def _modulation(cvec, ada_w, ada_b):
    depth, d, n = ada_w.shape
    tn = 1536
    return pl.pallas_call(
        _mod_kernel,
        grid=(depth, n // tn),
        in_specs=[pl.BlockSpec((16, d), lambda l, j: (0, 0)),
                  pl.BlockSpec((1, d, tn), lambda l, j: (l, 0, j)),
                  pl.BlockSpec((1, 1, tn), lambda l, j: (l, 0, j))],
        out_specs=pl.BlockSpec((1, 16, tn), lambda l, j: (l, 0, j)),
        out_shape=jax.ShapeDtypeStruct((depth, 16, n), F32),
        compiler_params=_params("arbitrary", "arbitrary"),
        name="adaln_mod",
    )(cvec, ada_w, ada_b.reshape(depth, 1, n))


def _inproj_even_kernel(rope, *refs):
    if rope:
        (x_ref, mod_ref, g_ref, w_ref, bd_ref, gq_ref, gk_ref, cos_ref, sin_ref,
         u_ref, q_ref, kn_ref, ka_ref, v_ref, va_ref) = refs
    else:
        (x_ref, mod_ref, g_ref, w_ref, bd_ref, gq_ref, gk_ref,
         u_ref, q_ref, kn_ref, ka_ref, v_ref, va_ref) = refs
    mod = mod_ref[0]
    h = _modulate(x_ref[0], g_ref[...], mod[0:1], mod[1:2])
    z = _dot(h.astype(BF16), w_ref[...])
    u = z[:, 0:512]
    q = _group_rms(z[:, 512:1024], bd_ref[...], gq_ref[...])
    k = _group_rms(z[:, 1024:1536], bd_ref[...], gk_ref[...])
    v = z[:, 1536:2048]
    u_ref[...] = u
    kn_ref[0] = k
    v_ref[0] = v
    va_ref[0] = v.astype(BF16)
    if rope:
        q = _rope(q, cos_ref[...], sin_ref[...])
        k = _rope(k, cos_ref[...], sin_ref[...])
    q_ref[0] = q.astype(BF16)
    ka_ref[0] = k.astype(BF16)


def _inproj_even(x, mod, gain, w, ones_bd, gq, gk, rope_tabs, per_batch_mod):
    b, l, d = x.shape
    tl = TOKEN_TILE
    rope = rope_tabs is not None
    mod_map = (lambda bi, i: (bi, 0, 0)) if per_batch_mod else (lambda bi, i: (0, 0, 0))
    const2 = lambda bi, i: (0, 0)
    tok = lambda w_: pl.BlockSpec((1, tl, w_), lambda bi, i: (bi, i, 0))
    in_specs = [tok(d),
                pl.BlockSpec((1, 6, d), mod_map),
                pl.BlockSpec((1, d), const2),
                pl.BlockSpec(w.shape, const2),
                pl.BlockSpec(ones_bd.shape, const2),
                pl.BlockSpec((1, 512), const2),
                pl.BlockSpec((1, 512), const2)]
    args = [x, mod, gain, w, ones_bd, gq, gk]
    if rope:
        in_specs += [pl.BlockSpec((tl, 512), lambda bi, i: (i, 0))] * 2
        args += list(rope_tabs)
    out_shape = [jax.ShapeDtypeStruct((l, b * 512), F32),
                 jax.ShapeDtypeStruct((b, l, 512), BF16),
                 jax.ShapeDtypeStruct((b, l, 512), F32),
                 jax.ShapeDtypeStruct((b, l, 512), BF16),
                 jax.ShapeDtypeStruct((b, l, 512), F32),
                 jax.ShapeDtypeStruct((b, l, 512), BF16)]
    out_specs = [pl.BlockSpec((tl, 512), lambda bi, i: (i, bi)),
                 tok(512), tok(512), tok(512), tok(512), tok(512)]
    return pl.pallas_call(
        functools.partial(_inproj_even_kernel, rope),
        grid=(b, l // tl), in_specs=in_specs, out_specs=out_specs, out_shape=out_shape,
        compiler_params=_params("arbitrary", "arbitrary"),
        name="inproj_even_rope" if rope else "inproj_even",
    )(*args)


def _inproj_odd_kernel(rope, *refs):
    if rope:
        (x_ref, mod_ref, g_ref, w_ref, bd_ref, gq_ref, gk_ref, cos_ref, sin_ref,
         q_ref, kn_ref, ka_ref, v_ref, va_ref, u_ref) = refs
    else:
        (x_ref, mod_ref, g_ref, w_ref, bd_ref, gq_ref, gk_ref,
         q_ref, kn_ref, ka_ref, v_ref, va_ref, u_ref) = refs
    mod = mod_ref[0]
    h = _modulate(x_ref[0], g_ref[...], mod[0:1], mod[1:2])
    z = _dot(h.astype(BF16), w_ref[...])
    bd = bd_ref[...]
    q = _group_rms(z[:, 0:512], bd, gq_ref[...])
    k = _group_rms(z[:, 512:640], bd[0:128, 0:128], gk_ref[...])
    v = z[:, 640:768]
    kn_ref[0] = k
    v_ref[0] = v
    va_ref[0] = v.astype(BF16)
    u_ref[0] = z[:, 768:1280]
    if rope:
        q = _rope(q, cos_ref[...], sin_ref[...])
        k = _rope(k, cos_ref[:, 0:128], sin_ref[:, 0:128])
    q_ref[0] = q.astype(BF16)
    ka_ref[0] = k.astype(BF16)


def _inproj_odd(x, mod, gain, w, ones_bd, gq, gk, rope_tabs, per_batch_mod):
    b, l, d = x.shape
    tl = TOKEN_TILE
    rope = rope_tabs is not None
    mod_map = (lambda bi, i: (bi, 0, 0)) if per_batch_mod else (lambda bi, i: (0, 0, 0))
    const2 = lambda bi, i: (0, 0)
    tok = lambda w_: pl.BlockSpec((1, tl, w_), lambda bi, i: (bi, i, 0))
    in_specs = [tok(d),
                pl.BlockSpec((1, 6, d), mod_map),
                pl.BlockSpec((1, d), const2),
                pl.BlockSpec(w.shape, const2),
                pl.BlockSpec(ones_bd.shape, const2),
                pl.BlockSpec((1, 512), const2),
                pl.BlockSpec((1, 128), const2)]
    args = [x, mod, gain, w, ones_bd, gq, gk]
    if rope:
        in_specs += [pl.BlockSpec((tl, 512), lambda bi, i: (i, 0))] * 2
        args += list(rope_tabs)
    out_shape = [jax.ShapeDtypeStruct((b, l, 512), BF16),
                 jax.ShapeDtypeStruct((b, l, 128), F32),
                 jax.ShapeDtypeStruct((b, l, 128), BF16),
                 jax.ShapeDtypeStruct((b, l, 128), F32),
                 jax.ShapeDtypeStruct((b, l, 128), BF16),
                 jax.ShapeDtypeStruct((b, l, 512), F32)]
    out_specs = [tok(512), tok(128), tok(128), tok(128), tok(128), tok(512)]
    return pl.pallas_call(
        functools.partial(_inproj_odd_kernel, rope),
        grid=(b, l // tl), in_specs=in_specs, out_specs=out_specs, out_shape=out_shape,
        compiler_params=_params("arbitrary", "arbitrary"),
        name="inproj_odd_rope" if rope else "inproj_odd",
    )(*args)


def _s5_kernel(uf_ref, ub_ref, bd_ref, cd_ref, a_ref, h0_ref, yf_ref, yb_ref, fin_ref,
               bu_ref, carry_ref):
    c = pl.program_id(1)
    rows = S5_CHUNK * S5_BATCH

    @pl.when(c == 0)
    def _():
        carry_ref[...] = h0_ref[...]

    for d in range(2):
        u = (uf_ref if d == 0 else ub_ref)[...].reshape(rows, MIX_HALF).astype(BF16)
        for j in range(S5_JB):
            bu_ref[d, :, S5_BLK * j:S5_BLK * (j + 1)] = _dot(u[:, 128 * j:128 * (j + 1)], bd_ref[d, j])
        for j in range(S5_JB):
            lo = S5_BLK * j
            a_re = jnp.broadcast_to(a_ref[d, 0, j:j + 1, :], (S5_BATCH, 512))
            a_im = jnp.broadcast_to(a_ref[d, 1, j:j + 1, :], (S5_BATCH, 512))

            def step(s, hc, d=d, lo=lo, a_re=a_re, a_im=a_im):
                h_re, h_im = hc
                t = s if d == 0 else S5_CHUNK - 1 - s
                row = pl.multiple_of(t * S5_BATCH, S5_BATCH)
                b_re = bu_ref[d, pl.ds(row, S5_BATCH), lo:lo + 512]
                b_im = bu_ref[d, pl.ds(row, S5_BATCH), lo + 512:lo + 1024]
                n_re = a_re * h_re - a_im * h_im + b_re
                n_im = a_re * h_im + a_im * h_re + b_im
                bu_ref[d, pl.ds(row, S5_BATCH), lo:lo + 512] = n_re
                bu_ref[d, pl.ds(row, S5_BATCH), lo + 512:lo + 1024] = n_im
                return n_re, n_im

            h_re, h_im = lax.fori_loop(
                0, S5_CHUNK, step,
                (carry_ref[d, :, lo:lo + 512], carry_ref[d, :, lo + 512:lo + 1024]), unroll=4)
            carry_ref[d, :, lo:lo + 512] = h_re
            carry_ref[d, :, lo + 512:lo + 1024] = h_im
        ys = [_dot(bu_ref[d, :, S5_BLK * j:S5_BLK * (j + 1)].astype(BF16), cd_ref[d, j])
              for j in range(S5_JB)]
        y = jnp.concatenate(ys, axis=1).reshape(S5_CHUNK, S5_BATCH, MIX_HALF)
        if d == 0:
            yf_ref[...] = y
        else:
            yb_ref[...] = y

    @pl.when(c == pl.num_programs(1) - 1)
    def _():
        fin_ref[...] = carry_ref[...]


def _s5_scan(u_tm, bd, cd, acoef, h0):
    l, b, _ = u_tm.shape
    nc = l // S5_CHUNK
    ng = b // S5_BATCH
    state_w = S5_JB * S5_BLK
    blk = (S5_CHUNK, S5_BATCH, MIX_HALF)
    full = lambda a: pl.BlockSpec(a.shape, lambda g, c: (0,) * a.ndim)
    return pl.pallas_call(
        _s5_kernel,
        grid=(ng, nc),
        in_specs=[pl.BlockSpec(blk, lambda g, c: (c, g, 0)),
                  pl.BlockSpec(blk, lambda g, c: (nc - 1 - c, g, 0)),
                  full(bd), full(cd), full(acoef),
                  pl.BlockSpec((2, S5_BATCH, state_w), lambda g, c: (0, g, 0))],
        out_specs=[pl.BlockSpec(blk, lambda g, c: (c, g, 0)),
                   pl.BlockSpec(blk, lambda g, c: (nc - 1 - c, g, 0)),
                   pl.BlockSpec((2, S5_BATCH, state_w), lambda g, c: (0, g, 0))],
        out_shape=[jax.ShapeDtypeStruct((l, b, MIX_HALF), F32),
                   jax.ShapeDtypeStruct((l, b, MIX_HALF), F32),
                   jax.ShapeDtypeStruct((2, b, state_w), F32)],
        scratch_shapes=[pltpu.VMEM((2, S5_CHUNK * S5_BATCH, state_w), F32),
                        pltpu.VMEM((2, S5_BATCH, state_w), F32)],
        compiler_params=_params("arbitrary", "arbitrary"),
        name="s5_scan",
    )(u_tm, u_tm, bd, cd, acoef, h0)


def _s5_tables(lam_re, lam_im, log_step, b_re, b_im, c_re, c_im):
    step = jnp.exp(log_step.astype(F32))[..., None]
    lr, li = lam_re.astype(F32), lam_im.astype(F32)
    er = jnp.exp(lr * step)
    a_re, a_im = er * jnp.cos(li * step), er * jnp.sin(li * step)
    den = lr * lr + li * li
    q_re = ((a_re - 1.0) * lr + a_im * li) / den
    q_im = (a_im * lr - (a_re - 1.0) * li) / den
    bb_re = q_re[..., None] * b_re - q_im[..., None] * b_im
    bb_im = q_re[..., None] * b_im + q_im[..., None] * b_re
    eye = jnp.eye(8, dtype=F32)
    bb = jnp.stack([bb_re, bb_im], axis=1).reshape(2, 2, S5_JB, 8, S5_STATE, S5_GROUP)
    bd = jnp.einsum('drjgpc,gh->djgcrhp', bb, eye).reshape(2, S5_JB, 128, S5_BLK).astype(BF16)
    cc = jnp.stack([c_re, -c_im], axis=1).astype(F32).reshape(2, 2, S5_JB, 8, S5_GROUP, S5_STATE)
    cd = jnp.einsum('drjgcp,gh->djrgphc', cc, eye).reshape(2, S5_JB, S5_BLK, 128).astype(BF16)
    acoef = jnp.stack([a_re, a_im], axis=1).reshape(2, 2, S5_JB, 512)
    return bd, cd, acoef


def _s5_state_to_blocks(st):
    b = st.shape[0]
    st = st.astype(F32).reshape(b, 2, 2, S5_JB, 8, S5_STATE)
    return jnp.transpose(st, (1, 0, 3, 2, 4, 5)).reshape(2, b, S5_JB * S5_BLK)


def _s5_blocks_to_state(fin):
    b = fin.shape[1]
    fin = fin.reshape(2, b, S5_JB, 2, 8, S5_STATE)
    return jnp.transpose(fin, (1, 0, 3, 2, 4, 5)).reshape(b, 2, 2, S5_GROUPS, S5_STATE)


def _diff_attn_kernel(lam_init, q_ref, k_ref, v_ref, lv_ref, o_ref):
    lv = lv_ref[...]
    lam = (jnp.exp(jnp.sum(lv[0:1] * lv[1:2], axis=-1, keepdims=True))
           - jnp.exp(jnp.sum(lv[2:3] * lv[3:4], axis=-1, keepdims=True)) + lam_init)
    scale = DIFF_DK ** -0.5
    lane = lax.broadcasted_iota(jnp.int32, (1, 128), 1)
    outs = []
    for h in range(DIFF_HEADS):
        probs = []
        for m in range(2):
            j = m * 2 + h // 2
            keep = (lane < 64) if h % 2 == 0 else (lane >= 64)
            qb = jnp.where(keep, q_ref[0, :, 128 * j:128 * (j + 1)], jnp.zeros((), BF16))
            s = _dot_nt(qb, k_ref[0, :, 128 * j:128 * (j + 1)]) * scale
            e = jnp.exp(s - jnp.max(s, axis=-1, keepdims=True))
            probs.append(e / jnp.sum(e, axis=-1, keepdims=True))
        a = probs[0] - lam * probs[1]
        outs.append(_dot(a.astype(BF16), v_ref[0, :, 128 * h:128 * (h + 1)]))
    o_ref[0] = jnp.concatenate(outs, axis=1)


def _diff_attention(q, k, v, lv, lam_init):
    b, lq, _ = q.shape
    lk = k.shape[1]
    tq = TOKEN_TILE
    return pl.pallas_call(
        functools.partial(_diff_attn_kernel, lam_init),
        grid=(b, lq // tq),
        in_specs=[pl.BlockSpec((1, tq, 512), lambda bi, i: (bi, i, 0)),
                  pl.BlockSpec((1, lk, 512), lambda bi, i: (bi, 0, 0)),
                  pl.BlockSpec((1, lk, 512), lambda bi, i: (bi, 0, 0)),
                  pl.BlockSpec((4, DIFF_DK), lambda bi, i: (0, 0))],
        out_specs=pl.BlockSpec((1, tq, 512), lambda bi, i: (bi, i, 0)),
        out_shape=jax.ShapeDtypeStruct((b, lq, 512), F32),
        compiler_params=_params("arbitrary", "arbitrary"),
        name="diff_attention",
    )(q, k, v, lv)


def _win_attn_kernel(local, seq_len, *refs):
    if local:
        q_ref, kc_ref, vc_ref, kl_ref, vl_ref, sink_ref, o_ref = refs
    else:
        q_ref, kc_ref, vc_ref, sink_ref, o_ref = refs
    tq = q_ref.shape[1]
    scale = WIN_DH ** -0.5
    lane = lax.broadcasted_iota(jnp.int32, (1, 128), 1)
    low = lane < 64
    kc = kc_ref[0]
    vc = vc_ref[0]
    n_ctx = kc.shape[0]
    if local:
        span = tq + 2 * WINDOW
        start = pl.multiple_of(pl.program_id(1) * tq, tq)
        kl = kl_ref[0, pl.ds(start, span), :]
        vl = vl_ref[0, pl.ds(start, span), :]
        qpos = start + lax.broadcasted_iota(jnp.int32, (tq, span), 0)
        kpos = start - WINDOW + lax.broadcasted_iota(jnp.int32, (tq, span), 1)
        ok = (jnp.abs(qpos - kpos) <= WINDOW) & (kpos >= 0) & (kpos < seq_len)
    heads = {}
    for n in range(WIN_KV_HEADS):
        keep = low if n == 0 else jnp.logical_not(low)
        for g in range(WIN_GROUP):
            hd = n * WIN_GROUP + g
            j, half = hd // 2, hd % 2
            qb = q_ref[0, :, 128 * j:128 * (j + 1)]
            if half != n:
                qb = jnp.concatenate([qb[:, 64:128], qb[:, 0:64]], axis=1)
            qb = jnp.where(keep, qb, jnp.zeros((), BF16))
            sink = sink_ref[hd:hd + 1, 0:1]
            s = _dot_nt(qb, kc) * scale
            if local:
                sl = _dot_nt(qb, kl) * scale
                s = jnp.concatenate([s, jnp.where(ok, sl, NEG_INF)], axis=1)
            mx = jnp.maximum(jnp.max(s, axis=-1, keepdims=True), sink)
            e = jnp.exp(s - mx)
            den = jnp.sum(e, axis=-1, keepdims=True) + jnp.exp(sink - mx)
            p = (e / den).astype(BF16)
            o = _dot(p[:, 0:n_ctx], vc)
            if local:
                o = o + _dot(p[:, n_ctx:], vl)
            heads[hd] = o
    blocks = []
    for j in range(4):
        ev, od = heads[2 * j], heads[2 * j + 1]
        n = (2 * j) // WIN_GROUP
        if n == 1:
            ev = pltpu.roll(ev, 64, axis=1)
        else:
            od = pltpu.roll(od, 64, axis=1)
        blocks.append(jnp.where(low, ev, od))
    o_ref[0] = jnp.concatenate(blocks, axis=1)


def _win_attention(q, kc, vc, sink, k_lat=None, v_lat=None):
    b, lq, _ = q.shape
    n_ctx = kc.shape[1]
    local = k_lat is not None
    tq = TOKEN_TILE
    in_specs = [pl.BlockSpec((1, tq, 512), lambda bi, i: (bi, i, 0)),
                pl.BlockSpec((1, n_ctx, 128), lambda bi, i: (bi, 0, 0)),
                pl.BlockSpec((1, n_ctx, 128), lambda bi, i: (bi, 0, 0))]
    args = [q, kc, vc]
    if local:
        lp = k_lat.shape[1]
        in_specs += [pl.BlockSpec((1, lp, 128), lambda bi, i: (bi, 0, 0))] * 2
        args += [k_lat, v_lat]
    in_specs.append(pl.BlockSpec((8, 128), lambda bi, i: (0, 0)))
    args.append(sink)
    return pl.pallas_call(
        functools.partial(_win_attn_kernel, local, lq),
        grid=(b, lq // tq), in_specs=in_specs,
        out_specs=pl.BlockSpec((1, tq, 512), lambda bi, i: (bi, i, 0)),
        out_shape=jax.ShapeDtypeStruct((b, lq, 512), F32),
        compiler_params=_params("arbitrary", "arbitrary"),
        name="win_attention_local" if local else "win_attention",
    )(*args)


def _pool_kernel(u_ref, w_ref, scale_ref, o_ref):
    l = u_ref.shape[1]
    lp = l + 2 * POOL_PAD
    t = lax.broadcasted_iota(jnp.int32, (l, 128), 0)
    zpad = jnp.zeros((POOL_PAD, 128), F32)

    def shifted(a, s):
        return pltpu.roll(a, s, axis=0) + pltpu.roll(a, lp - s, axis=0)

    outs = []
    for gi, wl in enumerate(POOL_WINDOWS):
        x = u_ref[0, :, 128 * gi:128 * (gi + 1)]
        xp = jnp.concatenate([zpad, x, zpad], axis=0)
        acc = xp + pltpu.roll(xp, 1, axis=0)
        if wl >= 4:
            acc = shifted(acc, 1)
        if wl >= 8:
            acc = shifted(acc, 2)
        if wl >= 16:
            acc = shifted(acc, 4)
        win = acc[POOL_PAD:POOL_PAD + l]
        cnt = (jnp.minimum(t + wl // 2, l) - jnp.maximum(t - wl // 2, 0)).astype(F32)
        pooled = win / cnt - x
        outs.append(_dot(pooled.astype(BF16), w_ref[gi]))
    o_ref[0] = jnp.concatenate(outs, axis=1) * scale_ref[...]


def _pool_mixer(u, w, scale):
    b, l, _ = u.shape
    return pl.pallas_call(
        _pool_kernel,
        grid=(b,),
        in_specs=[pl.BlockSpec((1, l, 512), lambda bi: (bi, 0, 0)),
                  pl.BlockSpec(w.shape, lambda bi: (0, 0, 0)),
                  pl.BlockSpec((1, 512), lambda bi: (0, 0))],
        out_specs=pl.BlockSpec((1, l, 512), lambda bi: (bi, 0, 0)),
        out_shape=jax.ShapeDtypeStruct((b, l, 512), F32),
        compiler_params=_params("arbitrary"),
        name="pool_mixer",
    )(u, w, scale)


def _mixout_even_kernel(out_scale, x_ref, mod_ref, u_ref, yf_ref, yb_ref, oa_ref, sd_ref,
                        wglu_ref, gsub_ref, wout_ref, o_ref):
    mod = mod_ref[0]
    y = sd_ref[...] * u_ref[...] + yf_ref[...] + yb_ref[...]
    g = _gelu(y)
    o_a = g * _sigmoid(_dot(g.astype(BF16), wglu_ref[...]))
    parts = [o_a.astype(BF16)]
    for h in range(DIFF_HEADS):
        blk = oa_ref[0, :, 128 * h:128 * (h + 1)]
        ms = jnp.mean(blk * blk, axis=-1, keepdims=True)
        parts.append(((blk * lax.rsqrt(ms + EPS) * gsub_ref[...]) * out_scale).astype(BF16))
    cat = jnp.concatenate(parts, axis=1)
    o_ref[0] = x_ref[0] + mod[2:3] * _dot(cat, wout_ref[...])


def _mixout_even(x, mod, u_tm2, yf2, yb2, o_attn, s5_d, w_glu, g_sub, w_out, out_scale, per_batch_mod):
    b, l, d = x.shape
    tl = TOKEN_TILE
    mod_map = (lambda bi, i: (bi, 0, 0)) if per_batch_mod else (lambda bi, i: (0, 0, 0))
    const2 = lambda bi, i: (0, 0)
    tm = pl.BlockSpec((tl, 512), lambda bi, i: (i, bi))
    return pl.pallas_call(
        functools.partial(_mixout_even_kernel, out_scale),
        grid=(b, l // tl),
        in_specs=[pl.BlockSpec((1, tl, d), lambda bi, i: (bi, i, 0)),
                  pl.BlockSpec((1, 6, d), mod_map),
                  tm, tm, tm,
                  pl.BlockSpec((1, tl, 512), lambda bi, i: (bi, i, 0)),
                  pl.BlockSpec((1, 512), const2),
                  pl.BlockSpec((512, 512), const2),
                  pl.BlockSpec((1, 128), const2),
                  pl.BlockSpec((d, d), const2)],
        out_specs=pl.BlockSpec((1, tl, d), lambda bi, i: (bi, i, 0)),
        out_shape=jax.ShapeDtypeStruct((b, l, d), F32),
        compiler_params=_params("arbitrary", "arbitrary"),
        name="mixout_even",
    )(x, mod, u_tm2, yf2, yb2, o_attn, s5_d, w_glu, g_sub, w_out)


def _mixout_odd_kernel(x_ref, mod_ref, oc_ref, od_ref, wout_ref, o_ref):
    mod = mod_ref[0]
    cat = jnp.concatenate([oc_ref[0].astype(BF16), od_ref[0].astype(BF16)], axis=1)
    o_ref[0] = x_ref[0] + mod[2:3] * _dot(cat, wout_ref[...])


def _mixout_odd(x, mod, o_c, o_d, w_out, per_batch_mod):
    b, l, d = x.shape
    tl = TOKEN_TILE
    mod_map = (lambda bi, i: (bi, 0, 0)) if per_batch_mod else (lambda bi, i: (0, 0, 0))
    tok = lambda w_: pl.BlockSpec((1, tl, w_), lambda bi, i: (bi, i, 0))
    return pl.pallas_call(
        _mixout_odd_kernel,
        grid=(b, l // tl),
        in_specs=[tok(d), pl.BlockSpec((1, 6, d), mod_map), tok(512), tok(512),
                  pl.BlockSpec((d, d), lambda bi, i: (0, 0))],
        out_specs=tok(d),
        out_shape=jax.ShapeDtypeStruct((b, l, d), F32),
        compiler_params=_params("arbitrary", "arbitrary"),
        name="mixout_odd",
    )(x, mod, o_c, o_d, w_out)


def _top16(s, kio):
    n = s.shape[0]
    vals, idxs = [], []
    for _ in range(PEER_TOPK):
        m = jnp.max(s, axis=0, keepdims=True)
        ix = jnp.min(jnp.where(s == m, kio, float(n)), axis=0, keepdims=True)
        vals.append(m)
        idxs.append(ix)
        s = jnp.where(kio == ix, -jnp.inf, s)
    return vals, idxs


def _peer_topk_kernel(x_ref, mod_ref, g_ref, wq_ref, keys_ref, h_ref, n_ref, r2_ref, p1_ref, p2_ref):
    mod = mod_ref[0]
    h = _modulate(x_ref[0], g_ref[...], mod[3:4], mod[4:5]).astype(BF16)
    h_ref[0] = h
    q = _dot(h, wq_ref[...])
    tl = q.shape[0]
    kio = lax.broadcasted_iota(jnp.int32, (PEER_NKEYS, tl), 0).astype(F32)
    cio = lax.broadcasted_iota(jnp.int32, (PEER_TOPK * PEER_TOPK, tl), 0).astype(F32)
    for hd in range(PEER_HEADS):
        s1 = _dot_nt(keys_ref[2 * hd], q[:, 256 * hd:256 * hd + 128].astype(BF16))
        s2 = _dot_nt(keys_ref[2 * hd + 1], q[:, 256 * hd + 128:256 * hd + 256].astype(BF16))
        v1, i1 = _top16(s1, kio)
        v2, i2 = _top16(s2, kio)
        v2s = jnp.concatenate(v2, axis=0)
        cand = jnp.concatenate([v1[a] + v2s for a in range(PEER_TOPK)], axis=0)
        best, pos = _top16(cand, cio)
        z = jnp.ones_like(best[0])
        for r in range(1, PEER_TOPK):
            z = z + jnp.exp(best[r] - best[0])
        n_e1 = jnp.zeros((PEER_NKEYS, tl), F32)
        for a in range(PEER_TOPK):
            n_a = jnp.zeros_like(z)
            for r in range(PEER_TOPK):
                n_a = n_a + jnp.where(jnp.floor(pos[r] * (1.0 / PEER_TOPK)) == float(a), 1.0, 0.0)
            n_e1 = jnp.where(kio == i1[a], n_a, n_e1)
        rank2 = jnp.full((PEER_NKEYS, tl), float(PEER_TOPK), F32)
        for bb in range(PEER_TOPK):
            rank2 = jnp.where(kio == i2[bb], float(bb), rank2)
        n_ref[0, hd] = n_e1
        r2_ref[0, hd] = rank2
        p1_ref[0, hd] = jnp.exp(s1 - v1[0]) / z
        p2_ref[0, hd] = jnp.exp(s2 - v2[0])


def _peer_topk(x, mod, gain, w_q, keys, per_batch_mod):
    b, l, d = x.shape
    tl = TOKEN_TILE
    mod_map = (lambda bi, i: (bi, 0, 0)) if per_batch_mod else (lambda bi, i: (0, 0, 0))
    sel = pl.BlockSpec((1, PEER_HEADS, PEER_NKEYS, tl), lambda bi, i: (bi, 0, 0, i))
    sel_shape = jax.ShapeDtypeStruct((b, PEER_HEADS, PEER_NKEYS, l), F32)
    return pl.pallas_call(
        _peer_topk_kernel,
        grid=(b, l // tl),
        in_specs=[pl.BlockSpec((1, tl, d), lambda bi, i: (bi, i, 0)),
                  pl.BlockSpec((1, 6, d), mod_map),
                  pl.BlockSpec((1, d), lambda bi, i: (0, 0)),
                  pl.BlockSpec(w_q.shape, lambda bi, i: (0, 0)),
                  pl.BlockSpec(keys.shape, lambda bi, i: (0, 0, 0))],
        out_specs=[pl.BlockSpec((1, tl, d), lambda bi, i: (bi, i, 0)), sel, sel, sel, sel],
        out_shape=[jax.ShapeDtypeStruct((b, l, d), BF16), sel_shape, sel_shape, sel_shape, sel_shape],
        compiler_params=_params("arbitrary", "arbitrary"),
        name="peer_topk",
    )(x, mod, gain, w_q, keys)


def _peer_dense_kernel(x_ref, mod_ref, h_ref, u_ref, vt_ref, n_ref, r2_ref, p1_ref, p2_ref, o_ref,
                       ht_ref, acc_ref):
    e = pl.program_id(2)
    rows_per_step = PEER_EXPERT_BLOCK // PEER_NKEYS

    @pl.when(e == 0)
    def _():
        ht_ref[...] = h_ref[0].astype(F32).T.astype(BF16)
        acc_ref[...] = jnp.zeros_like(acc_ref)

    act = _gelu(_dot(u_ref[...], ht_ref[...]))
    ws = []
    for r in range(rows_per_step):
        e1 = e * rows_per_step + r
        gate = jnp.zeros((PEER_NKEYS, act.shape[1]), F32)
        for hd in range(PEER_HEADS):
            n_row = n_ref[0, hd, pl.ds(e1, 1), :]
            p1_row = p1_ref[0, hd, pl.ds(e1, 1), :]
            gate = gate + jnp.where(r2_ref[0, hd] < n_row, p2_ref[0, hd], 0.0) * p1_row
        ws.append((gate * act[PEER_NKEYS * r:PEER_NKEYS * (r + 1)]).astype(BF16))
    acc_ref[...] += _dot(vt_ref[...], jnp.concatenate(ws, axis=0))

    @pl.when(e == pl.num_programs(2) - 1)
    def _():
        o_ref[0] = x_ref[0] + mod_ref[0][5:6] * acc_ref[...].T


def _peer_dense(x, mod, h, u_tab, vt_tab, n_e1, rank2, p1n, p2, per_batch_mod):
    b, l, d = x.shape
    tm = TOKEN_TILE
    eb = PEER_EXPERT_BLOCK
    n_exp = u_tab.shape[0]
    mod_map = (lambda bi, i, e: (bi, 0, 0)) if per_batch_mod else (lambda bi, i, e: (0, 0, 0))
    tok = lambda: pl.BlockSpec((1, tm, d), lambda bi, i, e: (bi, i, 0))
    sel = pl.BlockSpec((1, PEER_HEADS, PEER_NKEYS, tm), lambda bi, i, e: (bi, 0, 0, i))
    return pl.pallas_call(
        _peer_dense_kernel,
        grid=(b, l // tm, n_exp // eb),
        in_specs=[tok(), pl.BlockSpec((1, 6, d), mod_map), tok(),
                  pl.BlockSpec((eb, d), lambda bi, i, e: (e, 0)),
                  pl.BlockSpec((d, eb), lambda bi, i, e: (0, e)),
                  sel, sel, sel, sel],
        out_specs=tok(),
        out_shape=jax.ShapeDtypeStruct((b, l, d), F32),
        scratch_shapes=[pltpu.VMEM((d, tm), BF16), pltpu.VMEM((d, tm), F32)],
        compiler_params=_params("arbitrary", "arbitrary", "arbitrary"),
        name="peer_dense",
    )(x, mod, h, u_tab, vt_tab, n_e1, rank2, p1n, p2)


def _rope_tables(n_lat, width):
    q4 = DIFF_DK // 4
    n_rows = n_lat // GRID_W
    row = jnp.repeat(jnp.arange(n_rows), GRID_W).astype(F32)
    col = jnp.tile(jnp.arange(GRID_W), n_rows).astype(F32)
    freqs = ROPE_BASE ** (-jnp.arange(q4, dtype=F32) / q4)
    ang = jnp.stack([row[:, None] * freqs, col[:, None] * freqs], axis=1)
    cos, sin = jnp.cos(ang), jnp.sin(ang)
    cos64 = jnp.stack([cos, cos], axis=2).reshape(n_lat, 64)
    sin64 = jnp.stack([-sin, sin], axis=2).reshape(n_lat, 64)
    reps = width // 64
    return jnp.tile(cos64, (1, reps)), jnp.tile(sin64, (1, reps))


def _ones_block_diag(width, group):
    idx = jnp.arange(width) // group
    return jnp.where(idx[:, None] == idx[None, :], 1.0 / group, 0.0).astype(BF16)


def _peer_layer(x, mod, W, l, per_batch_mod):
    h, n_e1, rank2, p1n, p2 = _peer_topk(x, mod, W['norm_ffn'][l], W['peer_w_q'][l], W['peer_keys'][l],
                                         per_batch_mod)
    return _peer_dense(x, mod, h, W['peer_u'][l], W['peer_vt'][l], n_e1, rank2, p1n, p2, per_batch_mod)


def _even_layer(x, mod, W, l, cache, per_batch_mod):
    b, seq, _ = x.shape
    i = l // 2
    lam_init = 0.8 - 0.6 * math.exp(-0.3 * l)
    rope_tabs = W['rope'] if cache is not None else None
    u_tm2, q, k_norm, k_att, v, v_att = _inproj_even(
        x, mod, W['norm_mix'][l], W['even_w_in'][i], W['ones_bd'], W['diff_q_norm'][i],
        W['diff_k_norm'][i], rope_tabs, per_batch_mod)
    if cache is None:
        h0 = jnp.zeros((2, b, S5_JB * S5_BLK), F32)
        k_all, v_all = k_att, v_att
    else:
        h0 = _s5_state_to_blocks(cache['s5'][:, i])
        k_all = jnp.concatenate([cache['diff_k'][:, i].reshape(b, -1, 512).astype(BF16), k_att], axis=1)
        v_all = jnp.concatenate([cache['diff_v'][:, i].reshape(b, -1, 512).astype(BF16), v_att], axis=1)
    yf, yb, fin = _s5_scan(u_tm2.reshape(seq, b, 512), W['s5_bd'][i], W['s5_cd'][i], W['s5_a'][i], h0)
    o_attn = _diff_attention(q, k_all, v_all, W['diff_lambda'][i], lam_init)
    x = _mixout_even(x, mod, u_tm2, yf.reshape(seq, b * 512), yb.reshape(seq, b * 512), o_attn,
                     W['s5_d'][i], W['s5_w_glu'][i], W['diff_sub_norm'][i], W['even_w_out'][i],
                     1.0 - lam_init, per_batch_mod)
    return x, (k_norm, v, fin)


def _odd_layer(x, mod, W, l, cache, per_batch_mod):
    b, seq, _ = x.shape
    i = l // 2
    rope_tabs = W['rope'] if cache is not None else None
    q, k_norm, k_att, v, v_att, u = _inproj_odd(
        x, mod, W['norm_mix'][l], W['odd_w_in'][i], W['ones_bd'], W['win_q_norm'][i],
        W['win_k_norm'][i], rope_tabs, per_batch_mod)
    if cache is None:
        o_c = _win_attention(q, k_att, v_att, W['win_sink'][i])
    else:
        pad = ((0, 0), (WINDOW, WINDOW), (0, 0))
        o_c = _win_attention(q, cache['win_k'][:, i].reshape(b, -1, 128).astype(BF16),
                             cache['win_v'][:, i].reshape(b, -1, 128).astype(BF16), W['win_sink'][i],
                             jnp.pad(k_att, pad), jnp.pad(v_att, pad))
    o_d = _pool_mixer(u, W['pool_w'][i], W['pool_scale'][i])
    x = _mixout_odd(x, mod, o_c, o_d, W['odd_w_out'][i], per_batch_mod)
    return x, (k_norm, v)


def kernel(x_prompt, x_sample, cache_diff_k, cache_diff_v, state_s5, cache_win_k, cache_win_v, c, c_ctx, ada_w, ada_b, norm_mix, norm_ffn, even_w_in, even_w_out, s5_lam_re, s5_lam_im, s5_log_step, s5_b_re, s5_b_im, s5_c_re, s5_c_im, s5_d, s5_w_glu, diff_q_norm, diff_k_norm, diff_lambda, diff_sub_norm, odd_w_in, odd_w_out, win_q_norm, win_k_norm, win_sink, pool_w, pool_scale, peer_w_q, peer_sub_keys, peer_u, peer_v):
    depth = ada_w.shape[0]
    n_even, n_odd = even_w_in.shape[0], odd_w_in.shape[0]
    bsz, seq, d = x_prompt.shape
    dec_b, dec_l, _ = x_sample.shape

    cvec = jnp.zeros((16, d), F32).at[0].set(c_ctx).at[1:1 + dec_b].set(c)
    mod = _modulation(cvec, ada_w, ada_b)

    s5 = [_s5_tables(s5_lam_re[i], s5_lam_im[i], s5_log_step[i], s5_b_re[i], s5_b_im[i],
                     s5_c_re[i], s5_c_im[i]) for i in range(n_even)]
    W = dict(
        norm_mix=norm_mix.reshape(depth, 1, d), norm_ffn=norm_ffn.reshape(depth, 1, d),
        even_w_in=even_w_in.astype(BF16), even_w_out=even_w_out.astype(BF16),
        odd_w_in=odd_w_in.astype(BF16), odd_w_out=odd_w_out.astype(BF16),
        ones_bd=_ones_block_diag(512, 64),
        diff_q_norm=jnp.tile(diff_q_norm, (1, 8)).reshape(n_even, 1, 512),
        diff_k_norm=jnp.tile(diff_k_norm, (1, 8)).reshape(n_even, 1, 512),
        diff_lambda=diff_lambda, diff_sub_norm=diff_sub_norm.reshape(n_even, 1, DIFF_DV),
        s5_bd=[t[0] for t in s5], s5_cd=[t[1] for t in s5], s5_a=[t[2] for t in s5],
        s5_d=s5_d.reshape(n_even, 1, 512), s5_w_glu=s5_w_glu.astype(BF16),
        win_q_norm=jnp.tile(win_q_norm, (1, 8)).reshape(n_odd, 1, 512),
        win_k_norm=jnp.tile(win_k_norm, (1, 2)).reshape(n_odd, 1, 128),
        win_sink=jnp.broadcast_to(win_sink[:, :, None], (n_odd, 8, 128)),
        pool_w=pool_w.astype(BF16), pool_scale=pool_scale.reshape(n_odd, 1, 512),
        peer_w_q=peer_w_q.astype(BF16),
        peer_keys=peer_sub_keys.astype(BF16).reshape(depth, 2 * PEER_HEADS, PEER_NKEYS, -1),
        peer_u=peer_u.astype(BF16),
        peer_vt=jnp.swapaxes(peer_v, 1, 2).astype(BF16),
        rope=_rope_tables(dec_l, 512),
    )
    cache = {'diff_k': cache_diff_k, 'diff_v': cache_diff_v, 's5': state_s5,
             'win_k': cache_win_k, 'win_v': cache_win_v}

    def run(x, mods, cch, per_batch_mod):
        states = []
        for l in range(depth):
            if l % 2 == 0:
                x, st = _even_layer(x, mods[l], W, l, cch, per_batch_mod)
            else:
                x, st = _odd_layer(x, mods[l], W, l, cch, per_batch_mod)
            states.append(st)
            x = _peer_layer(x, mods[l], W, l, per_batch_mod)
        return x, states

    ctx_mods = [mod[l, 0:1].reshape(1, 6, d) for l in range(depth)]
    dec_mods = [mod[l, 1:1 + dec_b].reshape(dec_b, 6, d) for l in range(depth)]
    y_prompt, states = run(x_prompt, ctx_mods, None, False)
    y_sample, _ = run(x_sample, dec_mods, cache, True)

    evens = [states[l] for l in range(depth) if l % 2 == 0]
    odds = [states[l] for l in range(depth) if l % 2 == 1]
    new_diff_k = jnp.stack([s[0].reshape(bsz, seq, 2, DIFF_HEADS, DIFF_DK) for s in evens], axis=1)
    new_diff_v = jnp.stack([s[1].reshape(bsz, seq, DIFF_HEADS, DIFF_DV) for s in evens], axis=1)
    new_s5 = jnp.stack([_s5_blocks_to_state(s[2]) for s in evens], axis=1)
    new_win_k = jnp.stack([s[0].reshape(bsz, seq, WIN_KV_HEADS, WIN_DH) for s in odds], axis=1)
    new_win_v = jnp.stack([s[1].reshape(bsz, seq, WIN_KV_HEADS, WIN_DH) for s in odds], axis=1)
    return (y_prompt, y_sample, new_diff_k, new_diff_v, new_s5, new_win_k, new_win_v)
```

```python
import functools
import math

import jax
import jax.numpy as jnp
from jax import lax
from jax.experimental import pallas as pl
from jax.experimental.pallas import tpu as pltpu

F32 = jnp.float32
BF16 = jnp.bfloat16

D_MODEL = 1024
MIX_HALF = 512
GRID_W = 64
EPS = 1e-6
NEG_INF = -1e30
ROPE_BASE = 10000.0

S5_GROUPS = 32
S5_GROUP = 16
S5_STATE = 64
S5_JB = 4
S5_BLK = 1024
S5_CHUNK = 32
S5_BATCH = 8

DIFF_HEADS = 4
DIFF_DK = 64
DIFF_DV = 128
WIN_KV_HEADS = 2
WIN_GROUP = 4
WIN_DH = 64
WINDOW = 128
POOL_WINDOWS = (2, 4, 8, 16)
POOL_PAD = 8

PEER_HEADS = 8
PEER_NKEYS = 128
PEER_TOPK = 16
PEER_CAND_ROWS = 80
PEER_SUB = 512
PEER_STEP = 2048
PEER_TOKENS = 512

TOKEN_TILE = 256
VMEM_LIMIT = 56 * 1024 * 1024


def _params(*sem):
    return pltpu.CompilerParams(dimension_semantics=sem, vmem_limit_bytes=VMEM_LIMIT)


def _dot(a, b):
    return jnp.dot(a, b, preferred_element_type=F32)


def _dot_nt(a, b):
    return lax.dot_general(a, b, (((1,), (1,)), ((), ())), preferred_element_type=F32)


def _sigmoid(x):
    return 1.0 / (1.0 + jnp.exp(-x))


def _gelu(x):
    c = math.sqrt(2.0 / math.pi)
    hx = 0.5 * x
    return hx + hx * jnp.tanh(x * (c + (c * 0.044715) * (x * x)))


def _modulate(x, gain, shift, scale):
    ms = jnp.mean(x * x, axis=-1, keepdims=True)
    return (x * lax.rsqrt(ms + EPS) * gain) * (1.0 + scale) + shift


def _group_rms(z, ones_bd, gain):
    ms = _dot((z * z).astype(BF16), ones_bd)
    return z * lax.rsqrt(ms + EPS) * gain


def _rope(z, cos_t, sin_t):
    lane = lax.broadcasted_iota(jnp.int32, (1, 128), 1) % 32
    first = lane < 16
    parts = []
    for j in range(z.shape[1] // 128):
        c = z[:, 128 * j:128 * (j + 1)]
        parts.append(jnp.where(first, pltpu.roll(c, 112, axis=1), pltpu.roll(c, 16, axis=1)))
    swapped = parts[0] if len(parts) == 1 else jnp.concatenate(parts, axis=1)
    return z * cos_t + swapped * sin_t


def _mod_kernel(c_ref, w_ref, b_ref, o_ref):
    c = c_ref[...]
    s = c * _sigmoid(c)
    o_ref[0] = jnp.dot(s, w_ref[0], precision=lax.Precision.HIGHEST,
                       preferred_element_type=F32) + b_ref[0]


def _modulation(cvec, ada_w, ada_b):
    depth, d, n = ada_w.shape
    tn = 1536
    return pl.pallas_call(
        _mod_kernel,
        grid=(depth, n // tn),
        in_specs=[pl.BlockSpec((16, d), lambda l, j: (0, 0)),
                  pl.BlockSpec((1, d, tn), lambda l, j: (l, 0, j)),
                  pl.BlockSpec((1, 1, tn), lambda l, j: (l, 0, j))],
        out_specs=pl.BlockSpec((1, 16, tn), lambda l, j: (l, 0, j)),
        out_shape=jax.ShapeDtypeStruct((depth, 16, n), F32),
        compiler_params=_params("arbitrary", "arbitrary"),
        name="adaln_mod",
    )(cvec, ada_w, ada_b.reshape(depth, 1, n))


def _inproj_even_kernel(rope, *refs):
    if rope:
        (x_ref, mod_ref, g_ref, w_ref, bd_ref, gq_ref, gk_ref, cos_ref, sin_ref,
         u_ref, q_ref, kn_ref, ka_ref, v_ref, va_ref) = refs
    else:
        (x_ref, mod_ref, g_ref, w_ref, bd_ref, gq_ref, gk_ref,
         u_ref, q_ref, kn_ref, ka_ref, v_ref, va_ref) = refs
    mod = mod_ref[0]
    h = _modulate(x_ref[0], g_ref[...], mod[0:1], mod[1:2])
    z = _dot(h.astype(BF16), w_ref[...])
    u = z[:, 0:512]
    q = _group_rms(z[:, 512:1024], bd_ref[...], gq_ref[...])
    k = _group_rms(z[:, 1024:1536], bd_ref[...], gk_ref[...])
    v = z[:, 1536:2048]
    u_ref[...] = u
    kn_ref[0] = k
    v_ref[0] = v
    va_ref[0] = v.astype(BF16)
    if rope:
        q = _rope(q, cos_ref[...], sin_ref[...])
        k = _rope(k, cos_ref[...], sin_ref[...])
    q_ref[0] = q.astype(BF16)
    ka_ref[0] = k.astype(BF16)


def _inproj_even(x, mod, gain, w, ones_bd, gq, gk, rope_tabs, per_batch_mod):
    b, l, d = x.shape
    tl = TOKEN_TILE
    rope = rope_tabs is not None
    mod_map = (lambda bi, i: (bi, 0, 0)) if per_batch_mod else (lambda bi, i: (0, 0, 0))
    const2 = lambda bi, i: (0, 0)
    tok = lambda w_: pl.BlockSpec((1, tl, w_), lambda bi, i: (bi, i, 0))
    in_specs = [tok(d),
                pl.BlockSpec((1, 6, d), mod_map),
                pl.BlockSpec((1, d), const2),
                pl.BlockSpec(w.shape, const2),
                pl.BlockSpec(ones_bd.shape, const2),
                pl.BlockSpec((1, 512), const2),
                pl.BlockSpec((1, 512), const2)]
    args = [x, mod, gain, w, ones_bd, gq, gk]
    if rope:
        in_specs += [pl.BlockSpec((tl, 512), lambda bi, i: (i, 0))] * 2
        args += list(rope_tabs)
    out_shape = [jax.ShapeDtypeStruct((l, b * 512), F32),
                 jax.ShapeDtypeStruct((b, l, 512), BF16),
                 jax.ShapeDtypeStruct((b, l, 512), F32),
                 jax.ShapeDtypeStruct((b, l, 512), BF16),
                 jax.ShapeDtypeStruct((b, l, 512), F32),
                 jax.ShapeDtypeStruct((b, l, 512), BF16)]
    out_specs = [pl.BlockSpec((tl, 512), lambda bi, i: (i, bi)),
                 tok(512), tok(512), tok(512), tok(512), tok(512)]
    return pl.pallas_call(
        functools.partial(_inproj_even_kernel, rope),
        grid=(b, l // tl), in_specs=in_specs, out_specs=out_specs, out_shape=out_shape,
        compiler_params=_params("arbitrary", "arbitrary"),
        name="inproj_even_rope" if rope else "inproj_even",
    )(*args)


def _inproj_odd_kernel(rope, *refs):
    if rope:
        (x_ref, mod_ref, g_ref, w_ref, bd_ref, gq_ref, gk_ref, cos_ref, sin_ref,
         q_ref, kn_ref, ka_ref, v_ref, va_ref, u_ref) = refs
    else:
        (x_ref, mod_ref, g_ref, w_ref, bd_ref, gq_ref, gk_ref,
         q_ref, kn_ref, ka_ref, v_ref, va_ref, u_ref) = refs
    mod = mod_ref[0]
    h = _modulate(x_ref[0], g_ref[...], mod[0:1], mod[1:2])
    z = _dot(h.astype(BF16), w_ref[...])
    bd = bd_ref[...]
    q = _group_rms(z[:, 0:512], bd, gq_ref[...])
    k = _group_rms(z[:, 512:640], bd[0:128, 0:128], gk_ref[...])
    v = z[:, 640:768]
    kn_ref[0] = k
    v_ref[0] = v
    va_ref[0] = v.astype(BF16)
    u_ref[0] = z[:, 768:1280]
    if rope:
        q = _rope(q, cos_ref[...], sin_ref[...])
        k = _rope(k, cos_ref[:, 0:128], sin_ref[:, 0:128])
    q_ref[0] = q.astype(BF16)
    ka_ref[0] = k.astype(BF16)


def _inproj_odd(x, mod, gain, w, ones_bd, gq, gk, rope_tabs, per_batch_mod):
    b, l, d = x.shape
    tl = TOKEN_TILE
    rope = rope_tabs is not None
    mod_map = (lambda bi, i: (bi, 0, 0)) if per_batch_mod else (lambda bi, i: (0, 0, 0))
    const2 = lambda bi, i: (0, 0)
    tok = lambda w_: pl.BlockSpec((1, tl, w_), lambda bi, i: (bi, i, 0))
    in_specs = [tok(d),
                pl.BlockSpec((1, 6, d), mod_map),
                pl.BlockSpec((1, d), const2),
                pl.BlockSpec(w.shape, const2),
                pl.BlockSpec(ones_bd.shape, const2),
                pl.BlockSpec((1, 512), const2),
                pl.BlockSpec((1, 128), const2)]
    args = [x, mod, gain, w, ones_bd, gq, gk]
    if rope:
        in_specs += [pl.BlockSpec((tl, 512), lambda bi, i: (i, 0))] * 2
        args += list(rope_tabs)
    out_shape = [jax.ShapeDtypeStruct((b, l, 512), BF16),
                 jax.ShapeDtypeStruct((b, l, 128), F32),
                 jax.ShapeDtypeStruct((b, l, 128), BF16),
                 jax.ShapeDtypeStruct((b, l, 128), F32),
                 jax.ShapeDtypeStruct((b, l, 128), BF16),
                 jax.ShapeDtypeStruct((b, l, 512), F32)]
    out_specs = [tok(512), tok(128), tok(128), tok(128), tok(128), tok(512)]
    return pl.pallas_call(
        functools.partial(_inproj_odd_kernel, rope),
        grid=(b, l // tl), in_specs=in_specs, out_specs=out_specs, out_shape=out_shape,
        compiler_params=_params("arbitrary", "arbitrary"),
        name="inproj_odd_rope" if rope else "inproj_odd",
    )(*args)


def _s5_kernel(uf_ref, ub_ref, bd_ref, cd_ref, a_ref, h0_ref, yf_ref, yb_ref, fin_ref,
               bu_ref, carry_ref):
    c = pl.program_id(1)
    rows = S5_CHUNK * S5_BATCH

    @pl.when(c == 0)
    def _():
        carry_ref[...] = h0_ref[...]

    for d in range(2):
        u = (uf_ref if d == 0 else ub_ref)[...].reshape(rows, MIX_HALF).astype(BF16)
        for j in range(S5_JB):
            bu_ref[d, :, S5_BLK * j:S5_BLK * (j + 1)] = _dot(u[:, 128 * j:128 * (j + 1)], bd_ref[d, j])
        for j in range(S5_JB):
            lo = S5_BLK * j
            a_re = jnp.broadcast_to(a_ref[d, 0, j:j + 1, :], (S5_BATCH, 512))
            a_im = jnp.broadcast_to(a_ref[d, 1, j:j + 1, :], (S5_BATCH, 512))

            def step(s, hc, d=d, lo=lo, a_re=a_re, a_im=a_im):
                h_re, h_im = hc
                t = s if d == 0 else S5_CHUNK - 1 - s
                row = pl.multiple_of(t * S5_BATCH, S5_BATCH)
                b_re = bu_ref[d, pl.ds(row, S5_BATCH), lo:lo + 512]
                b_im = bu_ref[d, pl.ds(row, S5_BATCH), lo + 512:lo + 1024]
                n_re = a_re * h_re - a_im * h_im + b_re
                n_im = a_re * h_im + a_im * h_re + b_im
                bu_ref[d, pl.ds(row, S5_BATCH), lo:lo + 512] = n_re
                bu_ref[d, pl.ds(row, S5_BATCH), lo + 512:lo + 1024] = n_im
                return n_re, n_im

            h_re, h_im = lax.fori_loop(
                0, S5_CHUNK, step,
                (carry_ref[d, :, lo:lo + 512], carry_ref[d, :, lo + 512:lo + 1024]), unroll=4)
            carry_ref[d, :, lo:lo + 512] = h_re
            carry_ref[d, :, lo + 512:lo + 1024] = h_im
        ys = [_dot(bu_ref[d, :, S5_BLK * j:S5_BLK * (j + 1)].astype(BF16), cd_ref[d, j])
              for j in range(S5_JB)]
        y = jnp.concatenate(ys, axis=1).reshape(S5_CHUNK, S5_BATCH, MIX_HALF)
        if d == 0:
            yf_ref[...] = y
        else:
            yb_ref[...] = y

    @pl.when(c == pl.num_programs(1) - 1)
    def _():
        fin_ref[...] = carry_ref[...]


def _s5_scan(u_tm, bd, cd, acoef, h0):
    l, b, _ = u_tm.shape
    nc = l // S5_CHUNK
    ng = b // S5_BATCH
    state_w = S5_JB * S5_BLK
    blk = (S5_CHUNK, S5_BATCH, MIX_HALF)
    full = lambda a: pl.BlockSpec(a.shape, lambda g, c: (0,) * a.ndim)
    return pl.pallas_call(
        _s5_kernel,
        grid=(ng, nc),
        in_specs=[pl.BlockSpec(blk, lambda g, c: (c, g, 0)),
                  pl.BlockSpec(blk, lambda g, c: (nc - 1 - c, g, 0)),
                  full(bd), full(cd), full(acoef),
                  pl.BlockSpec((2, S5_BATCH, state_w), lambda g, c: (0, g, 0))],
        out_specs=[pl.BlockSpec(blk, lambda g, c: (c, g, 0)),
                   pl.BlockSpec(blk, lambda g, c: (nc - 1 - c, g, 0)),
                   pl.BlockSpec((2, S5_BATCH, state_w), lambda g, c: (0, g, 0))],
        out_shape=[jax.ShapeDtypeStruct((l, b, MIX_HALF), F32),
                   jax.ShapeDtypeStruct((l, b, MIX_HALF), F32),
                   jax.ShapeDtypeStruct((2, b, state_w), F32)],
        scratch_shapes=[pltpu.VMEM((2, S5_CHUNK * S5_BATCH, state_w), F32),
                        pltpu.VMEM((2, S5_BATCH, state_w), F32)],
        compiler_params=_params("arbitrary", "arbitrary"),
        name="s5_scan",
    )(u_tm, u_tm, bd, cd, acoef, h0)


def _s5_tables(lam_re, lam_im, log_step, b_re, b_im, c_re, c_im):
    step = jnp.exp(log_step.astype(F32))[..., None]
    lr, li = lam_re.astype(F32), lam_im.astype(F32)
    er = jnp.exp(lr * step)
    a_re, a_im = er * jnp.cos(li * step), er * jnp.sin(li * step)
    den = lr * lr + li * li
    q_re = ((a_re - 1.0) * lr + a_im * li) / den
    q_im = (a_im * lr - (a_re - 1.0) * li) / den
    bb_re = q_re[..., None] * b_re - q_im[..., None] * b_im
    bb_im = q_re[..., None] * b_im + q_im[..., None] * b_re
    eye = jnp.eye(8, dtype=F32)
    bb = jnp.stack([bb_re, bb_im], axis=1).reshape(2, 2, S5_JB, 8, S5_STATE, S5_GROUP)
    bd = jnp.einsum('drjgpc,gh->djgcrhp', bb, eye).reshape(2, S5_JB, 128, S5_BLK).astype(BF16)
    cc = jnp.stack([c_re, -c_im], axis=1).astype(F32).reshape(2, 2, S5_JB, 8, S5_GROUP, S5_STATE)
    cd = jnp.einsum('drjgcp,gh->djrgphc', cc, eye).reshape(2, S5_JB, S5_BLK, 128).astype(BF16)
    acoef = jnp.stack([a_re, a_im], axis=1).reshape(2, 2, S5_JB, 512)
    return bd, cd, acoef


def _s5_state_to_blocks(st):
    b = st.shape[0]
    st = st.astype(F32).reshape(b, 2, 2, S5_JB, 8, S5_STATE)
    return jnp.transpose(st, (1, 0, 3, 2, 4, 5)).reshape(2, b, S5_JB * S5_BLK)


def _s5_blocks_to_state(fin):
    b = fin.shape[1]
    fin = fin.reshape(2, b, S5_JB, 2, 8, S5_STATE)
    return jnp.transpose(fin, (1, 0, 3, 2, 4, 5)).reshape(b, 2, 2, S5_GROUPS, S5_STATE)


def _diff_attn_kernel(lam_init, q_ref, k_ref, v_ref, lv_ref, o_ref):
    lv = lv_ref[...]
    lam = (jnp.exp(jnp.sum(lv[0:1] * lv[1:2], axis=-1, keepdims=True))
           - jnp.exp(jnp.sum(lv[2:3] * lv[3:4], axis=-1, keepdims=True)) + lam_init)
    scale = DIFF_DK ** -0.5
    lane = lax.broadcasted_iota(jnp.int32, (1, 128), 1)
    outs = []
    for h in range(DIFF_HEADS):
        probs = []
        for m in range(2):
            j = m * 2 + h // 2
            keep = (lane < 64) if h % 2 == 0 else (lane >= 64)
            qb = jnp.where(keep, q_ref[0, :, 128 * j:128 * (j + 1)], jnp.zeros((), BF16))
            s = _dot_nt(qb, k_ref[0, :, 128 * j:128 * (j + 1)]) * scale
            e = jnp.exp(s - jnp.max(s, axis=-1, keepdims=True))
            probs.append(e / jnp.sum(e, axis=-1, keepdims=True))
        a = probs[0] - lam * probs[1]
        outs.append(_dot(a.astype(BF16), v_ref[0, :, 128 * h:128 * (h + 1)]))
    o_ref[0] = jnp.concatenate(outs, axis=1)


def _diff_attention(q, k, v, lv, lam_init):
    b, lq, _ = q.shape
    lk = k.shape[1]
    tq = TOKEN_TILE
    return pl.pallas_call(
        functools.partial(_diff_attn_kernel, lam_init),
        grid=(b, lq // tq),
        in_specs=[pl.BlockSpec((1, tq, 512), lambda bi, i: (bi, i, 0)),
                  pl.BlockSpec((1, lk, 512), lambda bi, i: (bi, 0, 0)),
                  pl.BlockSpec((1, lk, 512), lambda bi, i: (bi, 0, 0)),
                  pl.BlockSpec((4, DIFF_DK), lambda bi, i: (0, 0))],
        out_specs=pl.BlockSpec((1, tq, 512), lambda bi, i: (bi, i, 0)),
        out_shape=jax.ShapeDtypeStruct((b, lq, 512), F32),
        compiler_params=_params("arbitrary", "arbitrary"),
        name="diff_attention",
    )(q, k, v, lv)


def _win_attn_kernel(local, seq_len, *refs):
    if local:
        q_ref, kc_ref, vc_ref, kl_ref, vl_ref, sink_ref, o_ref = refs
    else:
        q_ref, kc_ref, vc_ref, sink_ref, o_ref = refs
    tq = q_ref.shape[1]
    scale = WIN_DH ** -0.5
    lane = lax.broadcasted_iota(jnp.int32, (1, 128), 1)
    low = lane < 64
    kc = kc_ref[0]
    vc = vc_ref[0]
    n_ctx = kc.shape[0]
    if local:
        span = tq + 2 * WINDOW
        start = pl.multiple_of(pl.program_id(1) * tq, tq)
        kl = kl_ref[0, pl.ds(start, span), :]
        vl = vl_ref[0, pl.ds(start, span), :]
        qpos = start + lax.broadcasted_iota(jnp.int32, (tq, span), 0)
        kpos = start - WINDOW + lax.broadcasted_iota(jnp.int32, (tq, span), 1)
        ok = (jnp.abs(qpos - kpos) <= WINDOW) & (kpos >= 0) & (kpos < seq_len)
    heads = {}
    for n in range(WIN_KV_HEADS):
        keep = low if n == 0 else jnp.logical_not(low)
        for g in range(WIN_GROUP):
            hd = n * WIN_GROUP + g
            j, half = hd // 2, hd % 2
            qb = q_ref[0, :, 128 * j:128 * (j + 1)]
            if half != n:
                qb = jnp.concatenate([qb[:, 64:128], qb[:, 0:64]], axis=1)
            qb = jnp.where(keep, qb, jnp.zeros((), BF16))
            sink = sink_ref[hd:hd + 1, 0:1]
            s = _dot_nt(qb, kc) * scale
            if local:
                sl = _dot_nt(qb, kl) * scale
                s = jnp.concatenate([s, jnp.where(ok, sl, NEG_INF)], axis=1)
            mx = jnp.maximum(jnp.max(s, axis=-1, keepdims=True), sink)
            e = jnp.exp(s - mx)
            den = jnp.sum(e, axis=-1, keepdims=True) + jnp.exp(sink - mx)
            p = (e / den).astype(BF16)
            o = _dot(p[:, 0:n_ctx], vc)
            if local:
                o = o + _dot(p[:, n_ctx:], vl)
            heads[hd] = o
    blocks = []
    for j in range(4):
        ev, od = heads[2 * j], heads[2 * j + 1]
        n = (2 * j) // WIN_GROUP
        if n == 1:
            ev = pltpu.roll(ev, 64, axis=1)
        else:
            od = pltpu.roll(od, 64, axis=1)
        blocks.append(jnp.where(low, ev, od))
    o_ref[0] = jnp.concatenate(blocks, axis=1)


def _win_attention(q, kc, vc, sink, k_lat=None, v_lat=None):
    b, lq, _ = q.shape
    n_ctx = kc.shape[1]
    local = k_lat is not None
    tq = TOKEN_TILE
    in_specs = [pl.BlockSpec((1, tq, 512), lambda bi, i: (bi, i, 0)),
                pl.BlockSpec((1, n_ctx, 128), lambda bi, i: (bi, 0, 0)),
                pl.BlockSpec((1, n_ctx, 128), lambda bi, i: (bi, 0, 0))]
    args = [q, kc, vc]
    if local:
        lp = k_lat.shape[1]
        in_specs += [pl.BlockSpec((1, lp, 128), lambda bi, i: (bi, 0, 0))] * 2
        args += [k_lat, v_lat]
    in_specs.append(pl.BlockSpec((8, 128), lambda bi, i: (0, 0)))
    args.append(sink)
    return pl.pallas_call(
        functools.partial(_win_attn_kernel, local, lq),
        grid=(b, lq // tq), in_specs=in_specs,
        out_specs=pl.BlockSpec((1, tq, 512), lambda bi, i: (bi, i, 0)),
        out_shape=jax.ShapeDtypeStruct((b, lq, 512), F32),
        compiler_params=_params("arbitrary", "arbitrary"),
        name="win_attention_local" if local else "win_attention",
    )(*args)


def _pool_kernel(u_ref, w_ref, scale_ref, o_ref):
    l = u_ref.shape[1]
    lp = l + 2 * POOL_PAD
    t = lax.broadcasted_iota(jnp.int32, (l, 128), 0)
    zpad = jnp.zeros((POOL_PAD, 128), F32)

    def shifted(a, s):
        return pltpu.roll(a, s, axis=0) + pltpu.roll(a, lp - s, axis=0)

    outs = []
    for gi, wl in enumerate(POOL_WINDOWS):
        x = u_ref[0, :, 128 * gi:128 * (gi + 1)]
        xp = jnp.concatenate([zpad, x, zpad], axis=0)
        acc = xp + pltpu.roll(xp, 1, axis=0)
        if wl >= 4:
            acc = shifted(acc, 1)
        if wl >= 8:
            acc = shifted(acc, 2)
        if wl >= 16:
            acc = shifted(acc, 4)
        win = acc[POOL_PAD:POOL_PAD + l]
        cnt = (jnp.minimum(t + wl // 2, l) - jnp.maximum(t - wl // 2, 0)).astype(F32)
        pooled = win / cnt - x
        outs.append(_dot(pooled.astype(BF16), w_ref[gi]))
    o_ref[0] = jnp.concatenate(outs, axis=1) * scale_ref[...]


def _pool_mixer(u, w, scale):
    b, l, _ = u.shape
    return pl.pallas_call(
        _pool_kernel,
        grid=(b,),
        in_specs=[pl.BlockSpec((1, l, 512), lambda bi: (bi, 0, 0)),
                  pl.BlockSpec(w.shape, lambda bi: (0, 0, 0)),
                  pl.BlockSpec((1, 512), lambda bi: (0, 0))],
        out_specs=pl.BlockSpec((1, l, 512), lambda bi: (bi, 0, 0)),
        out_shape=jax.ShapeDtypeStruct((b, l, 512), F32),
        compiler_params=_params("arbitrary"),
        name="pool_mixer",
    )(u, w, scale)


def _mixout_even_kernel(out_scale, x_ref, mod_ref, u_ref, yf_ref, yb_ref, oa_ref, sd_ref,
                        wglu_ref, gsub_ref, wout_ref, o_ref):
    mod = mod_ref[0]
    y = sd_ref[...] * u_ref[...] + yf_ref[...] + yb_ref[...]
    g = _gelu(y)
    o_a = g * _sigmoid(_dot(g.astype(BF16), wglu_ref[...]))
    parts = [o_a.astype(BF16)]
    for h in range(DIFF_HEADS):
        blk = oa_ref[0, :, 128 * h:128 * (h + 1)]
        ms = jnp.mean(blk * blk, axis=-1, keepdims=True)
        parts.append(((blk * lax.rsqrt(ms + EPS) * gsub_ref[...]) * out_scale).astype(BF16))
    cat = jnp.concatenate(parts, axis=1)
    o_ref[0] = x_ref[0] + mod[2:3] * _dot(cat, wout_ref[...])


def _mixout_even(x, mod, u_tm2, yf2, yb2, o_attn, s5_d, w_glu, g_sub, w_out, out_scale, per_batch_mod):
    b, l, d = x.shape
    tl = TOKEN_TILE
    mod_map = (lambda bi, i: (bi, 0, 0)) if per_batch_mod else (lambda bi, i: (0, 0, 0))
    const2 = lambda bi, i: (0, 0)
    tm = pl.BlockSpec((tl, 512), lambda bi, i: (i, bi))
    return pl.pallas_call(
        functools.partial(_mixout_even_kernel, out_scale),
        grid=(b, l // tl),
        in_specs=[pl.BlockSpec((1, tl, d), lambda bi, i: (bi, i, 0)),
                  pl.BlockSpec((1, 6, d), mod_map),
                  tm, tm, tm,
                  pl.BlockSpec((1, tl, 512), lambda bi, i: (bi, i, 0)),
                  pl.BlockSpec((1, 512), const2),
                  pl.BlockSpec((512, 512), const2),
                  pl.BlockSpec((1, 128), const2),
                  pl.BlockSpec((d, d), const2)],
        out_specs=pl.BlockSpec((1, tl, d), lambda bi, i: (bi, i, 0)),
        out_shape=jax.ShapeDtypeStruct((b, l, d), F32),
        compiler_params=_params("arbitrary", "arbitrary"),
        name="mixout_even",
    )(x, mod, u_tm2, yf2, yb2, o_attn, s5_d, w_glu, g_sub, w_out)


def _mixout_odd_kernel(x_ref, mod_ref, oc_ref, od_ref, wout_ref, o_ref):
    mod = mod_ref[0]
    cat = jnp.concatenate([oc_ref[0].astype(BF16), od_ref[0].astype(BF16)], axis=1)
    o_ref[0] = x_ref[0] + mod[2:3] * _dot(cat, wout_ref[...])


def _mixout_odd(x, mod, o_c, o_d, w_out, per_batch_mod):
    b, l, d = x.shape
    tl = TOKEN_TILE
    mod_map = (lambda bi, i: (bi, 0, 0)) if per_batch_mod else (lambda bi, i: (0, 0, 0))
    tok = lambda w_: pl.BlockSpec((1, tl, w_), lambda bi, i: (bi, i, 0))
    return pl.pallas_call(
        _mixout_odd_kernel,
        grid=(b, l // tl),
        in_specs=[tok(d), pl.BlockSpec((1, 6, d), mod_map), tok(512), tok(512),
                  pl.BlockSpec((d, d), lambda bi, i: (0, 0))],
        out_specs=tok(d),
        out_shape=jax.ShapeDtypeStruct((b, l, d), F32),
        compiler_params=_params("arbitrary", "arbitrary"),
        name="mixout_odd",
    )(x, mod, o_c, o_d, w_out)


def _rank16(s, kio):
    n = s.shape[0]
    rank = jnp.full(s.shape, float(PEER_TOPK), F32)
    vals = []
    for r in range(PEER_TOPK):
        m = jnp.max(s, axis=0, keepdims=True)
        ix = jnp.min(jnp.where(s == m, kio, float(n)), axis=0, keepdims=True)
        hit = kio == ix
        rank = jnp.where(hit, float(r), rank)
        s = jnp.where(hit, -jnp.inf, s)
        vals.append(m)
    return vals, rank


def _candidate_positions(tl):
    i = lax.broadcasted_iota(jnp.int32, (PEER_CAND_ROWS, tl), 0).astype(F32)
    mid = i + 8.0 * jnp.floor((i - 16.0) * 0.125)
    return jnp.where(i < 16.0, i, jnp.where(i < 72.0, mid, 16.0 * (i - 64.0)))


def _peer_topk_kernel(x_ref, mod_ref, g_ref, wq_ref, keys_ref, h_ref, n_ref, p1_ref, r2_ref, p2_ref):
    mod = mod_ref[0]
    h = _modulate(x_ref[0], g_ref[...], mod[3:4], mod[4:5]).astype(BF16)
    h_ref[0] = h
    q = _dot(h, wq_ref[...])
    tl = q.shape[0]
    kio = lax.broadcasted_iota(jnp.int32, (PEER_NKEYS, tl), 0).astype(F32)
    pos = _candidate_positions(tl)
    for hd in range(PEER_HEADS):
        s1 = _dot_nt(keys_ref[2 * hd], q[:, 256 * hd:256 * hd + 128].astype(BF16))
        s2 = _dot_nt(keys_ref[2 * hd + 1], q[:, 256 * hd + 128:256 * hd + 256].astype(BF16))
        v1, rank1 = _rank16(s1, kio)
        v2, rank2 = _rank16(s2, kio)
        v1s = jnp.concatenate(v1, axis=0)
        v2s = jnp.concatenate(v2, axis=0)
        cand = jnp.concatenate([v1[0] + v2s] + [v1[a] + v2s[0:8] for a in range(1, 8)]
                               + [v1s[8:16] + v2[0]], axis=0)
        sel_a = []
        for r in range(PEER_TOPK):
            m = jnp.max(cand, axis=0, keepdims=True)
            px = jnp.min(jnp.where(cand == m, pos, 999.0), axis=0, keepdims=True)
            cand = jnp.where(pos == px, -jnp.inf, cand)
            sel_a.append(jnp.floor(px * (1.0 / PEER_TOPK)))
            if r == 0:
                best0, z = m, jnp.ones_like(m)
            else:
                z = z + jnp.exp(m - best0)
        n_e1 = jnp.zeros((PEER_NKEYS, tl), F32)
        for a in range(PEER_TOPK):
            n_a = jnp.zeros_like(z)
            for r in range(a, PEER_TOPK):
                n_a = n_a + jnp.where(sel_a[r] == float(a), 1.0, 0.0)
            n_e1 = jnp.where(rank1 == float(a), n_a, n_e1)
        n_ref[0, hd] = n_e1
        p1_ref[0, hd] = jnp.exp(s1 - v1[0]) / z
        r2_ref[0, hd] = rank2.astype(BF16)
        p2_ref[0, hd] = jnp.exp(s2 - v2[0]).astype(BF16)


def _peer_topk(x, mod, gain, w_q, keys, per_batch_mod):
    b, l, d = x.shape
    tl = TOKEN_TILE
    mod_map = (lambda bi, i: (bi, 0, 0)) if per_batch_mod else (lambda bi, i: (0, 0, 0))
    sel = pl.BlockSpec((1, PEER_HEADS, PEER_NKEYS, tl), lambda bi, i: (bi, 0, 0, i))
    sel_f32 = jax.ShapeDtypeStruct((b, PEER_HEADS, PEER_NKEYS, l), F32)
    sel_bf16 = jax.ShapeDtypeStruct((b, PEER_HEADS, PEER_NKEYS, l), BF16)
    return pl.pallas_call(
        _peer_topk_kernel,
        grid=(b, l // tl),
        in_specs=[pl.BlockSpec((1, tl, d), lambda bi, i: (bi, i, 0)),
                  pl.BlockSpec((1, 6, d), mod_map),
                  pl.BlockSpec((1, d), lambda bi, i: (0, 0)),
                  pl.BlockSpec(w_q.shape, lambda bi, i: (0, 0)),
                  pl.BlockSpec(keys.shape, lambda bi, i: (0, 0, 0))],
        out_specs=[pl.BlockSpec((1, tl, d), lambda bi, i: (bi, i, 0)), sel, sel, sel, sel],
        out_shape=[jax.ShapeDtypeStruct((b, l, d), BF16), sel_f32, sel_f32, sel_bf16, sel_bf16],
        compiler_params=_params("arbitrary", "arbitrary"),
        name="peer_topk",
    )(x, mod, gain, w_q, keys)


def _peer_dense_kernel(x_ref, mod_ref, h_ref, u_ref, vt_ref, n_ref, p1_ref, r2_ref, p2_ref, o_ref,
                       ht_ref, acc_ref):
    e = pl.program_id(2)
    rows_per_sub = PEER_SUB // PEER_NKEYS

    @pl.when(e == 0)
    def _():
        ht_ref[...] = h_ref[0].astype(F32).T.astype(BF16)
        acc_ref[...] = jnp.zeros_like(acc_ref)

    ht = ht_ref[...]
    tm = ht.shape[1]
    total = None
    for sb in range(PEER_STEP // PEER_SUB):
        act = _gelu(_dot(u_ref[PEER_SUB * sb:PEER_SUB * (sb + 1), :], ht)).astype(BF16)
        ws = []
        for r in range(rows_per_sub):
            row = sb * rows_per_sub + r
            gate = [None] * (PEER_NKEYS // 16)
            for hd in range(PEER_HEADS):
                n_b = jnp.broadcast_to(n_ref[0, hd, row:row + 1, :], (16, tm)).astype(BF16)
                p1_b = jnp.broadcast_to(p1_ref[0, hd, row:row + 1, :], (16, tm)).astype(BF16)
                for k in range(PEER_NKEYS // 16):
                    term = jnp.where(r2_ref[0, hd, 16 * k:16 * (k + 1), :] < n_b,
                                     p2_ref[0, hd, 16 * k:16 * (k + 1), :], jnp.zeros((), BF16)) * p1_b
                    gate[k] = term if gate[k] is None else gate[k] + term
            base = PEER_NKEYS * r
            ws += [gate[k] * act[base + 16 * k:base + 16 * (k + 1)] for k in range(PEER_NKEYS // 16)]
        part = _dot(vt_ref[:, PEER_SUB * sb:PEER_SUB * (sb + 1)], jnp.concatenate(ws, axis=0))
        total = part if total is None else total + part
    acc_ref[...] += total

    @pl.when(e == pl.num_programs(2) - 1)
    def _():
        o_ref[0] = x_ref[0] + mod_ref[0][5:6] * acc_ref[...].T


def _peer_dense(x, mod, h, u_tab, vt_tab, n_e1, p1n, rank2, p2, per_batch_mod):
    b, l, d = x.shape
    tm = PEER_TOKENS
    n_exp = u_tab.shape[0]
    rows = PEER_STEP // PEER_NKEYS
    mod_map = (lambda bi, i, e: (bi, 0, 0)) if per_batch_mod else (lambda bi, i, e: (0, 0, 0))
    tok = lambda: pl.BlockSpec((1, tm, d), lambda bi, i, e: (bi, i, 0))
    by_row = pl.BlockSpec((1, PEER_HEADS, rows, tm), lambda bi, i, e: (bi, 0, e, i))
    by_key = pl.BlockSpec((1, PEER_HEADS, PEER_NKEYS, tm), lambda bi, i, e: (bi, 0, 0, i))
    return pl.pallas_call(
        _peer_dense_kernel,
        grid=(b, l // tm, n_exp // PEER_STEP),
        in_specs=[tok(), pl.BlockSpec((1, 6, d), mod_map), tok(),
                  pl.BlockSpec((PEER_STEP, d), lambda bi, i, e: (e, 0)),
                  pl.BlockSpec((d, PEER_STEP), lambda bi, i, e: (0, e)),
                  by_row, by_row, by_key, by_key],
        out_specs=tok(),
        out_shape=jax.ShapeDtypeStruct((b, l, d), F32),
        scratch_shapes=[pltpu.VMEM((d, tm), BF16), pltpu.VMEM((d, tm), F32)],
        compiler_params=_params("arbitrary", "arbitrary", "arbitrary"),
        name="peer_dense",
    )(x, mod, h, u_tab, vt_tab, n_e1, p1n, rank2, p2)


def _rope_tables(n_lat, width):
    q4 = DIFF_DK // 4
    n_rows = n_lat // GRID_W
    row = jnp.repeat(jnp.arange(n_rows), GRID_W).astype(F32)
    col = jnp.tile(jnp.arange(GRID_W), n_rows).astype(F32)
    freqs = ROPE_BASE ** (-jnp.arange(q4, dtype=F32) / q4)
    ang = jnp.stack([row[:, None] * freqs, col[:, None] * freqs], axis=1)
    cos, sin = jnp.cos(ang), jnp.sin(ang)
    cos64 = jnp.stack([cos, cos], axis=2).reshape(n_lat, 64)
    sin64 = jnp.stack([-sin, sin], axis=2).reshape(n_lat, 64)
    reps = width // 64
    return jnp.tile(cos64, (1, reps)), jnp.tile(sin64, (1, reps))


def _ones_block_diag(width, group):
    idx = jnp.arange(width) // group
    return jnp.where(idx[:, None] == idx[None, :], 1.0 / group, 0.0).astype(BF16)


def _peer_layer(x, mod, W, l, per_batch_mod):
    shape = x.shape
    if not per_batch_mod:
        x = x.reshape(1, -1, shape[-1])
    h, n_e1, p1n, rank2, p2 = _peer_topk(x, mod, W['norm_ffn'][l], W['peer_w_q'][l], W['peer_keys'][l],
                                         per_batch_mod)
    y = _peer_dense(x, mod, h, W['peer_u'][l], W['peer_vt'][l], n_e1, p1n, rank2, p2, per_batch_mod)
    return y.reshape(shape)


def _even_layer(x, mod, W, l, cache, per_batch_mod):
    b, seq, _ = x.shape
    i = l // 2
    lam_init = 0.8 - 0.6 * math.exp(-0.3 * l)
    rope_tabs = W['rope'] if cache is not None else None
    u_tm2, q, k_norm, k_att, v, v_att = _inproj_even(
        x, mod, W['norm_mix'][l], W['even_w_in'][i], W['ones_bd'], W['diff_q_norm'][i],
        W['diff_k_norm'][i], rope_tabs, per_batch_mod)
    if cache is None:
        h0 = jnp.zeros((2, b, S5_JB * S5_BLK), F32)
        k_all, v_all = k_att, v_att
    else:
        h0 = _s5_state_to_blocks(cache['s5'][:, i])
        k_all = jnp.concatenate([cache['diff_k'][:, i].reshape(b, -1, 512).astype(BF16), k_att], axis=1)
        v_all = jnp.concatenate([cache['diff_v'][:, i].reshape(b, -1, 512).astype(BF16), v_att], axis=1)
    yf, yb, fin = _s5_scan(u_tm2.reshape(seq, b, 512), W['s5_bd'][i], W['s5_cd'][i], W['s5_a'][i], h0)
    o_attn = _diff_attention(q, k_all, v_all, W['diff_lambda'][i], lam_init)
    x = _mixout_even(x, mod, u_tm2, yf.reshape(seq, b * 512), yb.reshape(seq, b * 512), o_attn,
                     W['s5_d'][i], W['s5_w_glu'][i], W['diff_sub_norm'][i], W['even_w_out'][i],
                     1.0 - lam_init, per_batch_mod)
    return x, (k_norm, v, fin)


def _odd_layer(x, mod, W, l, cache, per_batch_mod):
    b, seq, _ = x.shape
    i = l // 2
    rope_tabs = W['rope'] if cache is not None else None
    q, k_norm, k_att, v, v_att, u = _inproj_odd(
        x, mod, W['norm_mix'][l], W['odd_w_in'][i], W['ones_bd'], W['win_q_norm'][i],
        W['win_k_norm'][i], rope_tabs, per_batch_mod)
    if cache is None:
        o_c = _win_attention(q, k_att, v_att, W['win_sink'][i])
    else:
        pad = ((0, 0), (WINDOW, WINDOW), (0, 0))
        o_c = _win_attention(q, cache['win_k'][:, i].reshape(b, -1, 128).astype(BF16),
                             cache['win_v'][:, i].reshape(b, -1, 128).astype(BF16), W['win_sink'][i],
                             jnp.pad(k_att, pad), jnp.pad(v_att, pad))
    o_d = _pool_mixer(u, W['pool_w'][i], W['pool_scale'][i])
    x = _mixout_odd(x, mod, o_c, o_d, W['odd_w_out'][i], per_batch_mod)
    return x, (k_norm, v)


def kernel(x_prompt, x_sample, cache_diff_k, cache_diff_v, state_s5, cache_win_k, cache_win_v, c, c_ctx, ada_w, ada_b, norm_mix, norm_ffn, even_w_in, even_w_out, s5_lam_re, s5_lam_im, s5_log_step, s5_b_re, s5_b_im, s5_c_re, s5_c_im, s5_d, s5_w_glu, diff_q_norm, diff_k_norm, diff_lambda, diff_sub_norm, odd_w_in, odd_w_out, win_q_norm, win_k_norm, win_sink, pool_w, pool_scale, peer_w_q, peer_sub_keys, peer_u, peer_v):
    depth = ada_w.shape[0]
    n_even, n_odd = even_w_in.shape[0], odd_w_in.shape[0]
    bsz, seq, d = x_prompt.shape
    dec_b, dec_l, _ = x_sample.shape

    cvec = jnp.zeros((16, d), F32).at[0].set(c_ctx).at[1:1 + dec_b].set(c)
    mod = _modulation(cvec, ada_w, ada_b)

    s5 = [_s5_tables(s5_lam_re[i], s5_lam_im[i], s5_log_step[i], s5_b_re[i], s5_b_im[i],
                     s5_c_re[i], s5_c_im[i]) for i in range(n_even)]
    W = dict(
        norm_mix=norm_mix.reshape(depth, 1, d), norm_ffn=norm_ffn.reshape(depth, 1, d),
        even_w_in=even_w_in.astype(BF16), even_w_out=even_w_out.astype(BF16),
        odd_w_in=odd_w_in.astype(BF16), odd_w_out=odd_w_out.astype(BF16),
        ones_bd=_ones_block_diag(512, 64),
        diff_q_norm=jnp.tile(diff_q_norm, (1, 8)).reshape(n_even, 1, 512),
        diff_k_norm=jnp.tile(diff_k_norm, (1, 8)).reshape(n_even, 1, 512),
        diff_lambda=diff_lambda, diff_sub_norm=diff_sub_norm.reshape(n_even, 1, DIFF_DV),
        s5_bd=[t[0] for t in s5], s5_cd=[t[1] for t in s5], s5_a=[t[2] for t in s5],
        s5_d=s5_d.reshape(n_even, 1, 512), s5_w_glu=s5_w_glu.astype(BF16),
        win_q_norm=jnp.tile(win_q_norm, (1, 8)).reshape(n_odd, 1, 512),
        win_k_norm=jnp.tile(win_k_norm, (1, 2)).reshape(n_odd, 1, 128),
        win_sink=jnp.broadcast_to(win_sink[:, :, None], (n_odd, 8, 128)),
        pool_w=pool_w.astype(BF16), pool_scale=pool_scale.reshape(n_odd, 1, 512),
        peer_w_q=peer_w_q.astype(BF16),
        peer_keys=peer_sub_keys.astype(BF16).reshape(depth, 2 * PEER_HEADS, PEER_NKEYS, -1),
        peer_u=peer_u.astype(BF16),
        peer_vt=jnp.swapaxes(peer_v, 1, 2).astype(BF16),
        rope=_rope_tables(dec_l, 512),
    )
    cache = {'diff_k': cache_diff_k, 'diff_v': cache_diff_v, 's5': state_s5,
             'win_k': cache_win_k, 'win_v': cache_win_v}

    def run(x, mods, cch, per_batch_mod):
        states = []
        for l in range(depth):
            if l % 2 == 0:
                x, st = _even_layer(x, mods[l], W, l, cch, per_batch_mod)
            else:
                x, st = _odd_layer(x, mods[l], W, l, cch, per_batch_mod)
            states.append(st)
            x = _peer_layer(x, mods[l], W, l, per_batch_mod)
        return x, states

    ctx_mods = [mod[l, 0:1].reshape(1, 6, d) for l in range(depth)]
    dec_mods = [mod[l, 1:1 + dec_b].reshape(dec_b, 6, d) for l in range(depth)]
    y_prompt, states = run(x_prompt, ctx_mods, None, False)
    y_sample, _ = run(x_sample, dec_mods, cache, True)

    evens = [states[l] for l in range(depth) if l % 2 == 0]
    odds = [states[l] for l in range(depth) if l % 2 == 1]
    new_diff_k = jnp.stack([s[0].reshape(bsz, seq, 2, DIFF_HEADS, DIFF_DK) for s in evens], axis=1)
    new_diff_v = jnp.stack([s[1].reshape(bsz, seq, DIFF_HEADS, DIFF_DV) for s in evens], axis=1)
    new_s5 = jnp.stack([_s5_blocks_to_state(s[2]) for s in evens], axis=1)
    new_win_k = jnp.stack([s[0].reshape(bsz, seq, WIN_KV_HEADS, WIN_DH) for s in odds], axis=1)
    new_win_v = jnp.stack([s[1].reshape(bsz, seq, WIN_KV_HEADS, WIN_DH) for s in odds], axis=1)
    return (y_prompt, y_sample, new_diff_k, new_diff_v, new_s5, new_win_k, new_win_v)
```

```python
import functools
import math

import jax
import jax.numpy as jnp
from jax import lax
from jax.experimental import pallas as pl
from jax.experimental.pallas import tpu as pltpu

F32 = jnp.float32
BF16 = jnp.bfloat16

D_MODEL = 1024
MIX_HALF = 512
GRID_W = 64
EPS = 1e-6
NEG_INF = -1e30
ROPE_BASE = 10000.0

S5_GROUPS = 32
S5_GROUP = 16
S5_STATE = 64
S5_JB = 4
S5_BLK = 1024
S5_CHUNK = 32
S5_BATCH = 8

DIFF_HEADS = 4
DIFF_DK = 64
DIFF_DV = 128
WIN_KV_HEADS = 2
WIN_GROUP = 4
WIN_DH = 64
WINDOW = 128
POOL_WINDOWS = (2, 4, 8, 16)
POOL_PAD = 8

PEER_HEADS = 8
PEER_NKEYS = 128
PEER_TOPK = 16
PEER_CAND_ROWS = 80
PEER_SUB = 512
PEER_STEP = 2048
PEER_TOKENS = 512

TOKEN_TILE = 256
VMEM_LIMIT = 56 * 1024 * 1024


def _params(*sem):
    return pltpu.CompilerParams(dimension_semantics=sem, vmem_limit_bytes=VMEM_LIMIT)


def _dot(a, b):
    return jnp.dot(a, b, preferred_element_type=F32)


def _dot_nt(a, b):
    return lax.dot_general(a, b, (((1,), (1,)), ((), ())), preferred_element_type=F32)


def _sigmoid(x):
    return 1.0 / (1.0 + jnp.exp(-x))


def _gelu(x):
    c = math.sqrt(2.0 / math.pi)
    hx = 0.5 * x
    return hx + hx * jnp.tanh(x * (c + (c * 0.044715) * (x * x)))


def _modulate(x, gain, shift, scale):
    ms = jnp.mean(x * x, axis=-1, keepdims=True)
    return (x * lax.rsqrt(ms + EPS) * gain) * (1.0 + scale) + shift


def _group_rms(z, ones_bd, gain):
    ms = _dot((z * z).astype(BF16), ones_bd)
    return z * lax.rsqrt(ms + EPS) * gain


def _rope(z, cos_t, sin_t):
    lane = lax.broadcasted_iota(jnp.int32, (1, 128), 1) % 32
    first = lane < 16
    parts = []
    for j in range(z.shape[1] // 128):
        c = z[:, 128 * j:128 * (j + 1)]
        parts.append(jnp.where(first, pltpu.roll(c, 112, axis=1), pltpu.roll(c, 16, axis=1)))
    swapped = parts[0] if len(parts) == 1 else jnp.concatenate(parts, axis=1)
    return z * cos_t + swapped * sin_t


def _mod_kernel(c_ref, w_ref, b_ref, o_ref):
    c = c_ref[...]
    s = c * _sigmoid(c)
    o_ref[0] = jnp.dot(s, w_ref[0], precision=lax.Precision.HIGHEST,
                       preferred_element_type=F32) + b_ref[0]


def _modulation(cvec, ada_w, ada_b):
    depth, d, n = ada_w.shape
    tn = 1536
    return pl.pallas_call(
        _mod_kernel,
        grid=(depth, n // tn),
        in_specs=[pl.BlockSpec((16, d), lambda l, j: (0, 0)),
                  pl.BlockSpec((1, d, tn), lambda l, j: (l, 0, j)),
                  pl.BlockSpec((1, 1, tn), lambda l, j: (l, 0, j))],
        out_specs=pl.BlockSpec((1, 16, tn), lambda l, j: (l, 0, j)),
        out_shape=jax.ShapeDtypeStruct((depth, 16, n), F32),
        compiler_params=_params("arbitrary", "arbitrary"),
        name="adaln_mod",
    )(cvec, ada_w, ada_b.reshape(depth, 1, n))


def _inproj_even_kernel(rope, *refs):
    if rope:
        (x_ref, mod_ref, g_ref, w_ref, bd_ref, gq_ref, gk_ref, cos_ref, sin_ref,
         u_ref, q_ref, kn_ref, ka_ref, v_ref, va_ref) = refs
    else:
        (x_ref, mod_ref, g_ref, w_ref, bd_ref, gq_ref, gk_ref,
         u_ref, q_ref, kn_ref, ka_ref, v_ref, va_ref) = refs
    mod = mod_ref[0]
    h = _modulate(x_ref[0], g_ref[...], mod[0:1], mod[1:2])
    z = _dot(h.astype(BF16), w_ref[...])
    u = z[:, 0:512]
    q = _group_rms(z[:, 512:1024], bd_ref[...], gq_ref[...])
    k = _group_rms(z[:, 1024:1536], bd_ref[...], gk_ref[...])
    v = z[:, 1536:2048]
    u_ref[...] = u
    kn_ref[0] = k
    v_ref[0] = v
    va_ref[0] = v.astype(BF16)
    if rope:
        q = _rope(q, cos_ref[...], sin_ref[...])
        k = _rope(k, cos_ref[...], sin_ref[...])
    q_ref[0] = q.astype(BF16)
    ka_ref[0] = k.astype(BF16)


def _inproj_even(x, mod, gain, w, ones_bd, gq, gk, rope_tabs, per_batch_mod):
    b, l, d = x.shape
    tl = TOKEN_TILE
    rope = rope_tabs is not None
    mod_map = (lambda bi, i: (bi, 0, 0)) if per_batch_mod else (lambda bi, i: (0, 0, 0))
    const2 = lambda bi, i: (0, 0)
    tok = lambda w_: pl.BlockSpec((1, tl, w_), lambda bi, i: (bi, i, 0))
    in_specs = [tok(d),
                pl.BlockSpec((1, 6, d), mod_map),
                pl.BlockSpec((1, d), const2),
                pl.BlockSpec(w.shape, const2),
                pl.BlockSpec(ones_bd.shape, const2),
                pl.BlockSpec((1, 512), const2),
                pl.BlockSpec((1, 512), const2)]
    args = [x, mod, gain, w, ones_bd, gq, gk]
    if rope:
        in_specs += [pl.BlockSpec((tl, 512), lambda bi, i: (i, 0))] * 2
        args += list(rope_tabs)
    out_shape = [jax.ShapeDtypeStruct((l, b * 512), F32),
                 jax.ShapeDtypeStruct((b, l, 512), BF16),
                 jax.ShapeDtypeStruct((b, l, 512), F32),
                 jax.ShapeDtypeStruct((b, l, 512), BF16),
                 jax.ShapeDtypeStruct((b, l, 512), F32),
                 jax.ShapeDtypeStruct((b, l, 512), BF16)]
    out_specs = [pl.BlockSpec((tl, 512), lambda bi, i: (i, bi)),
                 tok(512), tok(512), tok(512), tok(512), tok(512)]
    return pl.pallas_call(
        functools.partial(_inproj_even_kernel, rope),
        grid=(b, l // tl), in_specs=in_specs, out_specs=out_specs, out_shape=out_shape,
        compiler_params=_params("arbitrary", "arbitrary"),
        name="inproj_even_rope" if rope else "inproj_even",
    )(*args)


def _inproj_odd_kernel(rope, *refs):
    if rope:
        (x_ref, mod_ref, g_ref, w_ref, bd_ref, gq_ref, gk_ref, cos_ref, sin_ref,
         q_ref, kn_ref, ka_ref, v_ref, va_ref, u_ref) = refs
    else:
        (x_ref, mod_ref, g_ref, w_ref, bd_ref, gq_ref, gk_ref,
         q_ref, kn_ref, ka_ref, v_ref, va_ref, u_ref) = refs
    mod = mod_ref[0]
    h = _modulate(x_ref[0], g_ref[...], mod[0:1], mod[1:2])
    z = _dot(h.astype(BF16), w_ref[...])
    bd = bd_ref[...]
    q = _group_rms(z[:, 0:512], bd, gq_ref[...])
    k = _group_rms(z[:, 512:640], bd[0:128, 0:128], gk_ref[...])
    v = z[:, 640:768]
    kn_ref[0] = k
    v_ref[0] = v
    va_ref[0] = v.astype(BF16)
    u_ref[0] = z[:, 768:1280]
    if rope:
        q = _rope(q, cos_ref[...], sin_ref[...])
        k = _rope(k, cos_ref[:, 0:128], sin_ref[:, 0:128])
    q_ref[0] = q.astype(BF16)
    ka_ref[0] = k.astype(BF16)


def _inproj_odd(x, mod, gain, w, ones_bd, gq, gk, rope_tabs, per_batch_mod):
    b, l, d = x.shape
    tl = TOKEN_TILE
    rope = rope_tabs is not None
    mod_map = (lambda bi, i: (bi, 0, 0)) if per_batch_mod else (lambda bi, i: (0, 0, 0))
    const2 = lambda bi, i: (0, 0)
    tok = lambda w_: pl.BlockSpec((1, tl, w_), lambda bi, i: (bi, i, 0))
    in_specs = [tok(d),
                pl.BlockSpec((1, 6, d), mod_map),
                pl.BlockSpec((1, d), const2),
                pl.BlockSpec(w.shape, const2),
                pl.BlockSpec(ones_bd.shape, const2),
                pl.BlockSpec((1, 512), const2),
                pl.BlockSpec((1, 128), const2)]
    args = [x, mod, gain, w, ones_bd, gq, gk]
    if rope:
        in_specs += [pl.BlockSpec((tl, 512), lambda bi, i: (i, 0))] * 2
        args += list(rope_tabs)
    out_shape = [jax.ShapeDtypeStruct((b, l, 512), BF16),
                 jax.ShapeDtypeStruct((b, l, 128), F32),
                 jax.ShapeDtypeStruct((b, l, 128), BF16),
                 jax.ShapeDtypeStruct((b, l, 128), F32),
                 jax.ShapeDtypeStruct((b, l, 128), BF16),
                 jax.ShapeDtypeStruct((b, l, 512), F32)]
    out_specs = [tok(512), tok(128), tok(128), tok(128), tok(128), tok(512)]
    return pl.pallas_call(
        functools.partial(_inproj_odd_kernel, rope),
        grid=(b, l // tl), in_specs=in_specs, out_specs=out_specs, out_shape=out_shape,
        compiler_params=_params("arbitrary", "arbitrary"),
        name="inproj_odd_rope" if rope else "inproj_odd",
    )(*args)


def _s5_kernel(uf_ref, ub_ref, bd_ref, cd_ref, a_ref, h0_ref, yf_ref, yb_ref, fin_ref,
               bu_ref, carry_ref):
    c = pl.program_id(1)
    rows = S5_CHUNK * S5_BATCH

    @pl.when(c == 0)
    def _():
        carry_ref[...] = h0_ref[...]

    for d in range(2):
        u = (uf_ref if d == 0 else ub_ref)[...].reshape(rows, MIX_HALF).astype(BF16)
        for j in range(S5_JB):
            bu_ref[d, :, S5_BLK * j:S5_BLK * (j + 1)] = _dot(u[:, 128 * j:128 * (j + 1)], bd_ref[d, j])
        for j in range(S5_JB):
            lo = S5_BLK * j
            a_re = jnp.broadcast_to(a_ref[d, 0, j:j + 1, :], (S5_BATCH, 512))
            a_im = jnp.broadcast_to(a_ref[d, 1, j:j + 1, :], (S5_BATCH, 512))

            def step(s, hc, d=d, lo=lo, a_re=a_re, a_im=a_im):
                h_re, h_im = hc
                t = s if d == 0 else S5_CHUNK - 1 - s
                row = pl.multiple_of(t * S5_BATCH, S5_BATCH)
                b_re = bu_ref[d, pl.ds(row, S5_BATCH), lo:lo + 512]
                b_im = bu_ref[d, pl.ds(row, S5_BATCH), lo + 512:lo + 1024]
                n_re = a_re * h_re - a_im * h_im + b_re
                n_im = a_re * h_im + a_im * h_re + b_im
                bu_ref[d, pl.ds(row, S5_BATCH), lo:lo + 512] = n_re
                bu_ref[d, pl.ds(row, S5_BATCH), lo + 512:lo + 1024] = n_im
                return n_re, n_im

            h_re, h_im = lax.fori_loop(
                0, S5_CHUNK, step,
                (carry_ref[d, :, lo:lo + 512], carry_ref[d, :, lo + 512:lo + 1024]), unroll=4)
            carry_ref[d, :, lo:lo + 512] = h_re
            carry_ref[d, :, lo + 512:lo + 1024] = h_im
        ys = [_dot(bu_ref[d, :, S5_BLK * j:S5_BLK * (j + 1)].astype(BF16), cd_ref[d, j])
              for j in range(S5_JB)]
        y = jnp.concatenate(ys, axis=1).reshape(S5_CHUNK, S5_BATCH, MIX_HALF)
        if d == 0:
            yf_ref[...] = y
        else:
            yb_ref[...] = y

    @pl.when(c == pl.num_programs(1) - 1)
    def _():
        fin_ref[...] = carry_ref[...]


def _s5_scan(u_tm, bd, cd, acoef, h0):
    l, b, _ = u_tm.shape
    nc = l // S5_CHUNK
    ng = b // S5_BATCH
    state_w = S5_JB * S5_BLK
    blk = (S5_CHUNK, S5_BATCH, MIX_HALF)
    full = lambda a: pl.BlockSpec(a.shape, lambda g, c: (0,) * a.ndim)
    return pl.pallas_call(
        _s5_kernel,
        grid=(ng, nc),
        in_specs=[pl.BlockSpec(blk, lambda g, c: (c, g, 0)),
                  pl.BlockSpec(blk, lambda g, c: (nc - 1 - c, g, 0)),
                  full(bd), full(cd), full(acoef),
                  pl.BlockSpec((2, S5_BATCH, state_w), lambda g, c: (0, g, 0))],
        out_specs=[pl.BlockSpec(blk, lambda g, c: (c, g, 0)),
                   pl.BlockSpec(blk, lambda g, c: (nc - 1 - c, g, 0)),
                   pl.BlockSpec((2, S5_BATCH, state_w), lambda g, c: (0, g, 0))],
        out_shape=[jax.ShapeDtypeStruct((l, b, MIX_HALF), F32),
                   jax.ShapeDtypeStruct((l, b, MIX_HALF), F32),
                   jax.ShapeDtypeStruct((2, b, state_w), F32)],
        scratch_shapes=[pltpu.VMEM((2, S5_CHUNK * S5_BATCH, state_w), F32),
                        pltpu.VMEM((2, S5_BATCH, state_w), F32)],
        compiler_params=_params("arbitrary", "arbitrary"),
        name="s5_scan",
    )(u_tm, u_tm, bd, cd, acoef, h0)


def _s5_tables(lam_re, lam_im, log_step, b_re, b_im, c_re, c_im):
    step = jnp.exp(log_step.astype(F32))[..., None]
    lr, li = lam_re.astype(F32), lam_im.astype(F32)
    er = jnp.exp(lr * step)
    a_re, a_im = er * jnp.cos(li * step), er * jnp.sin(li * step)
    den = lr * lr + li * li
    q_re = ((a_re - 1.0) * lr + a_im * li) / den
    q_im = (a_im * lr - (a_re - 1.0) * li) / den
    bb_re = q_re[..., None] * b_re - q_im[..., None] * b_im
    bb_im = q_re[..., None] * b_im + q_im[..., None] * b_re
    eye = jnp.eye(8, dtype=F32)
    bb = jnp.stack([bb_re, bb_im], axis=1).reshape(2, 2, S5_JB, 8, S5_STATE, S5_GROUP)
    bd = jnp.einsum('drjgpc,gh->djgcrhp', bb, eye).reshape(2, S5_JB, 128, S5_BLK).astype(BF16)
    cc = jnp.stack([c_re, -c_im], axis=1).astype(F32).reshape(2, 2, S5_JB, 8, S5_GROUP, S5_STATE)
    cd = jnp.einsum('drjgcp,gh->djrgphc', cc, eye).reshape(2, S5_JB, S5_BLK, 128).astype(BF16)
    acoef = jnp.stack([a_re, a_im], axis=1).reshape(2, 2, S5_JB, 512)
    return bd, cd, acoef


def _s5_state_to_blocks(st):
    b = st.shape[0]
    st = st.astype(F32).reshape(b, 2, 2, S5_JB, 8, S5_STATE)
    return jnp.transpose(st, (1, 0, 3, 2, 4, 5)).reshape(2, b, S5_JB * S5_BLK)


def _s5_blocks_to_state(fin):
    b = fin.shape[1]
    fin = fin.reshape(2, b, S5_JB, 2, 8, S5_STATE)
    return jnp.transpose(fin, (1, 0, 3, 2, 4, 5)).reshape(b, 2, 2, S5_GROUPS, S5_STATE)


def _diff_attn_kernel(lam_init, q_ref, k_ref, v_ref, lv_ref, o_ref):
    lv = lv_ref[...]
    lam = (jnp.exp(jnp.sum(lv[0:1] * lv[1:2], axis=-1, keepdims=True))
           - jnp.exp(jnp.sum(lv[2:3] * lv[3:4], axis=-1, keepdims=True)) + lam_init)
    scale = DIFF_DK ** -0.5
    lane = lax.broadcasted_iota(jnp.int32, (1, 128), 1)
    outs = []
    for h in range(DIFF_HEADS):
        probs = []
        for m in range(2):
            j = m * 2 + h // 2
            keep = (lane < 64) if h % 2 == 0 else (lane >= 64)
            qb = jnp.where(keep, q_ref[0, :, 128 * j:128 * (j + 1)], jnp.zeros((), BF16))
            s = _dot_nt(qb, k_ref[0, :, 128 * j:128 * (j + 1)]) * scale
            e = jnp.exp(s - jnp.max(s, axis=-1, keepdims=True))
            probs.append(e / jnp.sum(e, axis=-1, keepdims=True))
        a = probs[0] - lam * probs[1]
        outs.append(_dot(a.astype(BF16), v_ref[0, :, 128 * h:128 * (h + 1)]))
    o_ref[0] = jnp.concatenate(outs, axis=1)


def _diff_attention(q, k, v, lv, lam_init):
    b, lq, _ = q.shape
    lk = k.shape[1]
    tq = TOKEN_TILE
    return pl.pallas_call(
        functools.partial(_diff_attn_kernel, lam_init),
        grid=(b, lq // tq),
        in_specs=[pl.BlockSpec((1, tq, 512), lambda bi, i: (bi, i, 0)),
                  pl.BlockSpec((1, lk, 512), lambda bi, i: (bi, 0, 0)),
                  pl.BlockSpec((1, lk, 512), lambda bi, i: (bi, 0, 0)),
                  pl.BlockSpec((4, DIFF_DK), lambda bi, i: (0, 0))],
        out_specs=pl.BlockSpec((1, tq, 512), lambda bi, i: (bi, i, 0)),
        out_shape=jax.ShapeDtypeStruct((b, lq, 512), F32),
        compiler_params=_params("arbitrary", "arbitrary"),
        name="diff_attention",
    )(q, k, v, lv)


def _win_attn_kernel(local, seq_len, *refs):
    if local:
        q_ref, kc_ref, vc_ref, kl_ref, vl_ref, sink_ref, o_ref = refs
    else:
        q_ref, kc_ref, vc_ref, sink_ref, o_ref = refs
    tq = q_ref.shape[1]
    scale = WIN_DH ** -0.5
    lane = lax.broadcasted_iota(jnp.int32, (1, 128), 1)
    low = lane < 64
    kc = kc_ref[0]
    vc = vc_ref[0]
    n_ctx = kc.shape[0]
    if local:
        span = tq + 2 * WINDOW
        start = pl.multiple_of(pl.program_id(1) * tq, tq)
        kl = kl_ref[0, pl.ds(start, span), :]
        vl = vl_ref[0, pl.ds(start, span), :]
        qpos = start + lax.broadcasted_iota(jnp.int32, (tq, span), 0)
        kpos = start - WINDOW + lax.broadcasted_iota(jnp.int32, (tq, span), 1)
        ok = (jnp.abs(qpos - kpos) <= WINDOW) & (kpos >= 0) & (kpos < seq_len)
    heads = {}
    for n in range(WIN_KV_HEADS):
        keep = low if n == 0 else jnp.logical_not(low)
        for g in range(WIN_GROUP):
            hd = n * WIN_GROUP + g
            j, half = hd // 2, hd % 2
            qb = q_ref[0, :, 128 * j:128 * (j + 1)]
            if half != n:
                qb = jnp.concatenate([qb[:, 64:128], qb[:, 0:64]], axis=1)
            qb = jnp.where(keep, qb, jnp.zeros((), BF16))
            sink = sink_ref[hd:hd + 1, 0:1]
            s = _dot_nt(qb, kc) * scale
            if local:
                sl = _dot_nt(qb, kl) * scale
                s = jnp.concatenate([s, jnp.where(ok, sl, NEG_INF)], axis=1)
            mx = jnp.maximum(jnp.max(s, axis=-1, keepdims=True), sink)
            e = jnp.exp(s - mx)
            den = jnp.sum(e, axis=-1, keepdims=True) + jnp.exp(sink - mx)
            p = (e / den).astype(BF16)
            o = _dot(p[:, 0:n_ctx], vc)
            if local:
                o = o + _dot(p[:, n_ctx:], vl)
            heads[hd] = o
    blocks = []
    for j in range(4):
        ev, od = heads[2 * j], heads[2 * j + 1]
        n = (2 * j) // WIN_GROUP
        if n == 1:
            ev = pltpu.roll(ev, 64, axis=1)
        else:
            od = pltpu.roll(od, 64, axis=1)
        blocks.append(jnp.where(low, ev, od))
    o_ref[0] = jnp.concatenate(blocks, axis=1)


def _win_attention(q, kc, vc, sink, k_lat=None, v_lat=None):
    b, lq, _ = q.shape
    n_ctx = kc.shape[1]
    local = k_lat is not None
    tq = TOKEN_TILE
    in_specs = [pl.BlockSpec((1, tq, 512), lambda bi, i: (bi, i, 0)),
                pl.BlockSpec((1, n_ctx, 128), lambda bi, i: (bi, 0, 0)),
                pl.BlockSpec((1, n_ctx, 128), lambda bi, i: (bi, 0, 0))]
    args = [q, kc, vc]
    if local:
        lp = k_lat.shape[1]
        in_specs += [pl.BlockSpec((1, lp, 128), lambda bi, i: (bi, 0, 0))] * 2
        args += [k_lat, v_lat]
    in_specs.append(pl.BlockSpec((8, 128), lambda bi, i: (0, 0)))
    args.append(sink)
    return pl.pallas_call(
        functools.partial(_win_attn_kernel, local, lq),
        grid=(b, lq // tq), in_specs=in_specs,
        out_specs=pl.BlockSpec((1, tq, 512), lambda bi, i: (bi, i, 0)),
        out_shape=jax.ShapeDtypeStruct((b, lq, 512), F32),
        compiler_params=_params("arbitrary", "arbitrary"),
        name="win_attention_local" if local else "win_attention",
    )(*args)


def _pool_kernel(u_ref, w_ref, scale_ref, o_ref):
    l = u_ref.shape[1]
    lp = l + 2 * POOL_PAD
    t = lax.broadcasted_iota(jnp.int32, (l, 128), 0)
    zpad = jnp.zeros((POOL_PAD, 128), F32)

    def shifted(a, s):
        return pltpu.roll(a, s, axis=0) + pltpu.roll(a, lp - s, axis=0)

    outs = []
    for gi, wl in enumerate(POOL_WINDOWS):
        x = u_ref[0, :, 128 * gi:128 * (gi + 1)]
        xp = jnp.concatenate([zpad, x, zpad], axis=0)
        acc = xp + pltpu.roll(xp, 1, axis=0)
        if wl >= 4:
            acc = shifted(acc, 1)
        if wl >= 8:
            acc = shifted(acc, 2)
        if wl >= 16:
            acc = shifted(acc, 4)
        win = acc[POOL_PAD:POOL_PAD + l]
        cnt = (jnp.minimum(t + wl // 2, l) - jnp.maximum(t - wl // 2, 0)).astype(F32)
        pooled = win / cnt - x
        outs.append(_dot(pooled.astype(BF16), w_ref[gi]))
    o_ref[0] = jnp.concatenate(outs, axis=1) * scale_ref[...]


def _pool_mixer(u, w, scale):
    b, l, _ = u.shape
    return pl.pallas_call(
        _pool_kernel,
        grid=(b,),
        in_specs=[pl.BlockSpec((1, l, 512), lambda bi: (bi, 0, 0)),
                  pl.BlockSpec(w.shape, lambda bi: (0, 0, 0)),
                  pl.BlockSpec((1, 512), lambda bi: (0, 0))],
        out_specs=pl.BlockSpec((1, l, 512), lambda bi: (bi, 0, 0)),
        out_shape=jax.ShapeDtypeStruct((b, l, 512), F32),
        compiler_params=_params("arbitrary"),
        name="pool_mixer",
    )(u, w, scale)


def _mixout_even_kernel(out_scale, x_ref, mod_ref, u_ref, yf_ref, yb_ref, oa_ref, sd_ref,
                        wglu_ref, gsub_ref, wout_ref, o_ref):
    mod = mod_ref[0]
    y = sd_ref[...] * u_ref[...] + yf_ref[...] + yb_ref[...]
    g = _gelu(y)
    o_a = g * _sigmoid(_dot(g.astype(BF16), wglu_ref[...]))
    parts = [o_a.astype(BF16)]
    for h in range(DIFF_HEADS):
        blk = oa_ref[0, :, 128 * h:128 * (h + 1)]
        ms = jnp.mean(blk * blk, axis=-1, keepdims=True)
        parts.append(((blk * lax.rsqrt(ms + EPS) * gsub_ref[...]) * out_scale).astype(BF16))
    cat = jnp.concatenate(parts, axis=1)
    o_ref[0] = x_ref[0] + mod[2:3] * _dot(cat, wout_ref[...])


def _mixout_even(x, mod, u_tm2, yf2, yb2, o_attn, s5_d, w_glu, g_sub, w_out, out_scale, per_batch_mod):
    b, l, d = x.shape
    tl = TOKEN_TILE
    mod_map = (lambda bi, i: (bi, 0, 0)) if per_batch_mod else (lambda bi, i: (0, 0, 0))
    const2 = lambda bi, i: (0, 0)
    tm = pl.BlockSpec((tl, 512), lambda bi, i: (i, bi))
    return pl.pallas_call(
        functools.partial(_mixout_even_kernel, out_scale),
        grid=(b, l // tl),
        in_specs=[pl.BlockSpec((1, tl, d), lambda bi, i: (bi, i, 0)),
                  pl.BlockSpec((1, 6, d), mod_map),
                  tm, tm, tm,
                  pl.BlockSpec((1, tl, 512), lambda bi, i: (bi, i, 0)),
                  pl.BlockSpec((1, 512), const2),
                  pl.BlockSpec((512, 512), const2),
                  pl.BlockSpec((1, 128), const2),
                  pl.BlockSpec((d, d), const2)],
        out_specs=pl.BlockSpec((1, tl, d), lambda bi, i: (bi, i, 0)),
        out_shape=jax.ShapeDtypeStruct((b, l, d), F32),
        compiler_params=_params("arbitrary", "arbitrary"),
        name="mixout_even",
    )(x, mod, u_tm2, yf2, yb2, o_attn, s5_d, w_glu, g_sub, w_out)


def _mixout_odd_kernel(x_ref, mod_ref, oc_ref, od_ref, wout_ref, o_ref):
    mod = mod_ref[0]
    cat = jnp.concatenate([oc_ref[0].astype(BF16), od_ref[0].astype(BF16)], axis=1)
    o_ref[0] = x_ref[0] + mod[2:3] * _dot(cat, wout_ref[...])


def _mixout_odd(x, mod, o_c, o_d, w_out, per_batch_mod):
    b, l, d = x.shape
    tl = TOKEN_TILE
    mod_map = (lambda bi, i: (bi, 0, 0)) if per_batch_mod else (lambda bi, i: (0, 0, 0))
    tok = lambda w_: pl.BlockSpec((1, tl, w_), lambda bi, i: (bi, i, 0))
    return pl.pallas_call(
        _mixout_odd_kernel,
        grid=(b, l // tl),
        in_specs=[tok(d), pl.BlockSpec((1, 6, d), mod_map), tok(512), tok(512),
                  pl.BlockSpec((d, d), lambda bi, i: (0, 0))],
        out_specs=tok(d),
        out_shape=jax.ShapeDtypeStruct((b, l, d), F32),
        compiler_params=_params("arbitrary", "arbitrary"),
        name="mixout_odd",
    )(x, mod, o_c, o_d, w_out)


def _rank16(s, kio, exact):
    n = s.shape[0]
    rank = jnp.full(s.shape, float(PEER_TOPK), F32)
    vals = []
    for r in range(PEER_TOPK):
        m = jnp.max(s, axis=0, keepdims=True)
        hit = s == m
        if exact:
            ix = jnp.min(jnp.where(hit, kio, float(n)), axis=0, keepdims=True)
            hit = kio == ix
        rank = jnp.where(hit, float(r), rank)
        s = jnp.where(hit, -jnp.inf, s)
        vals.append(m)
    return vals, rank


def _candidate_positions(tl):
    i = lax.broadcasted_iota(jnp.int32, (PEER_CAND_ROWS, tl), 0).astype(F32)
    mid = i + 8.0 * jnp.floor((i - 16.0) * 0.125)
    return jnp.where(i < 16.0, i, jnp.where(i < 72.0, mid, 16.0 * (i - 64.0)))


def _peer_topk_kernel(x_ref, mod_ref, g_ref, wq_ref, keys_ref, h_ref, n_ref, p1_ref, r2_ref, p2_ref):
    mod = mod_ref[0]
    h = _modulate(x_ref[0], g_ref[...], mod[3:4], mod[4:5]).astype(BF16)
    h_ref[0] = h
    q = _dot(h, wq_ref[...])
    tl = q.shape[0]
    kio = lax.broadcasted_iota(jnp.int32, (PEER_NKEYS, tl), 0).astype(F32)
    pos = _candidate_positions(tl)

    def head(hd, s1, s2, exact):
        v1, rank1 = _rank16(s1, kio, exact)
        v2, rank2 = _rank16(s2, kio, exact)
        v1s = jnp.concatenate(v1, axis=0)
        v2s = jnp.concatenate(v2, axis=0)
        cand = jnp.concatenate([v1[0] + v2s] + [v1[a] + v2s[0:8] for a in range(1, 8)]
                               + [v1s[8:16] + v2[0]], axis=0)
        sel_a = []
        for r in range(PEER_TOPK):
            m = jnp.max(cand, axis=0, keepdims=True)
            px = jnp.min(jnp.where(cand == m, pos, 999.0), axis=0, keepdims=True)
            cand = jnp.where(pos == px, -jnp.inf, cand)
            sel_a.append(jnp.floor(px * (1.0 / PEER_TOPK)))
            if r == 0:
                best0, z = m, jnp.ones_like(m)
            else:
                z = z + jnp.exp(m - best0)
        n_e1 = jnp.zeros((PEER_NKEYS, tl), F32)
        for a in range(PEER_TOPK):
            n_a = jnp.zeros_like(z)
            for r in range(a, PEER_TOPK):
                n_a = n_a + jnp.where(sel_a[r] == float(a), 1.0, 0.0)
            n_e1 = jnp.where(rank1 == float(a), n_a, n_e1)
        n_ref[0, hd] = n_e1
        p1_ref[0, hd] = jnp.exp(s1 - v1[0]) / z
        r2_ref[0, hd] = rank2.astype(BF16)
        p2_ref[0, hd] = jnp.exp(s2 - v2[0]).astype(BF16)
        ranked = (jnp.sum(jnp.where(rank1 < float(PEER_TOPK), 1.0, 0.0), axis=0, keepdims=True)
                  + jnp.sum(jnp.where(rank2 < float(PEER_TOPK), 1.0, 0.0), axis=0, keepdims=True))
        return jnp.max(jnp.abs(ranked - 2.0 * PEER_TOPK))

    for hd in range(PEER_HEADS):
        s1 = _dot_nt(keys_ref[2 * hd], q[:, 256 * hd:256 * hd + 128].astype(BF16))
        s2 = _dot_nt(keys_ref[2 * hd + 1], q[:, 256 * hd + 128:256 * hd + 256].astype(BF16))
        miscount = head(hd, s1, s2, False)

        @pl.when(miscount > 0.0)
        def _(hd=hd, s1=s1, s2=s2):
            head(hd, s1, s2, True)


def _peer_topk(x, mod, gain, w_q, keys, per_batch_mod):
    b, l, d = x.shape
    tl = TOKEN_TILE
    mod_map = (lambda bi, i: (bi, 0, 0)) if per_batch_mod else (lambda bi, i: (0, 0, 0))
    sel = pl.BlockSpec((1, PEER_HEADS, PEER_NKEYS, tl), lambda bi, i: (bi, 0, 0, i))
    sel_f32 = jax.ShapeDtypeStruct((b, PEER_HEADS, PEER_NKEYS, l), F32)
    sel_bf16 = jax.ShapeDtypeStruct((b, PEER_HEADS, PEER_NKEYS, l), BF16)
    return pl.pallas_call(
        _peer_topk_kernel,
        grid=(b, l // tl),
        in_specs=[pl.BlockSpec((1, tl, d), lambda bi, i: (bi, i, 0)),
                  pl.BlockSpec((1, 6, d), mod_map),
                  pl.BlockSpec((1, d), lambda bi, i: (0, 0)),
                  pl.BlockSpec(w_q.shape, lambda bi, i: (0, 0)),
                  pl.BlockSpec(keys.shape, lambda bi, i: (0, 0, 0))],
        out_specs=[pl.BlockSpec((1, tl, d), lambda bi, i: (bi, i, 0)), sel, sel, sel, sel],
        out_shape=[jax.ShapeDtypeStruct((b, l, d), BF16), sel_f32, sel_f32, sel_bf16, sel_bf16],
        compiler_params=_params("arbitrary", "arbitrary"),
        name="peer_topk",
    )(x, mod, gain, w_q, keys)


def _peer_dense_kernel(x_ref, mod_ref, h_ref, u_ref, vt_ref, n_ref, p1_ref, r2_ref, p2_ref, o_ref,
                       ht_ref, acc_ref, s_ref, gate_ref):
    e = pl.program_id(2)
    rows_per_sub = PEER_SUB // PEER_NKEYS

    @pl.when(e == 0)
    def _():
        ht_ref[...] = h_ref[0].astype(F32).T.astype(BF16)
        acc_ref[...] = jnp.zeros_like(acc_ref)

    ht = ht_ref[...]
    tm = ht.shape[1]
    n_sub = PEER_STEP // PEER_SUB

    def gate_rows(sb):
        for r in range(rows_per_sub):
            row = sb * rows_per_sub + r
            gate = [None] * (PEER_NKEYS // 16)
            for hd in range(PEER_HEADS):
                n_b = jnp.broadcast_to(n_ref[0, hd, row:row + 1, :], (16, tm)).astype(BF16)
                p1_b = jnp.broadcast_to(p1_ref[0, hd, row:row + 1, :], (16, tm)).astype(BF16)
                for k in range(PEER_NKEYS // 16):
                    term = jnp.where(r2_ref[0, hd, 16 * k:16 * (k + 1), :] < n_b,
                                     p2_ref[0, hd, 16 * k:16 * (k + 1), :], jnp.zeros((), BF16)) * p1_b
                    gate[k] = term if gate[k] is None else gate[k] + term
            for k in range(PEER_NKEYS // 16):
                lo = PEER_NKEYS * row + 16 * k
                gate_ref[lo:lo + 16, :] = gate[k]

    for sb in range(n_sub):
        lo = PEER_SUB * sb
        s_ref[lo:lo + PEER_SUB, :] = _dot(u_ref[lo:lo + PEER_SUB, :], ht)
        gate_rows(sb)
    total = None
    for sb in range(n_sub):
        lo = PEER_SUB * sb
        w = gate_ref[lo:lo + PEER_SUB, :] * _gelu(s_ref[lo:lo + PEER_SUB, :]).astype(BF16)
        part = _dot(vt_ref[:, lo:lo + PEER_SUB], w)
        total = part if total is None else total + part
    acc_ref[...] += total

    @pl.when(e == pl.num_programs(2) - 1)
    def _():
        o_ref[0] = x_ref[0] + mod_ref[0][5:6] * acc_ref[...].T


def _peer_dense(x, mod, h, u_tab, vt_tab, n_e1, p1n, rank2, p2, per_batch_mod):
    b, l, d = x.shape
    tm = PEER_TOKENS
    n_exp = u_tab.shape[0]
    rows = PEER_STEP // PEER_NKEYS
    mod_map = (lambda bi, i, e: (bi, 0, 0)) if per_batch_mod else (lambda bi, i, e: (0, 0, 0))
    tok = lambda: pl.BlockSpec((1, tm, d), lambda bi, i, e: (bi, i, 0))
    by_row = pl.BlockSpec((1, PEER_HEADS, rows, tm), lambda bi, i, e: (bi, 0, e, i))
    by_key = pl.BlockSpec((1, PEER_HEADS, PEER_NKEYS, tm), lambda bi, i, e: (bi, 0, 0, i))
    return pl.pallas_call(
        _peer_dense_kernel,
        grid=(b, l // tm, n_exp // PEER_STEP),
        in_specs=[tok(), pl.BlockSpec((1, 6, d), mod_map), tok(),
                  pl.BlockSpec((PEER_STEP, d), lambda bi, i, e: (e, 0)),
                  pl.BlockSpec((d, PEER_STEP), lambda bi, i, e: (0, e)),
                  by_row, by_row, by_key, by_key],
        out_specs=tok(),
        out_shape=jax.ShapeDtypeStruct((b, l, d), F32),
        scratch_shapes=[pltpu.VMEM((d, tm), BF16), pltpu.VMEM((d, tm), F32),
                        pltpu.VMEM((PEER_STEP, tm), F32), pltpu.VMEM((PEER_STEP, tm), BF16)],
        compiler_params=_params("arbitrary", "arbitrary", "arbitrary"),
        name="peer_dense",
    )(x, mod, h, u_tab, vt_tab, n_e1, p1n, rank2, p2)


def _rope_tables(n_lat, width):
    q4 = DIFF_DK // 4
    n_rows = n_lat // GRID_W
    row = jnp.repeat(jnp.arange(n_rows), GRID_W).astype(F32)
    col = jnp.tile(jnp.arange(GRID_W), n_rows).astype(F32)
    freqs = ROPE_BASE ** (-jnp.arange(q4, dtype=F32) / q4)
    ang = jnp.stack([row[:, None] * freqs, col[:, None] * freqs], axis=1)
    cos, sin = jnp.cos(ang), jnp.sin(ang)
    cos64 = jnp.stack([cos, cos], axis=2).reshape(n_lat, 64)
    sin64 = jnp.stack([-sin, sin], axis=2).reshape(n_lat, 64)
    reps = width // 64
    return jnp.tile(cos64, (1, reps)), jnp.tile(sin64, (1, reps))


def _ones_block_diag(width, group):
    idx = jnp.arange(width) // group
    return jnp.where(idx[:, None] == idx[None, :], 1.0 / group, 0.0).astype(BF16)


def _peer_layer(x, mod, W, l, per_batch_mod):
    shape = x.shape
    if not per_batch_mod:
        x = x.reshape(1, -1, shape[-1])
    h, n_e1, p1n, rank2, p2 = _peer_topk(x, mod, W['norm_ffn'][l], W['peer_w_q'][l], W['peer_keys'][l],
                                         per_batch_mod)
    y = _peer_dense(x, mod, h, W['peer_u'][l], W['peer_vt'][l], n_e1, p1n, rank2, p2, per_batch_mod)
    return y.reshape(shape)


def _even_layer(x, mod, W, l, cache, per_batch_mod):
    b, seq, _ = x.shape
    i = l // 2
    lam_init = 0.8 - 0.6 * math.exp(-0.3 * l)
    rope_tabs = W['rope'] if cache is not None else None
    u_tm2, q, k_norm, k_att, v, v_att = _inproj_even(
        x, mod, W['norm_mix'][l], W['even_w_in'][i], W['ones_bd'], W['diff_q_norm'][i],
        W['diff_k_norm'][i], rope_tabs, per_batch_mod)
    if cache is None:
        h0 = jnp.zeros((2, b, S5_JB * S5_BLK), F32)
        k_all, v_all = k_att, v_att
    else:
        h0 = _s5_state_to_blocks(cache['s5'][:, i])
        k_all = jnp.concatenate([cache['diff_k'][:, i].reshape(b, -1, 512).astype(BF16), k_att], axis=1)
        v_all = jnp.concatenate([cache['diff_v'][:, i].reshape(b, -1, 512).astype(BF16), v_att], axis=1)
    yf, yb, fin = _s5_scan(u_tm2.reshape(seq, b, 512), W['s5_bd'][i], W['s5_cd'][i], W['s5_a'][i], h0)
    o_attn = _diff_attention(q, k_all, v_all, W['diff_lambda'][i], lam_init)
    x = _mixout_even(x, mod, u_tm2, yf.reshape(seq, b * 512), yb.reshape(seq, b * 512), o_attn,
                     W['s5_d'][i], W['s5_w_glu'][i], W['diff_sub_norm'][i], W['even_w_out'][i],
                     1.0 - lam_init, per_batch_mod)
    return x, (k_norm, v, fin)


def _odd_layer(x, mod, W, l, cache, per_batch_mod):
    b, seq, _ = x.shape
    i = l // 2
    rope_tabs = W['rope'] if cache is not None else None
    q, k_norm, k_att, v, v_att, u = _inproj_odd(
        x, mod, W['norm_mix'][l], W['odd_w_in'][i], W['ones_bd'], W['win_q_norm'][i],
        W['win_k_norm'][i], rope_tabs, per_batch_mod)
    if cache is None:
        o_c = _win_attention(q, k_att, v_att, W['win_sink'][i])
    else:
        pad = ((0, 0), (WINDOW, WINDOW), (0, 0))
        o_c = _win_attention(q, cache['win_k'][:, i].reshape(b, -1, 128).astype(BF16),
                             cache['win_v'][:, i].reshape(b, -1, 128).astype(BF16), W['win_sink'][i],
                             jnp.pad(k_att, pad), jnp.pad(v_att, pad))
    o_d = _pool_mixer(u, W['pool_w'][i], W['pool_scale'][i])
    x = _mixout_odd(x, mod, o_c, o_d, W['odd_w_out'][i], per_batch_mod)
    return x, (k_norm, v)


def kernel(x_prompt, x_sample, cache_diff_k, cache_diff_v, state_s5, cache_win_k, cache_win_v, c, c_ctx, ada_w, ada_b, norm_mix, norm_ffn, even_w_in, even_w_out, s5_lam_re, s5_lam_im, s5_log_step, s5_b_re, s5_b_im, s5_c_re, s5_c_im, s5_d, s5_w_glu, diff_q_norm, diff_k_norm, diff_lambda, diff_sub_norm, odd_w_in, odd_w_out, win_q_norm, win_k_norm, win_sink, pool_w, pool_scale, peer_w_q, peer_sub_keys, peer_u, peer_v):
    depth = ada_w.shape[0]
    n_even, n_odd = even_w_in.shape[0], odd_w_in.shape[0]
    bsz, seq, d = x_prompt.shape
    dec_b, dec_l, _ = x_sample.shape

    cvec = jnp.zeros((16, d), F32).at[0].set(c_ctx).at[1:1 + dec_b].set(c)
    mod = _modulation(cvec, ada_w, ada_b)

    s5 = [_s5_tables(s5_lam_re[i], s5_lam_im[i], s5_log_step[i], s5_b_re[i], s5_b_im[i],
                     s5_c_re[i], s5_c_im[i]) for i in range(n_even)]
    W = dict(
        norm_mix=norm_mix.reshape(depth, 1, d), norm_ffn=norm_ffn.reshape(depth, 1, d),
        even_w_in=even_w_in.astype(BF16), even_w_out=even_w_out.astype(BF16),
        odd_w_in=odd_w_in.astype(BF16), odd_w_out=odd_w_out.astype(BF16),
        ones_bd=_ones_block_diag(512, 64),
        diff_q_norm=jnp.tile(diff_q_norm, (1, 8)).reshape(n_even, 1, 512),
        diff_k_norm=jnp.tile(diff_k_norm, (1, 8)).reshape(n_even, 1, 512),
        diff_lambda=diff_lambda, diff_sub_norm=diff_sub_norm.reshape(n_even, 1, DIFF_DV),
        s5_bd=[t[0] for t in s5], s5_cd=[t[1] for t in s5], s5_a=[t[2] for t in s5],
        s5_d=s5_d.reshape(n_even, 1, 512), s5_w_glu=s5_w_glu.astype(BF16),
        win_q_norm=jnp.tile(win_q_norm, (1, 8)).reshape(n_odd, 1, 512),
        win_k_norm=jnp.tile(win_k_norm, (1, 2)).reshape(n_odd, 1, 128),
        win_sink=jnp.broadcast_to(win_sink[:, :, None], (n_odd, 8, 128)),
        pool_w=pool_w.astype(BF16), pool_scale=pool_scale.reshape(n_odd, 1, 512),
        peer_w_q=peer_w_q.astype(BF16),
        peer_keys=peer_sub_keys.astype(BF16).reshape(depth, 2 * PEER_HEADS, PEER_NKEYS, -1),
        peer_u=peer_u.astype(BF16),
        peer_vt=jnp.swapaxes(peer_v, 1, 2).astype(BF16),
        rope=_rope_tables(dec_l, 512),
    )
    cache = {'diff_k': cache_diff_k, 'diff_v': cache_diff_v, 's5': state_s5,
             'win_k': cache_win_k, 'win_v': cache_win_v}

    def run(x, mods, cch, per_batch_mod):
        states = []
        for l in range(depth):
            if l % 2 == 0:
                x, st = _even_layer(x, mods[l], W, l, cch, per_batch_mod)
            else:
                x, st = _odd_layer(x, mods[l], W, l, cch, per_batch_mod)
            states.append(st)
            x = _peer_layer(x, mods[l], W, l, per_batch_mod)
        return x, states

    ctx_mods = [mod[l, 0:1].reshape(1, 6, d) for l in range(depth)]
    dec_mods = [mod[l, 1:1 + dec_b].reshape(dec_b, 6, d) for l in range(depth)]
    y_prompt, states = run(x_prompt, ctx_mods, None, False)
    y_sample, _ = run(x_sample, dec_mods, cache, True)

    evens = [states[l] for l in range(depth) if l % 2 == 0]
    odds = [states[l] for l in range(depth) if l % 2 == 1]
    new_diff_k = jnp.stack([s[0].reshape(bsz, seq, 2, DIFF_HEADS, DIFF_DK) for s in evens], axis=1)
    new_diff_v = jnp.stack([s[1].reshape(bsz, seq, DIFF_HEADS, DIFF_DV) for s in evens], axis=1)
    new_s5 = jnp.stack([_s5_blocks_to_state(s[2]) for s in evens], axis=1)
    new_win_k = jnp.stack([s[0].reshape(bsz, seq, WIN_KV_HEADS, WIN_DH) for s in odds], axis=1)
    new_win_v = jnp.stack([s[1].reshape(bsz, seq, WIN_KV_HEADS, WIN_DH) for s in odds], axis=1)
    return (y_prompt, y_sample, new_diff_k, new_diff_v, new_s5, new_win_k, new_win_v)
```

```python
import functools
import math

import jax
import jax.numpy as jnp
from jax import lax
from jax.experimental import pallas as pl
from jax.experimental.pallas import tpu as pltpu

F32 = jnp.float32
BF16 = jnp.bfloat16

D_MODEL = 1024
MIX_HALF = 512
GRID_W = 64
EPS = 1e-6
NEG_INF = -1e30
ROPE_BASE = 10000.0

S5_GROUPS = 32
S5_GROUP = 16
S5_STATE = 64
S5_JB = 4
S5_BLK = 1024
S5_CHUNK = 32
S5_BATCH = 8

DIFF_HEADS = 4
DIFF_DK = 64
DIFF_DV = 128
WIN_KV_HEADS = 2
WIN_GROUP = 4
WIN_DH = 64
ATTN_SCALE = DIFF_DK ** -0.5
WINDOW = 128
POOL_WINDOWS = (2, 4, 8, 16)
POOL_PAD = 8

PEER_HEADS = 8
PEER_NKEYS = 128
PEER_TOPK = 16
PEER_CAND_ROWS = 80
PEER_SUBS = (512, 512, 512, 512)
PEER_STEP = sum(PEER_SUBS)
PEER_TOKENS = 512

TOKEN_TILE = 256
VMEM_LIMIT = 56 * 1024 * 1024


def _params(*sem):
    return pltpu.CompilerParams(dimension_semantics=sem, vmem_limit_bytes=VMEM_LIMIT)


def _dot(a, b):
    return jnp.dot(a, b, preferred_element_type=F32)


def _dot_nt(a, b):
    return lax.dot_general(a, b, (((1,), (1,)), ((), ())), preferred_element_type=F32)


def _sigmoid(x):
    return 1.0 / (1.0 + jnp.exp(-x))


def _gelu(x):
    c = math.sqrt(2.0 / math.pi)
    hx = 0.5 * x
    return hx + hx * jnp.tanh(x * (c + (c * 0.044715) * (x * x)))


def _modulate(x, gain, shift, scale):
    ms = jnp.mean(x * x, axis=-1, keepdims=True)
    return (x * lax.rsqrt(ms + EPS) * gain) * (1.0 + scale) + shift


def _group_rms(z, ones_bd, gain):
    ms = _dot((z * z).astype(BF16), ones_bd)
    return z * lax.rsqrt(ms + EPS) * gain


def _rope(z, cos_t, sin_t):
    lane = lax.broadcasted_iota(jnp.int32, (1, 128), 1) % 32
    first = lane < 16
    parts = []
    for j in range(z.shape[1] // 128):
        c = z[:, 128 * j:128 * (j + 1)]
        parts.append(jnp.where(first, pltpu.roll(c, 112, axis=1), pltpu.roll(c, 16, axis=1)))
    swapped = parts[0] if len(parts) == 1 else jnp.concatenate(parts, axis=1)
    return z * cos_t + swapped * sin_t


def _mod_kernel(c_ref, w_ref, b_ref, o_ref):
    c = c_ref[...]
    s = c * _sigmoid(c)
    o_ref[0] = jnp.dot(s, w_ref[0], precision=lax.Precision.HIGHEST,
                       preferred_element_type=F32) + b_ref[0]


def _modulation(cvec, ada_w, ada_b):
    depth, d, n = ada_w.shape
    tn = 1536
    return pl.pallas_call(
        _mod_kernel,
        grid=(depth, n // tn),
        in_specs=[pl.BlockSpec((16, d), lambda l, j: (0, 0)),
                  pl.BlockSpec((1, d, tn), lambda l, j: (l, 0, j)),
                  pl.BlockSpec((1, 1, tn), lambda l, j: (l, 0, j))],
        out_specs=pl.BlockSpec((1, 16, tn), lambda l, j: (l, 0, j)),
        out_shape=jax.ShapeDtypeStruct((depth, 16, n), F32),
        compiler_params=_params("arbitrary", "arbitrary"),
        name="adaln_mod",
    )(cvec, ada_w, ada_b.reshape(depth, 1, n))


def _inproj_even_kernel(rope, *refs):
    if rope:
        (x_ref, mod_ref, g_ref, w_ref, bd_ref, gq_ref, gk_ref, cos_ref, sin_ref,
         u_ref, q_ref, kn_ref, ka_ref, v_ref, va_ref) = refs
    else:
        (x_ref, mod_ref, g_ref, w_ref, bd_ref, gq_ref, gk_ref,
         u_ref, q_ref, kn_ref, ka_ref, v_ref, va_ref) = refs
    mod = mod_ref[0]
    h = _modulate(x_ref[0], g_ref[...], mod[0:1], mod[1:2])
    z = _dot(h.astype(BF16), w_ref[...])
    u = z[:, 0:512]
    q = _group_rms(z[:, 512:1024], bd_ref[...], gq_ref[...])
    k = _group_rms(z[:, 1024:1536], bd_ref[...], gk_ref[...])
    v = z[:, 1536:2048]
    u_ref[...] = u
    kn_ref[0] = k
    v_ref[0] = v
    va_ref[0] = v.astype(BF16)
    if rope:
        q = _rope(q, cos_ref[...], sin_ref[...])
        k = _rope(k, cos_ref[...], sin_ref[...])
    q_ref[0] = (q * ATTN_SCALE).astype(BF16)
    ka_ref[0] = k.astype(BF16)


def _inproj_even(x, mod, gain, w, ones_bd, gq, gk, rope_tabs, per_batch_mod):
    b, l, d = x.shape
    tl = TOKEN_TILE
    rope = rope_tabs is not None
    mod_map = (lambda bi, i: (bi, 0, 0)) if per_batch_mod else (lambda bi, i: (0, 0, 0))
    const2 = lambda bi, i: (0, 0)
    tok = lambda w_: pl.BlockSpec((1, tl, w_), lambda bi, i: (bi, i, 0))
    in_specs = [tok(d),
                pl.BlockSpec((1, 6, d), mod_map),
                pl.BlockSpec((1, d), const2),
                pl.BlockSpec(w.shape, const2),
                pl.BlockSpec(ones_bd.shape, const2),
                pl.BlockSpec((1, 512), const2),
                pl.BlockSpec((1, 512), const2)]
    args = [x, mod, gain, w, ones_bd, gq, gk]
    if rope:
        in_specs += [pl.BlockSpec((tl, 512), lambda bi, i: (i, 0))] * 2
        args += list(rope_tabs)
    out_shape = [jax.ShapeDtypeStruct((l, b * 512), F32),
                 jax.ShapeDtypeStruct((b, l, 512), BF16),
                 jax.ShapeDtypeStruct((b, l, 512), F32),
                 jax.ShapeDtypeStruct((b, l, 512), BF16),
                 jax.ShapeDtypeStruct((b, l, 512), F32),
                 jax.ShapeDtypeStruct((b, l, 512), BF16)]
    out_specs = [pl.BlockSpec((tl, 512), lambda bi, i: (i, bi)),
                 tok(512), tok(512), tok(512), tok(512), tok(512)]
    return pl.pallas_call(
        functools.partial(_inproj_even_kernel, rope),
        grid=(b, l // tl), in_specs=in_specs, out_specs=out_specs, out_shape=out_shape,
        compiler_params=_params("arbitrary", "arbitrary"),
        name="inproj_even_rope" if rope else "inproj_even",
    )(*args)


def _inproj_odd_kernel(rope, *refs):
    if rope:
        (x_ref, mod_ref, g_ref, w_ref, bd_ref, gq_ref, gk_ref, cos_ref, sin_ref,
         q_ref, kn_ref, ka_ref, v_ref, va_ref, u_ref) = refs
    else:
        (x_ref, mod_ref, g_ref, w_ref, bd_ref, gq_ref, gk_ref,
         q_ref, kn_ref, ka_ref, v_ref, va_ref, u_ref) = refs
    mod = mod_ref[0]
    h = _modulate(x_ref[0], g_ref[...], mod[0:1], mod[1:2])
    z = _dot(h.astype(BF16), w_ref[...])
    bd = bd_ref[...]
    q = _group_rms(z[:, 0:512], bd, gq_ref[...])
    k = _group_rms(z[:, 512:640], bd[0:128, 0:128], gk_ref[...])
    v = z[:, 640:768]
    kn_ref[0] = k
    v_ref[0] = v
    va_ref[0] = v.astype(BF16)
    u_ref[0] = z[:, 768:1280]
    if rope:
        q = _rope(q, cos_ref[...], sin_ref[...])
        k = _rope(k, cos_ref[:, 0:128], sin_ref[:, 0:128])
    q_ref[0] = (q * ATTN_SCALE).astype(BF16)
    ka_ref[0] = k.astype(BF16)


def _inproj_odd(x, mod, gain, w, ones_bd, gq, gk, rope_tabs, per_batch_mod):
    b, l, d = x.shape
    tl = TOKEN_TILE
    rope = rope_tabs is not None
    mod_map = (lambda bi, i: (bi, 0, 0)) if per_batch_mod else (lambda bi, i: (0, 0, 0))
    const2 = lambda bi, i: (0, 0)
    tok = lambda w_: pl.BlockSpec((1, tl, w_), lambda bi, i: (bi, i, 0))
    in_specs = [tok(d),
                pl.BlockSpec((1, 6, d), mod_map),
                pl.BlockSpec((1, d), const2),
                pl.BlockSpec(w.shape, const2),
                pl.BlockSpec(ones_bd.shape, const2),
                pl.BlockSpec((1, 512), const2),
                pl.BlockSpec((1, 128), const2)]
    args = [x, mod, gain, w, ones_bd, gq, gk]
    if rope:
        in_specs += [pl.BlockSpec((tl, 512), lambda bi, i: (i, 0))] * 2
        args += list(rope_tabs)
    out_shape = [jax.ShapeDtypeStruct((b, l, 512), BF16),
                 jax.ShapeDtypeStruct((b, l, 128), F32),
                 jax.ShapeDtypeStruct((b, l, 128), BF16),
                 jax.ShapeDtypeStruct((b, l, 128), F32),
                 jax.ShapeDtypeStruct((b, l, 128), BF16),
                 jax.ShapeDtypeStruct((b, l, 512), F32)]
    out_specs = [tok(512), tok(128), tok(128), tok(128), tok(128), tok(512)]
    return pl.pallas_call(
        functools.partial(_inproj_odd_kernel, rope),
        grid=(b, l // tl), in_specs=in_specs, out_specs=out_specs, out_shape=out_shape,
        compiler_params=_params("arbitrary", "arbitrary"),
        name="inproj_odd_rope" if rope else "inproj_odd",
    )(*args)


def _s5_kernel(uf_ref, ub_ref, bd_ref, cd_ref, a_ref, h0_ref, yf_ref, yb_ref, fin_ref,
               bu_ref, carry_ref):
    c = pl.program_id(1)
    rows = S5_CHUNK * S5_BATCH

    @pl.when(c == 0)
    def _():
        carry_ref[...] = h0_ref[...]

    for d in range(2):
        u = (uf_ref if d == 0 else ub_ref)[...].reshape(rows, MIX_HALF).astype(BF16)
        for j in range(S5_JB):
            bu_ref[d, :, S5_BLK * j:S5_BLK * (j + 1)] = _dot(u[:, 128 * j:128 * (j + 1)], bd_ref[d, j])
        for j in range(S5_JB):
            lo = S5_BLK * j
            a_re = jnp.broadcast_to(a_ref[d, 0, j:j + 1, :], (S5_BATCH, 512))
            a_im = jnp.broadcast_to(a_ref[d, 1, j:j + 1, :], (S5_BATCH, 512))

            def step(s, hc, d=d, lo=lo, a_re=a_re, a_im=a_im):
                h_re, h_im = hc
                t = s if d == 0 else S5_CHUNK - 1 - s
                row = pl.multiple_of(t * S5_BATCH, S5_BATCH)
                b_re = bu_ref[d, pl.ds(row, S5_BATCH), lo:lo + 512]
                b_im = bu_ref[d, pl.ds(row, S5_BATCH), lo + 512:lo + 1024]
                n_re = a_re * h_re - a_im * h_im + b_re
                n_im = a_re * h_im + a_im * h_re + b_im
                bu_ref[d, pl.ds(row, S5_BATCH), lo:lo + 512] = n_re
                bu_ref[d, pl.ds(row, S5_BATCH), lo + 512:lo + 1024] = n_im
                return n_re, n_im

            h_re, h_im = lax.fori_loop(
                0, S5_CHUNK, step,
                (carry_ref[d, :, lo:lo + 512], carry_ref[d, :, lo + 512:lo + 1024]), unroll=4)
            carry_ref[d, :, lo:lo + 512] = h_re
            carry_ref[d, :, lo + 512:lo + 1024] = h_im
        ys = [_dot(bu_ref[d, :, S5_BLK * j:S5_BLK * (j + 1)].astype(BF16), cd_ref[d, j])
              for j in range(S5_JB)]
        y = jnp.concatenate(ys, axis=1).reshape(S5_CHUNK, S5_BATCH, MIX_HALF)
        if d == 0:
            yf_ref[...] = y
        else:
            yb_ref[...] = y

    @pl.when(c == pl.num_programs(1) - 1)
    def _():
        fin_ref[...] = carry_ref[...]


def _s5_scan(u_tm, bd, cd, acoef, h0):
    l, b, _ = u_tm.shape
    nc = l // S5_CHUNK
    ng = b // S5_BATCH
    state_w = S5_JB * S5_BLK
    blk = (S5_CHUNK, S5_BATCH, MIX_HALF)
    full = lambda a: pl.BlockSpec(a.shape, lambda g, c: (0,) * a.ndim)
    return pl.pallas_call(
        _s5_kernel,
        grid=(ng, nc),
        in_specs=[pl.BlockSpec(blk, lambda g, c: (c, g, 0)),
                  pl.BlockSpec(blk, lambda g, c: (nc - 1 - c, g, 0)),
                  full(bd), full(cd), full(acoef),
                  pl.BlockSpec((2, S5_BATCH, state_w), lambda g, c: (0, g, 0))],
        out_specs=[pl.BlockSpec(blk, lambda g, c: (c, g, 0)),
                   pl.BlockSpec(blk, lambda g, c: (nc - 1 - c, g, 0)),
                   pl.BlockSpec((2, S5_BATCH, state_w), lambda g, c: (0, g, 0))],
        out_shape=[jax.ShapeDtypeStruct((l, b, MIX_HALF), F32),
                   jax.ShapeDtypeStruct((l, b, MIX_HALF), F32),
                   jax.ShapeDtypeStruct((2, b, state_w), F32)],
        scratch_shapes=[pltpu.VMEM((2, S5_CHUNK * S5_BATCH, state_w), F32),
                        pltpu.VMEM((2, S5_BATCH, state_w), F32)],
        compiler_params=_params("arbitrary", "arbitrary"),
        name="s5_scan",
    )(u_tm, u_tm, bd, cd, acoef, h0)


def _s5_tables(lam_re, lam_im, log_step, b_re, b_im, c_re, c_im):
    step = jnp.exp(log_step.astype(F32))[..., None]
    lr, li = lam_re.astype(F32), lam_im.astype(F32)
    er = jnp.exp(lr * step)
    a_re, a_im = er * jnp.cos(li * step), er * jnp.sin(li * step)
    den = lr * lr + li * li
    q_re = ((a_re - 1.0) * lr + a_im * li) / den
    q_im = (a_im * lr - (a_re - 1.0) * li) / den
    bb_re = q_re[..., None] * b_re - q_im[..., None] * b_im
    bb_im = q_re[..., None] * b_im + q_im[..., None] * b_re
    eye = jnp.eye(8, dtype=F32)
    bb = jnp.stack([bb_re, bb_im], axis=1).reshape(2, 2, S5_JB, 8, S5_STATE, S5_GROUP)
    bd = jnp.einsum('drjgpc,gh->djgcrhp', bb, eye).reshape(2, S5_JB, 128, S5_BLK).astype(BF16)
    cc = jnp.stack([c_re, -c_im], axis=1).astype(F32).reshape(2, 2, S5_JB, 8, S5_GROUP, S5_STATE)
    cd = jnp.einsum('drjgcp,gh->djrgphc', cc, eye).reshape(2, S5_JB, S5_BLK, 128).astype(BF16)
    acoef = jnp.stack([a_re, a_im], axis=1).reshape(2, 2, S5_JB, 512)
    return bd, cd, acoef


def _s5_state_to_blocks(st):
    b = st.shape[0]
    st = st.astype(F32).reshape(b, 2, 2, S5_JB, 8, S5_STATE)
    return jnp.transpose(st, (1, 0, 3, 2, 4, 5)).reshape(2, b, S5_JB * S5_BLK)


def _s5_blocks_to_state(fin):
    b = fin.shape[1]
    fin = fin.reshape(2, b, S5_JB, 2, 8, S5_STATE)
    return jnp.transpose(fin, (1, 0, 3, 2, 4, 5)).reshape(b, 2, 2, S5_GROUPS, S5_STATE)


def _diff_attn_kernel(lam_init, q_ref, k_ref, v_ref, lv_ref, o_ref):
    lv = lv_ref[...]
    lam = (jnp.exp(jnp.sum(lv[0:1] * lv[1:2], axis=-1, keepdims=True))
           - jnp.exp(jnp.sum(lv[2:3] * lv[3:4], axis=-1, keepdims=True)) + lam_init)
    lane = lax.broadcasted_iota(jnp.int32, (1, 128), 1)
    outs = []
    for h in range(DIFF_HEADS):
        v_h = v_ref[0, :, 128 * h:128 * (h + 1)]
        maps = []
        for m in range(2):
            j = m * 2 + h // 2
            keep = (lane < 64) if h % 2 == 0 else (lane >= 64)
            qb = jnp.where(keep, q_ref[0, :, 128 * j:128 * (j + 1)], jnp.zeros((), BF16))
            s = _dot_nt(qb, k_ref[0, :, 128 * j:128 * (j + 1)])
            e = jnp.exp(s - jnp.max(s, axis=-1, keepdims=True))
            maps.append(_dot(e.astype(BF16), v_h) / jnp.sum(e, axis=-1, keepdims=True))
        outs.append(maps[0] - lam * maps[1])
    o_ref[0] = jnp.concatenate(outs, axis=1)


def _diff_attention(q, k, v, lv, lam_init):
    b, lq, _ = q.shape
    lk = k.shape[1]
    tq = TOKEN_TILE
    return pl.pallas_call(
        functools.partial(_diff_attn_kernel, lam_init),
        grid=(b, lq // tq),
        in_specs=[pl.BlockSpec((1, tq, 512), lambda bi, i: (bi, i, 0)),
                  pl.BlockSpec((1, lk, 512), lambda bi, i: (bi, 0, 0)),
                  pl.BlockSpec((1, lk, 512), lambda bi, i: (bi, 0, 0)),
                  pl.BlockSpec((4, DIFF_DK), lambda bi, i: (0, 0))],
        out_specs=pl.BlockSpec((1, tq, 512), lambda bi, i: (bi, i, 0)),
        out_shape=jax.ShapeDtypeStruct((b, lq, 512), F32),
        compiler_params=_params("arbitrary", "arbitrary"),
        name="diff_attention",
    )(q, k, v, lv)


def _win_attn_kernel(local, seq_len, *refs):
    if local:
        q_ref, kc_ref, vc_ref, kl_ref, vl_ref, sink_ref, o_ref = refs
    else:
        q_ref, kc_ref, vc_ref, sink_ref, o_ref = refs
    tq = q_ref.shape[1]
    lane = lax.broadcasted_iota(jnp.int32, (1, 128), 1)
    low = lane < 64
    kc = kc_ref[0]
    vc = vc_ref[0]
    n_ctx = kc.shape[0]
    if local:
        span = tq + 2 * WINDOW
        start = pl.multiple_of(pl.program_id(1) * tq, tq)
        kl = kl_ref[0, pl.ds(start, span), :]
        vl = vl_ref[0, pl.ds(start, span), :]
        qpos = start + lax.broadcasted_iota(jnp.int32, (tq, span), 0)
        kpos = start - WINDOW + lax.broadcasted_iota(jnp.int32, (tq, span), 1)
        ok = (jnp.abs(qpos - kpos) <= WINDOW) & (kpos >= 0) & (kpos < seq_len)
    heads = {}
    for n in range(WIN_KV_HEADS):
        keep = low if n == 0 else jnp.logical_not(low)
        for g in range(WIN_GROUP):
            hd = n * WIN_GROUP + g
            j, half = hd // 2, hd % 2
            qb = q_ref[0, :, 128 * j:128 * (j + 1)]
            if half != n:
                qb = jnp.concatenate([qb[:, 64:128], qb[:, 0:64]], axis=1)
            qb = jnp.where(keep, qb, jnp.zeros((), BF16))
            sink = sink_ref[hd:hd + 1, 0:1]
            s = _dot_nt(qb, kc)
            if local:
                s = jnp.concatenate([s, jnp.where(ok, _dot_nt(qb, kl), NEG_INF)], axis=1)
            mx = jnp.maximum(jnp.max(s, axis=-1, keepdims=True), sink)
            e = jnp.exp(s - mx)
            den = jnp.sum(e, axis=-1, keepdims=True) + jnp.exp(sink - mx)
            p = e.astype(BF16)
            o = _dot(p[:, 0:n_ctx], vc)
            if local:
                o = o + _dot(p[:, n_ctx:], vl)
            heads[hd] = o / den
    blocks = []
    for j in range(4):
        ev, od = heads[2 * j], heads[2 * j + 1]
        n = (2 * j) // WIN_GROUP
        if n == 1:
            ev = pltpu.roll(ev, 64, axis=1)
        else:
            od = pltpu.roll(od, 64, axis=1)
        blocks.append(jnp.where(low, ev, od))
    o_ref[0] = jnp.concatenate(blocks, axis=1)


def _win_attention(q, kc, vc, sink, k_lat=None, v_lat=None):
    b, lq, _ = q.shape
    n_ctx = kc.shape[1]
    local = k_lat is not None
    tq = TOKEN_TILE
    in_specs = [pl.BlockSpec((1, tq, 512), lambda bi, i: (bi, i, 0)),
                pl.BlockSpec((1, n_ctx, 128), lambda bi, i: (bi, 0, 0)),
                pl.BlockSpec((1, n_ctx, 128), lambda bi, i: (bi, 0, 0))]
    args = [q, kc, vc]
    if local:
        lp = k_lat.shape[1]
        in_specs += [pl.BlockSpec((1, lp, 128), lambda bi, i: (bi, 0, 0))] * 2
        args += [k_lat, v_lat]
    in_specs.append(pl.BlockSpec((8, 128), lambda bi, i: (0, 0)))
    args.append(sink)
    return pl.pallas_call(
        functools.partial(_win_attn_kernel, local, lq),
        grid=(b, lq // tq), in_specs=in_specs,
        out_specs=pl.BlockSpec((1, tq, 512), lambda bi, i: (bi, i, 0)),
        out_shape=jax.ShapeDtypeStruct((b, lq, 512), F32),
        compiler_params=_params("arbitrary", "arbitrary"),
        name="win_attention_local" if local else "win_attention",
    )(*args)


def _pool_kernel(u_ref, w_ref, scale_ref, o_ref):
    l = u_ref.shape[1]
    lp = l + 2 * POOL_PAD
    t = lax.broadcasted_iota(jnp.int32, (l, 128), 0)
    zpad = jnp.zeros((POOL_PAD, 128), F32)

    def shifted(a, s):
        return pltpu.roll(a, s, axis=0) + pltpu.roll(a, lp - s, axis=0)

    outs = []
    for gi, wl in enumerate(POOL_WINDOWS):
        x = u_ref[0, :, 128 * gi:128 * (gi + 1)]
        xp = jnp.concatenate([zpad, x, zpad], axis=0)
        acc = xp + pltpu.roll(xp, 1, axis=0)
        if wl >= 4:
            acc = shifted(acc, 1)
        if wl >= 8:
            acc = shifted(acc, 2)
        if wl >= 16:
            acc = shifted(acc, 4)
        win = acc[POOL_PAD:POOL_PAD + l]
        cnt = (jnp.minimum(t + wl // 2, l) - jnp.maximum(t - wl // 2, 0)).astype(F32)
        pooled = win / cnt - x
        outs.append(_dot(pooled.astype(BF16), w_ref[gi]))
    o_ref[0] = jnp.concatenate(outs, axis=1) * scale_ref[...]


def _pool_mixer(u, w, scale):
    b, l, _ = u.shape
    return pl.pallas_call(
        _pool_kernel,
        grid=(b,),
        in_specs=[pl.BlockSpec((1, l, 512), lambda bi: (bi, 0, 0)),
                  pl.BlockSpec(w.shape, lambda bi: (0, 0, 0)),
                  pl.BlockSpec((1, 512), lambda bi: (0, 0))],
        out_specs=pl.BlockSpec((1, l, 512), lambda bi: (bi, 0, 0)),
        out_shape=jax.ShapeDtypeStruct((b, l, 512), F32),
        compiler_params=_params("arbitrary"),
        name="pool_mixer",
    )(u, w, scale)


def _mixout_even_kernel(out_scale, x_ref, mod_ref, u_ref, yf_ref, yb_ref, oa_ref, sd_ref,
                        wglu_ref, gsub_ref, wout_ref, o_ref):
    mod = mod_ref[0]
    y = sd_ref[...] * u_ref[...] + yf_ref[...] + yb_ref[...]
    g = _gelu(y)
    o_a = g * _sigmoid(_dot(g.astype(BF16), wglu_ref[...]))
    parts = [o_a.astype(BF16)]
    for h in range(DIFF_HEADS):
        blk = oa_ref[0, :, 128 * h:128 * (h + 1)]
        ms = jnp.mean(blk * blk, axis=-1, keepdims=True)
        parts.append(((blk * lax.rsqrt(ms + EPS) * gsub_ref[...]) * out_scale).astype(BF16))
    cat = jnp.concatenate(parts, axis=1)
    o_ref[0] = x_ref[0] + mod[2:3] * _dot(cat, wout_ref[...])


def _mixout_even(x, mod, u_tm2, yf2, yb2, o_attn, s5_d, w_glu, g_sub, w_out, out_scale, per_batch_mod):
    b, l, d = x.shape
    tl = TOKEN_TILE
    mod_map = (lambda bi, i: (bi, 0, 0)) if per_batch_mod else (lambda bi, i: (0, 0, 0))
    const2 = lambda bi, i: (0, 0)
    tm = pl.BlockSpec((tl, 512), lambda bi, i: (i, bi))
    return pl.pallas_call(
        functools.partial(_mixout_even_kernel, out_scale),
        grid=(b, l // tl),
        in_specs=[pl.BlockSpec((1, tl, d), lambda bi, i: (bi, i, 0)),
                  pl.BlockSpec((1, 6, d), mod_map),
                  tm, tm, tm,
                  pl.BlockSpec((1, tl, 512), lambda bi, i: (bi, i, 0)),
                  pl.BlockSpec((1, 512), const2),
                  pl.BlockSpec((512, 512), const2),
                  pl.BlockSpec((1, 128), const2),
                  pl.BlockSpec((d, d), const2)],
        out_specs=pl.BlockSpec((1, tl, d), lambda bi, i: (bi, i, 0)),
        out_shape=jax.ShapeDtypeStruct((b, l, d), F32),
        compiler_params=_params("arbitrary", "arbitrary"),
        name="mixout_even",
    )(x, mod, u_tm2, yf2, yb2, o_attn, s5_d, w_glu, g_sub, w_out)


def _mixout_odd_kernel(x_ref, mod_ref, oc_ref, od_ref, wout_ref, o_ref):
    mod = mod_ref[0]
    cat = jnp.concatenate([oc_ref[0].astype(BF16), od_ref[0].astype(BF16)], axis=1)
    o_ref[0] = x_ref[0] + mod[2:3] * _dot(cat, wout_ref[...])


def _mixout_odd(x, mod, o_c, o_d, w_out, per_batch_mod):
    b, l, d = x.shape
    tl = TOKEN_TILE
    mod_map = (lambda bi, i: (bi, 0, 0)) if per_batch_mod else (lambda bi, i: (0, 0, 0))
    tok = lambda w_: pl.BlockSpec((1, tl, w_), lambda bi, i: (bi, i, 0))
    return pl.pallas_call(
        _mixout_odd_kernel,
        grid=(b, l // tl),
        in_specs=[tok(d), pl.BlockSpec((1, 6, d), mod_map), tok(512), tok(512),
                  pl.BlockSpec((d, d), lambda bi, i: (0, 0))],
        out_specs=tok(d),
        out_shape=jax.ShapeDtypeStruct((b, l, d), F32),
        compiler_params=_params("arbitrary", "arbitrary"),
        name="mixout_odd",
    )(x, mod, o_c, o_d, w_out)


def _rank16(s, kio, exact):
    n = s.shape[0]
    rank = jnp.full(s.shape, float(PEER_TOPK), F32)
    vals = []
    for r in range(PEER_TOPK):
        m = jnp.max(s, axis=0, keepdims=True)
        hit = s == m
        if exact:
            ix = jnp.min(jnp.where(hit, kio, float(n)), axis=0, keepdims=True)
            hit = kio == ix
        rank = jnp.where(hit, float(r), rank)
        s = jnp.where(hit, -jnp.inf, s)
        vals.append(m)
    return vals, rank


def _candidate_positions(tl):
    i = lax.broadcasted_iota(jnp.int32, (PEER_CAND_ROWS, tl), 0).astype(F32)
    mid = i + 8.0 * jnp.floor((i - 16.0) * 0.125)
    return jnp.where(i < 16.0, i, jnp.where(i < 72.0, mid, 16.0 * (i - 64.0)))


def _peer_topk_kernel(x_ref, mod_ref, g_ref, wq_ref, keys_ref, h_ref, n_ref, p1_ref, r2_ref, p2_ref):
    mod = mod_ref[0]
    h = _modulate(x_ref[0], g_ref[...], mod[3:4], mod[4:5]).astype(BF16)
    h_ref[0] = h
    q = _dot(h, wq_ref[...])
    tl = q.shape[0]
    lanes = tl
    kio = lax.broadcasted_iota(jnp.int32, (PEER_NKEYS, lanes), 0).astype(F32)
    pos = _candidate_positions(lanes)

    def head(hd, s1, s2, exact, at):
        s1, s2 = s1[:, at:at + lanes], s2[:, at:at + lanes]
        v1, rank1 = _rank16(s1, kio, exact)
        v2, rank2 = _rank16(s2, kio, exact)
        v1s = jnp.concatenate(v1, axis=0)
        v2s = jnp.concatenate(v2, axis=0)
        cand = jnp.concatenate([v1[0] + v2s] + [v1[a] + v2s[0:8] for a in range(1, 8)]
                               + [v1s[8:16] + v2[0]], axis=0)
        sel_a = []
        for r in range(PEER_TOPK):
            m = jnp.max(cand, axis=0, keepdims=True)
            px = jnp.min(jnp.where(cand == m, pos, 999.0), axis=0, keepdims=True)
            cand = jnp.where(pos == px, -jnp.inf, cand)
            sel_a.append(jnp.floor(px * (1.0 / PEER_TOPK)))
            if r == 0:
                best0, z = m, jnp.ones_like(m)
            else:
                z = z + jnp.exp(m - best0)
        n_e1 = jnp.zeros((PEER_NKEYS, lanes), F32)
        for a in range(PEER_TOPK):
            n_a = jnp.zeros_like(z)
            for r in range(a, PEER_TOPK):
                n_a = n_a + jnp.where(sel_a[r] == float(a), 1.0, 0.0)
            n_e1 = jnp.where(rank1 == float(a), n_a, n_e1)
        n_ref[0, hd, :, at:at + lanes] = n_e1
        p1_ref[0, hd, :, at:at + lanes] = jnp.exp(s1 - v1[0]) / z
        r2_ref[0, hd, :, at:at + lanes] = rank2.astype(BF16)
        p2_ref[0, hd, :, at:at + lanes] = jnp.exp(s2 - v2[0]).astype(BF16)
        ranked = (jnp.sum(jnp.where(rank1 < float(PEER_TOPK), 1.0, 0.0), axis=0, keepdims=True)
                  + jnp.sum(jnp.where(rank2 < float(PEER_TOPK), 1.0, 0.0), axis=0, keepdims=True))
        return jnp.max(jnp.abs(ranked - 2.0 * PEER_TOPK))

    for hd in range(PEER_HEADS):
        s1 = _dot_nt(keys_ref[2 * hd], q[:, 256 * hd:256 * hd + 128].astype(BF16))
        s2 = _dot_nt(keys_ref[2 * hd + 1], q[:, 256 * hd + 128:256 * hd + 256].astype(BF16))
        for at in range(0, tl, lanes):
            miscount = head(hd, s1, s2, False, at)

            @pl.when(miscount > 0.0)
            def _(hd=hd, s1=s1, s2=s2, at=at):
                head(hd, s1, s2, True, at)


def _peer_topk(x, mod, gain, w_q, keys, per_batch_mod):
    b, l, d = x.shape
    tl = TOKEN_TILE
    mod_map = (lambda bi, i: (bi, 0, 0)) if per_batch_mod else (lambda bi, i: (0, 0, 0))
    sel = pl.BlockSpec((1, PEER_HEADS, PEER_NKEYS, tl), lambda bi, i: (bi, 0, 0, i))
    sel_f32 = jax.ShapeDtypeStruct((b, PEER_HEADS, PEER_NKEYS, l), F32)
    sel_bf16 = jax.ShapeDtypeStruct((b, PEER_HEADS, PEER_NKEYS, l), BF16)
    return pl.pallas_call(
        _peer_topk_kernel,
        grid=(b, l // tl),
        in_specs=[pl.BlockSpec((1, tl, d), lambda bi, i: (bi, i, 0)),
                  pl.BlockSpec((1, 6, d), mod_map),
                  pl.BlockSpec((1, d), lambda bi, i: (0, 0)),
                  pl.BlockSpec(w_q.shape, lambda bi, i: (0, 0)),
                  pl.BlockSpec(keys.shape, lambda bi, i: (0, 0, 0))],
        out_specs=[pl.BlockSpec((1, tl, d), lambda bi, i: (bi, i, 0)), sel, sel, sel, sel],
        out_shape=[jax.ShapeDtypeStruct((b, l, d), BF16), sel_f32, sel_f32, sel_bf16, sel_bf16],
        compiler_params=_params("arbitrary", "arbitrary"),
        name="peer_topk",
    )(x, mod, gain, w_q, keys)


def _peer_dense_kernel(x_ref, mod_ref, h_ref, u_ref, vt_ref, n_ref, p1_ref, r2_ref, p2_ref, o_ref,
                       ht_ref, acc_ref, s_ref, gate_ref):
    e = pl.program_id(2)

    @pl.when(e == 0)
    def _():
        ht_ref[...] = h_ref[0].astype(F32).T.astype(BF16)
        acc_ref[...] = jnp.zeros_like(acc_ref)

    ht = ht_ref[...]
    tm = ht.shape[1]
    bounds = [sum(PEER_SUBS[:i]) for i in range(len(PEER_SUBS) + 1)]

    def gate_rows(lo, hi):
        for row in range(lo // PEER_NKEYS, hi // PEER_NKEYS):
            gate = [None] * (PEER_NKEYS // 16)
            for hd in range(PEER_HEADS):
                n_b = jnp.broadcast_to(n_ref[0, hd, row:row + 1, :], (16, tm)).astype(BF16)
                p1_b = jnp.broadcast_to(p1_ref[0, hd, row:row + 1, :], (16, tm)).astype(BF16)
                for k in range(PEER_NKEYS // 16):
                    term = jnp.where(r2_ref[0, hd, 16 * k:16 * (k + 1), :] < n_b,
                                     p2_ref[0, hd, 16 * k:16 * (k + 1), :], jnp.zeros((), BF16)) * p1_b
                    gate[k] = term if gate[k] is None else gate[k] + term
            for k in range(PEER_NKEYS // 16):
                at = PEER_NKEYS * row + 16 * k
                gate_ref[at:at + 16, :] = gate[k]

    for lo, hi in zip(bounds[:-1], bounds[1:]):
        s_ref[lo:hi, :] = _dot(u_ref[lo:hi, :], ht).astype(BF16)
        gate_rows(lo, hi)
    total = None
    for lo, hi in zip(bounds[:-1], bounds[1:]):
        w = gate_ref[lo:hi, :] * _gelu(s_ref[lo:hi, :])
        part = _dot(vt_ref[:, lo:hi], w)
        total = part if total is None else total + part
    acc_ref[...] += total

    @pl.when(e == pl.num_programs(2) - 1)
    def _():
        o_ref[0] = x_ref[0] + mod_ref[0][5:6] * acc_ref[...].T


def _peer_dense(x, mod, h, u_tab, vt_tab, n_e1, p1n, rank2, p2, per_batch_mod):
    b, l, d = x.shape
    tm = PEER_TOKENS
    n_exp = u_tab.shape[0]
    rows = PEER_STEP // PEER_NKEYS
    mod_map = (lambda bi, i, e: (bi, 0, 0)) if per_batch_mod else (lambda bi, i, e: (0, 0, 0))
    tok = lambda: pl.BlockSpec((1, tm, d), lambda bi, i, e: (bi, i, 0))
    by_row = pl.BlockSpec((1, PEER_HEADS, rows, tm), lambda bi, i, e: (bi, 0, e, i))
    by_key = pl.BlockSpec((1, PEER_HEADS, PEER_NKEYS, tm), lambda bi, i, e: (bi, 0, 0, i))
    return pl.pallas_call(
        _peer_dense_kernel,
        grid=(b, l // tm, n_exp // PEER_STEP),
        in_specs=[tok(), pl.BlockSpec((1, 6, d), mod_map), tok(),
                  pl.BlockSpec((PEER_STEP, d), lambda bi, i, e: (e, 0)),
                  pl.BlockSpec((d, PEER_STEP), lambda bi, i, e: (0, e)),
                  by_row, by_row, by_key, by_key],
        out_specs=tok(),
        out_shape=jax.ShapeDtypeStruct((b, l, d), F32),
        scratch_shapes=[pltpu.VMEM((d, tm), BF16), pltpu.VMEM((d, tm), F32),
                        pltpu.VMEM((PEER_STEP, tm), BF16), pltpu.VMEM((PEER_STEP, tm), BF16)],
        compiler_params=_params("arbitrary", "arbitrary", "arbitrary"),
        name="peer_dense",
    )(x, mod, h, u_tab, vt_tab, n_e1, p1n, rank2, p2)


def _rope_tables(n_lat, width):
    q4 = DIFF_DK // 4
    n_rows = n_lat // GRID_W
    row = jnp.repeat(jnp.arange(n_rows), GRID_W).astype(F32)
    col = jnp.tile(jnp.arange(GRID_W), n_rows).astype(F32)
    freqs = ROPE_BASE ** (-jnp.arange(q4, dtype=F32) / q4)
    ang = jnp.stack([row[:, None] * freqs, col[:, None] * freqs], axis=1)
    cos, sin = jnp.cos(ang), jnp.sin(ang)
    cos64 = jnp.stack([cos, cos], axis=2).reshape(n_lat, 64)
    sin64 = jnp.stack([-sin, sin], axis=2).reshape(n_lat, 64)
    reps = width // 64
    return jnp.tile(cos64, (1, reps)), jnp.tile(sin64, (1, reps))


def _ones_block_diag(width, group):
    idx = jnp.arange(width) // group
    return jnp.where(idx[:, None] == idx[None, :], 1.0 / group, 0.0).astype(BF16)


def _peer_layer(x, mod, W, l, per_batch_mod):
    shape = x.shape
    if not per_batch_mod:
        x = x.reshape(1, -1, shape[-1])
    h, n_e1, p1n, rank2, p2 = _peer_topk(x, mod, W['norm_ffn'][l], W['peer_w_q'][l], W['peer_keys'][l],
                                         per_batch_mod)
    y = _peer_dense(x, mod, h, W['peer_u'][l], W['peer_vt'][l], n_e1, p1n, rank2, p2, per_batch_mod)
    return y.reshape(shape)


def _even_layer(x, mod, W, l, cache, per_batch_mod):
    b, seq, _ = x.shape
    i = l // 2
    lam_init = 0.8 - 0.6 * math.exp(-0.3 * l)
    rope_tabs = W['rope'] if cache is not None else None
    u_tm2, q, k_norm, k_att, v, v_att = _inproj_even(
        x, mod, W['norm_mix'][l], W['even_w_in'][i], W['ones_bd'], W['diff_q_norm'][i],
        W['diff_k_norm'][i], rope_tabs, per_batch_mod)
    if cache is None:
        h0 = jnp.zeros((2, b, S5_JB * S5_BLK), F32)
        k_all, v_all = k_att, v_att
    else:
        h0 = _s5_state_to_blocks(cache['s5'][:, i])
        k_all = jnp.concatenate([cache['diff_k'][:, i].reshape(b, -1, 512).astype(BF16), k_att], axis=1)
        v_all = jnp.concatenate([cache['diff_v'][:, i].reshape(b, -1, 512).astype(BF16), v_att], axis=1)
    yf, yb, fin = _s5_scan(u_tm2.reshape(seq, b, 512), W['s5_bd'][i], W['s5_cd'][i], W['s5_a'][i], h0)
    o_attn = _diff_attention(q, k_all, v_all, W['diff_lambda'][i], lam_init)
    x = _mixout_even(x, mod, u_tm2, yf.reshape(seq, b * 512), yb.reshape(seq, b * 512), o_attn,
                     W['s5_d'][i], W['s5_w_glu'][i], W['diff_sub_norm'][i], W['even_w_out'][i],
                     1.0 - lam_init, per_batch_mod)
    return x, (k_norm, v, fin)


def _odd_layer(x, mod, W, l, cache, per_batch_mod):
    b, seq, _ = x.shape
    i = l // 2
    rope_tabs = W['rope'] if cache is not None else None
    q, k_norm, k_att, v, v_att, u = _inproj_odd(
        x, mod, W['norm_mix'][l], W['odd_w_in'][i], W['ones_bd'], W['win_q_norm'][i],
        W['win_k_norm'][i], rope_tabs, per_batch_mod)
    if cache is None:
        o_c = _win_attention(q, k_att, v_att, W['win_sink'][i])
    else:
        pad = ((0, 0), (WINDOW, WINDOW), (0, 0))
        o_c = _win_attention(q, cache['win_k'][:, i].reshape(b, -1, 128).astype(BF16),
                             cache['win_v'][:, i].reshape(b, -1, 128).astype(BF16), W['win_sink'][i],
                             jnp.pad(k_att, pad), jnp.pad(v_att, pad))
    o_d = _pool_mixer(u, W['pool_w'][i], W['pool_scale'][i])
    x = _mixout_odd(x, mod, o_c, o_d, W['odd_w_out'][i], per_batch_mod)
    return x, (k_norm, v)


def kernel(x_prompt, x_sample, cache_diff_k, cache_diff_v, state_s5, cache_win_k, cache_win_v, c, c_ctx, ada_w, ada_b, norm_mix, norm_ffn, even_w_in, even_w_out, s5_lam_re, s5_lam_im, s5_log_step, s5_b_re, s5_b_im, s5_c_re, s5_c_im, s5_d, s5_w_glu, diff_q_norm, diff_k_norm, diff_lambda, diff_sub_norm, odd_w_in, odd_w_out, win_q_norm, win_k_norm, win_sink, pool_w, pool_scale, peer_w_q, peer_sub_keys, peer_u, peer_v):
    depth = ada_w.shape[0]
    n_even, n_odd = even_w_in.shape[0], odd_w_in.shape[0]
    bsz, seq, d = x_prompt.shape
    dec_b, dec_l, _ = x_sample.shape

    cvec = jnp.zeros((16, d), F32).at[0].set(c_ctx).at[1:1 + dec_b].set(c)
    mod = _modulation(cvec, ada_w, ada_b)

    s5 = [_s5_tables(s5_lam_re[i], s5_lam_im[i], s5_log_step[i], s5_b_re[i], s5_b_im[i],
                     s5_c_re[i], s5_c_im[i]) for i in range(n_even)]
    W = dict(
        norm_mix=norm_mix.reshape(depth, 1, d), norm_ffn=norm_ffn.reshape(depth, 1, d),
        even_w_in=even_w_in.astype(BF16), even_w_out=even_w_out.astype(BF16),
        odd_w_in=odd_w_in.astype(BF16), odd_w_out=odd_w_out.astype(BF16),
        ones_bd=_ones_block_diag(512, 64),
        diff_q_norm=jnp.tile(diff_q_norm, (1, 8)).reshape(n_even, 1, 512),
        diff_k_norm=jnp.tile(diff_k_norm, (1, 8)).reshape(n_even, 1, 512),
        diff_lambda=diff_lambda, diff_sub_norm=diff_sub_norm.reshape(n_even, 1, DIFF_DV),
        s5_bd=[t[0] for t in s5], s5_cd=[t[1] for t in s5], s5_a=[t[2] for t in s5],
        s5_d=s5_d.reshape(n_even, 1, 512), s5_w_glu=s5_w_glu.astype(BF16),
        win_q_norm=jnp.tile(win_q_norm, (1, 8)).reshape(n_odd, 1, 512),
        win_k_norm=jnp.tile(win_k_norm, (1, 2)).reshape(n_odd, 1, 128),
        win_sink=jnp.broadcast_to(win_sink[:, :, None], (n_odd, 8, 128)),
        pool_w=pool_w.astype(BF16), pool_scale=pool_scale.reshape(n_odd, 1, 512),
        peer_w_q=peer_w_q.astype(BF16),
        peer_keys=peer_sub_keys.astype(BF16).reshape(depth, 2 * PEER_HEADS, PEER_NKEYS, -1),
        peer_u=peer_u.astype(BF16),
        peer_vt=jnp.swapaxes(peer_v, 1, 2).astype(BF16),
        rope=_rope_tables(dec_l, 512),
    )
    cache = {'diff_k': cache_diff_k, 'diff_v': cache_diff_v, 's5': state_s5,
             'win_k': cache_win_k, 'win_v': cache_win_v}

    def run(x, mods, cch, per_batch_mod):
        states = []
        for l in range(depth):
            if l % 2 == 0:
                x, st = _even_layer(x, mods[l], W, l, cch, per_batch_mod)
            else:
                x, st = _odd_layer(x, mods[l], W, l, cch, per_batch_mod)
            states.append(st)
            x = _peer_layer(x, mods[l], W, l, per_batch_mod)
        return x, states

    ctx_mods = [mod[l, 0:1].reshape(1, 6, d) for l in range(depth)]
    dec_mods = [mod[l, 1:1 + dec_b].reshape(dec_b, 6, d) for l in range(depth)]
    y_prompt, states = run(x_prompt, ctx_mods, None, False)
    y_sample, _ = run(x_sample, dec_mods, cache, True)

    evens = [states[l] for l in range(depth) if l % 2 == 0]
    odds = [states[l] for l in range(depth) if l % 2 == 1]
    new_diff_k = jnp.stack([s[0].reshape(bsz, seq, 2, DIFF_HEADS, DIFF_DK) for s in evens], axis=1)
    new_diff_v = jnp.stack([s[1].reshape(bsz, seq, DIFF_HEADS, DIFF_DV) for s in evens], axis=1)
    new_s5 = jnp.stack([_s5_blocks_to_state(s[2]) for s in evens], axis=1)
    new_win_k = jnp.stack([s[0].reshape(bsz, seq, WIN_KV_HEADS, WIN_DH) for s in odds], axis=1)
    new_win_v = jnp.stack([s[1].reshape(bsz, seq, WIN_KV_HEADS, WIN_DH) for s in odds], axis=1)
    return (y_prompt, y_sample, new_diff_k, new_diff_v, new_s5, new_win_k, new_win_v)
```

```python
import functools
import math

import jax
import jax.numpy as jnp
from jax import lax
from jax.experimental import pallas as pl
from jax.experimental.pallas import tpu as pltpu

F32 = jnp.float32
BF16 = jnp.bfloat16

D_MODEL = 1024
MIX_HALF = 512
GRID_W = 64
EPS = 1e-6
NEG_INF = -1e30
ROPE_BASE = 10000.0

S5_GROUPS = 32
S5_GROUP = 16
S5_STATE = 64
S5_JB = 4
S5_BLK = 1024
S5_CHUNK = 32
S5_BATCH = 8

DIFF_HEADS = 4
DIFF_DK = 64
DIFF_DV = 128
WIN_KV_HEADS = 2
WIN_GROUP = 4
WIN_DH = 64
ATTN_SCALE = DIFF_DK ** -0.5
WINDOW = 128
POOL_WINDOWS = (2, 4, 8, 16)
POOL_PAD = 8

PEER_HEADS = 8
PEER_NKEYS = 128
PEER_TOPK = 16
PEER_CAND_ROWS = 80
RANK_CODE = 2.0 ** 100
PEER_SUBS = (512, 512, 512, 512)
PEER_STEP = sum(PEER_SUBS)
PEER_TOKENS = 512

TOKEN_TILE = 256
VMEM_LIMIT = 56 * 1024 * 1024


def _params(*sem):
    return pltpu.CompilerParams(dimension_semantics=sem, vmem_limit_bytes=VMEM_LIMIT)


def _dot(a, b):
    return jnp.dot(a, b, preferred_element_type=F32)


def _dot_nt(a, b):
    return lax.dot_general(a, b, (((1,), (1,)), ((), ())), preferred_element_type=F32)


def _sigmoid(x):
    return 1.0 / (1.0 + jnp.exp(-x))


def _gelu(x):
    c = math.sqrt(2.0 / math.pi)
    hx = 0.5 * x
    return hx + hx * jnp.tanh(x * (c + (c * 0.044715) * (x * x)))


def _modulate(x, gain, shift, scale):
    ms = jnp.mean(x * x, axis=-1, keepdims=True)
    return (x * lax.rsqrt(ms + EPS) * gain) * (1.0 + scale) + shift


def _group_rms(z, ones_bd, gain):
    ms = _dot((z * z).astype(BF16), ones_bd)
    return z * lax.rsqrt(ms + EPS) * gain


def _rope(z, cos_t, sin_t):
    lane = lax.broadcasted_iota(jnp.int32, (1, 128), 1) % 32
    first = lane < 16
    parts = []
    for j in range(z.shape[1] // 128):
        c = z[:, 128 * j:128 * (j + 1)]
        parts.append(jnp.where(first, pltpu.roll(c, 112, axis=1), pltpu.roll(c, 16, axis=1)))
    swapped = parts[0] if len(parts) == 1 else jnp.concatenate(parts, axis=1)
    return z * cos_t + swapped * sin_t


def _mod_kernel(c_ref, w_ref, b_ref, o_ref):
    c = c_ref[...]
    s = c * _sigmoid(c)
    o_ref[0] = jnp.dot(s, w_ref[0], precision=lax.Precision.HIGHEST,
                       preferred_element_type=F32) + b_ref[0]


def _modulation(cvec, ada_w, ada_b):
    depth, d, n = ada_w.shape
    tn = 1536
    return pl.pallas_call(
        _mod_kernel,
        grid=(depth, n // tn),
        in_specs=[pl.BlockSpec((16, d), lambda l, j: (0, 0)),
                  pl.BlockSpec((1, d, tn), lambda l, j: (l, 0, j)),
                  pl.BlockSpec((1, 1, tn), lambda l, j: (l, 0, j))],
        out_specs=pl.BlockSpec((1, 16, tn), lambda l, j: (l, 0, j)),
        out_shape=jax.ShapeDtypeStruct((depth, 16, n), F32),
        compiler_params=_params("arbitrary", "arbitrary"),
        name="adaln_mod",
    )(cvec, ada_w, ada_b.reshape(depth, 1, n))


def _inproj_even_kernel(rope, *refs):
    if rope:
        (x_ref, mod_ref, g_ref, w_ref, bd_ref, gq_ref, gk_ref, cos_ref, sin_ref,
         u_ref, q_ref, kn_ref, ka_ref, v_ref, va_ref) = refs
    else:
        (x_ref, mod_ref, g_ref, w_ref, bd_ref, gq_ref, gk_ref,
         u_ref, q_ref, kn_ref, ka_ref, v_ref, va_ref) = refs
    mod = mod_ref[0]
    h = _modulate(x_ref[0], g_ref[...], mod[0:1], mod[1:2])
    z = _dot(h.astype(BF16), w_ref[...])
    u = z[:, 0:512]
    q = _group_rms(z[:, 512:1024], bd_ref[...], gq_ref[...])
    k = _group_rms(z[:, 1024:1536], bd_ref[...], gk_ref[...])
    v = z[:, 1536:2048]
    u_ref[...] = u
    kn_ref[0] = k
    v_ref[0] = v
    va_ref[0] = v.astype(BF16)
    if rope:
        q = _rope(q, cos_ref[...], sin_ref[...])
        k = _rope(k, cos_ref[...], sin_ref[...])
    q_ref[0] = (q * ATTN_SCALE).astype(BF16)
    ka_ref[0] = k.astype(BF16)


def _inproj_even(x, mod, gain, w, ones_bd, gq, gk, rope_tabs, per_batch_mod):
    b, l, d = x.shape
    tl = TOKEN_TILE
    rope = rope_tabs is not None
    mod_map = (lambda bi, i: (bi, 0, 0)) if per_batch_mod else (lambda bi, i: (0, 0, 0))
    const2 = lambda bi, i: (0, 0)
    tok = lambda w_: pl.BlockSpec((1, tl, w_), lambda bi, i: (bi, i, 0))
    in_specs = [tok(d),
                pl.BlockSpec((1, 6, d), mod_map),
                pl.BlockSpec((1, d), const2),
                pl.BlockSpec(w.shape, const2),
                pl.BlockSpec(ones_bd.shape, const2),
                pl.BlockSpec((1, 512), const2),
                pl.BlockSpec((1, 512), const2)]
    args = [x, mod, gain, w, ones_bd, gq, gk]
    if rope:
        in_specs += [pl.BlockSpec((tl, 512), lambda bi, i: (i, 0))] * 2
        args += list(rope_tabs)
    out_shape = [jax.ShapeDtypeStruct((l, b * 512), F32),
                 jax.ShapeDtypeStruct((b, l, 512), BF16),
                 jax.ShapeDtypeStruct((b, l, 512), F32),
                 jax.ShapeDtypeStruct((b, l, 512), BF16),
                 jax.ShapeDtypeStruct((b, l, 512), F32),
                 jax.ShapeDtypeStruct((b, l, 512), BF16)]
    out_specs = [pl.BlockSpec((tl, 512), lambda bi, i: (i, bi)),
                 tok(512), tok(512), tok(512), tok(512), tok(512)]
    return pl.pallas_call(
        functools.partial(_inproj_even_kernel, rope),
        grid=(b, l // tl), in_specs=in_specs, out_specs=out_specs, out_shape=out_shape,
        compiler_params=_params("arbitrary", "arbitrary"),
        name="inproj_even_rope" if rope else "inproj_even",
    )(*args)


def _inproj_odd_kernel(rope, *refs):
    if rope:
        (x_ref, mod_ref, g_ref, w_ref, bd_ref, gq_ref, gk_ref, cos_ref, sin_ref,
         q_ref, kn_ref, ka_ref, v_ref, va_ref, u_ref) = refs
    else:
        (x_ref, mod_ref, g_ref, w_ref, bd_ref, gq_ref, gk_ref,
         q_ref, kn_ref, ka_ref, v_ref, va_ref, u_ref) = refs
    mod = mod_ref[0]
    h = _modulate(x_ref[0], g_ref[...], mod[0:1], mod[1:2])
    z = _dot(h.astype(BF16), w_ref[...])
    bd = bd_ref[...]
    q = _group_rms(z[:, 0:512], bd, gq_ref[...])
    k = _group_rms(z[:, 512:640], bd[0:128, 0:128], gk_ref[...])
    v = z[:, 640:768]
    kn_ref[0] = k
    v_ref[0] = v
    va_ref[0] = v.astype(BF16)
    u_ref[0] = z[:, 768:1280]
    if rope:
        q = _rope(q, cos_ref[...], sin_ref[...])
        k = _rope(k, cos_ref[:, 0:128], sin_ref[:, 0:128])
    q_ref[0] = (q * ATTN_SCALE).astype(BF16)
    ka_ref[0] = k.astype(BF16)


def _inproj_odd(x, mod, gain, w, ones_bd, gq, gk, rope_tabs, per_batch_mod):
    b, l, d = x.shape
    tl = TOKEN_TILE
    rope = rope_tabs is not None
    mod_map = (lambda bi, i: (bi, 0, 0)) if per_batch_mod else (lambda bi, i: (0, 0, 0))
    const2 = lambda bi, i: (0, 0)
    tok = lambda w_: pl.BlockSpec((1, tl, w_), lambda bi, i: (bi, i, 0))
    in_specs = [tok(d),
                pl.BlockSpec((1, 6, d), mod_map),
                pl.BlockSpec((1, d), const2),
                pl.BlockSpec(w.shape, const2),
                pl.BlockSpec(ones_bd.shape, const2),
                pl.BlockSpec((1, 512), const2),
                pl.BlockSpec((1, 128), const2)]
    args = [x, mod, gain, w, ones_bd, gq, gk]
    if rope:
        in_specs += [pl.BlockSpec((tl, 512), lambda bi, i: (i, 0))] * 2
        args += list(rope_tabs)
    out_shape = [jax.ShapeDtypeStruct((b, l, 512), BF16),
                 jax.ShapeDtypeStruct((b, l, 128), F32),
                 jax.ShapeDtypeStruct((b, l, 128), BF16),
                 jax.ShapeDtypeStruct((b, l, 128), F32),
                 jax.ShapeDtypeStruct((b, l, 128), BF16),
                 jax.ShapeDtypeStruct((b, l, 512), F32)]
    out_specs = [tok(512), tok(128), tok(128), tok(128), tok(128), tok(512)]
    return pl.pallas_call(
        functools.partial(_inproj_odd_kernel, rope),
        grid=(b, l // tl), in_specs=in_specs, out_specs=out_specs, out_shape=out_shape,
        compiler_params=_params("arbitrary", "arbitrary"),
        name="inproj_odd_rope" if rope else "inproj_odd",
    )(*args)


def _s5_kernel(uf_ref, ub_ref, bd_ref, cd_ref, a_ref, h0_ref, yf_ref, yb_ref, fin_ref,
               bu_ref, carry_ref):
    c = pl.program_id(1)
    rows = S5_CHUNK * S5_BATCH

    @pl.when(c == 0)
    def _():
        carry_ref[...] = h0_ref[...]

    for d in range(2):
        u = (uf_ref if d == 0 else ub_ref)[...].reshape(rows, MIX_HALF).astype(BF16)
        for j in range(S5_JB):
            bu_ref[d, :, S5_BLK * j:S5_BLK * (j + 1)] = _dot(u[:, 128 * j:128 * (j + 1)], bd_ref[d, j])
        for j in range(S5_JB):
            lo = S5_BLK * j
            a_re = jnp.broadcast_to(a_ref[d, 0, j:j + 1, :], (S5_BATCH, 512))
            a_im = jnp.broadcast_to(a_ref[d, 1, j:j + 1, :], (S5_BATCH, 512))

            def step(s, hc, d=d, lo=lo, a_re=a_re, a_im=a_im):
                h_re, h_im = hc
                t = s if d == 0 else S5_CHUNK - 1 - s
                row = pl.multiple_of(t * S5_BATCH, S5_BATCH)
                b_re = bu_ref[d, pl.ds(row, S5_BATCH), lo:lo + 512]
                b_im = bu_ref[d, pl.ds(row, S5_BATCH), lo + 512:lo + 1024]
                n_re = a_re * h_re - a_im * h_im + b_re
                n_im = a_re * h_im + a_im * h_re + b_im
                bu_ref[d, pl.ds(row, S5_BATCH), lo:lo + 512] = n_re
                bu_ref[d, pl.ds(row, S5_BATCH), lo + 512:lo + 1024] = n_im
                return n_re, n_im

            h_re, h_im = lax.fori_loop(
                0, S5_CHUNK, step,
                (carry_ref[d, :, lo:lo + 512], carry_ref[d, :, lo + 512:lo + 1024]), unroll=4)
            carry_ref[d, :, lo:lo + 512] = h_re
            carry_ref[d, :, lo + 512:lo + 1024] = h_im
        ys = [_dot(bu_ref[d, :, S5_BLK * j:S5_BLK * (j + 1)].astype(BF16), cd_ref[d, j])
              for j in range(S5_JB)]
        y = jnp.concatenate(ys, axis=1).reshape(S5_CHUNK, S5_BATCH, MIX_HALF)
        if d == 0:
            yf_ref[...] = y
        else:
            yb_ref[...] = y

    @pl.when(c == pl.num_programs(1) - 1)
    def _():
        fin_ref[...] = carry_ref[...]


def _s5_scan(u_tm, bd, cd, acoef, h0):
    l, b, _ = u_tm.shape
    nc = l // S5_CHUNK
    ng = b // S5_BATCH
    state_w = S5_JB * S5_BLK
    blk = (S5_CHUNK, S5_BATCH, MIX_HALF)
    full = lambda a: pl.BlockSpec(a.shape, lambda g, c: (0,) * a.ndim)
    return pl.pallas_call(
        _s5_kernel,
        grid=(ng, nc),
        in_specs=[pl.BlockSpec(blk, lambda g, c: (c, g, 0)),
                  pl.BlockSpec(blk, lambda g, c: (nc - 1 - c, g, 0)),
                  full(bd), full(cd), full(acoef),
                  pl.BlockSpec((2, S5_BATCH, state_w), lambda g, c: (0, g, 0))],
        out_specs=[pl.BlockSpec(blk, lambda g, c: (c, g, 0)),
                   pl.BlockSpec(blk, lambda g, c: (nc - 1 - c, g, 0)),
                   pl.BlockSpec((2, S5_BATCH, state_w), lambda g, c: (0, g, 0))],
        out_shape=[jax.ShapeDtypeStruct((l, b, MIX_HALF), F32),
                   jax.ShapeDtypeStruct((l, b, MIX_HALF), F32),
                   jax.ShapeDtypeStruct((2, b, state_w), F32)],
        scratch_shapes=[pltpu.VMEM((2, S5_CHUNK * S5_BATCH, state_w), F32),
                        pltpu.VMEM((2, S5_BATCH, state_w), F32)],
        compiler_params=_params("arbitrary", "arbitrary"),
        name="s5_scan",
    )(u_tm, u_tm, bd, cd, acoef, h0)


def _s5_tables(lam_re, lam_im, log_step, b_re, b_im, c_re, c_im):
    step = jnp.exp(log_step.astype(F32))[..., None]
    lr, li = lam_re.astype(F32), lam_im.astype(F32)
    er = jnp.exp(lr * step)
    a_re, a_im = er * jnp.cos(li * step), er * jnp.sin(li * step)
    den = lr * lr + li * li
    q_re = ((a_re - 1.0) * lr + a_im * li) / den
    q_im = (a_im * lr - (a_re - 1.0) * li) / den
    bb_re = q_re[..., None] * b_re - q_im[..., None] * b_im
    bb_im = q_re[..., None] * b_im + q_im[..., None] * b_re
    eye = jnp.eye(8, dtype=F32)
    bb = jnp.stack([bb_re, bb_im], axis=1).reshape(2, 2, S5_JB, 8, S5_STATE, S5_GROUP)
    bd = jnp.einsum('drjgpc,gh->djgcrhp', bb, eye).reshape(2, S5_JB, 128, S5_BLK).astype(BF16)
    cc = jnp.stack([c_re, -c_im], axis=1).astype(F32).reshape(2, 2, S5_JB, 8, S5_GROUP, S5_STATE)
    cd = jnp.einsum('drjgcp,gh->djrgphc', cc, eye).reshape(2, S5_JB, S5_BLK, 128).astype(BF16)
    acoef = jnp.stack([a_re, a_im], axis=1).reshape(2, 2, S5_JB, 512)
    return bd, cd, acoef


def _s5_state_to_blocks(st):
    b = st.shape[0]
    st = st.astype(F32).reshape(b, 2, 2, S5_JB, 8, S5_STATE)
    return jnp.transpose(st, (1, 0, 3, 2, 4, 5)).reshape(2, b, S5_JB * S5_BLK)


def _s5_blocks_to_state(fin):
    b = fin.shape[1]
    fin = fin.reshape(2, b, S5_JB, 2, 8, S5_STATE)
    return jnp.transpose(fin, (1, 0, 3, 2, 4, 5)).reshape(b, 2, 2, S5_GROUPS, S5_STATE)


def _diff_attn_kernel(lam_init, q_ref, k_ref, v_ref, lv_ref, o_ref):
    lv = lv_ref[...]
    lam = (jnp.exp(jnp.sum(lv[0:1] * lv[1:2], axis=-1, keepdims=True))
           - jnp.exp(jnp.sum(lv[2:3] * lv[3:4], axis=-1, keepdims=True)) + lam_init)
    lane = lax.broadcasted_iota(jnp.int32, (1, 128), 1)
    outs = []
    for h in range(DIFF_HEADS):
        es, sums = [], []
        for m in range(2):
            j = m * 2 + h // 2
            keep = (lane < 64) if h % 2 == 0 else (lane >= 64)
            qb = jnp.where(keep, q_ref[0, :, 128 * j:128 * (j + 1)], jnp.zeros((), BF16))
            s = _dot_nt(qb, k_ref[0, :, 128 * j:128 * (j + 1)])
            e = jnp.exp(s - jnp.max(s, axis=-1, keepdims=True))
            es.append(e)
            sums.append(jnp.sum(e, axis=-1, keepdims=True))
        a = es[0] - (lam * sums[0] / sums[1]) * es[1]
        outs.append(_dot(a.astype(BF16), v_ref[0, :, 128 * h:128 * (h + 1)]) / sums[0])
    o_ref[0] = jnp.concatenate(outs, axis=1)


def _diff_attention(q, k, v, lv, lam_init):
    b, lq, _ = q.shape
    lk = k.shape[1]
    tq = TOKEN_TILE
    return pl.pallas_call(
        functools.partial(_diff_attn_kernel, lam_init),
        grid=(b, lq // tq),
        in_specs=[pl.BlockSpec((1, tq, 512), lambda bi, i: (bi, i, 0)),
                  pl.BlockSpec((1, lk, 512), lambda bi, i: (bi, 0, 0)),
                  pl.BlockSpec((1, lk, 512), lambda bi, i: (bi, 0, 0)),
                  pl.BlockSpec((4, DIFF_DK), lambda bi, i: (0, 0))],
        out_specs=pl.BlockSpec((1, tq, 512), lambda bi, i: (bi, i, 0)),
        out_shape=jax.ShapeDtypeStruct((b, lq, 512), F32),
        compiler_params=_params("arbitrary", "arbitrary"),
        name="diff_attention",
    )(q, k, v, lv)


def _win_attn_kernel(local, seq_len, *refs):
    if local:
        q_ref, kc_ref, vc_ref, kl_ref, vl_ref, sink_ref, o_ref = refs
    else:
        q_ref, kc_ref, vc_ref, sink_ref, o_ref = refs
    tq = q_ref.shape[1]
    lane = lax.broadcasted_iota(jnp.int32, (1, 128), 1)
    low = lane < 64
    kc = kc_ref[0]
    vc = vc_ref[0]
    n_ctx = kc.shape[0]
    if local:
        span = tq + 2 * WINDOW
        start = pl.multiple_of(pl.program_id(1) * tq, tq)
        kl = kl_ref[0, pl.ds(start, span), :]
        vl = vl_ref[0, pl.ds(start, span), :]
        qpos = start + lax.broadcasted_iota(jnp.int32, (tq, span), 0)
        kpos = start - WINDOW + lax.broadcasted_iota(jnp.int32, (tq, span), 1)
        ok = (jnp.abs(qpos - kpos) <= WINDOW) & (kpos >= 0) & (kpos < seq_len)
    heads = {}
    for n in range(WIN_KV_HEADS):
        keep = low if n == 0 else jnp.logical_not(low)
        for g in range(WIN_GROUP):
            hd = n * WIN_GROUP + g
            j, half = hd // 2, hd % 2
            qb = q_ref[0, :, 128 * j:128 * (j + 1)]
            if half != n:
                qb = jnp.concatenate([qb[:, 64:128], qb[:, 0:64]], axis=1)
            qb = jnp.where(keep, qb, jnp.zeros((), BF16))
            sink = sink_ref[hd:hd + 1, 0:1]
            s = _dot_nt(qb, kc)
            if local:
                s = jnp.concatenate([s, jnp.where(ok, _dot_nt(qb, kl), NEG_INF)], axis=1)
            mx = jnp.maximum(jnp.max(s, axis=-1, keepdims=True), sink)
            e = jnp.exp(s - mx)
            den = jnp.sum(e, axis=-1, keepdims=True) + jnp.exp(sink - mx)
            p = e.astype(BF16)
            o = _dot(p[:, 0:n_ctx], vc)
            if local:
                o = o + _dot(p[:, n_ctx:], vl)
            heads[hd] = o / den
    blocks = []
    for j in range(4):
        ev, od = heads[2 * j], heads[2 * j + 1]
        n = (2 * j) // WIN_GROUP
        if n == 1:
            ev = pltpu.roll(ev, 64, axis=1)
        else:
            od = pltpu.roll(od, 64, axis=1)
        blocks.append(jnp.where(low, ev, od))
    o_ref[0] = jnp.concatenate(blocks, axis=1).astype(BF16)


def _win_attention(q, kc, vc, sink, k_lat=None, v_lat=None):
    b, lq, _ = q.shape
    n_ctx = kc.shape[1]
    local = k_lat is not None
    tq = TOKEN_TILE
    in_specs = [pl.BlockSpec((1, tq, 512), lambda bi, i: (bi, i, 0)),
                pl.BlockSpec((1, n_ctx, 128), lambda bi, i: (bi, 0, 0)),
                pl.BlockSpec((1, n_ctx, 128), lambda bi, i: (bi, 0, 0))]
    args = [q, kc, vc]
    if local:
        lp = k_lat.shape[1]
        in_specs += [pl.BlockSpec((1, lp, 128), lambda bi, i: (bi, 0, 0))] * 2
        args += [k_lat, v_lat]
    in_specs.append(pl.BlockSpec((8, 128), lambda bi, i: (0, 0)))
    args.append(sink)
    return pl.pallas_call(
        functools.partial(_win_attn_kernel, local, lq),
        grid=(b, lq // tq), in_specs=in_specs,
        out_specs=pl.BlockSpec((1, tq, 512), lambda bi, i: (bi, i, 0)),
        out_shape=jax.ShapeDtypeStruct((b, lq, 512), BF16),
        compiler_params=_params("arbitrary", "arbitrary"),
        name="win_attention_local" if local else "win_attention",
    )(*args)


def _pool_kernel(u_ref, w_ref, scale_ref, o_ref):
    l = u_ref.shape[1]
    lp = l + 2 * POOL_PAD
    t = lax.broadcasted_iota(jnp.int32, (l, 128), 0)
    zpad = jnp.zeros((POOL_PAD, 128), F32)

    def shifted(a, s):
        return pltpu.roll(a, s, axis=0) + pltpu.roll(a, lp - s, axis=0)

    outs = []
    for gi, wl in enumerate(POOL_WINDOWS):
        x = u_ref[0, :, 128 * gi:128 * (gi + 1)]
        xp = jnp.concatenate([zpad, x, zpad], axis=0)
        acc = xp + pltpu.roll(xp, 1, axis=0)
        if wl >= 4:
            acc = shifted(acc, 1)
        if wl >= 8:
            acc = shifted(acc, 2)
        if wl >= 16:
            acc = shifted(acc, 4)
        win = acc[POOL_PAD:POOL_PAD + l]
        cnt = (jnp.minimum(t + wl // 2, l) - jnp.maximum(t - wl // 2, 0)).astype(F32)
        pooled = win / cnt - x
        outs.append(_dot(pooled.astype(BF16), w_ref[gi]))
    o_ref[0] = (jnp.concatenate(outs, axis=1) * scale_ref[...]).astype(BF16)


def _pool_mixer(u, w, scale):
    b, l, _ = u.shape
    return pl.pallas_call(
        _pool_kernel,
        grid=(b,),
        in_specs=[pl.BlockSpec((1, l, 512), lambda bi: (bi, 0, 0)),
                  pl.BlockSpec(w.shape, lambda bi: (0, 0, 0)),
                  pl.BlockSpec((1, 512), lambda bi: (0, 0))],
        out_specs=pl.BlockSpec((1, l, 512), lambda bi: (bi, 0, 0)),
        out_shape=jax.ShapeDtypeStruct((b, l, 512), BF16),
        compiler_params=_params("arbitrary"),
        name="pool_mixer",
    )(u, w, scale)


def _mixout_even_kernel(out_scale, x_ref, mod_ref, u_ref, yf_ref, yb_ref, oa_ref, sd_ref,
                        wglu_ref, gsub_ref, wout_ref, o_ref):
    mod = mod_ref[0]
    y = sd_ref[...] * u_ref[...] + yf_ref[...] + yb_ref[...]
    g = _gelu(y)
    o_a = g * _sigmoid(_dot(g.astype(BF16), wglu_ref[...]))
    parts = [o_a.astype(BF16)]
    for h in range(DIFF_HEADS):
        blk = oa_ref[0, :, 128 * h:128 * (h + 1)]
        ms = jnp.mean(blk * blk, axis=-1, keepdims=True)
        parts.append(((blk * lax.rsqrt(ms + EPS) * gsub_ref[...]) * out_scale).astype(BF16))
    cat = jnp.concatenate(parts, axis=1)
    o_ref[0] = x_ref[0] + mod[2:3] * _dot(cat, wout_ref[...])


def _mixout_even(x, mod, u_tm2, yf2, yb2, o_attn, s5_d, w_glu, g_sub, w_out, out_scale, per_batch_mod):
    b, l, d = x.shape
    tl = TOKEN_TILE
    mod_map = (lambda bi, i: (bi, 0, 0)) if per_batch_mod else (lambda bi, i: (0, 0, 0))
    const2 = lambda bi, i: (0, 0)
    tm = pl.BlockSpec((tl, 512), lambda bi, i: (i, bi))
    return pl.pallas_call(
        functools.partial(_mixout_even_kernel, out_scale),
        grid=(b, l // tl),
        in_specs=[pl.BlockSpec((1, tl, d), lambda bi, i: (bi, i, 0)),
                  pl.BlockSpec((1, 6, d), mod_map),
                  tm, tm, tm,
                  pl.BlockSpec((1, tl, 512), lambda bi, i: (bi, i, 0)),
                  pl.BlockSpec((1, 512), const2),
                  pl.BlockSpec((512, 512), const2),
                  pl.BlockSpec((1, 128), const2),
                  pl.BlockSpec((d, d), const2)],
        out_specs=pl.BlockSpec((1, tl, d), lambda bi, i: (bi, i, 0)),
        out_shape=jax.ShapeDtypeStruct((b, l, d), F32),
        compiler_params=_params("arbitrary", "arbitrary"),
        name="mixout_even",
    )(x, mod, u_tm2, yf2, yb2, o_attn, s5_d, w_glu, g_sub, w_out)


def _mixout_odd_kernel(x_ref, mod_ref, oc_ref, od_ref, wout_ref, o_ref):
    mod = mod_ref[0]
    cat = jnp.concatenate([oc_ref[0], od_ref[0]], axis=1)
    o_ref[0] = x_ref[0] + mod[2:3] * _dot(cat, wout_ref[...])


def _mixout_odd(x, mod, o_c, o_d, w_out, per_batch_mod):
    b, l, d = x.shape
    tl = TOKEN_TILE
    mod_map = (lambda bi, i: (bi, 0, 0)) if per_batch_mod else (lambda bi, i: (0, 0, 0))
    tok = lambda w_: pl.BlockSpec((1, tl, w_), lambda bi, i: (bi, i, 0))
    return pl.pallas_call(
        _mixout_odd_kernel,
        grid=(b, l // tl),
        in_specs=[tok(d), pl.BlockSpec((1, 6, d), mod_map), tok(512), tok(512),
                  pl.BlockSpec((d, d), lambda bi, i: (0, 0))],
        out_specs=tok(d),
        out_shape=jax.ShapeDtypeStruct((b, l, d), F32),
        compiler_params=_params("arbitrary", "arbitrary"),
        name="mixout_odd",
    )(x, mod, o_c, o_d, w_out)


def _rank16(s, kio, exact):
    n = s.shape[0]
    vals = []
    if exact:
        rank = jnp.full(s.shape, float(PEER_TOPK), F32)
        for r in range(PEER_TOPK):
            m = jnp.max(s, axis=0, keepdims=True)
            ix = jnp.min(jnp.where(s == m, kio, float(n)), axis=0, keepdims=True)
            hit = kio == ix
            rank = jnp.where(hit, float(r), rank)
            s = jnp.where(hit, -jnp.inf, s)
            vals.append(m)
        return vals, rank
    for r in range(PEER_TOPK):
        m = jnp.max(s, axis=0, keepdims=True)
        s = jnp.where(s == m, -RANK_CODE * (1.0 + r / 32.0), s)
        vals.append(m)
    code = s * (-1.0 / RANK_CODE)
    return vals, jnp.where(code >= 1.0, (code - 1.0) * 32.0, float(PEER_TOPK))


def _candidate_positions(tl):
    i = lax.broadcasted_iota(jnp.int32, (PEER_CAND_ROWS, tl), 0).astype(F32)
    mid = i + 8.0 * jnp.floor((i - 16.0) * 0.125)
    return jnp.where(i < 16.0, i, jnp.where(i < 72.0, mid, 16.0 * (i - 64.0)))


def _peer_topk_kernel(x_ref, mod_ref, g_ref, wq_ref, keys_ref, h_ref, n_ref, p1_ref, r2_ref, p2_ref):
    mod = mod_ref[0]
    h = _modulate(x_ref[0], g_ref[...], mod[3:4], mod[4:5]).astype(BF16)
    h_ref[0] = h
    q = _dot(h, wq_ref[...])
    tl = q.shape[0]
    lanes = tl
    kio = lax.broadcasted_iota(jnp.int32, (PEER_NKEYS, lanes), 0).astype(F32)
    pos = _candidate_positions(lanes)

    def head(hd, s1, s2, exact, at):
        s1, s2 = s1[:, at:at + lanes], s2[:, at:at + lanes]
        v1, rank1 = _rank16(s1, kio, exact)
        v2, rank2 = _rank16(s2, kio, exact)
        v1s = jnp.concatenate(v1, axis=0)
        v2s = jnp.concatenate(v2, axis=0)
        cand = jnp.concatenate([v1[0] + v2s] + [v1[a] + v2s[0:8] for a in range(1, 8)]
                               + [v1s[8:16] + v2[0]], axis=0)
        sel_a = []
        for r in range(PEER_TOPK):
            m = jnp.max(cand, axis=0, keepdims=True)
            hit = cand == m
            if exact:
                px = jnp.min(jnp.where(hit, pos, 999.0), axis=0, keepdims=True)
                hit = pos == px
            else:
                px = jnp.max(jnp.where(hit, pos, -1.0), axis=0, keepdims=True)
            cand = jnp.where(hit, -jnp.inf, cand)
            sel_a.append(jnp.floor(px * (1.0 / PEER_TOPK)))
            if r == 0:
                best0, z = m, jnp.ones_like(m)
            else:
                z = z + jnp.exp(m - best0)
        n_e1 = jnp.zeros((PEER_NKEYS, lanes), F32)
        for a in range(PEER_TOPK):
            n_a = jnp.zeros_like(z)
            for r in range(a, PEER_TOPK):
                n_a = n_a + jnp.where(sel_a[r] == float(a), 1.0, 0.0)
            n_e1 = jnp.where(rank1 == float(a), n_a, n_e1)
        n_ref[0, hd, :, at:at + lanes] = n_e1
        p1_ref[0, hd, :, at:at + lanes] = jnp.exp(s1 - v1[0]) / z
        r2_ref[0, hd, :, at:at + lanes] = rank2.astype(BF16)
        p2_ref[0, hd, :, at:at + lanes] = jnp.exp(s2 - v2[0]).astype(BF16)
        ranked = (jnp.sum(jnp.where(rank1 < float(PEER_TOPK), 1.0, 0.0), axis=0, keepdims=True)
                  + jnp.sum(jnp.where(rank2 < float(PEER_TOPK), 1.0, 0.0), axis=0, keepdims=True)
                  + jnp.sum(jnp.where(cand == -jnp.inf, 1.0, 0.0), axis=0, keepdims=True))
        return jnp.max(jnp.abs(ranked - 3.0 * PEER_TOPK))

    for hd in range(PEER_HEADS):
        s1 = _dot_nt(keys_ref[2 * hd], q[:, 256 * hd:256 * hd + 128].astype(BF16))
        s2 = _dot_nt(keys_ref[2 * hd + 1], q[:, 256 * hd + 128:256 * hd + 256].astype(BF16))
        for at in range(0, tl, lanes):
            miscount = head(hd, s1, s2, False, at)

            @pl.when(miscount > 0.0)
            def _(hd=hd, s1=s1, s2=s2, at=at):
                head(hd, s1, s2, True, at)


def _peer_topk(x, mod, gain, w_q, keys, per_batch_mod):
    b, l, d = x.shape
    tl = TOKEN_TILE
    mod_map = (lambda bi, i: (bi, 0, 0)) if per_batch_mod else (lambda bi, i: (0, 0, 0))
    sel = pl.BlockSpec((1, PEER_HEADS, PEER_NKEYS, tl), lambda bi, i: (bi, 0, 0, i))
    sel_f32 = jax.ShapeDtypeStruct((b, PEER_HEADS, PEER_NKEYS, l), F32)
    sel_bf16 = jax.ShapeDtypeStruct((b, PEER_HEADS, PEER_NKEYS, l), BF16)
    return pl.pallas_call(
        _peer_topk_kernel,
        grid=(b, l // tl),
        in_specs=[pl.BlockSpec((1, tl, d), lambda bi, i: (bi, i, 0)),
                  pl.BlockSpec((1, 6, d), mod_map),
                  pl.BlockSpec((1, d), lambda bi, i: (0, 0)),
                  pl.BlockSpec(w_q.shape, lambda bi, i: (0, 0)),
                  pl.BlockSpec(keys.shape, lambda bi, i: (0, 0, 0))],
        out_specs=[pl.BlockSpec((1, tl, d), lambda bi, i: (bi, i, 0)), sel, sel, sel, sel],
        out_shape=[jax.ShapeDtypeStruct((b, l, d), BF16), sel_f32, sel_f32, sel_bf16, sel_bf16],
        compiler_params=_params("arbitrary", "arbitrary"),
        name="peer_topk",
    )(x, mod, gain, w_q, keys)


def _peer_dense_kernel(x_ref, mod_ref, h_ref, u_ref, vt_ref, n_ref, p1_ref, r2_ref, p2_ref, o_ref,
                       ht_ref, acc_ref, s_ref, gate_ref):
    e = pl.program_id(2)

    @pl.when(e == 0)
    def _():
        ht_ref[...] = h_ref[0].astype(F32).T.astype(BF16)
        acc_ref[...] = jnp.zeros_like(acc_ref)

    ht = ht_ref[...]
    tm = ht.shape[1]
    bounds = [sum(PEER_SUBS[:i]) for i in range(len(PEER_SUBS) + 1)]

    def gate_rows(lo, hi):
        for row in range(lo // PEER_NKEYS, hi // PEER_NKEYS):
            gate = [None] * (PEER_NKEYS // 16)
            for hd in range(PEER_HEADS):
                n_b = jnp.broadcast_to(n_ref[0, hd, row:row + 1, :], (16, tm)).astype(BF16)
                p1_b = jnp.broadcast_to(p1_ref[0, hd, row:row + 1, :], (16, tm)).astype(BF16)
                for k in range(PEER_NKEYS // 16):
                    term = jnp.where(r2_ref[0, hd, 16 * k:16 * (k + 1), :] < n_b,
                                     p2_ref[0, hd, 16 * k:16 * (k + 1), :], jnp.zeros((), BF16)) * p1_b
                    gate[k] = term if gate[k] is None else gate[k] + term
            for k in range(PEER_NKEYS // 16):
                at = PEER_NKEYS * row + 16 * k
                gate_ref[at:at + 16, :] = gate[k]

    for lo, hi in zip(bounds[:-1], bounds[1:]):
        s_ref[lo:hi, :] = _dot(u_ref[lo:hi, :], ht).astype(BF16)
        gate_rows(lo, hi)
    ws = [gate_ref[lo:hi, :] * _gelu(s_ref[lo:hi, :])
          for lo, hi in zip(bounds[:-1], bounds[1:])]
    acc_ref[...] += _dot(vt_ref[...], jnp.concatenate(ws, axis=0))

    @pl.when(e == pl.num_programs(2) - 1)
    def _():
        o_ref[0] = x_ref[0] + mod_ref[0][5:6] * acc_ref[...].T


def _peer_dense(x, mod, h, u_tabs, vt_tabs, layer, n_e1, p1n, rank2, p2, per_batch_mod):
    b, l, d = x.shape
    tm = PEER_TOKENS
    n_exp = u_tabs.shape[1]
    rows = PEER_STEP // PEER_NKEYS
    mod_map = (lambda bi, i, e: (bi, 0, 0)) if per_batch_mod else (lambda bi, i, e: (0, 0, 0))
    tok = lambda: pl.BlockSpec((1, tm, d), lambda bi, i, e: (bi, i, 0))
    by_row = pl.BlockSpec((1, PEER_HEADS, rows, tm), lambda bi, i, e: (bi, 0, e, i))
    by_key = pl.BlockSpec((1, PEER_HEADS, PEER_NKEYS, tm), lambda bi, i, e: (bi, 0, 0, i))
    return pl.pallas_call(
        _peer_dense_kernel,
        grid=(b, l // tm, n_exp // PEER_STEP),
        in_specs=[tok(), pl.BlockSpec((1, 6, d), mod_map), tok(),
                  pl.BlockSpec((None, PEER_STEP, d), lambda bi, i, e: (layer, e, 0)),
                  pl.BlockSpec((None, d, PEER_STEP), lambda bi, i, e: (layer, 0, e)),
                  by_row, by_row, by_key, by_key],
        out_specs=tok(),
        out_shape=jax.ShapeDtypeStruct((b, l, d), F32),
        scratch_shapes=[pltpu.VMEM((d, tm), BF16), pltpu.VMEM((d, tm), F32),
                        pltpu.VMEM((PEER_STEP, tm), BF16), pltpu.VMEM((PEER_STEP, tm), BF16)],
        compiler_params=_params("arbitrary", "arbitrary", "arbitrary"),
        name="peer_dense",
    )(x, mod, h, u_tabs, vt_tabs, n_e1, p1n, rank2, p2)


def _rope_tables(n_lat, width):
    q4 = DIFF_DK // 4
    n_rows = n_lat // GRID_W
    row = jnp.repeat(jnp.arange(n_rows), GRID_W).astype(F32)
    col = jnp.tile(jnp.arange(GRID_W), n_rows).astype(F32)
    freqs = ROPE_BASE ** (-jnp.arange(q4, dtype=F32) / q4)
    ang = jnp.stack([row[:, None] * freqs, col[:, None] * freqs], axis=1)
    cos, sin = jnp.cos(ang), jnp.sin(ang)
    cos64 = jnp.stack([cos, cos], axis=2).reshape(n_lat, 64)
    sin64 = jnp.stack([-sin, sin], axis=2).reshape(n_lat, 64)
    reps = width // 64
    return jnp.tile(cos64, (1, reps)), jnp.tile(sin64, (1, reps))


def _ones_block_diag(width, group):
    idx = jnp.arange(width) // group
    return jnp.where(idx[:, None] == idx[None, :], 1.0 / group, 0.0).astype(BF16)


def _peer_layer(x, mod, W, l, per_batch_mod):
    shape = x.shape
    if not per_batch_mod:
        x = x.reshape(1, -1, shape[-1])
    h, n_e1, p1n, rank2, p2 = _peer_topk(x, mod, W['norm_ffn'][l], W['peer_w_q'][l], W['peer_keys'][l],
                                         per_batch_mod)
    y = _peer_dense(x, mod, h, W['peer_u'], W['peer_vt'], l, n_e1, p1n, rank2, p2, per_batch_mod)
    return y.reshape(shape)


def _even_layer(x, mod, W, l, cache, per_batch_mod):
    b, seq, _ = x.shape
    i = l // 2
    lam_init = 0.8 - 0.6 * math.exp(-0.3 * l)
    rope_tabs = W['rope'] if cache is not None else None
    u_tm2, q, k_norm, k_att, v, v_att = _inproj_even(
        x, mod, W['norm_mix'][l], W['even_w_in'][i], W['ones_bd'], W['diff_q_norm'][i],
        W['diff_k_norm'][i], rope_tabs, per_batch_mod)
    if cache is None:
        h0 = jnp.zeros((2, b, S5_JB * S5_BLK), F32)
        k_all, v_all = k_att, v_att
    else:
        h0 = _s5_state_to_blocks(cache['s5'][:, i])
        k_all = jnp.concatenate([cache['diff_k'][:, i].reshape(b, -1, 512).astype(BF16), k_att], axis=1)
        v_all = jnp.concatenate([cache['diff_v'][:, i].reshape(b, -1, 512).astype(BF16), v_att], axis=1)
    yf, yb, fin = _s5_scan(u_tm2.reshape(seq, b, 512), W['s5_bd'][i], W['s5_cd'][i], W['s5_a'][i], h0)
    o_attn = _diff_attention(q, k_all, v_all, W['diff_lambda'][i], lam_init)
    x = _mixout_even(x, mod, u_tm2, yf.reshape(seq, b * 512), yb.reshape(seq, b * 512), o_attn,
                     W['s5_d'][i], W['s5_w_glu'][i], W['diff_sub_norm'][i], W['even_w_out'][i],
                     1.0 - lam_init, per_batch_mod)
    return x, (k_norm, v, fin)


def _odd_layer(x, mod, W, l, cache, per_batch_mod):
    b, seq, _ = x.shape
    i = l // 2
    rope_tabs = W['rope'] if cache is not None else None
    q, k_norm, k_att, v, v_att, u = _inproj_odd(
        x, mod, W['norm_mix'][l], W['odd_w_in'][i], W['ones_bd'], W['win_q_norm'][i],
        W['win_k_norm'][i], rope_tabs, per_batch_mod)
    if cache is None:
        o_c = _win_attention(q, k_att, v_att, W['win_sink'][i])
    else:
        pad = ((0, 0), (WINDOW, WINDOW), (0, 0))
        o_c = _win_attention(q, cache['win_k'][:, i].reshape(b, -1, 128).astype(BF16),
                             cache['win_v'][:, i].reshape(b, -1, 128).astype(BF16), W['win_sink'][i],
                             jnp.pad(k_att, pad), jnp.pad(v_att, pad))
    o_d = _pool_mixer(u, W['pool_w'][i], W['pool_scale'][i])
    x = _mixout_odd(x, mod, o_c, o_d, W['odd_w_out'][i], per_batch_mod)
    return x, (k_norm, v)


def kernel(x_prompt, x_sample, cache_diff_k, cache_diff_v, state_s5, cache_win_k, cache_win_v, c, c_ctx, ada_w, ada_b, norm_mix, norm_ffn, even_w_in, even_w_out, s5_lam_re, s5_lam_im, s5_log_step, s5_b_re, s5_b_im, s5_c_re, s5_c_im, s5_d, s5_w_glu, diff_q_norm, diff_k_norm, diff_lambda, diff_sub_norm, odd_w_in, odd_w_out, win_q_norm, win_k_norm, win_sink, pool_w, pool_scale, peer_w_q, peer_sub_keys, peer_u, peer_v):
    depth = ada_w.shape[0]
    n_even, n_odd = even_w_in.shape[0], odd_w_in.shape[0]
    bsz, seq, d = x_prompt.shape
    dec_b, dec_l, _ = x_sample.shape

    cvec = jnp.zeros((16, d), F32).at[0].set(c_ctx).at[1:1 + dec_b].set(c)
    mod = _modulation(cvec, ada_w, ada_b)

    s5 = [_s5_tables(s5_lam_re[i], s5_lam_im[i], s5_log_step[i], s5_b_re[i], s5_b_im[i],
                     s5_c_re[i], s5_c_im[i]) for i in range(n_even)]
    W = dict(
        norm_mix=norm_mix.reshape(depth, 1, d), norm_ffn=norm_ffn.reshape(depth, 1, d),
        even_w_in=even_w_in.astype(BF16), even_w_out=even_w_out.astype(BF16),
        odd_w_in=odd_w_in.astype(BF16), odd_w_out=odd_w_out.astype(BF16),
        ones_bd=_ones_block_diag(512, 64),
        diff_q_norm=jnp.tile(diff_q_norm, (1, 8)).reshape(n_even, 1, 512),
        diff_k_norm=jnp.tile(diff_k_norm, (1, 8)).reshape(n_even, 1, 512),
        diff_lambda=diff_lambda, diff_sub_norm=diff_sub_norm.reshape(n_even, 1, DIFF_DV),
        s5_bd=[t[0] for t in s5], s5_cd=[t[1] for t in s5], s5_a=[t[2] for t in s5],
        s5_d=s5_d.reshape(n_even, 1, 512), s5_w_glu=s5_w_glu.astype(BF16),
        win_q_norm=jnp.tile(win_q_norm, (1, 8)).reshape(n_odd, 1, 512),
        win_k_norm=jnp.tile(win_k_norm, (1, 2)).reshape(n_odd, 1, 128),
        win_sink=jnp.broadcast_to(win_sink[:, :, None], (n_odd, 8, 128)),
        pool_w=pool_w.astype(BF16), pool_scale=pool_scale.reshape(n_odd, 1, 512),
        peer_w_q=peer_w_q.astype(BF16),
        peer_keys=peer_sub_keys.astype(BF16).reshape(depth, 2 * PEER_HEADS, PEER_NKEYS, -1),
        peer_u=peer_u.astype(BF16),
        peer_vt=jnp.swapaxes(peer_v, 1, 2).astype(BF16),
        rope=_rope_tables(dec_l, 512),
    )
    cache = {'diff_k': cache_diff_k, 'diff_v': cache_diff_v, 's5': state_s5,
             'win_k': cache_win_k, 'win_v': cache_win_v}

    def run(x, mods, cch, per_batch_mod):
        states = []
        for l in range(depth):
            if l % 2 == 0:
                x, st = _even_layer(x, mods[l], W, l, cch, per_batch_mod)
            else:
                x, st = _odd_layer(x, mods[l], W, l, cch, per_batch_mod)
            states.append(st)
            x = _peer_layer(x, mods[l], W, l, per_batch_mod)
        return x, states

    ctx_mods = [mod[l, 0:1].reshape(1, 6, d) for l in range(depth)]
    dec_mods = [mod[l, 1:1 + dec_b].reshape(dec_b, 6, d) for l in range(depth)]
    y_prompt, states = run(x_prompt, ctx_mods, None, False)
    y_sample, _ = run(x_sample, dec_mods, cache, True)

    evens = [states[l] for l in range(depth) if l % 2 == 0]
    odds = [states[l] for l in range(depth) if l % 2 == 1]
    new_diff_k = jnp.stack([s[0].reshape(bsz, seq, 2, DIFF_HEADS, DIFF_DK) for s in evens], axis=1)
    new_diff_v = jnp.stack([s[1].reshape(bsz, seq, DIFF_HEADS, DIFF_DV) for s in evens], axis=1)
    new_s5 = jnp.stack([_s5_blocks_to_state(s[2]) for s in evens], axis=1)
    new_win_k = jnp.stack([s[0].reshape(bsz, seq, WIN_KV_HEADS, WIN_DH) for s in odds], axis=1)
    new_win_v = jnp.stack([s[1].reshape(bsz, seq, WIN_KV_HEADS, WIN_DH) for s in odds], axis=1)
    return (y_prompt, y_sample, new_diff_k, new_diff_v, new_s5, new_win_k, new_win_v)
```

```python
import functools
import math

import jax
import jax.numpy as jnp
from jax import lax
from jax.experimental import pallas as pl
from jax.experimental.pallas import tpu as pltpu

F32 = jnp.float32
BF16 = jnp.bfloat16

D_MODEL = 1024
MIX_HALF = 512
GRID_W = 64
EPS = 1e-6
NEG_INF = -1e30
ROPE_BASE = 10000.0

S5_GROUPS = 32
S5_GROUP = 16
S5_STATE = 64
S5_JB = 4
S5_BLK = 1024
S5_CHUNK = 32
S5_BATCH = 8

DIFF_HEADS = 4
DIFF_DK = 64
DIFF_DV = 128
WIN_KV_HEADS = 2
WIN_GROUP = 4
WIN_DH = 64
ATTN_SCALE = DIFF_DK ** -0.5
WINDOW = 128
POOL_WINDOWS = (2, 4, 8, 16)
POOL_PAD = 8

PEER_HEADS = 8
PEER_NKEYS = 128
PEER_TOPK = 16
PEER_CAND_ROWS = 80
RANK_CODE = 2.0 ** 100
PEER_SUBS = (128, 384, 512, 1024)
PEER_STEP = sum(PEER_SUBS)
PEER_TOKENS = 512

TOKEN_TILE = 256
VMEM_LIMIT = 56 * 1024 * 1024


def _params(*sem):
    return pltpu.CompilerParams(dimension_semantics=sem, vmem_limit_bytes=VMEM_LIMIT)


def _dot(a, b):
    return jnp.dot(a, b, preferred_element_type=F32)


def _dot_nt(a, b):
    return lax.dot_general(a, b, (((1,), (1,)), ((), ())), preferred_element_type=F32)


def _sigmoid(x):
    return 1.0 / (1.0 + jnp.exp(-x))


def _gelu(x):
    c = math.sqrt(2.0 / math.pi)
    hx = 0.5 * x
    return hx + hx * jnp.tanh(x * (c + (c * 0.044715) * (x * x)))


def _modulate(x, gain, shift, scale):
    ms = jnp.mean(x * x, axis=-1, keepdims=True)
    return (x * lax.rsqrt(ms + EPS) * gain) * (1.0 + scale) + shift


def _group_rms(z, ones_bd, gain):
    ms = _dot((z * z).astype(BF16), ones_bd)
    return z * lax.rsqrt(ms + EPS) * gain


def _rope(z, cos_t, sin_t):
    lane = lax.broadcasted_iota(jnp.int32, (1, 128), 1) % 32
    first = lane < 16
    parts = []
    for j in range(z.shape[1] // 128):
        c = z[:, 128 * j:128 * (j + 1)]
        parts.append(jnp.where(first, pltpu.roll(c, 112, axis=1), pltpu.roll(c, 16, axis=1)))
    swapped = parts[0] if len(parts) == 1 else jnp.concatenate(parts, axis=1)
    return z * cos_t + swapped * sin_t


def _mod_kernel(c_ref, w_ref, b_ref, o_ref):
    c = c_ref[...]
    s = c * _sigmoid(c)
    o_ref[0] = jnp.dot(s, w_ref[0], precision=lax.Precision.HIGHEST,
                       preferred_element_type=F32) + b_ref[0]


def _modulation(cvec, ada_w, ada_b):
    depth, d, n = ada_w.shape
    tn = 1536
    return pl.pallas_call(
        _mod_kernel,
        grid=(depth, n // tn),
        in_specs=[pl.BlockSpec((16, d), lambda l, j: (0, 0)),
                  pl.BlockSpec((1, d, tn), lambda l, j: (l, 0, j)),
                  pl.BlockSpec((1, 1, tn), lambda l, j: (l, 0, j))],
        out_specs=pl.BlockSpec((1, 16, tn), lambda l, j: (l, 0, j)),
        out_shape=jax.ShapeDtypeStruct((depth, 16, n), F32),
        compiler_params=_params("arbitrary", "arbitrary"),
        name="adaln_mod",
    )(cvec, ada_w, ada_b.reshape(depth, 1, n))


def _inproj_even_kernel(rope, *refs):
    if rope:
        (x_ref, mod_ref, g_ref, w_ref, bd_ref, gq_ref, gk_ref, cos_ref, sin_ref,
         u_ref, q_ref, kn_ref, ka_ref, v_ref, va_ref) = refs
    else:
        (x_ref, mod_ref, g_ref, w_ref, bd_ref, gq_ref, gk_ref,
         u_ref, q_ref, kn_ref, ka_ref, v_ref, va_ref) = refs
    mod = mod_ref[0]
    h = _modulate(x_ref[0], g_ref[...], mod[0:1], mod[1:2])
    z = _dot(h.astype(BF16), w_ref[...])
    u = z[:, 0:512]
    q = _group_rms(z[:, 512:1024], bd_ref[...], gq_ref[...])
    k = _group_rms(z[:, 1024:1536], bd_ref[...], gk_ref[...])
    v = z[:, 1536:2048]
    u_ref[...] = u
    kn_ref[0] = k
    v_ref[0] = v
    va_ref[0] = v.astype(BF16)
    if rope:
        q = _rope(q, cos_ref[...], sin_ref[...])
        k = _rope(k, cos_ref[...], sin_ref[...])
    q_ref[0] = (q * ATTN_SCALE).astype(BF16)
    ka_ref[0] = k.astype(BF16)


def _inproj_even(x, mod, gain, w, ones_bd, gq, gk, rope_tabs, per_batch_mod):
    b, l, d = x.shape
    tl = TOKEN_TILE
    rope = rope_tabs is not None
    mod_map = (lambda bi, i: (bi, 0, 0)) if per_batch_mod else (lambda bi, i: (0, 0, 0))
    const2 = lambda bi, i: (0, 0)
    tok = lambda w_: pl.BlockSpec((1, tl, w_), lambda bi, i: (bi, i, 0))
    in_specs = [tok(d),
                pl.BlockSpec((1, 6, d), mod_map),
                pl.BlockSpec((1, d), const2),
                pl.BlockSpec(w.shape, const2),
                pl.BlockSpec(ones_bd.shape, const2),
                pl.BlockSpec((1, 512), const2),
                pl.BlockSpec((1, 512), const2)]
    args = [x, mod, gain, w, ones_bd, gq, gk]
    if rope:
        in_specs += [pl.BlockSpec((tl, 512), lambda bi, i: (i, 0))] * 2
        args += list(rope_tabs)
    out_shape = [jax.ShapeDtypeStruct((l, b * 512), F32),
                 jax.ShapeDtypeStruct((b, l, 512), BF16),
                 jax.ShapeDtypeStruct((b, l, 512), F32),
                 jax.ShapeDtypeStruct((b, l, 512), BF16),
                 jax.ShapeDtypeStruct((b, l, 512), F32),
                 jax.ShapeDtypeStruct((b, l, 512), BF16)]
    out_specs = [pl.BlockSpec((tl, 512), lambda bi, i: (i, bi)),
                 tok(512), tok(512), tok(512), tok(512), tok(512)]
    return pl.pallas_call(
        functools.partial(_inproj_even_kernel, rope),
        grid=(b, l // tl), in_specs=in_specs, out_specs=out_specs, out_shape=out_shape,
        compiler_params=_params("arbitrary", "arbitrary"),
        name="inproj_even_rope" if rope else "inproj_even",
    )(*args)


def _inproj_odd_kernel(rope, *refs):
    if rope:
        (x_ref, mod_ref, g_ref, w_ref, bd_ref, gq_ref, gk_ref, cos_ref, sin_ref,
         q_ref, kn_ref, ka_ref, v_ref, va_ref, u_ref) = refs
    else:
        (x_ref, mod_ref, g_ref, w_ref, bd_ref, gq_ref, gk_ref,
         q_ref, kn_ref, ka_ref, v_ref, va_ref, u_ref) = refs
    mod = mod_ref[0]
    h = _modulate(x_ref[0], g_ref[...], mod[0:1], mod[1:2])
    z = _dot(h.astype(BF16), w_ref[...])
    bd = bd_ref[...]
    q = _group_rms(z[:, 0:512], bd, gq_ref[...])
    k = _group_rms(z[:, 512:640], bd[0:128, 0:128], gk_ref[...])
    v = z[:, 640:768]
    kn_ref[0] = k
    v_ref[0] = v
    va_ref[0] = v.astype(BF16)
    u_ref[0] = z[:, 768:1280]
    if rope:
        q = _rope(q, cos_ref[...], sin_ref[...])
        k = _rope(k, cos_ref[:, 0:128], sin_ref[:, 0:128])
    q_ref[0] = (q * ATTN_SCALE).astype(BF16)
    ka_ref[0] = k.astype(BF16)


def _inproj_odd(x, mod, gain, w, ones_bd, gq, gk, rope_tabs, per_batch_mod):
    b, l, d = x.shape
    tl = TOKEN_TILE
    rope = rope_tabs is not None
    mod_map = (lambda bi, i: (bi, 0, 0)) if per_batch_mod else (lambda bi, i: (0, 0, 0))
    const2 = lambda bi, i: (0, 0)
    tok = lambda w_: pl.BlockSpec((1, tl, w_), lambda bi, i: (bi, i, 0))
    in_specs = [tok(d),
                pl.BlockSpec((1, 6, d), mod_map),
                pl.BlockSpec((1, d), const2),
                pl.BlockSpec(w.shape, const2),
                pl.BlockSpec(ones_bd.shape, const2),
                pl.BlockSpec((1, 512), const2),
                pl.BlockSpec((1, 128), const2)]
    args = [x, mod, gain, w, ones_bd, gq, gk]
    if rope:
        in_specs += [pl.BlockSpec((tl, 512), lambda bi, i: (i, 0))] * 2
        args += list(rope_tabs)
    out_shape = [jax.ShapeDtypeStruct((b, l, 512), BF16),
                 jax.ShapeDtypeStruct((b, l, 128), F32),
                 jax.ShapeDtypeStruct((b, l, 128), BF16),
                 jax.ShapeDtypeStruct((b, l, 128), F32),
                 jax.ShapeDtypeStruct((b, l, 128), BF16),
                 jax.ShapeDtypeStruct((b, l, 512), F32)]
    out_specs = [tok(512), tok(128), tok(128), tok(128), tok(128), tok(512)]
    return pl.pallas_call(
        functools.partial(_inproj_odd_kernel, rope),
        grid=(b, l // tl), in_specs=in_specs, out_specs=out_specs, out_shape=out_shape,
        compiler_params=_params("arbitrary", "arbitrary"),
        name="inproj_odd_rope" if rope else "inproj_odd",
    )(*args)


def _s5_kernel(uf_ref, ub_ref, bd_ref, cd_ref, a_ref, h0_ref, yf_ref, yb_ref, fin_ref,
               bu_ref, carry_ref):
    c = pl.program_id(1)
    rows = S5_CHUNK * S5_BATCH

    @pl.when(c == 0)
    def _():
        carry_ref[...] = h0_ref[...]

    for d in range(2):
        u = (uf_ref if d == 0 else ub_ref)[...].reshape(rows, MIX_HALF).astype(BF16)
        for j in range(S5_JB):
            bu_ref[d, :, S5_BLK * j:S5_BLK * (j + 1)] = _dot(u[:, 128 * j:128 * (j + 1)], bd_ref[d, j])
        for j in range(S5_JB):
            lo = S5_BLK * j
            a_re = jnp.broadcast_to(a_ref[d, 0, j:j + 1, :], (S5_BATCH, 512))
            a_im = jnp.broadcast_to(a_ref[d, 1, j:j + 1, :], (S5_BATCH, 512))

            def step(s, hc, d=d, lo=lo, a_re=a_re, a_im=a_im):
                h_re, h_im = hc
                t = s if d == 0 else S5_CHUNK - 1 - s
                row = pl.multiple_of(t * S5_BATCH, S5_BATCH)
                b_re = bu_ref[d, pl.ds(row, S5_BATCH), lo:lo + 512]
                b_im = bu_ref[d, pl.ds(row, S5_BATCH), lo + 512:lo + 1024]
                n_re = a_re * h_re - a_im * h_im + b_re
                n_im = a_re * h_im + a_im * h_re + b_im
                bu_ref[d, pl.ds(row, S5_BATCH), lo:lo + 512] = n_re
                bu_ref[d, pl.ds(row, S5_BATCH), lo + 512:lo + 1024] = n_im
                return n_re, n_im

            h_re, h_im = lax.fori_loop(
                0, S5_CHUNK, step,
                (carry_ref[d, :, lo:lo + 512], carry_ref[d, :, lo + 512:lo + 1024]), unroll=4)
            carry_ref[d, :, lo:lo + 512] = h_re
            carry_ref[d, :, lo + 512:lo + 1024] = h_im
        ys = [_dot(bu_ref[d, :, S5_BLK * j:S5_BLK * (j + 1)].astype(BF16), cd_ref[d, j])
              for j in range(S5_JB)]
        y = jnp.concatenate(ys, axis=1).reshape(S5_CHUNK, S5_BATCH, MIX_HALF)
        if d == 0:
            yf_ref[...] = y
        else:
            yb_ref[...] = y

    @pl.when(c == pl.num_programs(1) - 1)
    def _():
        fin_ref[...] = carry_ref[...]


def _s5_scan(u_tm, bd, cd, acoef, h0):
    l, b, _ = u_tm.shape
    nc = l // S5_CHUNK
    ng = b // S5_BATCH
    state_w = S5_JB * S5_BLK
    blk = (S5_CHUNK, S5_BATCH, MIX_HALF)
    full = lambda a: pl.BlockSpec(a.shape, lambda g, c: (0,) * a.ndim)
    return pl.pallas_call(
        _s5_kernel,
        grid=(ng, nc),
        in_specs=[pl.BlockSpec(blk, lambda g, c: (c, g, 0)),
                  pl.BlockSpec(blk, lambda g, c: (nc - 1 - c, g, 0)),
                  full(bd), full(cd), full(acoef),
                  pl.BlockSpec((2, S5_BATCH, state_w), lambda g, c: (0, g, 0))],
        out_specs=[pl.BlockSpec(blk, lambda g, c: (c, g, 0)),
                   pl.BlockSpec(blk, lambda g, c: (nc - 1 - c, g, 0)),
                   pl.BlockSpec((2, S5_BATCH, state_w), lambda g, c: (0, g, 0))],
        out_shape=[jax.ShapeDtypeStruct((l, b, MIX_HALF), F32),
                   jax.ShapeDtypeStruct((l, b, MIX_HALF), F32),
                   jax.ShapeDtypeStruct((2, b, state_w), F32)],
        scratch_shapes=[pltpu.VMEM((2, S5_CHUNK * S5_BATCH, state_w), F32),
                        pltpu.VMEM((2, S5_BATCH, state_w), F32)],
        compiler_params=_params("arbitrary", "arbitrary"),
        name="s5_scan",
    )(u_tm, u_tm, bd, cd, acoef, h0)


def _s5_tables(lam_re, lam_im, log_step, b_re, b_im, c_re, c_im):
    step = jnp.exp(log_step.astype(F32))[..., None]
    lr, li = lam_re.astype(F32), lam_im.astype(F32)
    er = jnp.exp(lr * step)
    a_re, a_im = er * jnp.cos(li * step), er * jnp.sin(li * step)
    den = lr * lr + li * li
    q_re = ((a_re - 1.0) * lr + a_im * li) / den
    q_im = (a_im * lr - (a_re - 1.0) * li) / den
    bb_re = q_re[..., None] * b_re - q_im[..., None] * b_im
    bb_im = q_re[..., None] * b_im + q_im[..., None] * b_re
    eye = jnp.eye(8, dtype=F32)
    bb = jnp.stack([bb_re, bb_im], axis=1).reshape(2, 2, S5_JB, 8, S5_STATE, S5_GROUP)
    bd = jnp.einsum('drjgpc,gh->djgcrhp', bb, eye).reshape(2, S5_JB, 128, S5_BLK).astype(BF16)
    cc = jnp.stack([c_re, -c_im], axis=1).astype(F32).reshape(2, 2, S5_JB, 8, S5_GROUP, S5_STATE)
    cd = jnp.einsum('drjgcp,gh->djrgphc', cc, eye).reshape(2, S5_JB, S5_BLK, 128).astype(BF16)
    acoef = jnp.stack([a_re, a_im], axis=1).reshape(2, 2, S5_JB, 512)
    return bd, cd, acoef


def _s5_state_to_blocks(st):
    b = st.shape[0]
    st = st.astype(F32).reshape(b, 2, 2, S5_JB, 8, S5_STATE)
    return jnp.transpose(st, (1, 0, 3, 2, 4, 5)).reshape(2, b, S5_JB * S5_BLK)


def _s5_blocks_to_state(fin):
    b = fin.shape[1]
    fin = fin.reshape(2, b, S5_JB, 2, 8, S5_STATE)
    return jnp.transpose(fin, (1, 0, 3, 2, 4, 5)).reshape(b, 2, 2, S5_GROUPS, S5_STATE)


def _diff_attn_kernel(lam_init, q_ref, k_ref, v_ref, lv_ref, o_ref):
    lv = lv_ref[...]
    lam = (jnp.exp(jnp.sum(lv[0:1] * lv[1:2], axis=-1, keepdims=True))
           - jnp.exp(jnp.sum(lv[2:3] * lv[3:4], axis=-1, keepdims=True)) + lam_init)
    lane = lax.broadcasted_iota(jnp.int32, (1, 128), 1)
    outs = []
    for h in range(DIFF_HEADS):
        es, sums = [], []
        for m in range(2):
            j = m * 2 + h // 2
            keep = (lane < 64) if h % 2 == 0 else (lane >= 64)
            qb = jnp.where(keep, q_ref[0, :, 128 * j:128 * (j + 1)], jnp.zeros((), BF16))
            s = _dot_nt(qb, k_ref[0, :, 128 * j:128 * (j + 1)])
            e = jnp.exp(s - jnp.max(s, axis=-1, keepdims=True))
            es.append(e)
            sums.append(jnp.sum(e, axis=-1, keepdims=True))
        a = es[0] - (lam * sums[0] / sums[1]) * es[1]
        outs.append(_dot(a.astype(BF16), v_ref[0, :, 128 * h:128 * (h + 1)]) / sums[0])
    o_ref[0] = jnp.concatenate(outs, axis=1)


def _diff_attention(q, k, v, lv, lam_init):
    b, lq, _ = q.shape
    lk = k.shape[1]
    tq = TOKEN_TILE
    return pl.pallas_call(
        functools.partial(_diff_attn_kernel, lam_init),
        grid=(b, lq // tq),
        in_specs=[pl.BlockSpec((1, tq, 512), lambda bi, i: (bi, i, 0)),
                  pl.BlockSpec((1, lk, 512), lambda bi, i: (bi, 0, 0)),
                  pl.BlockSpec((1, lk, 512), lambda bi, i: (bi, 0, 0)),
                  pl.BlockSpec((4, DIFF_DK), lambda bi, i: (0, 0))],
        out_specs=pl.BlockSpec((1, tq, 512), lambda bi, i: (bi, i, 0)),
        out_shape=jax.ShapeDtypeStruct((b, lq, 512), F32),
        compiler_params=_params("arbitrary", "arbitrary"),
        name="diff_attention",
    )(q, k, v, lv)


def _win_attn_kernel(local, seq_len, *refs):
    if local:
        q_ref, kc_ref, vc_ref, kl_ref, vl_ref, sink_ref, o_ref = refs
    else:
        q_ref, kc_ref, vc_ref, sink_ref, o_ref = refs
    tq = q_ref.shape[1]
    lane = lax.broadcasted_iota(jnp.int32, (1, 128), 1)
    low = lane < 64
    kc = kc_ref[0]
    vc = vc_ref[0]
    n_ctx = kc.shape[0]
    if local:
        span = tq + 2 * WINDOW
        start = pl.multiple_of(pl.program_id(1) * tq, tq)
        kl = kl_ref[0, pl.ds(start, span), :]
        vl = vl_ref[0, pl.ds(start, span), :]
        qpos = start + lax.broadcasted_iota(jnp.int32, (tq, span), 0)
        kpos = start - WINDOW + lax.broadcasted_iota(jnp.int32, (tq, span), 1)
        ok = (jnp.abs(qpos - kpos) <= WINDOW) & (kpos >= 0) & (kpos < seq_len)
    heads = {}
    for n in range(WIN_KV_HEADS):
        keep = low if n == 0 else jnp.logical_not(low)
        for g in range(WIN_GROUP):
            hd = n * WIN_GROUP + g
            j, half = hd // 2, hd % 2
            qb = q_ref[0, :, 128 * j:128 * (j + 1)]
            if half != n:
                qb = jnp.concatenate([qb[:, 64:128], qb[:, 0:64]], axis=1)
            qb = jnp.where(keep, qb, jnp.zeros((), BF16))
            sink = sink_ref[hd:hd + 1, 0:1]
            s = _dot_nt(qb, kc)
            if local:
                s = jnp.concatenate([s, jnp.where(ok, _dot_nt(qb, kl), NEG_INF)], axis=1)
            mx = jnp.maximum(jnp.max(s, axis=-1, keepdims=True), sink)
            e = jnp.exp(s - mx)
            den = jnp.sum(e, axis=-1, keepdims=True) + jnp.exp(sink - mx)
            p = e.astype(BF16)
            o = _dot(p[:, 0:n_ctx], vc)
            if local:
                o = o + _dot(p[:, n_ctx:], vl)
            heads[hd] = o / den
    blocks = []
    for j in range(4):
        ev, od = heads[2 * j], heads[2 * j + 1]
        n = (2 * j) // WIN_GROUP
        if n == 1:
            ev = pltpu.roll(ev, 64, axis=1)
        else:
            od = pltpu.roll(od, 64, axis=1)
        blocks.append(jnp.where(low, ev, od))
    o_ref[0] = jnp.concatenate(blocks, axis=1).astype(BF16)


def _win_attention(q, kc, vc, sink, k_lat=None, v_lat=None):
    b, lq, _ = q.shape
    n_ctx = kc.shape[1]
    local = k_lat is not None
    tq = TOKEN_TILE
    in_specs = [pl.BlockSpec((1, tq, 512), lambda bi, i: (bi, i, 0)),
                pl.BlockSpec((1, n_ctx, 128), lambda bi, i: (bi, 0, 0)),
                pl.BlockSpec((1, n_ctx, 128), lambda bi, i: (bi, 0, 0))]
    args = [q, kc, vc]
    if local:
        lp = k_lat.shape[1]
        in_specs += [pl.BlockSpec((1, lp, 128), lambda bi, i: (bi, 0, 0))] * 2
        args += [k_lat, v_lat]
    in_specs.append(pl.BlockSpec((8, 128), lambda bi, i: (0, 0)))
    args.append(sink)
    return pl.pallas_call(
        functools.partial(_win_attn_kernel, local, lq),
        grid=(b, lq // tq), in_specs=in_specs,
        out_specs=pl.BlockSpec((1, tq, 512), lambda bi, i: (bi, i, 0)),
        out_shape=jax.ShapeDtypeStruct((b, lq, 512), BF16),
        compiler_params=_params("arbitrary", "arbitrary"),
        name="win_attention_local" if local else "win_attention",
    )(*args)


def _pool_kernel(u_ref, w_ref, scale_ref, o_ref):
    l = u_ref.shape[1]
    lp = l + 2 * POOL_PAD
    t = lax.broadcasted_iota(jnp.int32, (l, 128), 0)
    zpad = jnp.zeros((POOL_PAD, 128), F32)

    def shifted(a, s):
        return pltpu.roll(a, s, axis=0) + pltpu.roll(a, lp - s, axis=0)

    outs = []
    for gi, wl in enumerate(POOL_WINDOWS):
        x = u_ref[0, :, 128 * gi:128 * (gi + 1)]
        xp = jnp.concatenate([zpad, x, zpad], axis=0)
        acc = xp + pltpu.roll(xp, 1, axis=0)
        if wl >= 4:
            acc = shifted(acc, 1)
        if wl >= 8:
            acc = shifted(acc, 2)
        if wl >= 16:
            acc = shifted(acc, 4)
        win = acc[POOL_PAD:POOL_PAD + l]
        cnt = (jnp.minimum(t + wl // 2, l) - jnp.maximum(t - wl // 2, 0)).astype(F32)
        pooled = win / cnt - x
        outs.append(_dot(pooled.astype(BF16), w_ref[gi]))
    o_ref[0] = (jnp.concatenate(outs, axis=1) * scale_ref[...]).astype(BF16)


def _pool_mixer(u, w, scale):
    b, l, _ = u.shape
    return pl.pallas_call(
        _pool_kernel,
        grid=(b,),
        in_specs=[pl.BlockSpec((1, l, 512), lambda bi: (bi, 0, 0)),
                  pl.BlockSpec(w.shape, lambda bi: (0, 0, 0)),
                  pl.BlockSpec((1, 512), lambda bi: (0, 0))],
        out_specs=pl.BlockSpec((1, l, 512), lambda bi: (bi, 0, 0)),
        out_shape=jax.ShapeDtypeStruct((b, l, 512), BF16),
        compiler_params=_params("arbitrary"),
        name="pool_mixer",
    )(u, w, scale)


def _mixout_even_kernel(out_scale, x_ref, mod_ref, u_ref, yf_ref, yb_ref, oa_ref, sd_ref,
                        wglu_ref, gsub_ref, wout_ref, o_ref):
    mod = mod_ref[0]
    y = sd_ref[...] * u_ref[...] + yf_ref[...] + yb_ref[...]
    g = _gelu(y)
    o_a = g * _sigmoid(_dot(g.astype(BF16), wglu_ref[...]))
    parts = [o_a.astype(BF16)]
    for h in range(DIFF_HEADS):
        blk = oa_ref[0, :, 128 * h:128 * (h + 1)]
        ms = jnp.mean(blk * blk, axis=-1, keepdims=True)
        parts.append(((blk * lax.rsqrt(ms + EPS) * gsub_ref[...]) * out_scale).astype(BF16))
    cat = jnp.concatenate(parts, axis=1)
    o_ref[0] = x_ref[0] + mod[2:3] * _dot(cat, wout_ref[...])


def _mixout_even(x, mod, u_tm2, yf2, yb2, o_attn, s5_d, w_glu, g_sub, w_out, out_scale, per_batch_mod):
    b, l, d = x.shape
    tl = TOKEN_TILE
    mod_map = (lambda bi, i: (bi, 0, 0)) if per_batch_mod else (lambda bi, i: (0, 0, 0))
    const2 = lambda bi, i: (0, 0)
    tm = pl.BlockSpec((tl, 512), lambda bi, i: (i, bi))
    return pl.pallas_call(
        functools.partial(_mixout_even_kernel, out_scale),
        grid=(b, l // tl),
        in_specs=[pl.BlockSpec((1, tl, d), lambda bi, i: (bi, i, 0)),
                  pl.BlockSpec((1, 6, d), mod_map),
                  tm, tm, tm,
                  pl.BlockSpec((1, tl, 512), lambda bi, i: (bi, i, 0)),
                  pl.BlockSpec((1, 512), const2),
                  pl.BlockSpec((512, 512), const2),
                  pl.BlockSpec((1, 128), const2),
                  pl.BlockSpec((d, d), const2)],
        out_specs=pl.BlockSpec((1, tl, d), lambda bi, i: (bi, i, 0)),
        out_shape=jax.ShapeDtypeStruct((b, l, d), F32),
        compiler_params=_params("arbitrary", "arbitrary"),
        name="mixout_even",
    )(x, mod, u_tm2, yf2, yb2, o_attn, s5_d, w_glu, g_sub, w_out)


def _mixout_odd_kernel(x_ref, mod_ref, oc_ref, od_ref, wout_ref, o_ref):
    mod = mod_ref[0]
    cat = jnp.concatenate([oc_ref[0], od_ref[0]], axis=1)
    o_ref[0] = x_ref[0] + mod[2:3] * _dot(cat, wout_ref[...])


def _mixout_odd(x, mod, o_c, o_d, w_out, per_batch_mod):
    b, l, d = x.shape
    tl = TOKEN_TILE
    mod_map = (lambda bi, i: (bi, 0, 0)) if per_batch_mod else (lambda bi, i: (0, 0, 0))
    tok = lambda w_: pl.BlockSpec((1, tl, w_), lambda bi, i: (bi, i, 0))
    return pl.pallas_call(
        _mixout_odd_kernel,
        grid=(b, l // tl),
        in_specs=[tok(d), pl.BlockSpec((1, 6, d), mod_map), tok(512), tok(512),
                  pl.BlockSpec((d, d), lambda bi, i: (0, 0))],
        out_specs=tok(d),
        out_shape=jax.ShapeDtypeStruct((b, l, d), F32),
        compiler_params=_params("arbitrary", "arbitrary"),
        name="mixout_odd",
    )(x, mod, o_c, o_d, w_out)


def _rank16(s, kio, exact):
    n = s.shape[0]
    vals = []
    if exact:
        rank = jnp.full(s.shape, float(PEER_TOPK), F32)
        for r in range(PEER_TOPK):
            m = jnp.max(s, axis=0, keepdims=True)
            ix = jnp.min(jnp.where(s == m, kio, float(n)), axis=0, keepdims=True)
            hit = kio == ix
            rank = jnp.where(hit, float(r), rank)
            s = jnp.where(hit, -jnp.inf, s)
            vals.append(m)
        return vals, rank
    for r in range(PEER_TOPK):
        m = jnp.max(s, axis=0, keepdims=True)
        s = jnp.where(s == m, -RANK_CODE * (1.0 + r / 32.0), s)
        vals.append(m)
    code = s * (-1.0 / RANK_CODE)
    return vals, jnp.where(code >= 1.0, (code - 1.0) * 32.0, float(PEER_TOPK))


def _candidate_positions(tl):
    i = lax.broadcasted_iota(jnp.int32, (PEER_CAND_ROWS, tl), 0).astype(F32)
    mid = i + 8.0 * jnp.floor((i - 16.0) * 0.125)
    return jnp.where(i < 16.0, i, jnp.where(i < 72.0, mid, 16.0 * (i - 64.0)))


def _peer_topk_kernel(x_ref, mod_ref, g_ref, wq_ref, keys_ref, h_ref, n_ref, p1_ref, r2_ref, p2_ref):
    mod = mod_ref[0]
    h = _modulate(x_ref[0], g_ref[...], mod[3:4], mod[4:5]).astype(BF16)
    h_ref[0] = h
    q = _dot(h, wq_ref[...])
    tl = q.shape[0]
    lanes = tl
    kio = lax.broadcasted_iota(jnp.int32, (PEER_NKEYS, lanes), 0).astype(F32)
    pos = _candidate_positions(lanes)

    def head(hd, s1, s2, exact, at):
        s1, s2 = s1[:, at:at + lanes], s2[:, at:at + lanes]
        v1, rank1 = _rank16(s1, kio, exact)
        v2, rank2 = _rank16(s2, kio, exact)
        v1s = jnp.concatenate(v1, axis=0)
        v2s = jnp.concatenate(v2, axis=0)
        cand = jnp.concatenate([v1[0] + v2s] + [v1[a] + v2s[0:8] for a in range(1, 8)]
                               + [v1s[8:16] + v2[0]], axis=0)
        sel_a = []
        for r in range(PEER_TOPK):
            m = jnp.max(cand, axis=0, keepdims=True)
            hit = cand == m
            if exact:
                px = jnp.min(jnp.where(hit, pos, 999.0), axis=0, keepdims=True)
                hit = pos == px
            else:
                px = jnp.max(jnp.where(hit, pos, -1.0), axis=0, keepdims=True)
            cand = jnp.where(hit, -jnp.inf, cand)
            sel_a.append(jnp.floor(px * (1.0 / PEER_TOPK)))
            if r == 0:
                best0, z = m, jnp.ones_like(m)
            else:
                z = z + jnp.exp(m - best0)
        n_e1 = jnp.zeros((PEER_NKEYS, lanes), F32)
        for a in range(PEER_TOPK):
            n_a = jnp.zeros_like(z)
            for r in range(a, PEER_TOPK):
                n_a = n_a + jnp.where(sel_a[r] == float(a), 1.0, 0.0)
            n_e1 = jnp.where(rank1 == float(a), n_a, n_e1)
        n_ref[0, hd, :, at:at + lanes] = n_e1
        p1_ref[0, hd, :, at:at + lanes] = jnp.exp(s1 - v1[0]) / z
        r2_ref[0, hd, :, at:at + lanes] = rank2.astype(BF16)
        p2_ref[0, hd, :, at:at + lanes] = jnp.exp(s2 - v2[0]).astype(BF16)
        ranked = (jnp.sum(jnp.where(rank1 < float(PEER_TOPK), 1.0, 0.0), axis=0, keepdims=True)
                  + jnp.sum(jnp.where(rank2 < float(PEER_TOPK), 1.0, 0.0), axis=0, keepdims=True)
                  + jnp.sum(jnp.where(cand == -jnp.inf, 1.0, 0.0), axis=0, keepdims=True))
        return jnp.max(jnp.abs(ranked - 3.0 * PEER_TOPK))

    def scores(hd):
        return (_dot_nt(keys_ref[2 * hd], q[:, 256 * hd:256 * hd + 128].astype(BF16)),
                _dot_nt(keys_ref[2 * hd + 1], q[:, 256 * hd + 128:256 * hd + 256].astype(BF16)))

    miscounts = []
    for hd in range(PEER_HEADS):
        s1, s2 = scores(hd)
        miscounts.append([head(hd, s1, s2, False, at) for at in range(0, tl, lanes)])
    for hd in range(PEER_HEADS):
        for at, miscount in zip(range(0, tl, lanes), miscounts[hd]):
            @pl.when(miscount > 0.0)
            def _(hd=hd, at=at):
                s1, s2 = scores(hd)
                head(hd, s1, s2, True, at)


def _peer_topk(x, mod, gain, w_q, keys, per_batch_mod):
    b, l, d = x.shape
    tl = TOKEN_TILE
    mod_map = (lambda bi, i: (bi, 0, 0)) if per_batch_mod else (lambda bi, i: (0, 0, 0))
    sel = pl.BlockSpec((1, PEER_HEADS, PEER_NKEYS, tl), lambda bi, i: (bi, 0, 0, i))
    sel_f32 = jax.ShapeDtypeStruct((b, PEER_HEADS, PEER_NKEYS, l), F32)
    sel_bf16 = jax.ShapeDtypeStruct((b, PEER_HEADS, PEER_NKEYS, l), BF16)
    return pl.pallas_call(
        _peer_topk_kernel,
        grid=(b, l // tl),
        in_specs=[pl.BlockSpec((1, tl, d), lambda bi, i: (bi, i, 0)),
                  pl.BlockSpec((1, 6, d), mod_map),
                  pl.BlockSpec((1, d), lambda bi, i: (0, 0)),
                  pl.BlockSpec(w_q.shape, lambda bi, i: (0, 0)),
                  pl.BlockSpec(keys.shape, lambda bi, i: (0, 0, 0))],
        out_specs=[pl.BlockSpec((1, tl, d), lambda bi, i: (bi, i, 0)), sel, sel, sel, sel],
        out_shape=[jax.ShapeDtypeStruct((b, l, d), BF16), sel_f32, sel_f32, sel_bf16, sel_bf16],
        compiler_params=_params("arbitrary", "arbitrary"),
        name="peer_topk",
    )(x, mod, gain, w_q, keys)


def _peer_dense_kernel(x_ref, mod_ref, h_ref, u_ref, vt_ref, n_ref, p1_ref, r2_ref, p2_ref, o_ref,
                       ht_ref, acc_ref, s_ref, gate_ref):
    e = pl.program_id(2)

    @pl.when(e == 0)
    def _():
        ht_ref[...] = h_ref[0].astype(F32).T.astype(BF16)
        acc_ref[...] = jnp.zeros_like(acc_ref)

    ht = ht_ref[...]
    tm = ht.shape[1]
    bounds = [sum(PEER_SUBS[:i]) for i in range(len(PEER_SUBS) + 1)]

    def gate_rows(lo, hi):
        for row in range(lo // PEER_NKEYS, hi // PEER_NKEYS):
            gate = [None] * (PEER_NKEYS // 16)
            for hd in range(PEER_HEADS):
                n_b = jnp.broadcast_to(n_ref[0, hd, row:row + 1, :], (16, tm)).astype(BF16)
                p1_b = jnp.broadcast_to(p1_ref[0, hd, row:row + 1, :], (16, tm)).astype(BF16)
                for k in range(PEER_NKEYS // 16):
                    term = jnp.where(r2_ref[0, hd, 16 * k:16 * (k + 1), :] < n_b,
                                     p2_ref[0, hd, 16 * k:16 * (k + 1), :], jnp.zeros((), BF16)) * p1_b
                    gate[k] = term if gate[k] is None else gate[k] + term
            for k in range(PEER_NKEYS // 16):
                at = PEER_NKEYS * row + 16 * k
                gate_ref[at:at + 16, :] = gate[k]

    for lo, hi in zip(bounds[:-1], bounds[1:]):
        s_ref[lo:hi, :] = _dot(u_ref[lo:hi, :], ht).astype(BF16)
        gate_rows(lo, hi)
    ws = [gate_ref[lo:hi, :] * _gelu(s_ref[lo:hi, :])
          for lo, hi in zip(bounds[:-1], bounds[1:])]
    acc_ref[...] += _dot(vt_ref[...], jnp.concatenate(ws, axis=0))

    @pl.when(e == pl.num_programs(2) - 1)
    def _():
        o_ref[0] = x_ref[0] + mod_ref[0][5:6] * acc_ref[...].T


def _peer_dense(x, mod, h, u_tabs, vt_tabs, layer, n_e1, p1n, rank2, p2, per_batch_mod):
    b, l, d = x.shape
    tm = PEER_TOKENS
    n_exp = u_tabs.shape[1]
    rows = PEER_STEP // PEER_NKEYS
    mod_map = (lambda bi, i, e: (bi, 0, 0)) if per_batch_mod else (lambda bi, i, e: (0, 0, 0))
    tok = lambda: pl.BlockSpec((1, tm, d), lambda bi, i, e: (bi, i, 0))
    by_row = pl.BlockSpec((1, PEER_HEADS, rows, tm), lambda bi, i, e: (bi, 0, e, i))
    by_key = pl.BlockSpec((1, PEER_HEADS, PEER_NKEYS, tm), lambda bi, i, e: (bi, 0, 0, i))
    return pl.pallas_call(
        _peer_dense_kernel,
        grid=(b, l // tm, n_exp // PEER_STEP),
        in_specs=[tok(), pl.BlockSpec((1, 6, d), mod_map), tok(),
                  pl.BlockSpec((None, PEER_STEP, d), lambda bi, i, e: (layer, e, 0)),
                  pl.BlockSpec((None, d, PEER_STEP), lambda bi, i, e: (layer, 0, e)),
                  by_row, by_row, by_key, by_key],
        out_specs=tok(),
        out_shape=jax.ShapeDtypeStruct((b, l, d), F32),
        scratch_shapes=[pltpu.VMEM((d, tm), BF16), pltpu.VMEM((d, tm), F32),
                        pltpu.VMEM((PEER_STEP, tm), BF16), pltpu.VMEM((PEER_STEP, tm), BF16)],
        compiler_params=_params("arbitrary", "arbitrary", "arbitrary"),
        name="peer_dense",
    )(x, mod, h, u_tabs, vt_tabs, n_e1, p1n, rank2, p2)


def _rope_tables(n_lat, width):
    q4 = DIFF_DK // 4
    n_rows = n_lat // GRID_W
    row = jnp.repeat(jnp.arange(n_rows), GRID_W).astype(F32)
    col = jnp.tile(jnp.arange(GRID_W), n_rows).astype(F32)
    freqs = ROPE_BASE ** (-jnp.arange(q4, dtype=F32) / q4)
    ang = jnp.stack([row[:, None] * freqs, col[:, None] * freqs], axis=1)
    cos, sin = jnp.cos(ang), jnp.sin(ang)
    cos64 = jnp.stack([cos, cos], axis=2).reshape(n_lat, 64)
    sin64 = jnp.stack([-sin, sin], axis=2).reshape(n_lat, 64)
    reps = width // 64
    return jnp.tile(cos64, (1, reps)), jnp.tile(sin64, (1, reps))


def _ones_block_diag(width, group):
    idx = jnp.arange(width) // group
    return jnp.where(idx[:, None] == idx[None, :], 1.0 / group, 0.0).astype(BF16)


def _peer_layer(x, mod, W, l, per_batch_mod):
    shape = x.shape
    if not per_batch_mod:
        x = x.reshape(1, -1, shape[-1])
    h, n_e1, p1n, rank2, p2 = _peer_topk(x, mod, W['norm_ffn'][l], W['peer_w_q'][l], W['peer_keys'][l],
                                         per_batch_mod)
    y = _peer_dense(x, mod, h, W['peer_u'], W['peer_vt'], l, n_e1, p1n, rank2, p2, per_batch_mod)
    return y.reshape(shape)


def _even_layer(x, mod, W, l, cache, per_batch_mod):
    b, seq, _ = x.shape
    i = l // 2
    lam_init = 0.8 - 0.6 * math.exp(-0.3 * l)
    rope_tabs = W['rope'] if cache is not None else None
    u_tm2, q, k_norm, k_att, v, v_att = _inproj_even(
        x, mod, W['norm_mix'][l], W['even_w_in'][i], W['ones_bd'], W['diff_q_norm'][i],
        W['diff_k_norm'][i], rope_tabs, per_batch_mod)
    if cache is None:
        h0 = jnp.zeros((2, b, S5_JB * S5_BLK), F32)
        k_all, v_all = k_att, v_att
    else:
        h0 = _s5_state_to_blocks(cache['s5'][:, i])
        k_all = jnp.concatenate([cache['diff_k'][:, i].reshape(b, -1, 512).astype(BF16), k_att], axis=1)
        v_all = jnp.concatenate([cache['diff_v'][:, i].reshape(b, -1, 512).astype(BF16), v_att], axis=1)
    yf, yb, fin = _s5_scan(u_tm2.reshape(seq, b, 512), W['s5_bd'][i], W['s5_cd'][i], W['s5_a'][i], h0)
    o_attn = _diff_attention(q, k_all, v_all, W['diff_lambda'][i], lam_init)
    x = _mixout_even(x, mod, u_tm2, yf.reshape(seq, b * 512), yb.reshape(seq, b * 512), o_attn,
                     W['s5_d'][i], W['s5_w_glu'][i], W['diff_sub_norm'][i], W['even_w_out'][i],
                     1.0 - lam_init, per_batch_mod)
    return x, (k_norm, v, fin)


def _odd_layer(x, mod, W, l, cache, per_batch_mod):
    b, seq, _ = x.shape
    i = l // 2
    rope_tabs = W['rope'] if cache is not None else None
    q, k_norm, k_att, v, v_att, u = _inproj_odd(
        x, mod, W['norm_mix'][l], W['odd_w_in'][i], W['ones_bd'], W['win_q_norm'][i],
        W['win_k_norm'][i], rope_tabs, per_batch_mod)
    if cache is None:
        o_c = _win_attention(q, k_att, v_att, W['win_sink'][i])
    else:
        pad = ((0, 0), (WINDOW, WINDOW), (0, 0))
        o_c = _win_attention(q, cache['win_k'][:, i].reshape(b, -1, 128).astype(BF16),
                             cache['win_v'][:, i].reshape(b, -1, 128).astype(BF16), W['win_sink'][i],
                             jnp.pad(k_att, pad), jnp.pad(v_att, pad))
    o_d = _pool_mixer(u, W['pool_w'][i], W['pool_scale'][i])
    x = _mixout_odd(x, mod, o_c, o_d, W['odd_w_out'][i], per_batch_mod)
    return x, (k_norm, v)


def kernel(x_prompt, x_sample, cache_diff_k, cache_diff_v, state_s5, cache_win_k, cache_win_v, c, c_ctx, ada_w, ada_b, norm_mix, norm_ffn, even_w_in, even_w_out, s5_lam_re, s5_lam_im, s5_log_step, s5_b_re, s5_b_im, s5_c_re, s5_c_im, s5_d, s5_w_glu, diff_q_norm, diff_k_norm, diff_lambda, diff_sub_norm, odd_w_in, odd_w_out, win_q_norm, win_k_norm, win_sink, pool_w, pool_scale, peer_w_q, peer_sub_keys, peer_u, peer_v):
    depth = ada_w.shape[0]
    n_even, n_odd = even_w_in.shape[0], odd_w_in.shape[0]
    bsz, seq, d = x_prompt.shape
    dec_b, dec_l, _ = x_sample.shape

    cvec = jnp.zeros((16, d), F32).at[0].set(c_ctx).at[1:1 + dec_b].set(c)
    mod = _modulation(cvec, ada_w, ada_b)

    s5 = [_s5_tables(s5_lam_re[i], s5_lam_im[i], s5_log_step[i], s5_b_re[i], s5_b_im[i],
                     s5_c_re[i], s5_c_im[i]) for i in range(n_even)]
    W = dict(
        norm_mix=norm_mix.reshape(depth, 1, d), norm_ffn=norm_ffn.reshape(depth, 1, d),
        even_w_in=even_w_in.astype(BF16), even_w_out=even_w_out.astype(BF16),
        odd_w_in=odd_w_in.astype(BF16), odd_w_out=odd_w_out.astype(BF16),
        ones_bd=_ones_block_diag(512, 64),
        diff_q_norm=jnp.tile(diff_q_norm, (1, 8)).reshape(n_even, 1, 512),
        diff_k_norm=jnp.tile(diff_k_norm, (1, 8)).reshape(n_even, 1, 512),
        diff_lambda=diff_lambda, diff_sub_norm=diff_sub_norm.reshape(n_even, 1, DIFF_DV),
        s5_bd=[t[0] for t in s5], s5_cd=[t[1] for t in s5], s5_a=[t[2] for t in s5],
        s5_d=s5_d.reshape(n_even, 1, 512), s5_w_glu=s5_w_glu.astype(BF16),
        win_q_norm=jnp.tile(win_q_norm, (1, 8)).reshape(n_odd, 1, 512),
        win_k_norm=jnp.tile(win_k_norm, (1, 2)).reshape(n_odd, 1, 128),
        win_sink=jnp.broadcast_to(win_sink[:, :, None], (n_odd, 8, 128)),
        pool_w=pool_w.astype(BF16), pool_scale=pool_scale.reshape(n_odd, 1, 512),
        peer_w_q=peer_w_q.astype(BF16),
        peer_keys=peer_sub_keys.astype(BF16).reshape(depth, 2 * PEER_HEADS, PEER_NKEYS, -1),
        peer_u=peer_u.astype(BF16),
        peer_vt=jnp.swapaxes(peer_v, 1, 2).astype(BF16),
        rope=_rope_tables(dec_l, 512),
    )
    cache = {'diff_k': cache_diff_k, 'diff_v': cache_diff_v, 's5': state_s5,
             'win_k': cache_win_k, 'win_v': cache_win_v}

    def run(x, mods, cch, per_batch_mod):
        states = []
        for l in range(depth):
            if l % 2 == 0:
                x, st = _even_layer(x, mods[l], W, l, cch, per_batch_mod)
            else:
                x, st = _odd_layer(x, mods[l], W, l, cch, per_batch_mod)
            states.append(st)
            x = _peer_layer(x, mods[l], W, l, per_batch_mod)
        return x, states

    ctx_mods = [mod[l, 0:1].reshape(1, 6, d) for l in range(depth)]
    dec_mods = [mod[l, 1:1 + dec_b].reshape(dec_b, 6, d) for l in range(depth)]
    y_prompt, states = run(x_prompt, ctx_mods, None, False)
    y_sample, _ = run(x_sample, dec_mods, cache, True)

    evens = [states[l] for l in range(depth) if l % 2 == 0]
    odds = [states[l] for l in range(depth) if l % 2 == 1]
    new_diff_k = jnp.stack([s[0].reshape(bsz, seq, 2, DIFF_HEADS, DIFF_DK) for s in evens], axis=1)
    new_diff_v = jnp.stack([s[1].reshape(bsz, seq, DIFF_HEADS, DIFF_DV) for s in evens], axis=1)
    new_s5 = jnp.stack([_s5_blocks_to_state(s[2]) for s in evens], axis=1)
    new_win_k = jnp.stack([s[0].reshape(bsz, seq, WIN_KV_HEADS, WIN_DH) for s in odds], axis=1)
    new_win_v = jnp.stack([s[1].reshape(bsz, seq, WIN_KV_HEADS, WIN_DH) for s in odds], axis=1)
    return (y_prompt, y_sample, new_diff_k, new_diff_v, new_s5, new_win_k, new_win_v)
```

```python
import functools
import math

import jax
import jax.numpy as jnp
from jax import lax
from jax.experimental import pallas as pl
from jax.experimental.pallas import tpu as pltpu

F32 = jnp.float32
BF16 = jnp.bfloat16

D_MODEL = 1024
MIX_HALF = 512
GRID_W = 64
EPS = 1e-6
NEG_INF = -1e30
ROPE_BASE = 10000.0

S5_GROUPS = 32
S5_GROUP = 16
S5_STATE = 64
S5_JB = 4
S5_BLK = 1024
S5_CHUNK = 32
S5_BATCH = 8

DIFF_HEADS = 4
DIFF_DK = 64
DIFF_DV = 128
WIN_KV_HEADS = 2
WIN_GROUP = 4
WIN_DH = 64
ATTN_SCALE = DIFF_DK ** -0.5
WINDOW = 128
POOL_WINDOWS = (2, 4, 8, 16)
POOL_PAD = 8

PEER_HEADS = 8
PEER_NKEYS = 128
PEER_TOPK = 16
PEER_CAND_UNUSED = 6
RANK_CODE = 2.0 ** 100
PEER_SUBS = (128, 384, 512, 1024)
PEER_STEP = sum(PEER_SUBS)
PEER_TOKENS = 512

TOKEN_TILE = 256
VMEM_LIMIT = 56 * 1024 * 1024


def _params(*sem):
    return pltpu.CompilerParams(dimension_semantics=sem, vmem_limit_bytes=VMEM_LIMIT)


def _dot(a, b):
    return jnp.dot(a, b, preferred_element_type=F32)


def _dot_nt(a, b):
    return lax.dot_general(a, b, (((1,), (1,)), ((), ())), preferred_element_type=F32)


def _sigmoid(x):
    return 1.0 / (1.0 + jnp.exp(-x))


def _gelu(x):
    c = math.sqrt(2.0 / math.pi)
    hx = 0.5 * x
    return hx + hx * jnp.tanh(x * (c + (c * 0.044715) * (x * x)))


def _modulate(x, gain, shift, scale):
    ms = jnp.mean(x * x, axis=-1, keepdims=True)
    return (x * lax.rsqrt(ms + EPS) * gain) * (1.0 + scale) + shift


def _group_rms(z, ones_bd, gain):
    ms = _dot((z * z).astype(BF16), ones_bd)
    return z * lax.rsqrt(ms + EPS) * gain


def _rope(z, cos_t, sin_t):
    lane = lax.broadcasted_iota(jnp.int32, (1, 128), 1) % 32
    first = lane < 16
    parts = []
    for j in range(z.shape[1] // 128):
        c = z[:, 128 * j:128 * (j + 1)]
        parts.append(jnp.where(first, pltpu.roll(c, 112, axis=1), pltpu.roll(c, 16, axis=1)))
    swapped = parts[0] if len(parts) == 1 else jnp.concatenate(parts, axis=1)
    return z * cos_t + swapped * sin_t


def _mod_kernel(c_ref, w_ref, b_ref, o_ref):
    c = c_ref[...]
    s = c * _sigmoid(c)
    o_ref[0] = jnp.dot(s, w_ref[0], precision=lax.Precision.HIGHEST,
                       preferred_element_type=F32) + b_ref[0]


def _modulation(cvec, ada_w, ada_b):
    depth, d, n = ada_w.shape
    tn = 1536
    return pl.pallas_call(
        _mod_kernel,
        grid=(depth, n // tn),
        in_specs=[pl.BlockSpec((16, d), lambda l, j: (0, 0)),
                  pl.BlockSpec((1, d, tn), lambda l, j: (l, 0, j)),
                  pl.BlockSpec((1, 1, tn), lambda l, j: (l, 0, j))],
        out_specs=pl.BlockSpec((1, 16, tn), lambda l, j: (l, 0, j)),
        out_shape=jax.ShapeDtypeStruct((depth, 16, n), F32),
        compiler_params=_params("arbitrary", "arbitrary"),
        name="adaln_mod",
    )(cvec, ada_w, ada_b.reshape(depth, 1, n))


def _inproj_even_kernel(rope, *refs):
    if rope:
        (x_ref, mod_ref, g_ref, w_ref, bd_ref, gq_ref, gk_ref, cos_ref, sin_ref,
         u_ref, q_ref, kn_ref, ka_ref, v_ref, va_ref) = refs
    else:
        (x_ref, mod_ref, g_ref, w_ref, bd_ref, gq_ref, gk_ref,
         u_ref, q_ref, kn_ref, ka_ref, v_ref, va_ref) = refs
    mod = mod_ref[0]
    h = _modulate(x_ref[0], g_ref[...], mod[0:1], mod[1:2])
    z = _dot(h.astype(BF16), w_ref[...])
    u = z[:, 0:512]
    q = _group_rms(z[:, 512:1024], bd_ref[...], gq_ref[...])
    k = _group_rms(z[:, 1024:1536], bd_ref[...], gk_ref[...])
    v = z[:, 1536:2048]
    u_ref[...] = u
    kn_ref[0] = k
    v_ref[0] = v
    va_ref[0] = v.astype(BF16)
    if rope:
        q = _rope(q, cos_ref[...], sin_ref[...])
        k = _rope(k, cos_ref[...], sin_ref[...])
    q_ref[0] = (q * ATTN_SCALE).astype(BF16)
    ka_ref[0] = k.astype(BF16)


def _inproj_even(x, mod, gain, w, ones_bd, gq, gk, rope_tabs, per_batch_mod):
    b, l, d = x.shape
    tl = TOKEN_TILE
    rope = rope_tabs is not None
    mod_map = (lambda bi, i: (bi, 0, 0)) if per_batch_mod else (lambda bi, i: (0, 0, 0))
    const2 = lambda bi, i: (0, 0)
    tok = lambda w_: pl.BlockSpec((1, tl, w_), lambda bi, i: (bi, i, 0))
    in_specs = [tok(d),
                pl.BlockSpec((1, 6, d), mod_map),
                pl.BlockSpec((1, d), const2),
                pl.BlockSpec(w.shape, const2),
                pl.BlockSpec(ones_bd.shape, const2),
                pl.BlockSpec((1, 512), const2),
                pl.BlockSpec((1, 512), const2)]
    args = [x, mod, gain, w, ones_bd, gq, gk]
    if rope:
        in_specs += [pl.BlockSpec((tl, 512), lambda bi, i: (i, 0))] * 2
        args += list(rope_tabs)
    out_shape = [jax.ShapeDtypeStruct((l, b * 512), F32),
                 jax.ShapeDtypeStruct((b, l, 512), BF16),
                 jax.ShapeDtypeStruct((b, l, 512), F32),
                 jax.ShapeDtypeStruct((b, l, 512), BF16),
                 jax.ShapeDtypeStruct((b, l, 512), F32),
                 jax.ShapeDtypeStruct((b, l, 512), BF16)]
    out_specs = [pl.BlockSpec((tl, 512), lambda bi, i: (i, bi)),
                 tok(512), tok(512), tok(512), tok(512), tok(512)]
    return pl.pallas_call(
        functools.partial(_inproj_even_kernel, rope),
        grid=(b, l // tl), in_specs=in_specs, out_specs=out_specs, out_shape=out_shape,
        compiler_params=_params("arbitrary", "arbitrary"),
        name="inproj_even_rope" if rope else "inproj_even",
    )(*args)


def _inproj_odd_kernel(rope, *refs):
    if rope:
        (x_ref, mod_ref, g_ref, w_ref, bd_ref, gq_ref, gk_ref, cos_ref, sin_ref,
         q_ref, kn_ref, ka_ref, v_ref, va_ref, u_ref) = refs
    else:
        (x_ref, mod_ref, g_ref, w_ref, bd_ref, gq_ref, gk_ref,
         q_ref, kn_ref, ka_ref, v_ref, va_ref, u_ref) = refs
    mod = mod_ref[0]
    h = _modulate(x_ref[0], g_ref[...], mod[0:1], mod[1:2])
    z = _dot(h.astype(BF16), w_ref[...])
    bd = bd_ref[...]
    q = _group_rms(z[:, 0:512], bd, gq_ref[...])
    k = _group_rms(z[:, 512:640], bd[0:128, 0:128], gk_ref[...])
    v = z[:, 640:768]
    kn_ref[0] = k
    v_ref[0] = v
    va_ref[0] = v.astype(BF16)
    u_ref[0] = z[:, 768:1280]
    if rope:
        q = _rope(q, cos_ref[...], sin_ref[...])
        k = _rope(k, cos_ref[:, 0:128], sin_ref[:, 0:128])
    q_ref[0] = (q * ATTN_SCALE).astype(BF16)
    ka_ref[0] = k.astype(BF16)


def _inproj_odd(x, mod, gain, w, ones_bd, gq, gk, rope_tabs, per_batch_mod):
    b, l, d = x.shape
    tl = TOKEN_TILE
    rope = rope_tabs is not None
    mod_map = (lambda bi, i: (bi, 0, 0)) if per_batch_mod else (lambda bi, i: (0, 0, 0))
    const2 = lambda bi, i: (0, 0)
    tok = lambda w_: pl.BlockSpec((1, tl, w_), lambda bi, i: (bi, i, 0))
    in_specs = [tok(d),
                pl.BlockSpec((1, 6, d), mod_map),
                pl.BlockSpec((1, d), const2),
                pl.BlockSpec(w.shape, const2),
                pl.BlockSpec(ones_bd.shape, const2),
                pl.BlockSpec((1, 512), const2),
                pl.BlockSpec((1, 128), const2)]
    args = [x, mod, gain, w, ones_bd, gq, gk]
    if rope:
        in_specs += [pl.BlockSpec((tl, 512), lambda bi, i: (i, 0))] * 2
        args += list(rope_tabs)
    out_shape = [jax.ShapeDtypeStruct((b, l, 512), BF16),
                 jax.ShapeDtypeStruct((b, l, 128), F32),
                 jax.ShapeDtypeStruct((b, l, 128), BF16),
                 jax.ShapeDtypeStruct((b, l, 128), F32),
                 jax.ShapeDtypeStruct((b, l, 128), BF16),
                 jax.ShapeDtypeStruct((b, l, 512), F32)]
    out_specs = [tok(512), tok(128), tok(128), tok(128), tok(128), tok(512)]
    return pl.pallas_call(
        functools.partial(_inproj_odd_kernel, rope),
        grid=(b, l // tl), in_specs=in_specs, out_specs=out_specs, out_shape=out_shape,
        compiler_params=_params("arbitrary", "arbitrary"),
        name="inproj_odd_rope" if rope else "inproj_odd",
    )(*args)


def _s5_kernel(uf_ref, ub_ref, bd_ref, cd_ref, a_ref, h0_ref, yf_ref, yb_ref, fin_ref,
               bu_ref, carry_ref):
    c = pl.program_id(1)
    rows = S5_CHUNK * S5_BATCH

    @pl.when(c == 0)
    def _():
        carry_ref[...] = h0_ref[...]

    for d in range(2):
        u = (uf_ref if d == 0 else ub_ref)[...].reshape(rows, MIX_HALF).astype(BF16)
        for j in range(S5_JB):
            bu_ref[d, :, S5_BLK * j:S5_BLK * (j + 1)] = _dot(u[:, 128 * j:128 * (j + 1)], bd_ref[d, j])
        for j in range(S5_JB):
            lo = S5_BLK * j
            a_re = jnp.broadcast_to(a_ref[d, 0, j:j + 1, :], (S5_BATCH, 512))
            a_im = jnp.broadcast_to(a_ref[d, 1, j:j + 1, :], (S5_BATCH, 512))

            h_re, h_im = carry_ref[d, :, lo:lo + 512], carry_ref[d, :, lo + 512:lo + 1024]
            for s in range(S5_CHUNK):
                row = (s if d == 0 else S5_CHUNK - 1 - s) * S5_BATCH
                b_re = bu_ref[d, row:row + S5_BATCH, lo:lo + 512]
                b_im = bu_ref[d, row:row + S5_BATCH, lo + 512:lo + 1024]
                h_re, h_im = a_re * h_re - a_im * h_im + b_re, a_re * h_im + a_im * h_re + b_im
                bu_ref[d, row:row + S5_BATCH, lo:lo + 512] = h_re
                bu_ref[d, row:row + S5_BATCH, lo + 512:lo + 1024] = h_im
            carry_ref[d, :, lo:lo + 512] = h_re
            carry_ref[d, :, lo + 512:lo + 1024] = h_im
        ys = [_dot(bu_ref[d, :, S5_BLK * j:S5_BLK * (j + 1)].astype(BF16), cd_ref[d, j])
              for j in range(S5_JB)]
        y = jnp.concatenate(ys, axis=1).reshape(S5_CHUNK, S5_BATCH, MIX_HALF)
        if d == 0:
            yf_ref[...] = y
        else:
            yb_ref[...] = y

    @pl.when(c == pl.num_programs(1) - 1)
    def _():
        fin_ref[...] = carry_ref[...]


def _s5_scan(u_tm, bd, cd, acoef, h0):
    l, b, _ = u_tm.shape
    nc = l // S5_CHUNK
    ng = b // S5_BATCH
    state_w = S5_JB * S5_BLK
    blk = (S5_CHUNK, S5_BATCH, MIX_HALF)
    full = lambda a: pl.BlockSpec(a.shape, lambda g, c: (0,) * a.ndim)
    return pl.pallas_call(
        _s5_kernel,
        grid=(ng, nc),
        in_specs=[pl.BlockSpec(blk, lambda g, c: (c, g, 0)),
                  pl.BlockSpec(blk, lambda g, c: (nc - 1 - c, g, 0)),
                  full(bd), full(cd), full(acoef),
                  pl.BlockSpec((2, S5_BATCH, state_w), lambda g, c: (0, g, 0))],
        out_specs=[pl.BlockSpec(blk, lambda g, c: (c, g, 0)),
                   pl.BlockSpec(blk, lambda g, c: (nc - 1 - c, g, 0)),
                   pl.BlockSpec((2, S5_BATCH, state_w), lambda g, c: (0, g, 0))],
        out_shape=[jax.ShapeDtypeStruct((l, b, MIX_HALF), F32),
                   jax.ShapeDtypeStruct((l, b, MIX_HALF), F32),
                   jax.ShapeDtypeStruct((2, b, state_w), F32)],
        scratch_shapes=[pltpu.VMEM((2, S5_CHUNK * S5_BATCH, state_w), F32),
                        pltpu.VMEM((2, S5_BATCH, state_w), F32)],
        compiler_params=_params("arbitrary", "arbitrary"),
        name="s5_scan",
    )(u_tm, u_tm, bd, cd, acoef, h0)


def _s5_tables(lam_re, lam_im, log_step, b_re, b_im, c_re, c_im):
    step = jnp.exp(log_step.astype(F32))[..., None]
    lr, li = lam_re.astype(F32), lam_im.astype(F32)
    er = jnp.exp(lr * step)
    a_re, a_im = er * jnp.cos(li * step), er * jnp.sin(li * step)
    den = lr * lr + li * li
    q_re = ((a_re - 1.0) * lr + a_im * li) / den
    q_im = (a_im * lr - (a_re - 1.0) * li) / den
    bb_re = q_re[..., None] * b_re - q_im[..., None] * b_im
    bb_im = q_re[..., None] * b_im + q_im[..., None] * b_re
    eye = jnp.eye(8, dtype=F32)
    bb = jnp.stack([bb_re, bb_im], axis=1).reshape(2, 2, S5_JB, 8, S5_STATE, S5_GROUP)
    bd = jnp.einsum('drjgpc,gh->djgcrhp', bb, eye).reshape(2, S5_JB, 128, S5_BLK).astype(BF16)
    cc = jnp.stack([c_re, -c_im], axis=1).astype(F32).reshape(2, 2, S5_JB, 8, S5_GROUP, S5_STATE)
    cd = jnp.einsum('drjgcp,gh->djrgphc', cc, eye).reshape(2, S5_JB, S5_BLK, 128).astype(BF16)
    acoef = jnp.stack([a_re, a_im], axis=1).reshape(2, 2, S5_JB, 512)
    return bd, cd, acoef


def _s5_state_to_blocks(st):
    b = st.shape[0]
    st = st.astype(F32).reshape(b, 2, 2, S5_JB, 8, S5_STATE)
    return jnp.transpose(st, (1, 0, 3, 2, 4, 5)).reshape(2, b, S5_JB * S5_BLK)


def _s5_blocks_to_state(fin):
    b = fin.shape[1]
    fin = fin.reshape(2, b, S5_JB, 2, 8, S5_STATE)
    return jnp.transpose(fin, (1, 0, 3, 2, 4, 5)).reshape(b, 2, 2, S5_GROUPS, S5_STATE)


def _diff_attn_kernel(lam_init, q_ref, k_ref, v_ref, lv_ref, o_ref):
    lv = lv_ref[...]
    lam = (jnp.exp(jnp.sum(lv[0:1] * lv[1:2], axis=-1, keepdims=True))
           - jnp.exp(jnp.sum(lv[2:3] * lv[3:4], axis=-1, keepdims=True)) + lam_init)
    lane = lax.broadcasted_iota(jnp.int32, (1, 128), 1)
    outs = []
    for h in range(DIFF_HEADS):
        es, sums = [], []
        for m in range(2):
            j = m * 2 + h // 2
            keep = (lane < 64) if h % 2 == 0 else (lane >= 64)
            qb = jnp.where(keep, q_ref[0, :, 128 * j:128 * (j + 1)], jnp.zeros((), BF16))
            s = _dot_nt(qb, k_ref[0, :, 128 * j:128 * (j + 1)])
            e = jnp.exp(s - jnp.max(s, axis=-1, keepdims=True))
            es.append(e)
            sums.append(jnp.sum(e, axis=-1, keepdims=True))
        a = es[0] - (lam * sums[0] / sums[1]) * es[1]
        outs.append(_dot(a.astype(BF16), v_ref[0, :, 128 * h:128 * (h + 1)]) / sums[0])
    o_ref[0] = jnp.concatenate(outs, axis=1)


def _diff_attention(q, k, v, lv, lam_init):
    b, lq, _ = q.shape
    lk = k.shape[1]
    tq = TOKEN_TILE
    return pl.pallas_call(
        functools.partial(_diff_attn_kernel, lam_init),
        grid=(b, lq // tq),
        in_specs=[pl.BlockSpec((1, tq, 512), lambda bi, i: (bi, i, 0)),
                  pl.BlockSpec((1, lk, 512), lambda bi, i: (bi, 0, 0)),
                  pl.BlockSpec((1, lk, 512), lambda bi, i: (bi, 0, 0)),
                  pl.BlockSpec((4, DIFF_DK), lambda bi, i: (0, 0))],
        out_specs=pl.BlockSpec((1, tq, 512), lambda bi, i: (bi, i, 0)),
        out_shape=jax.ShapeDtypeStruct((b, lq, 512), F32),
        compiler_params=_params("arbitrary", "arbitrary"),
        name="diff_attention",
    )(q, k, v, lv)


def _win_attn_kernel(local, seq_len, *refs):
    if local:
        q_ref, kc_ref, vc_ref, kl_ref, vl_ref, sink_ref, o_ref = refs
    else:
        q_ref, kc_ref, vc_ref, sink_ref, o_ref = refs
    tq = q_ref.shape[1]
    lane = lax.broadcasted_iota(jnp.int32, (1, 128), 1)
    low = lane < 64
    kc = kc_ref[0]
    vc = vc_ref[0]
    n_ctx = kc.shape[0]
    if local:
        span = tq + 2 * WINDOW
        start = pl.multiple_of(pl.program_id(1) * tq, tq)
        kl = kl_ref[0, pl.ds(start, span), :]
        vl = vl_ref[0, pl.ds(start, span), :]
        qpos = start + lax.broadcasted_iota(jnp.int32, (tq, span), 0)
        kpos = start - WINDOW + lax.broadcasted_iota(jnp.int32, (tq, span), 1)
        ok = (jnp.abs(qpos - kpos) <= WINDOW) & (kpos >= 0) & (kpos < seq_len)
    heads = {}
    for n in range(WIN_KV_HEADS):
        keep = low if n == 0 else jnp.logical_not(low)
        for g in range(WIN_GROUP):
            hd = n * WIN_GROUP + g
            j, half = hd // 2, hd % 2
            qb = q_ref[0, :, 128 * j:128 * (j + 1)]
            if half != n:
                qb = jnp.concatenate([qb[:, 64:128], qb[:, 0:64]], axis=1)
            qb = jnp.where(keep, qb, jnp.zeros((), BF16))
            sink = sink_ref[hd:hd + 1, 0:1]
            s = _dot_nt(qb, kc)
            if local:
                s = jnp.concatenate([s, jnp.where(ok, _dot_nt(qb, kl), NEG_INF)], axis=1)
            mx = jnp.maximum(jnp.max(s, axis=-1, keepdims=True), sink)
            e = jnp.exp(s - mx)
            den = jnp.sum(e, axis=-1, keepdims=True) + jnp.exp(sink - mx)
            p = e.astype(BF16)
            o = _dot(p[:, 0:n_ctx], vc)
            if local:
                o = o + _dot(p[:, n_ctx:], vl)
            heads[hd] = o / den
    blocks = []
    for j in range(4):
        ev, od = heads[2 * j], heads[2 * j + 1]
        n = (2 * j) // WIN_GROUP
        if n == 1:
            ev = pltpu.roll(ev, 64, axis=1)
        else:
            od = pltpu.roll(od, 64, axis=1)
        blocks.append(jnp.where(low, ev, od))
    o_ref[0] = jnp.concatenate(blocks, axis=1).astype(BF16)


def _win_attention(q, kc, vc, sink, k_lat=None, v_lat=None):
    b, lq, _ = q.shape
    n_ctx = kc.shape[1]
    local = k_lat is not None
    tq = TOKEN_TILE
    in_specs = [pl.BlockSpec((1, tq, 512), lambda bi, i: (bi, i, 0)),
                pl.BlockSpec((1, n_ctx, 128), lambda bi, i: (bi, 0, 0)),
                pl.BlockSpec((1, n_ctx, 128), lambda bi, i: (bi, 0, 0))]
    args = [q, kc, vc]
    if local:
        lp = k_lat.shape[1]
        in_specs += [pl.BlockSpec((1, lp, 128), lambda bi, i: (bi, 0, 0))] * 2
        args += [k_lat, v_lat]
    in_specs.append(pl.BlockSpec((8, 128), lambda bi, i: (0, 0)))
    args.append(sink)
    return pl.pallas_call(
        functools.partial(_win_attn_kernel, local, lq),
        grid=(b, lq // tq), in_specs=in_specs,
        out_specs=pl.BlockSpec((1, tq, 512), lambda bi, i: (bi, i, 0)),
        out_shape=jax.ShapeDtypeStruct((b, lq, 512), BF16),
        compiler_params=_params("arbitrary", "arbitrary"),
        name="win_attention_local" if local else "win_attention",
    )(*args)


def _pool_kernel(u_ref, w_ref, scale_ref, o_ref):
    l = u_ref.shape[1]
    lp = l + 2 * POOL_PAD
    t = lax.broadcasted_iota(jnp.int32, (l, 128), 0)
    zpad = jnp.zeros((POOL_PAD, 128), F32)

    def shifted(a, s):
        return pltpu.roll(a, s, axis=0) + pltpu.roll(a, lp - s, axis=0)

    outs = []
    for gi, wl in enumerate(POOL_WINDOWS):
        x = u_ref[0, :, 128 * gi:128 * (gi + 1)]
        xp = jnp.concatenate([zpad, x, zpad], axis=0)
        acc = xp + pltpu.roll(xp, 1, axis=0)
        if wl >= 4:
            acc = shifted(acc, 1)
        if wl >= 8:
            acc = shifted(acc, 2)
        if wl >= 16:
            acc = shifted(acc, 4)
        win = acc[POOL_PAD:POOL_PAD + l]
        cnt = (jnp.minimum(t + wl // 2, l) - jnp.maximum(t - wl // 2, 0)).astype(F32)
        pooled = win / cnt - x
        outs.append(_dot(pooled.astype(BF16), w_ref[gi]))
    o_ref[0] = (jnp.concatenate(outs, axis=1) * scale_ref[...]).astype(BF16)


def _pool_mixer(u, w, scale):
    b, l, _ = u.shape
    return pl.pallas_call(
        _pool_kernel,
        grid=(b,),
        in_specs=[pl.BlockSpec((1, l, 512), lambda bi: (bi, 0, 0)),
                  pl.BlockSpec(w.shape, lambda bi: (0, 0, 0)),
                  pl.BlockSpec((1, 512), lambda bi: (0, 0))],
        out_specs=pl.BlockSpec((1, l, 512), lambda bi: (bi, 0, 0)),
        out_shape=jax.ShapeDtypeStruct((b, l, 512), BF16),
        compiler_params=_params("arbitrary"),
        name="pool_mixer",
    )(u, w, scale)


def _mixout_even_kernel(out_scale, x_ref, mod_ref, u_ref, yf_ref, yb_ref, oa_ref, sd_ref,
                        wglu_ref, gsub_ref, wout_ref, o_ref):
    mod = mod_ref[0]
    y = sd_ref[...] * u_ref[...] + yf_ref[...] + yb_ref[...]
    g = _gelu(y)
    o_a = g * _sigmoid(_dot(g.astype(BF16), wglu_ref[...]))
    parts = [o_a.astype(BF16)]
    for h in range(DIFF_HEADS):
        blk = oa_ref[0, :, 128 * h:128 * (h + 1)]
        ms = jnp.mean(blk * blk, axis=-1, keepdims=True)
        parts.append(((blk * lax.rsqrt(ms + EPS) * gsub_ref[...]) * out_scale).astype(BF16))
    cat = jnp.concatenate(parts, axis=1)
    o_ref[0] = x_ref[0] + mod[2:3] * _dot(cat, wout_ref[...])


def _mixout_even(x, mod, u_tm2, yf2, yb2, o_attn, s5_d, w_glu, g_sub, w_out, out_scale, per_batch_mod):
    b, l, d = x.shape
    tl = TOKEN_TILE
    mod_map = (lambda bi, i: (bi, 0, 0)) if per_batch_mod else (lambda bi, i: (0, 0, 0))
    const2 = lambda bi, i: (0, 0)
    tm = pl.BlockSpec((tl, 512), lambda bi, i: (i, bi))
    return pl.pallas_call(
        functools.partial(_mixout_even_kernel, out_scale),
        grid=(b, l // tl),
        in_specs=[pl.BlockSpec((1, tl, d), lambda bi, i: (bi, i, 0)),
                  pl.BlockSpec((1, 6, d), mod_map),
                  tm, tm, tm,
                  pl.BlockSpec((1, tl, 512), lambda bi, i: (bi, i, 0)),
                  pl.BlockSpec((1, 512), const2),
                  pl.BlockSpec((512, 512), const2),
                  pl.BlockSpec((1, 128), const2),
                  pl.BlockSpec((d, d), const2)],
        out_specs=pl.BlockSpec((1, tl, d), lambda bi, i: (bi, i, 0)),
        out_shape=jax.ShapeDtypeStruct((b, l, d), F32),
        compiler_params=_params("arbitrary", "arbitrary"),
        name="mixout_even",
    )(x, mod, u_tm2, yf2, yb2, o_attn, s5_d, w_glu, g_sub, w_out)


def _mixout_odd_kernel(x_ref, mod_ref, oc_ref, od_ref, wout_ref, o_ref):
    mod = mod_ref[0]
    cat = jnp.concatenate([oc_ref[0], od_ref[0]], axis=1)
    o_ref[0] = x_ref[0] + mod[2:3] * _dot(cat, wout_ref[...])


def _mixout_odd(x, mod, o_c, o_d, w_out, per_batch_mod):
    b, l, d = x.shape
    tl = TOKEN_TILE
    mod_map = (lambda bi, i: (bi, 0, 0)) if per_batch_mod else (lambda bi, i: (0, 0, 0))
    tok = lambda w_: pl.BlockSpec((1, tl, w_), lambda bi, i: (bi, i, 0))
    return pl.pallas_call(
        _mixout_odd_kernel,
        grid=(b, l // tl),
        in_specs=[tok(d), pl.BlockSpec((1, 6, d), mod_map), tok(512), tok(512),
                  pl.BlockSpec((d, d), lambda bi, i: (0, 0))],
        out_specs=tok(d),
        out_shape=jax.ShapeDtypeStruct((b, l, d), F32),
        compiler_params=_params("arbitrary", "arbitrary"),
        name="mixout_odd",
    )(x, mod, o_c, o_d, w_out)


def _rank16(s, kio, exact):
    n = s.shape[0]
    vals = []
    if exact:
        rank = jnp.full(s.shape, float(PEER_TOPK), F32)
        for r in range(PEER_TOPK):
            m = jnp.max(s, axis=0, keepdims=True)
            ix = jnp.min(jnp.where(s == m, kio, float(n)), axis=0, keepdims=True)
            hit = kio == ix
            rank = jnp.where(hit, float(r), rank)
            s = jnp.where(hit, -jnp.inf, s)
            vals.append(m)
        return vals, rank
    for r in range(PEER_TOPK):
        m = jnp.max(s, axis=0, keepdims=True)
        s = jnp.where(s == m, -RANK_CODE * (1.0 + r / 32.0), s)
        vals.append(m)
    code = s * (-1.0 / RANK_CODE)
    return vals, jnp.where(code >= 1.0, (code - 1.0) * 32.0, float(PEER_TOPK))


def _candidate_rows(v1, v2, j):
    v1hi = jnp.concatenate(v1[8:16], axis=0)
    v2lo = jnp.concatenate(v2[0:8], axis=0)
    v2hi = jnp.concatenate(v2[8:16], axis=0)
    g3_a = jnp.where(j < 5.0, v1[2], v1[4])
    g3_b = jnp.where(j < 5.0, v2lo, pltpu.roll(v2lo, 5, axis=0))
    g4_a = jnp.where(j < 4.0, v1[3], jnp.where(j < 6.0, v1[5], v1[6]))
    g4_b = jnp.where(j < 4.0, v2lo, jnp.where(j < 6.0, pltpu.roll(v2lo, 4, axis=0), pltpu.roll(v2lo, 6, axis=0)))
    return jnp.concatenate([v1[0] + v2lo, v1[0] + v2hi, v1[1] + v2lo, g3_a + g3_b, g4_a + g4_b,
                            jnp.where(j < 2.0, v1[7] + v2lo, -jnp.inf), v1hi + v2[0]], axis=0)


def _candidate_positions(j):
    return jnp.concatenate([j, 8.0 + j, 16.0 + j,
                            jnp.where(j < 5.0, 32.0 + j, 59.0 + j),
                            jnp.where(j < 4.0, 48.0 + j, jnp.where(j < 6.0, 76.0 + j, 90.0 + j)),
                            jnp.where(j < 2.0, 112.0 + j, 999.0), 128.0 + 16.0 * j], axis=0)


def _peer_topk_kernel(x_ref, mod_ref, g_ref, wq_ref, keys_ref, h_ref, n_ref, p1_ref, r2_ref, p2_ref):
    mod = mod_ref[0]
    h = _modulate(x_ref[0], g_ref[...], mod[3:4], mod[4:5]).astype(BF16)
    h_ref[0] = h
    q = _dot(h, wq_ref[...])
    tl = q.shape[0]
    lanes = tl
    kio = lax.broadcasted_iota(jnp.int32, (PEER_NKEYS, lanes), 0).astype(F32)
    jrow = lax.broadcasted_iota(jnp.int32, (8, lanes), 0).astype(F32)
    pos = _candidate_positions(jrow)

    def head(hd, s1, s2, exact, at):
        s1, s2 = s1[:, at:at + lanes], s2[:, at:at + lanes]
        v1, rank1 = _rank16(s1, kio, exact)
        v2, rank2 = _rank16(s2, kio, exact)
        cand = _candidate_rows(v1, v2, jrow)
        sel_a = []
        for r in range(PEER_TOPK):
            m = jnp.max(cand, axis=0, keepdims=True)
            hit = cand == m
            if exact:
                px = jnp.min(jnp.where(hit, pos, 999.0), axis=0, keepdims=True)
                hit = pos == px
            else:
                px = jnp.max(jnp.where(hit, pos, -1.0), axis=0, keepdims=True)
            cand = jnp.where(hit, -jnp.inf, cand)
            sel_a.append(jnp.floor(px * (1.0 / PEER_TOPK)))
            if r == 0:
                best0, z = m, jnp.ones_like(m)
            else:
                z = z + jnp.exp(m - best0)
        n_e1 = jnp.zeros((PEER_NKEYS, lanes), F32)
        for a in range(PEER_TOPK):
            n_a = jnp.zeros_like(z)
            for r in range(a, PEER_TOPK):
                n_a = n_a + jnp.where(sel_a[r] == float(a), 1.0, 0.0)
            n_e1 = jnp.where(rank1 == float(a), n_a, n_e1)
        n_ref[0, hd, :, at:at + lanes] = n_e1
        p1_ref[0, hd, :, at:at + lanes] = jnp.exp(s1 - v1[0]) / z
        r2_ref[0, hd, :, at:at + lanes] = rank2.astype(BF16)
        p2_ref[0, hd, :, at:at + lanes] = jnp.exp(s2 - v2[0]).astype(BF16)
        ranked = (jnp.sum(jnp.where(rank1 < float(PEER_TOPK), 1.0, 0.0), axis=0, keepdims=True)
                  + jnp.sum(jnp.where(rank2 < float(PEER_TOPK), 1.0, 0.0), axis=0, keepdims=True)
                  + jnp.sum(jnp.where(cand == -jnp.inf, 1.0, 0.0), axis=0, keepdims=True))
        return jnp.max(jnp.abs(ranked - (3.0 * PEER_TOPK + PEER_CAND_UNUSED)))

    def scores(hd):
        return (_dot_nt(keys_ref[2 * hd], q[:, 256 * hd:256 * hd + 128].astype(BF16)),
                _dot_nt(keys_ref[2 * hd + 1], q[:, 256 * hd + 128:256 * hd + 256].astype(BF16)))

    miscounts = []
    for hd in range(PEER_HEADS):
        s1, s2 = scores(hd)
        miscounts.append([head(hd, s1, s2, False, at) for at in range(0, tl, lanes)])
    for hd in range(PEER_HEADS):
        for at, miscount in zip(range(0, tl, lanes), miscounts[hd]):
            @pl.when(miscount > 0.0)
            def _(hd=hd, at=at):
                s1, s2 = scores(hd)
                head(hd, s1, s2, True, at)


def _peer_topk(x, mod, gain, w_q, keys, per_batch_mod):
    b, l, d = x.shape
    tl = TOKEN_TILE
    mod_map = (lambda bi, i: (bi, 0, 0)) if per_batch_mod else (lambda bi, i: (0, 0, 0))
    sel = pl.BlockSpec((1, PEER_HEADS, PEER_NKEYS, tl), lambda bi, i: (bi, 0, 0, i))
    sel_f32 = jax.ShapeDtypeStruct((b, PEER_HEADS, PEER_NKEYS, l), F32)
    sel_bf16 = jax.ShapeDtypeStruct((b, PEER_HEADS, PEER_NKEYS, l), BF16)
    return pl.pallas_call(
        _peer_topk_kernel,
        grid=(b, l // tl),
        in_specs=[pl.BlockSpec((1, tl, d), lambda bi, i: (bi, i, 0)),
                  pl.BlockSpec((1, 6, d), mod_map),
                  pl.BlockSpec((1, d), lambda bi, i: (0, 0)),
                  pl.BlockSpec(w_q.shape, lambda bi, i: (0, 0)),
                  pl.BlockSpec(keys.shape, lambda bi, i: (0, 0, 0))],
        out_specs=[pl.BlockSpec((1, tl, d), lambda bi, i: (bi, i, 0)), sel, sel, sel, sel],
        out_shape=[jax.ShapeDtypeStruct((b, l, d), BF16), sel_f32, sel_f32, sel_bf16, sel_bf16],
        compiler_params=_params("arbitrary", "arbitrary"),
        name="peer_topk",
    )(x, mod, gain, w_q, keys)


def _peer_dense_kernel(x_ref, mod_ref, h_ref, u_ref, vt_ref, n_ref, p1_ref, r2_ref, p2_ref, o_ref,
                       ht_ref, acc_ref, s_ref, gate_ref):
    e = pl.program_id(2)

    @pl.when(e == 0)
    def _():
        ht_ref[...] = h_ref[0].astype(F32).T.astype(BF16)
        acc_ref[...] = jnp.zeros_like(acc_ref)

    ht = ht_ref[...]
    tm = ht.shape[1]
    bounds = [sum(PEER_SUBS[:i]) for i in range(len(PEER_SUBS) + 1)]

    def gate_rows(lo, hi):
        for row in range(lo // PEER_NKEYS, hi // PEER_NKEYS):
            gate = [None] * (PEER_NKEYS // 16)
            for hd in range(PEER_HEADS):
                n_b = jnp.broadcast_to(n_ref[0, hd, row:row + 1, :], (16, tm)).astype(BF16)
                p1_b = jnp.broadcast_to(p1_ref[0, hd, row:row + 1, :], (16, tm)).astype(BF16)
                for k in range(PEER_NKEYS // 16):
                    term = jnp.where(r2_ref[0, hd, 16 * k:16 * (k + 1), :] < n_b,
                                     p2_ref[0, hd, 16 * k:16 * (k + 1), :], jnp.zeros((), BF16)) * p1_b
                    gate[k] = term if gate[k] is None else gate[k] + term
            for k in range(PEER_NKEYS // 16):
                at = PEER_NKEYS * row + 16 * k
                gate_ref[at:at + 16, :] = gate[k]

    for lo, hi in zip(bounds[:-1], bounds[1:]):
        s_ref[lo:hi, :] = _dot(u_ref[lo:hi, :], ht).astype(BF16)
        gate_rows(lo, hi)
    ws = [gate_ref[lo:hi, :] * _gelu(s_ref[lo:hi, :])
          for lo, hi in zip(bounds[:-1], bounds[1:])]
    acc_ref[...] += _dot(vt_ref[...], jnp.concatenate(ws, axis=0))

    @pl.when(e == pl.num_programs(2) - 1)
    def _():
        o_ref[0] = x_ref[0] + mod_ref[0][5:6] * acc_ref[...].T


def _peer_dense(x, mod, h, u_tabs, vt_tabs, layer, n_e1, p1n, rank2, p2, per_batch_mod):
    b, l, d = x.shape
    tm = PEER_TOKENS
    n_exp = u_tabs.shape[1]
    rows = PEER_STEP // PEER_NKEYS
    mod_map = (lambda bi, i, e: (bi, 0, 0)) if per_batch_mod else (lambda bi, i, e: (0, 0, 0))
    tok = lambda: pl.BlockSpec((1, tm, d), lambda bi, i, e: (bi, i, 0))
    by_row = pl.BlockSpec((1, PEER_HEADS, rows, tm), lambda bi, i, e: (bi, 0, e, i))
    by_key = pl.BlockSpec((1, PEER_HEADS, PEER_NKEYS, tm), lambda bi, i, e: (bi, 0, 0, i))
    return pl.pallas_call(
        _peer_dense_kernel,
        grid=(b, l // tm, n_exp // PEER_STEP),
        in_specs=[tok(), pl.BlockSpec((1, 6, d), mod_map), tok(),
                  pl.BlockSpec((None, PEER_STEP, d), lambda bi, i, e: (layer, e, 0)),
                  pl.BlockSpec((None, d, PEER_STEP), lambda bi, i, e: (layer, 0, e)),
                  by_row, by_row, by_key, by_key],
        out_specs=tok(),
        out_shape=jax.ShapeDtypeStruct((b, l, d), F32),
        scratch_shapes=[pltpu.VMEM((d, tm), BF16), pltpu.VMEM((d, tm), F32),
                        pltpu.VMEM((PEER_STEP, tm), BF16), pltpu.VMEM((PEER_STEP, tm), BF16)],
        compiler_params=_params("arbitrary", "arbitrary", "arbitrary"),
        name="peer_dense",
    )(x, mod, h, u_tabs, vt_tabs, n_e1, p1n, rank2, p2)


def _rope_tables(n_lat, width):
    q4 = DIFF_DK // 4
    n_rows = n_lat // GRID_W
    row = jnp.repeat(jnp.arange(n_rows), GRID_W).astype(F32)
    col = jnp.tile(jnp.arange(GRID_W), n_rows).astype(F32)
    freqs = ROPE_BASE ** (-jnp.arange(q4, dtype=F32) / q4)
    ang = jnp.stack([row[:, None] * freqs, col[:, None] * freqs], axis=1)
    cos, sin = jnp.cos(ang), jnp.sin(ang)
    cos64 = jnp.stack([cos, cos], axis=2).reshape(n_lat, 64)
    sin64 = jnp.stack([-sin, sin], axis=2).reshape(n_lat, 64)
    reps = width // 64
    return jnp.tile(cos64, (1, reps)), jnp.tile(sin64, (1, reps))


def _ones_block_diag(width, group):
    idx = jnp.arange(width) // group
    return jnp.where(idx[:, None] == idx[None, :], 1.0 / group, 0.0).astype(BF16)


def _peer_layer(x, mod, W, l, per_batch_mod):
    shape = x.shape
    if not per_batch_mod:
        x = x.reshape(1, -1, shape[-1])
    h, n_e1, p1n, rank2, p2 = _peer_topk(x, mod, W['norm_ffn'][l], W['peer_w_q'][l], W['peer_keys'][l],
                                         per_batch_mod)
    y = _peer_dense(x, mod, h, W['peer_u'], W['peer_vt'], l, n_e1, p1n, rank2, p2, per_batch_mod)
    return y.reshape(shape)


def _even_layer(x, mod, W, l, cache, per_batch_mod):
    b, seq, _ = x.shape
    i = l // 2
    lam_init = 0.8 - 0.6 * math.exp(-0.3 * l)
    rope_tabs = W['rope'] if cache is not None else None
    u_tm2, q, k_norm, k_att, v, v_att = _inproj_even(
        x, mod, W['norm_mix'][l], W['even_w_in'][i], W['ones_bd'], W['diff_q_norm'][i],
        W['diff_k_norm'][i], rope_tabs, per_batch_mod)
    if cache is None:
        h0 = jnp.zeros((2, b, S5_JB * S5_BLK), F32)
        k_all, v_all = k_att, v_att
    else:
        h0 = _s5_state_to_blocks(cache['s5'][:, i])
        k_all = jnp.concatenate([cache['diff_k'][:, i].reshape(b, -1, 512).astype(BF16), k_att], axis=1)
        v_all = jnp.concatenate([cache['diff_v'][:, i].reshape(b, -1, 512).astype(BF16), v_att], axis=1)
    yf, yb, fin = _s5_scan(u_tm2.reshape(seq, b, 512), W['s5_bd'][i], W['s5_cd'][i], W['s5_a'][i], h0)
    o_attn = _diff_attention(q, k_all, v_all, W['diff_lambda'][i], lam_init)
    x = _mixout_even(x, mod, u_tm2, yf.reshape(seq, b * 512), yb.reshape(seq, b * 512), o_attn,
                     W['s5_d'][i], W['s5_w_glu'][i], W['diff_sub_norm'][i], W['even_w_out'][i],
                     1.0 - lam_init, per_batch_mod)
    return x, (k_norm, v, fin)


def _odd_layer(x, mod, W, l, cache, per_batch_mod):
    b, seq, _ = x.shape
    i = l // 2
    rope_tabs = W['rope'] if cache is not None else None
    q, k_norm, k_att, v, v_att, u = _inproj_odd(
        x, mod, W['norm_mix'][l], W['odd_w_in'][i], W['ones_bd'], W['win_q_norm'][i],
        W['win_k_norm'][i], rope_tabs, per_batch_mod)
    if cache is None:
        o_c = _win_attention(q, k_att, v_att, W['win_sink'][i])
    else:
        pad = ((0, 0), (WINDOW, WINDOW), (0, 0))
        o_c = _win_attention(q, cache['win_k'][:, i].reshape(b, -1, 128).astype(BF16),
                             cache['win_v'][:, i].reshape(b, -1, 128).astype(BF16), W['win_sink'][i],
                             jnp.pad(k_att, pad), jnp.pad(v_att, pad))
    o_d = _pool_mixer(u, W['pool_w'][i], W['pool_scale'][i])
    x = _mixout_odd(x, mod, o_c, o_d, W['odd_w_out'][i], per_batch_mod)
    return x, (k_norm, v)


def kernel(x_prompt, x_sample, cache_diff_k, cache_diff_v, state_s5, cache_win_k, cache_win_v, c, c_ctx, ada_w, ada_b, norm_mix, norm_ffn, even_w_in, even_w_out, s5_lam_re, s5_lam_im, s5_log_step, s5_b_re, s5_b_im, s5_c_re, s5_c_im, s5_d, s5_w_glu, diff_q_norm, diff_k_norm, diff_lambda, diff_sub_norm, odd_w_in, odd_w_out, win_q_norm, win_k_norm, win_sink, pool_w, pool_scale, peer_w_q, peer_sub_keys, peer_u, peer_v):
    depth = ada_w.shape[0]
    n_even, n_odd = even_w_in.shape[0], odd_w_in.shape[0]
    bsz, seq, d = x_prompt.shape
    dec_b, dec_l, _ = x_sample.shape

    cvec = jnp.zeros((16, d), F32).at[0].set(c_ctx).at[1:1 + dec_b].set(c)
    mod = _modulation(cvec, ada_w, ada_b)

    s5 = [_s5_tables(s5_lam_re[i], s5_lam_im[i], s5_log_step[i], s5_b_re[i], s5_b_im[i],
                     s5_c_re[i], s5_c_im[i]) for i in range(n_even)]
    W = dict(
        norm_mix=norm_mix.reshape(depth, 1, d), norm_ffn=norm_ffn.reshape(depth, 1, d),
        even_w_in=even_w_in.astype(BF16), even_w_out=even_w_out.astype(BF16),
        odd_w_in=odd_w_in.astype(BF16), odd_w_out=odd_w_out.astype(BF16),
        ones_bd=_ones_block_diag(512, 64),
        diff_q_norm=jnp.tile(diff_q_norm, (1, 8)).reshape(n_even, 1, 512),
        diff_k_norm=jnp.tile(diff_k_norm, (1, 8)).reshape(n_even, 1, 512),
        diff_lambda=diff_lambda, diff_sub_norm=diff_sub_norm.reshape(n_even, 1, DIFF_DV),
        s5_bd=[t[0] for t in s5], s5_cd=[t[1] for t in s5], s5_a=[t[2] for t in s5],
        s5_d=s5_d.reshape(n_even, 1, 512), s5_w_glu=s5_w_glu.astype(BF16),
        win_q_norm=jnp.tile(win_q_norm, (1, 8)).reshape(n_odd, 1, 512),
        win_k_norm=jnp.tile(win_k_norm, (1, 2)).reshape(n_odd, 1, 128),
        win_sink=jnp.broadcast_to(win_sink[:, :, None], (n_odd, 8, 128)),
        pool_w=pool_w.astype(BF16), pool_scale=pool_scale.reshape(n_odd, 1, 512),
        peer_w_q=peer_w_q.astype(BF16),
        peer_keys=peer_sub_keys.astype(BF16).reshape(depth, 2 * PEER_HEADS, PEER_NKEYS, -1),
        peer_u=peer_u.astype(BF16),
        peer_vt=jnp.swapaxes(peer_v, 1, 2).astype(BF16),
        rope=_rope_tables(dec_l, 512),
    )
    cache = {'diff_k': cache_diff_k, 'diff_v': cache_diff_v, 's5': state_s5,
             'win_k': cache_win_k, 'win_v': cache_win_v}

    def run(x, mods, cch, per_batch_mod):
        states = []
        for l in range(depth):
            if l % 2 == 0:
                x, st = _even_layer(x, mods[l], W, l, cch, per_batch_mod)
            else:
                x, st = _odd_layer(x, mods[l], W, l, cch, per_batch_mod)
            states.append(st)
            x = _peer_layer(x, mods[l], W, l, per_batch_mod)
        return x, states

    ctx_mods = [mod[l, 0:1].reshape(1, 6, d) for l in range(depth)]
    dec_mods = [mod[l, 1:1 + dec_b].reshape(dec_b, 6, d) for l in range(depth)]
    y_prompt, states = run(x_prompt, ctx_mods, None, False)
    y_sample, _ = run(x_sample, dec_mods, cache, True)

    evens = [states[l] for l in range(depth) if l % 2 == 0]
    odds = [states[l] for l in range(depth) if l % 2 == 1]
    new_diff_k = jnp.stack([s[0].reshape(bsz, seq, 2, DIFF_HEADS, DIFF_DK) for s in evens], axis=1)
    new_diff_v = jnp.stack([s[1].reshape(bsz, seq, DIFF_HEADS, DIFF_DV) for s in evens], axis=1)
    new_s5 = jnp.stack([_s5_blocks_to_state(s[2]) for s in evens], axis=1)
    new_win_k = jnp.stack([s[0].reshape(bsz, seq, WIN_KV_HEADS, WIN_DH) for s in odds], axis=1)
    new_win_v = jnp.stack([s[1].reshape(bsz, seq, WIN_KV_HEADS, WIN_DH) for s in odds], axis=1)
    return (y_prompt, y_sample, new_diff_k, new_diff_v, new_s5, new_win_k, new_win_v)
```

```python
import functools
import math

import jax
import jax.numpy as jnp
from jax import lax
from jax.experimental import pallas as pl
from jax.experimental.pallas import tpu as pltpu

F32 = jnp.float32
BF16 = jnp.bfloat16

D_MODEL = 1024
MIX_HALF = 512
GRID_W = 64
EPS = 1e-6
NEG_INF = -1e30
ROPE_BASE = 10000.0

S5_GROUPS = 32
S5_GROUP = 16
S5_STATE = 64
S5_JB = 4
S5_BLK = 1024
S5_CHUNK = 32
S5_BATCH = 8

DIFF_HEADS = 4
DIFF_DK = 64
DIFF_DV = 128
WIN_KV_HEADS = 2
WIN_GROUP = 4
WIN_DH = 64
ATTN_SCALE = DIFF_DK ** -0.5
WINDOW = 128
POOL_WINDOWS = (2, 4, 8, 16)
POOL_PAD = 8

PEER_HEADS = 8
PEER_NKEYS = 128
PEER_TOPK = 16
PEER_CAND_UNUSED = 6
RANK_CODE = 2.0 ** 100
PEER_SUBS = (128, 384, 512, 1024)
PEER_STEP = sum(PEER_SUBS)
PEER_TOKENS = 512

TOKEN_TILE = 256
VMEM_LIMIT = 56 * 1024 * 1024


def _params(*sem):
    return pltpu.CompilerParams(dimension_semantics=sem, vmem_limit_bytes=VMEM_LIMIT)


def _dot(a, b):
    return jnp.dot(a, b, preferred_element_type=F32)


def _dot_nt(a, b):
    return lax.dot_general(a, b, (((1,), (1,)), ((), ())), preferred_element_type=F32)


def _sigmoid(x):
    return 1.0 / (1.0 + jnp.exp(-x))


def _gelu(x):
    c = math.sqrt(2.0 / math.pi)
    hx = 0.5 * x
    return hx + hx * jnp.tanh(x * (c + (c * 0.044715) * (x * x)))


def _modulate(x, gain, shift, scale):
    ms = jnp.mean(x * x, axis=-1, keepdims=True)
    return (x * lax.rsqrt(ms + EPS) * gain) * (1.0 + scale) + shift


def _group_rms(z, ones_bd, gain):
    ms = _dot((z * z).astype(BF16), ones_bd)
    return z * lax.rsqrt(ms + EPS) * gain


def _rope(z, cos_t, sin_t):
    lane = lax.broadcasted_iota(jnp.int32, (1, 128), 1) % 32
    first = lane < 16
    parts = []
    for j in range(z.shape[1] // 128):
        c = z[:, 128 * j:128 * (j + 1)]
        parts.append(jnp.where(first, pltpu.roll(c, 112, axis=1), pltpu.roll(c, 16, axis=1)))
    swapped = parts[0] if len(parts) == 1 else jnp.concatenate(parts, axis=1)
    return z * cos_t + swapped * sin_t


def _mod_kernel(c_ref, w_ref, b_ref, o_ref):
    c = c_ref[...]
    s = c * _sigmoid(c)
    o_ref[0] = jnp.dot(s, w_ref[0], precision=lax.Precision.HIGHEST,
                       preferred_element_type=F32) + b_ref[0]


def _modulation(cvec, ada_w, ada_b):
    depth, d, n = ada_w.shape
    tn = 1536
    return pl.pallas_call(
        _mod_kernel,
        grid=(depth, n // tn),
        in_specs=[pl.BlockSpec((16, d), lambda l, j: (0, 0)),
                  pl.BlockSpec((1, d, tn), lambda l, j: (l, 0, j)),
                  pl.BlockSpec((1, 1, tn), lambda l, j: (l, 0, j))],
        out_specs=pl.BlockSpec((1, 16, tn), lambda l, j: (l, 0, j)),
        out_shape=jax.ShapeDtypeStruct((depth, 16, n), F32),
        compiler_params=_params("arbitrary", "arbitrary"),
        name="adaln_mod",
    )(cvec, ada_w, ada_b.reshape(depth, 1, n))


def _inproj_even_kernel(rope, *refs):
    if rope:
        (x_ref, mod_ref, g_ref, w_ref, bd_ref, gq_ref, gk_ref, cos_ref, sin_ref,
         u_ref, q_ref, kn_ref, ka_ref, v_ref, va_ref) = refs
    else:
        (x_ref, mod_ref, g_ref, w_ref, bd_ref, gq_ref, gk_ref,
         u_ref, q_ref, kn_ref, ka_ref, v_ref, va_ref) = refs
    mod = mod_ref[0]
    h = _modulate(x_ref[0], g_ref[...], mod[0:1], mod[1:2])
    z = _dot(h.astype(BF16), w_ref[...])
    u = z[:, 0:512]
    q = _group_rms(z[:, 512:1024], bd_ref[...], gq_ref[...])
    k = _group_rms(z[:, 1024:1536], bd_ref[...], gk_ref[...])
    v = z[:, 1536:2048]
    u_ref[...] = u
    kn_ref[0] = k
    v_ref[0] = v
    va_ref[0] = v.astype(BF16)
    if rope:
        q = _rope(q, cos_ref[...], sin_ref[...])
        k = _rope(k, cos_ref[...], sin_ref[...])
    q_ref[0] = (q * ATTN_SCALE).astype(BF16)
    ka_ref[0] = k.astype(BF16)


def _inproj_even(x, mod, gain, w, ones_bd, gq, gk, rope_tabs, per_batch_mod):
    b, l, d = x.shape
    tl = TOKEN_TILE
    rope = rope_tabs is not None
    mod_map = (lambda bi, i: (bi, 0, 0)) if per_batch_mod else (lambda bi, i: (0, 0, 0))
    const2 = lambda bi, i: (0, 0)
    tok = lambda w_: pl.BlockSpec((1, tl, w_), lambda bi, i: (bi, i, 0))
    in_specs = [tok(d),
                pl.BlockSpec((1, 6, d), mod_map),
                pl.BlockSpec((1, d), const2),
                pl.BlockSpec(w.shape, const2),
                pl.BlockSpec(ones_bd.shape, const2),
                pl.BlockSpec((1, 512), const2),
                pl.BlockSpec((1, 512), const2)]
    args = [x, mod, gain, w, ones_bd, gq, gk]
    if rope:
        in_specs += [pl.BlockSpec((tl, 512), lambda bi, i: (i, 0))] * 2
        args += list(rope_tabs)
    out_shape = [jax.ShapeDtypeStruct((l, b * 512), F32),
                 jax.ShapeDtypeStruct((b, l, 512), BF16),
                 jax.ShapeDtypeStruct((b, l, 512), F32),
                 jax.ShapeDtypeStruct((b, l, 512), BF16),
                 jax.ShapeDtypeStruct((b, l, 512), F32),
                 jax.ShapeDtypeStruct((b, l, 512), BF16)]
    out_specs = [pl.BlockSpec((tl, 512), lambda bi, i: (i, bi)),
                 tok(512), tok(512), tok(512), tok(512), tok(512)]
    return pl.pallas_call(
        functools.partial(_inproj_even_kernel, rope),
        grid=(b, l // tl), in_specs=in_specs, out_specs=out_specs, out_shape=out_shape,
        compiler_params=_params("arbitrary", "arbitrary"),
        name="inproj_even_rope" if rope else "inproj_even",
    )(*args)


def _inproj_odd_kernel(rope, *refs):
    if rope:
        (x_ref, mod_ref, g_ref, w_ref, bd_ref, gq_ref, gk_ref, cos_ref, sin_ref,
         q_ref, kn_ref, ka_ref, v_ref, va_ref, u_ref) = refs
    else:
        (x_ref, mod_ref, g_ref, w_ref, bd_ref, gq_ref, gk_ref,
         q_ref, kn_ref, ka_ref, v_ref, va_ref, u_ref) = refs
    mod = mod_ref[0]
    h = _modulate(x_ref[0], g_ref[...], mod[0:1], mod[1:2])
    z = _dot(h.astype(BF16), w_ref[...])
    bd = bd_ref[...]
    q = _group_rms(z[:, 0:512], bd, gq_ref[...])
    k = _group_rms(z[:, 512:640], bd[0:128, 0:128], gk_ref[...])
    v = z[:, 640:768]
    kn_ref[0] = k
    v_ref[0] = v
    va_ref[0] = v.astype(BF16)
    u_ref[0] = z[:, 768:1280]
    if rope:
        q = _rope(q, cos_ref[...], sin_ref[...])
        k = _rope(k, cos_ref[:, 0:128], sin_ref[:, 0:128])
    q_ref[0] = (q * ATTN_SCALE).astype(BF16)
    ka_ref[0] = k.astype(BF16)


def _inproj_odd(x, mod, gain, w, ones_bd, gq, gk, rope_tabs, per_batch_mod):
    b, l, d = x.shape
    tl = TOKEN_TILE
    rope = rope_tabs is not None
    mod_map = (lambda bi, i: (bi, 0, 0)) if per_batch_mod else (lambda bi, i: (0, 0, 0))
    const2 = lambda bi, i: (0, 0)
    tok = lambda w_: pl.BlockSpec((1, tl, w_), lambda bi, i: (bi, i, 0))
    in_specs = [tok(d),
                pl.BlockSpec((1, 6, d), mod_map),
                pl.BlockSpec((1, d), const2),
                pl.BlockSpec(w.shape, const2),
                pl.BlockSpec(ones_bd.shape, const2),
                pl.BlockSpec((1, 512), const2),
                pl.BlockSpec((1, 128), const2)]
    args = [x, mod, gain, w, ones_bd, gq, gk]
    if rope:
        in_specs += [pl.BlockSpec((tl, 512), lambda bi, i: (i, 0))] * 2
        args += list(rope_tabs)
    out_shape = [jax.ShapeDtypeStruct((b, l, 512), BF16),
                 jax.ShapeDtypeStruct((b, l, 128), F32),
                 jax.ShapeDtypeStruct((b, l, 128), BF16),
                 jax.ShapeDtypeStruct((b, l, 128), F32),
                 jax.ShapeDtypeStruct((b, l, 128), BF16),
                 jax.ShapeDtypeStruct((b, l, 512), F32)]
    out_specs = [tok(512), tok(128), tok(128), tok(128), tok(128), tok(512)]
    return pl.pallas_call(
        functools.partial(_inproj_odd_kernel, rope),
        grid=(b, l // tl), in_specs=in_specs, out_specs=out_specs, out_shape=out_shape,
        compiler_params=_params("arbitrary", "arbitrary"),
        name="inproj_odd_rope" if rope else "inproj_odd",
    )(*args)


def _s5_kernel(uf_ref, ub_ref, bd_ref, cd_ref, a_ref, h0_ref, yf_ref, yb_ref, fin_ref,
               bu_ref, carry_ref):
    c = pl.program_id(1)
    rows = S5_CHUNK * S5_BATCH

    @pl.when(c == 0)
    def _():
        carry_ref[...] = h0_ref[...]

    for d in range(2):
        u = (uf_ref if d == 0 else ub_ref)[...].reshape(rows, MIX_HALF).astype(BF16)
        for j in range(S5_JB):
            bu_ref[d, :, S5_BLK * j:S5_BLK * (j + 1)] = _dot(u[:, 128 * j:128 * (j + 1)], bd_ref[d, j])
        for j in range(S5_JB):
            lo = S5_BLK * j
            a_re = jnp.broadcast_to(a_ref[d, 0, j:j + 1, :], (S5_BATCH, 512))
            a_im = jnp.broadcast_to(a_ref[d, 1, j:j + 1, :], (S5_BATCH, 512))

            h_re, h_im = carry_ref[d, :, lo:lo + 512], carry_ref[d, :, lo + 512:lo + 1024]
            for s in range(S5_CHUNK):
                row = (s if d == 0 else S5_CHUNK - 1 - s) * S5_BATCH
                b_re = bu_ref[d, row:row + S5_BATCH, lo:lo + 512]
                b_im = bu_ref[d, row:row + S5_BATCH, lo + 512:lo + 1024]
                h_re, h_im = a_re * h_re - a_im * h_im + b_re, a_re * h_im + a_im * h_re + b_im
                bu_ref[d, row:row + S5_BATCH, lo:lo + 512] = h_re
                bu_ref[d, row:row + S5_BATCH, lo + 512:lo + 1024] = h_im
            carry_ref[d, :, lo:lo + 512] = h_re
            carry_ref[d, :, lo + 512:lo + 1024] = h_im
        ys = [_dot(bu_ref[d, :, S5_BLK * j:S5_BLK * (j + 1)].astype(BF16), cd_ref[d, j])
              for j in range(S5_JB)]
        y = jnp.concatenate(ys, axis=1).reshape(S5_CHUNK, S5_BATCH, MIX_HALF)
        if d == 0:
            yf_ref[...] = y
        else:
            yb_ref[...] = y

    @pl.when(c == pl.num_programs(1) - 1)
    def _():
        fin_ref[...] = carry_ref[...]


def _s5_scan(u_tm, bd, cd, acoef, h0):
    l, b, _ = u_tm.shape
    nc = l // S5_CHUNK
    ng = b // S5_BATCH
    state_w = S5_JB * S5_BLK
    blk = (S5_CHUNK, S5_BATCH, MIX_HALF)
    full = lambda a: pl.BlockSpec(a.shape, lambda g, c: (0,) * a.ndim)
    return pl.pallas_call(
        _s5_kernel,
        grid=(ng, nc),
        in_specs=[pl.BlockSpec(blk, lambda g, c: (c, g, 0)),
                  pl.BlockSpec(blk, lambda g, c: (nc - 1 - c, g, 0)),
                  full(bd), full(cd), full(acoef),
                  pl.BlockSpec((2, S5_BATCH, state_w), lambda g, c: (0, g, 0))],
        out_specs=[pl.BlockSpec(blk, lambda g, c: (c, g, 0)),
                   pl.BlockSpec(blk, lambda g, c: (nc - 1 - c, g, 0)),
                   pl.BlockSpec((2, S5_BATCH, state_w), lambda g, c: (0, g, 0))],
        out_shape=[jax.ShapeDtypeStruct((l, b, MIX_HALF), F32),
                   jax.ShapeDtypeStruct((l, b, MIX_HALF), F32),
                   jax.ShapeDtypeStruct((2, b, state_w), F32)],
        scratch_shapes=[pltpu.VMEM((2, S5_CHUNK * S5_BATCH, state_w), F32),
                        pltpu.VMEM((2, S5_BATCH, state_w), F32)],
        compiler_params=_params("arbitrary", "arbitrary"),
        name="s5_scan",
    )(u_tm, u_tm, bd, cd, acoef, h0)


def _s5_tables(lam_re, lam_im, log_step, b_re, b_im, c_re, c_im):
    step = jnp.exp(log_step.astype(F32))[..., None]
    lr, li = lam_re.astype(F32), lam_im.astype(F32)
    er = jnp.exp(lr * step)
    a_re, a_im = er * jnp.cos(li * step), er * jnp.sin(li * step)
    den = lr * lr + li * li
    q_re = ((a_re - 1.0) * lr + a_im * li) / den
    q_im = (a_im * lr - (a_re - 1.0) * li) / den
    bb_re = q_re[..., None] * b_re - q_im[..., None] * b_im
    bb_im = q_re[..., None] * b_im + q_im[..., None] * b_re
    eye = jnp.eye(8, dtype=F32)
    bb = jnp.stack([bb_re, bb_im], axis=1).reshape(2, 2, S5_JB, 8, S5_STATE, S5_GROUP)
    bd = jnp.einsum('drjgpc,gh->djgcrhp', bb, eye).reshape(2, S5_JB, 128, S5_BLK).astype(BF16)
    cc = jnp.stack([c_re, -c_im], axis=1).astype(F32).reshape(2, 2, S5_JB, 8, S5_GROUP, S5_STATE)
    cd = jnp.einsum('drjgcp,gh->djrgphc', cc, eye).reshape(2, S5_JB, S5_BLK, 128).astype(BF16)
    acoef = jnp.stack([a_re, a_im], axis=1).reshape(2, 2, S5_JB, 512)
    return bd, cd, acoef


def _s5_state_to_blocks(st):
    b = st.shape[0]
    st = st.astype(F32).reshape(b, 2, 2, S5_JB, 8, S5_STATE)
    return jnp.transpose(st, (1, 0, 3, 2, 4, 5)).reshape(2, b, S5_JB * S5_BLK)


def _s5_blocks_to_state(fin):
    b = fin.shape[1]
    fin = fin.reshape(2, b, S5_JB, 2, 8, S5_STATE)
    return jnp.transpose(fin, (1, 0, 3, 2, 4, 5)).reshape(b, 2, 2, S5_GROUPS, S5_STATE)


def _diff_attn_kernel(lam_init, q_ref, k_ref, v_ref, lv_ref, o_ref):
    lv = lv_ref[...]
    lam = (jnp.exp(jnp.sum(lv[0:1] * lv[1:2], axis=-1, keepdims=True))
           - jnp.exp(jnp.sum(lv[2:3] * lv[3:4], axis=-1, keepdims=True)) + lam_init)
    lane = lax.broadcasted_iota(jnp.int32, (1, 128), 1)
    outs = []
    for h in range(DIFF_HEADS):
        es, sums = [], []
        for m in range(2):
            j = m * 2 + h // 2
            keep = (lane < 64) if h % 2 == 0 else (lane >= 64)
            qb = jnp.where(keep, q_ref[0, :, 128 * j:128 * (j + 1)], jnp.zeros((), BF16))
            s = _dot_nt(qb, k_ref[0, :, 128 * j:128 * (j + 1)])
            e = jnp.exp(s - jnp.max(s, axis=-1, keepdims=True))
            es.append(e)
            sums.append(jnp.sum(e, axis=-1, keepdims=True))
        a = es[0] - (lam * sums[0] / sums[1]) * es[1]
        outs.append(_dot(a.astype(BF16), v_ref[0, :, 128 * h:128 * (h + 1)]) / sums[0])
    o_ref[0] = jnp.concatenate(outs, axis=1)


def _diff_attention(q, k, v, lv, lam_init):
    b, lq, _ = q.shape
    lk = k.shape[1]
    tq = TOKEN_TILE
    return pl.pallas_call(
        functools.partial(_diff_attn_kernel, lam_init),
        grid=(b, lq // tq),
        in_specs=[pl.BlockSpec((1, tq, 512), lambda bi, i: (bi, i, 0)),
                  pl.BlockSpec((1, lk, 512), lambda bi, i: (bi, 0, 0)),
                  pl.BlockSpec((1, lk, 512), lambda bi, i: (bi, 0, 0)),
                  pl.BlockSpec((4, DIFF_DK), lambda bi, i: (0, 0))],
        out_specs=pl.BlockSpec((1, tq, 512), lambda bi, i: (bi, i, 0)),
        out_shape=jax.ShapeDtypeStruct((b, lq, 512), F32),
        compiler_params=_params("arbitrary", "arbitrary"),
        name="diff_attention",
    )(q, k, v, lv)


def _win_attn_kernel(local, seq_len, *refs):
    if local:
        q_ref, kc_ref, vc_ref, kl_ref, vl_ref, sink_ref, o_ref = refs
    else:
        q_ref, kc_ref, vc_ref, sink_ref, o_ref = refs
    tq = q_ref.shape[1]
    lane = lax.broadcasted_iota(jnp.int32, (1, 128), 1)
    low = lane < 64
    kc = kc_ref[0]
    vc = vc_ref[0]
    n_ctx = kc.shape[0]
    if local:
        span = tq + 2 * WINDOW
        start = pl.multiple_of(pl.program_id(1) * tq, tq)
        kl = kl_ref[0, pl.ds(start, span), :]
        vl = vl_ref[0, pl.ds(start, span), :]
        qpos = start + lax.broadcasted_iota(jnp.int32, (tq, span), 0)
        kpos = start - WINDOW + lax.broadcasted_iota(jnp.int32, (tq, span), 1)
        ok = (jnp.abs(qpos - kpos) <= WINDOW) & (kpos >= 0) & (kpos < seq_len)
    heads = {}
    for n in range(WIN_KV_HEADS):
        keep = low if n == 0 else jnp.logical_not(low)
        for g in range(WIN_GROUP):
            hd = n * WIN_GROUP + g
            j, half = hd // 2, hd % 2
            qb = q_ref[0, :, 128 * j:128 * (j + 1)]
            if half != n:
                qb = jnp.concatenate([qb[:, 64:128], qb[:, 0:64]], axis=1)
            qb = jnp.where(keep, qb, jnp.zeros((), BF16))
            sink = sink_ref[hd:hd + 1, 0:1]
            s = _dot_nt(qb, kc)
            if local:
                s = jnp.concatenate([s, jnp.where(ok, _dot_nt(qb, kl), NEG_INF)], axis=1)
            mx = jnp.maximum(jnp.max(s, axis=-1, keepdims=True), sink)
            e = jnp.exp(s - mx)
            den = jnp.sum(e, axis=-1, keepdims=True) + jnp.exp(sink - mx)
            p = e.astype(BF16)
            o = _dot(p[:, 0:n_ctx], vc)
            if local:
                o = o + _dot(p[:, n_ctx:], vl)
            heads[hd] = o / den
    blocks = []
    for j in range(4):
        ev, od = heads[2 * j], heads[2 * j + 1]
        n = (2 * j) // WIN_GROUP
        if n == 1:
            ev = pltpu.roll(ev, 64, axis=1)
        else:
            od = pltpu.roll(od, 64, axis=1)
        blocks.append(jnp.where(low, ev, od))
    o_ref[0] = jnp.concatenate(blocks, axis=1).astype(BF16)


def _win_attention(q, kc, vc, sink, k_lat=None, v_lat=None):
    b, lq, _ = q.shape
    n_ctx = kc.shape[1]
    local = k_lat is not None
    tq = TOKEN_TILE
    in_specs = [pl.BlockSpec((1, tq, 512), lambda bi, i: (bi, i, 0)),
                pl.BlockSpec((1, n_ctx, 128), lambda bi, i: (bi, 0, 0)),
                pl.BlockSpec((1, n_ctx, 128), lambda bi, i: (bi, 0, 0))]
    args = [q, kc, vc]
    if local:
        lp = k_lat.shape[1]
        in_specs += [pl.BlockSpec((1, lp, 128), lambda bi, i: (bi, 0, 0))] * 2
        args += [k_lat, v_lat]
    in_specs.append(pl.BlockSpec((8, 128), lambda bi, i: (0, 0)))
    args.append(sink)
    return pl.pallas_call(
        functools.partial(_win_attn_kernel, local, lq),
        grid=(b, lq // tq), in_specs=in_specs,
        out_specs=pl.BlockSpec((1, tq, 512), lambda bi, i: (bi, i, 0)),
        out_shape=jax.ShapeDtypeStruct((b, lq, 512), BF16),
        compiler_params=_params("arbitrary", "arbitrary"),
        name="win_attention_local" if local else "win_attention",
    )(*args)


def _pool_kernel(u_ref, w_ref, scale_ref, o_ref):
    l = u_ref.shape[1]
    lp = l + 2 * POOL_PAD
    t = lax.broadcasted_iota(jnp.int32, (l, 128), 0)
    zpad = jnp.zeros((POOL_PAD, 128), F32)

    def shifted(a, s):
        return pltpu.roll(a, s, axis=0) + pltpu.roll(a, lp - s, axis=0)

    outs = []
    for gi, wl in enumerate(POOL_WINDOWS):
        x = u_ref[0, :, 128 * gi:128 * (gi + 1)]
        xp = jnp.concatenate([zpad, x, zpad], axis=0)
        acc = xp + pltpu.roll(xp, 1, axis=0)
        if wl >= 4:
            acc = shifted(acc, 1)
        if wl >= 8:
            acc = shifted(acc, 2)
        if wl >= 16:
            acc = shifted(acc, 4)
        win = acc[POOL_PAD:POOL_PAD + l]
        cnt = (jnp.minimum(t + wl // 2, l) - jnp.maximum(t - wl // 2, 0)).astype(F32)
        pooled = win / cnt - x
        outs.append(_dot(pooled.astype(BF16), w_ref[gi]))
    o_ref[0] = (jnp.concatenate(outs, axis=1) * scale_ref[...]).astype(BF16)


def _pool_mixer(u, w, scale):
    b, l, _ = u.shape
    return pl.pallas_call(
        _pool_kernel,
        grid=(b,),
        in_specs=[pl.BlockSpec((1, l, 512), lambda bi: (bi, 0, 0)),
                  pl.BlockSpec(w.shape, lambda bi: (0, 0, 0)),
                  pl.BlockSpec((1, 512), lambda bi: (0, 0))],
        out_specs=pl.BlockSpec((1, l, 512), lambda bi: (bi, 0, 0)),
        out_shape=jax.ShapeDtypeStruct((b, l, 512), BF16),
        compiler_params=_params("arbitrary"),
        name="pool_mixer",
    )(u, w, scale)


def _mixout_even_kernel(out_scale, x_ref, mod_ref, u_ref, yf_ref, yb_ref, oa_ref, sd_ref,
                        wglu_ref, gsub_ref, wout_ref, o_ref):
    mod = mod_ref[0]
    y = sd_ref[...] * u_ref[...] + yf_ref[...] + yb_ref[...]
    g = _gelu(y)
    o_a = g * _sigmoid(_dot(g.astype(BF16), wglu_ref[...]))
    parts = [o_a.astype(BF16)]
    for h in range(DIFF_HEADS):
        blk = oa_ref[0, :, 128 * h:128 * (h + 1)]
        ms = jnp.mean(blk * blk, axis=-1, keepdims=True)
        parts.append(((blk * lax.rsqrt(ms + EPS) * gsub_ref[...]) * out_scale).astype(BF16))
    cat = jnp.concatenate(parts, axis=1)
    o_ref[0] = x_ref[0] + mod[2:3] * _dot(cat, wout_ref[...])


def _mixout_even(x, mod, u_tm2, yf2, yb2, o_attn, s5_d, w_glu, g_sub, w_out, out_scale, per_batch_mod):
    b, l, d = x.shape
    tl = TOKEN_TILE
    mod_map = (lambda bi, i: (bi, 0, 0)) if per_batch_mod else (lambda bi, i: (0, 0, 0))
    const2 = lambda bi, i: (0, 0)
    tm = pl.BlockSpec((tl, 512), lambda bi, i: (i, bi))
    return pl.pallas_call(
        functools.partial(_mixout_even_kernel, out_scale),
        grid=(b, l // tl),
        in_specs=[pl.BlockSpec((1, tl, d), lambda bi, i: (bi, i, 0)),
                  pl.BlockSpec((1, 6, d), mod_map),
                  tm, tm, tm,
                  pl.BlockSpec((1, tl, 512), lambda bi, i: (bi, i, 0)),
                  pl.BlockSpec((1, 512), const2),
                  pl.BlockSpec((512, 512), const2),
                  pl.BlockSpec((1, 128), const2),
                  pl.BlockSpec((d, d), const2)],
        out_specs=pl.BlockSpec((1, tl, d), lambda bi, i: (bi, i, 0)),
        out_shape=jax.ShapeDtypeStruct((b, l, d), F32),
        compiler_params=_params("arbitrary", "arbitrary"),
        name="mixout_even",
    )(x, mod, u_tm2, yf2, yb2, o_attn, s5_d, w_glu, g_sub, w_out)


def _mixout_odd_kernel(x_ref, mod_ref, oc_ref, od_ref, wout_ref, o_ref):
    mod = mod_ref[0]
    cat = jnp.concatenate([oc_ref[0], od_ref[0]], axis=1)
    o_ref[0] = x_ref[0] + mod[2:3] * _dot(cat, wout_ref[...])


def _mixout_odd(x, mod, o_c, o_d, w_out, per_batch_mod):
    b, l, d = x.shape
    tl = TOKEN_TILE
    mod_map = (lambda bi, i: (bi, 0, 0)) if per_batch_mod else (lambda bi, i: (0, 0, 0))
    tok = lambda w_: pl.BlockSpec((1, tl, w_), lambda bi, i: (bi, i, 0))
    return pl.pallas_call(
        _mixout_odd_kernel,
        grid=(b, l // tl),
        in_specs=[tok(d), pl.BlockSpec((1, 6, d), mod_map), tok(512), tok(512),
                  pl.BlockSpec((d, d), lambda bi, i: (0, 0))],
        out_specs=tok(d),
        out_shape=jax.ShapeDtypeStruct((b, l, d), F32),
        compiler_params=_params("arbitrary", "arbitrary"),
        name="mixout_odd",
    )(x, mod, o_c, o_d, w_out)


def _rank16(s, kio, exact):
    n = s.shape[0]
    vals = []
    if exact:
        rank = jnp.full(s.shape, float(PEER_TOPK), F32)
        for r in range(PEER_TOPK):
            m = jnp.max(s, axis=0, keepdims=True)
            ix = jnp.min(jnp.where(s == m, kio, float(n)), axis=0, keepdims=True)
            hit = kio == ix
            rank = jnp.where(hit, float(r), rank)
            s = jnp.where(hit, -jnp.inf, s)
            vals.append(m)
        return vals, rank
    for r in range(PEER_TOPK):
        m = jnp.max(s, axis=0, keepdims=True)
        s = jnp.where(s == m, -RANK_CODE * (1.0 + r / 32.0), s)
        vals.append(m)
    code = s * (-1.0 / RANK_CODE)
    return vals, jnp.where(code >= 1.0, (code - 1.0) * 32.0, float(PEER_TOPK))


_SORT16 = ((0, 1), (2, 3), (0, 2), (1, 3), (1, 2), (4, 5), (6, 7), (4, 6), (5, 7), (5, 6), (0, 4), (2, 6), (2, 4),
           (1, 5), (3, 7), (3, 5), (1, 2), (3, 4), (5, 6), (8, 9), (10, 11), (8, 10), (9, 11), (9, 10), (12, 13),
           (14, 15), (12, 14), (13, 15), (13, 14), (8, 12), (10, 14), (10, 12), (9, 13), (11, 15), (11, 13), (9, 10),
           (11, 12), (13, 14), (0, 8), (4, 12), (4, 8), (2, 10), (6, 14), (6, 10), (2, 4), (6, 8), (10, 12), (1, 9),
           (5, 13), (5, 9), (3, 11), (7, 15), (7, 11), (3, 5), (7, 9), (11, 13), (1, 2), (3, 4), (5, 6), (7, 8),
           (9, 10), (11, 12), (13, 14))


def _sorted_top16(s):
    cols = [s[8 * i:8 * (i + 1), :] for i in range(16)]
    for i, j in _SORT16:
        cols[i], cols[j] = jnp.maximum(cols[i], cols[j]), jnp.minimum(cols[i], cols[j])
    vals = []
    for r in range(PEER_TOPK):
        m = cols[0]
        for shift in (4, 2, 1):
            m = jnp.maximum(m, pltpu.roll(m, shift, axis=0))
        vals.append(m[0:1])
        hit = cols[0] == m
        for i in range(PEER_TOPK - 1 - r):
            cols[i] = jnp.where(hit, cols[i + 1], cols[i])
    return vals


def _candidate_rows(v1, v2, j):
    v1hi = jnp.concatenate(v1[8:16], axis=0)
    v2lo = jnp.concatenate(v2[0:8], axis=0)
    v2hi = jnp.concatenate(v2[8:16], axis=0)
    g3_a = jnp.where(j < 5.0, v1[2], v1[4])
    g3_b = jnp.where(j < 5.0, v2lo, pltpu.roll(v2lo, 5, axis=0))
    g4_a = jnp.where(j < 4.0, v1[3], jnp.where(j < 6.0, v1[5], v1[6]))
    g4_b = jnp.where(j < 4.0, v2lo, jnp.where(j < 6.0, pltpu.roll(v2lo, 4, axis=0), pltpu.roll(v2lo, 6, axis=0)))
    return jnp.concatenate([v1[0] + v2lo, v1[0] + v2hi, v1[1] + v2lo, g3_a + g3_b, g4_a + g4_b,
                            jnp.where(j < 2.0, v1[7] + v2lo, -jnp.inf), v1hi + v2[0]], axis=0)


def _candidate_positions(j):
    return jnp.concatenate([j, 8.0 + j, 16.0 + j,
                            jnp.where(j < 5.0, 32.0 + j, 59.0 + j),
                            jnp.where(j < 4.0, 48.0 + j, jnp.where(j < 6.0, 76.0 + j, 90.0 + j)),
                            jnp.where(j < 2.0, 112.0 + j, 999.0), 128.0 + 16.0 * j], axis=0)


def _peer_topk_kernel(x_ref, mod_ref, g_ref, wq_ref, keys_ref, h_ref, n_ref, p1_ref, r2_ref, p2_ref):
    mod = mod_ref[0]
    h = _modulate(x_ref[0], g_ref[...], mod[3:4], mod[4:5]).astype(BF16)
    h_ref[0] = h
    q = _dot(h, wq_ref[...])
    tl = q.shape[0]
    lanes = tl
    kio = lax.broadcasted_iota(jnp.int32, (PEER_NKEYS, lanes), 0).astype(F32)
    jrow = lax.broadcasted_iota(jnp.int32, (8, lanes), 0).astype(F32)
    pos = _candidate_positions(jrow)

    def head(hd, s1, s2, exact, at):
        s1, s2 = s1[:, at:at + lanes], s2[:, at:at + lanes]
        if exact:
            v1, rank1 = _rank16(s1, kio, True)
        else:
            v1 = _sorted_top16(s1)
        v2, rank2 = _rank16(s2, kio, exact)
        cand = _candidate_rows(v1, v2, jrow)
        sel_a = []
        for r in range(PEER_TOPK):
            m = jnp.max(cand, axis=0, keepdims=True)
            hit = cand == m
            if exact:
                px = jnp.min(jnp.where(hit, pos, 999.0), axis=0, keepdims=True)
                hit = pos == px
            else:
                px = jnp.max(jnp.where(hit, pos, -1.0), axis=0, keepdims=True)
            cand = jnp.where(hit, -jnp.inf, cand)
            sel_a.append(jnp.floor(px * (1.0 / PEER_TOPK)))
            if r == 0:
                best0, z = m, jnp.ones_like(m)
            else:
                z = z + jnp.exp(m - best0)
        n_e1 = jnp.zeros((PEER_NKEYS, lanes), F32)
        for a in range(PEER_TOPK):
            n_a = jnp.zeros_like(z)
            for r in range(a, PEER_TOPK):
                n_a = n_a + jnp.where(sel_a[r] == float(a), 1.0, 0.0)
            n_e1 = jnp.where((rank1 == float(a)) if exact else (s1 == v1[a]), n_a, n_e1)
        n_ref[0, hd, :, at:at + lanes] = n_e1
        p1_ref[0, hd, :, at:at + lanes] = jnp.exp(s1 - v1[0]) / z
        r2_ref[0, hd, :, at:at + lanes] = rank2.astype(BF16)
        p2_ref[0, hd, :, at:at + lanes] = jnp.exp(s2 - v2[0]).astype(BF16)
        if exact:
            return None
        total = (jnp.sum(jnp.where(s1 >= v1[PEER_TOPK - 1], 1.0, 0.0), axis=0, keepdims=True)
                 + jnp.sum(jnp.where(rank2 < float(PEER_TOPK), 1.0, 0.0), axis=0, keepdims=True)
                 + jnp.sum(jnp.where(cand == -jnp.inf, 1.0, 0.0), axis=0, keepdims=True))
        for r in range(PEER_TOPK - 1):
            total = total + jnp.where(v1[r] <= v1[r + 1], 1.0, 0.0)
        return jnp.max(jnp.abs(total - (3.0 * PEER_TOPK + PEER_CAND_UNUSED)))

    def scores(hd):
        return (_dot_nt(keys_ref[2 * hd], q[:, 256 * hd:256 * hd + 128].astype(BF16)),
                _dot_nt(keys_ref[2 * hd + 1], q[:, 256 * hd + 128:256 * hd + 256].astype(BF16)))

    miscounts = []
    for hd in range(PEER_HEADS):
        s1, s2 = scores(hd)
        miscounts.append([head(hd, s1, s2, False, at) for at in range(0, tl, lanes)])
    for hd in range(PEER_HEADS):
        for at, miscount in zip(range(0, tl, lanes), miscounts[hd]):
            @pl.when(miscount > 0.0)
            def _(hd=hd, at=at):
                s1, s2 = scores(hd)
                head(hd, s1, s2, True, at)


def _peer_topk(x, mod, gain, w_q, keys, per_batch_mod):
    b, l, d = x.shape
    tl = TOKEN_TILE
    mod_map = (lambda bi, i: (bi, 0, 0)) if per_batch_mod else (lambda bi, i: (0, 0, 0))
    sel = pl.BlockSpec((1, PEER_HEADS, PEER_NKEYS, tl), lambda bi, i: (bi, 0, 0, i))
    sel_f32 = jax.ShapeDtypeStruct((b, PEER_HEADS, PEER_NKEYS, l), F32)
    sel_bf16 = jax.ShapeDtypeStruct((b, PEER_HEADS, PEER_NKEYS, l), BF16)
    return pl.pallas_call(
        _peer_topk_kernel,
        grid=(b, l // tl),
        in_specs=[pl.BlockSpec((1, tl, d), lambda bi, i: (bi, i, 0)),
                  pl.BlockSpec((1, 6, d), mod_map),
                  pl.BlockSpec((1, d), lambda bi, i: (0, 0)),
                  pl.BlockSpec(w_q.shape, lambda bi, i: (0, 0)),
                  pl.BlockSpec(keys.shape, lambda bi, i: (0, 0, 0))],
        out_specs=[pl.BlockSpec((1, tl, d), lambda bi, i: (bi, i, 0)), sel, sel, sel, sel],
        out_shape=[jax.ShapeDtypeStruct((b, l, d), BF16), sel_f32, sel_f32, sel_bf16, sel_bf16],
        compiler_params=_params("arbitrary", "arbitrary"),
        name="peer_topk",
    )(x, mod, gain, w_q, keys)


def _peer_dense_kernel(x_ref, mod_ref, h_ref, u_ref, vt_ref, n_ref, p1_ref, r2_ref, p2_ref, o_ref,
                       ht_ref, acc_ref, s_ref, gate_ref):
    e = pl.program_id(2)

    @pl.when(e == 0)
    def _():
        ht_ref[...] = h_ref[0].astype(F32).T.astype(BF16)
        acc_ref[...] = jnp.zeros_like(acc_ref)

    ht = ht_ref[...]
    tm = ht.shape[1]
    bounds = [sum(PEER_SUBS[:i]) for i in range(len(PEER_SUBS) + 1)]

    def gate_rows(lo, hi):
        for row in range(lo // PEER_NKEYS, hi // PEER_NKEYS):
            gate = [None] * (PEER_NKEYS // 16)
            for hd in range(PEER_HEADS):
                n_b = jnp.broadcast_to(n_ref[0, hd, row:row + 1, :], (16, tm)).astype(BF16)
                p1_b = jnp.broadcast_to(p1_ref[0, hd, row:row + 1, :], (16, tm)).astype(BF16)
                for k in range(PEER_NKEYS // 16):
                    term = jnp.where(r2_ref[0, hd, 16 * k:16 * (k + 1), :] < n_b,
                                     p2_ref[0, hd, 16 * k:16 * (k + 1), :], jnp.zeros((), BF16)) * p1_b
                    gate[k] = term if gate[k] is None else gate[k] + term
            for k in range(PEER_NKEYS // 16):
                at = PEER_NKEYS * row + 16 * k
                gate_ref[at:at + 16, :] = gate[k]

    for lo, hi in zip(bounds[:-1], bounds[1:]):
        s_ref[lo:hi, :] = _dot(u_ref[lo:hi, :], ht).astype(BF16)
        gate_rows(lo, hi)
    ws = [gate_ref[lo:hi, :] * _gelu(s_ref[lo:hi, :])
          for lo, hi in zip(bounds[:-1], bounds[1:])]
    acc_ref[...] += _dot(vt_ref[...], jnp.concatenate(ws, axis=0))

    @pl.when(e == pl.num_programs(2) - 1)
    def _():
        o_ref[0] = x_ref[0] + mod_ref[0][5:6] * acc_ref[...].T


def _peer_dense(x, mod, h, u_tabs, vt_tabs, layer, n_e1, p1n, rank2, p2, per_batch_mod):
    b, l, d = x.shape
    tm = PEER_TOKENS
    n_exp = u_tabs.shape[1]
    rows = PEER_STEP // PEER_NKEYS
    mod_map = (lambda bi, i, e: (bi, 0, 0)) if per_batch_mod else (lambda bi, i, e: (0, 0, 0))
    tok = lambda: pl.BlockSpec((1, tm, d), lambda bi, i, e: (bi, i, 0))
    by_row = pl.BlockSpec((1, PEER_HEADS, rows, tm), lambda bi, i, e: (bi, 0, e, i))
    by_key = pl.BlockSpec((1, PEER_HEADS, PEER_NKEYS, tm), lambda bi, i, e: (bi, 0, 0, i))
    return pl.pallas_call(
        _peer_dense_kernel,
        grid=(b, l // tm, n_exp // PEER_STEP),
        in_specs=[tok(), pl.BlockSpec((1, 6, d), mod_map), tok(),
                  pl.BlockSpec((None, PEER_STEP, d), lambda bi, i, e: (layer, e, 0)),
                  pl.BlockSpec((None, d, PEER_STEP), lambda bi, i, e: (layer, 0, e)),
                  by_row, by_row, by_key, by_key],
        out_specs=tok(),
        out_shape=jax.ShapeDtypeStruct((b, l, d), F32),
        scratch_shapes=[pltpu.VMEM((d, tm), BF16), pltpu.VMEM((d, tm), F32),
                        pltpu.VMEM((PEER_STEP, tm), BF16), pltpu.VMEM((PEER_STEP, tm), BF16)],
        compiler_params=_params("arbitrary", "arbitrary", "arbitrary"),
        name="peer_dense",
    )(x, mod, h, u_tabs, vt_tabs, n_e1, p1n, rank2, p2)


def _rope_tables(n_lat, width):
    q4 = DIFF_DK // 4
    n_rows = n_lat // GRID_W
    row = jnp.repeat(jnp.arange(n_rows), GRID_W).astype(F32)
    col = jnp.tile(jnp.arange(GRID_W), n_rows).astype(F32)
    freqs = ROPE_BASE ** (-jnp.arange(q4, dtype=F32) / q4)
    ang = jnp.stack([row[:, None] * freqs, col[:, None] * freqs], axis=1)
    cos, sin = jnp.cos(ang), jnp.sin(ang)
    cos64 = jnp.stack([cos, cos], axis=2).reshape(n_lat, 64)
    sin64 = jnp.stack([-sin, sin], axis=2).reshape(n_lat, 64)
    reps = width // 64
    return jnp.tile(cos64, (1, reps)), jnp.tile(sin64, (1, reps))


def _ones_block_diag(width, group):
    idx = jnp.arange(width) // group
    return jnp.where(idx[:, None] == idx[None, :], 1.0 / group, 0.0).astype(BF16)


def _peer_layer(x, mod, W, l, per_batch_mod):
    shape = x.shape
    if not per_batch_mod:
        x = x.reshape(1, -1, shape[-1])
    h, n_e1, p1n, rank2, p2 = _peer_topk(x, mod, W['norm_ffn'][l], W['peer_w_q'][l], W['peer_keys'][l],
                                         per_batch_mod)
    y = _peer_dense(x, mod, h, W['peer_u'], W['peer_vt'], l, n_e1, p1n, rank2, p2, per_batch_mod)
    return y.reshape(shape)


def _even_layer(x, mod, W, l, cache, per_batch_mod):
    b, seq, _ = x.shape
    i = l // 2
    lam_init = 0.8 - 0.6 * math.exp(-0.3 * l)
    rope_tabs = W['rope'] if cache is not None else None
    u_tm2, q, k_norm, k_att, v, v_att = _inproj_even(
        x, mod, W['norm_mix'][l], W['even_w_in'][i], W['ones_bd'], W['diff_q_norm'][i],
        W['diff_k_norm'][i], rope_tabs, per_batch_mod)
    if cache is None:
        h0 = jnp.zeros((2, b, S5_JB * S5_BLK), F32)
        k_all, v_all = k_att, v_att
    else:
        h0 = _s5_state_to_blocks(cache['s5'][:, i])
        k_all = jnp.concatenate([cache['diff_k'][:, i].reshape(b, -1, 512).astype(BF16), k_att], axis=1)
        v_all = jnp.concatenate([cache['diff_v'][:, i].reshape(b, -1, 512).astype(BF16), v_att], axis=1)
    yf, yb, fin = _s5_scan(u_tm2.reshape(seq, b, 512), W['s5_bd'][i], W['s5_cd'][i], W['s5_a'][i], h0)
    o_attn = _diff_attention(q, k_all, v_all, W['diff_lambda'][i], lam_init)
    x = _mixout_even(x, mod, u_tm2, yf.reshape(seq, b * 512), yb.reshape(seq, b * 512), o_attn,
                     W['s5_d'][i], W['s5_w_glu'][i], W['diff_sub_norm'][i], W['even_w_out'][i],
                     1.0 - lam_init, per_batch_mod)
    return x, (k_norm, v, fin)


def _odd_layer(x, mod, W, l, cache, per_batch_mod):
    b, seq, _ = x.shape
    i = l // 2
    rope_tabs = W['rope'] if cache is not None else None
    q, k_norm, k_att, v, v_att, u = _inproj_odd(
        x, mod, W['norm_mix'][l], W['odd_w_in'][i], W['ones_bd'], W['win_q_norm'][i],
        W['win_k_norm'][i], rope_tabs, per_batch_mod)
    if cache is None:
        o_c = _win_attention(q, k_att, v_att, W['win_sink'][i])
    else:
        pad = ((0, 0), (WINDOW, WINDOW), (0, 0))
        o_c = _win_attention(q, cache['win_k'][:, i].reshape(b, -1, 128).astype(BF16),
                             cache['win_v'][:, i].reshape(b, -1, 128).astype(BF16), W['win_sink'][i],
                             jnp.pad(k_att, pad), jnp.pad(v_att, pad))
    o_d = _pool_mixer(u, W['pool_w'][i], W['pool_scale'][i])
    x = _mixout_odd(x, mod, o_c, o_d, W['odd_w_out'][i], per_batch_mod)
    return x, (k_norm, v)


def kernel(x_prompt, x_sample, cache_diff_k, cache_diff_v, state_s5, cache_win_k, cache_win_v, c, c_ctx, ada_w, ada_b, norm_mix, norm_ffn, even_w_in, even_w_out, s5_lam_re, s5_lam_im, s5_log_step, s5_b_re, s5_b_im, s5_c_re, s5_c_im, s5_d, s5_w_glu, diff_q_norm, diff_k_norm, diff_lambda, diff_sub_norm, odd_w_in, odd_w_out, win_q_norm, win_k_norm, win_sink, pool_w, pool_scale, peer_w_q, peer_sub_keys, peer_u, peer_v):
    depth = ada_w.shape[0]
    n_even, n_odd = even_w_in.shape[0], odd_w_in.shape[0]
    bsz, seq, d = x_prompt.shape
    dec_b, dec_l, _ = x_sample.shape

    cvec = jnp.zeros((16, d), F32).at[0].set(c_ctx).at[1:1 + dec_b].set(c)
    mod = _modulation(cvec, ada_w, ada_b)

    s5 = [_s5_tables(s5_lam_re[i], s5_lam_im[i], s5_log_step[i], s5_b_re[i], s5_b_im[i],
                     s5_c_re[i], s5_c_im[i]) for i in range(n_even)]
    W = dict(
        norm_mix=norm_mix.reshape(depth, 1, d), norm_ffn=norm_ffn.reshape(depth, 1, d),
        even_w_in=even_w_in.astype(BF16), even_w_out=even_w_out.astype(BF16),
        odd_w_in=odd_w_in.astype(BF16), odd_w_out=odd_w_out.astype(BF16),
        ones_bd=_ones_block_diag(512, 64),
        diff_q_norm=jnp.tile(diff_q_norm, (1, 8)).reshape(n_even, 1, 512),
        diff_k_norm=jnp.tile(diff_k_norm, (1, 8)).reshape(n_even, 1, 512),
        diff_lambda=diff_lambda, diff_sub_norm=diff_sub_norm.reshape(n_even, 1, DIFF_DV),
        s5_bd=[t[0] for t in s5], s5_cd=[t[1] for t in s5], s5_a=[t[2] for t in s5],
        s5_d=s5_d.reshape(n_even, 1, 512), s5_w_glu=s5_w_glu.astype(BF16),
        win_q_norm=jnp.tile(win_q_norm, (1, 8)).reshape(n_odd, 1, 512),
        win_k_norm=jnp.tile(win_k_norm, (1, 2)).reshape(n_odd, 1, 128),
        win_sink=jnp.broadcast_to(win_sink[:, :, None], (n_odd, 8, 128)),
        pool_w=pool_w.astype(BF16), pool_scale=pool_scale.reshape(n_odd, 1, 512),
        peer_w_q=peer_w_q.astype(BF16),
        peer_keys=peer_sub_keys.astype(BF16).reshape(depth, 2 * PEER_HEADS, PEER_NKEYS, -1),
        peer_u=peer_u.astype(BF16),
        peer_vt=jnp.swapaxes(peer_v, 1, 2).astype(BF16),
        rope=_rope_tables(dec_l, 512),
    )
    cache = {'diff_k': cache_diff_k, 'diff_v': cache_diff_v, 's5': state_s5,
             'win_k': cache_win_k, 'win_v': cache_win_v}

    def run(x, mods, cch, per_batch_mod):
        states = []
        for l in range(depth):
            if l % 2 == 0:
                x, st = _even_layer(x, mods[l], W, l, cch, per_batch_mod)
            else:
                x, st = _odd_layer(x, mods[l], W, l, cch, per_batch_mod)
            states.append(st)
            x = _peer_layer(x, mods[l], W, l, per_batch_mod)
        return x, states

    ctx_mods = [mod[l, 0:1].reshape(1, 6, d) for l in range(depth)]
    dec_mods = [mod[l, 1:1 + dec_b].reshape(dec_b, 6, d) for l in range(depth)]
    y_prompt, states = run(x_prompt, ctx_mods, None, False)
    y_sample, _ = run(x_sample, dec_mods, cache, True)

    evens = [states[l] for l in range(depth) if l % 2 == 0]
    odds = [states[l] for l in range(depth) if l % 2 == 1]
    new_diff_k = jnp.stack([s[0].reshape(bsz, seq, 2, DIFF_HEADS, DIFF_DK) for s in evens], axis=1)
    new_diff_v = jnp.stack([s[1].reshape(bsz, seq, DIFF_HEADS, DIFF_DV) for s in evens], axis=1)
    new_s5 = jnp.stack([_s5_blocks_to_state(s[2]) for s in evens], axis=1)
    new_win_k = jnp.stack([s[0].reshape(bsz, seq, WIN_KV_HEADS, WIN_DH) for s in odds], axis=1)
    new_win_v = jnp.stack([s[1].reshape(bsz, seq, WIN_KV_HEADS, WIN_DH) for s in odds], axis=1)
    return (y_prompt, y_sample, new_diff_k, new_diff_v, new_s5, new_win_k, new_win_v)
```

```python
import functools
import math

import jax
import jax.numpy as jnp
from jax import lax
from jax.experimental import pallas as pl
from jax.experimental.pallas import tpu as pltpu

F32 = jnp.float32
BF16 = jnp.bfloat16

D_MODEL = 1024
MIX_HALF = 512
GRID_W = 64
EPS = 1e-6
NEG_INF = -1e30
ROPE_BASE = 10000.0

S5_GROUPS = 32
S5_GROUP = 16
S5_STATE = 64
S5_JB = 4
S5_BLK = 1024
S5_CHUNK = 32
S5_BATCH = 8

DIFF_HEADS = 4
DIFF_DK = 64
DIFF_DV = 128
WIN_KV_HEADS = 2
WIN_GROUP = 4
WIN_DH = 64
ATTN_SCALE = DIFF_DK ** -0.5
WINDOW = 128
POOL_WINDOWS = (2, 4, 8, 16)
POOL_PAD = 8

PEER_HEADS = 8
PEER_NKEYS = 128
PEER_TOPK = 16
RANK_CODE = 2.0 ** 100
PEER_SUBS = (128, 384, 512, 1024)
PEER_STEP = sum(PEER_SUBS)
PEER_TOKENS = 512

TOKEN_TILE = 256
VMEM_LIMIT = 56 * 1024 * 1024


def _params(*sem):
    return pltpu.CompilerParams(dimension_semantics=sem, vmem_limit_bytes=VMEM_LIMIT)


def _dot(a, b):
    return jnp.dot(a, b, preferred_element_type=F32)


def _dot_nt(a, b):
    return lax.dot_general(a, b, (((1,), (1,)), ((), ())), preferred_element_type=F32)


def _sigmoid(x):
    return 1.0 / (1.0 + jnp.exp(-x))


def _gelu(x):
    c = math.sqrt(2.0 / math.pi)
    hx = 0.5 * x
    return hx + hx * jnp.tanh(x * (c + (c * 0.044715) * (x * x)))


def _modulate(x, gain, shift, scale):
    ms = jnp.mean(x * x, axis=-1, keepdims=True)
    return (x * lax.rsqrt(ms + EPS) * gain) * (1.0 + scale) + shift


def _group_rms(z, ones_bd, gain):
    ms = _dot((z * z).astype(BF16), ones_bd)
    return z * lax.rsqrt(ms + EPS) * gain


def _rope(z, cos_t, sin_t):
    lane = lax.broadcasted_iota(jnp.int32, (1, 128), 1) % 32
    first = lane < 16
    parts = []
    for j in range(z.shape[1] // 128):
        c = z[:, 128 * j:128 * (j + 1)]
        parts.append(jnp.where(first, pltpu.roll(c, 112, axis=1), pltpu.roll(c, 16, axis=1)))
    swapped = parts[0] if len(parts) == 1 else jnp.concatenate(parts, axis=1)
    return z * cos_t + swapped * sin_t


def _mod_kernel(c_ref, w_ref, b_ref, o_ref):
    c = c_ref[...]
    s = c * _sigmoid(c)
    o_ref[0] = jnp.dot(s, w_ref[0], precision=lax.Precision.HIGHEST,
                       preferred_element_type=F32) + b_ref[0]


def _modulation(cvec, ada_w, ada_b):
    depth, d, n = ada_w.shape
    tn = 1536
    return pl.pallas_call(
        _mod_kernel,
        grid=(depth, n // tn),
        in_specs=[pl.BlockSpec((16, d), lambda l, j: (0, 0)),
                  pl.BlockSpec((1, d, tn), lambda l, j: (l, 0, j)),
                  pl.BlockSpec((1, 1, tn), lambda l, j: (l, 0, j))],
        out_specs=pl.BlockSpec((1, 16, tn), lambda l, j: (l, 0, j)),
        out_shape=jax.ShapeDtypeStruct((depth, 16, n), F32),
        compiler_params=_params("arbitrary", "arbitrary"),
        name="adaln_mod",
    )(cvec, ada_w, ada_b.reshape(depth, 1, n))


def _inproj_even_kernel(rope, *refs):
    if rope:
        (x_ref, mod_ref, g_ref, w_ref, bd_ref, gq_ref, gk_ref, cos_ref, sin_ref,
         u_ref, q_ref, kn_ref, ka_ref, v_ref, va_ref) = refs
    else:
        (x_ref, mod_ref, g_ref, w_ref, bd_ref, gq_ref, gk_ref,
         u_ref, q_ref, kn_ref, ka_ref, v_ref, va_ref) = refs
    mod = mod_ref[0]
    h = _modulate(x_ref[0], g_ref[...], mod[0:1], mod[1:2])
    z = _dot(h.astype(BF16), w_ref[...])
    u = z[:, 0:512]
    q = _group_rms(z[:, 512:1024], bd_ref[...], gq_ref[...])
    k = _group_rms(z[:, 1024:1536], bd_ref[...], gk_ref[...])
    v = z[:, 1536:2048]
    u_ref[...] = u
    kn_ref[0] = k
    v_ref[0] = v
    va_ref[0] = v.astype(BF16)
    if rope:
        q = _rope(q, cos_ref[...], sin_ref[...])
        k = _rope(k, cos_ref[...], sin_ref[...])
    q_ref[0] = (q * ATTN_SCALE).astype(BF16)
    ka_ref[0] = k.astype(BF16)


def _inproj_even(x, mod, gain, w, ones_bd, gq, gk, rope_tabs, per_batch_mod):
    b, l, d = x.shape
    tl = TOKEN_TILE
    rope = rope_tabs is not None
    mod_map = (lambda bi, i: (bi, 0, 0)) if per_batch_mod else (lambda bi, i: (0, 0, 0))
    const2 = lambda bi, i: (0, 0)
    tok = lambda w_: pl.BlockSpec((1, tl, w_), lambda bi, i: (bi, i, 0))
    in_specs = [tok(d),
                pl.BlockSpec((1, 6, d), mod_map),
                pl.BlockSpec((1, d), const2),
                pl.BlockSpec(w.shape, const2),
                pl.BlockSpec(ones_bd.shape, const2),
                pl.BlockSpec((1, 512), const2),
                pl.BlockSpec((1, 512), const2)]
    args = [x, mod, gain, w, ones_bd, gq, gk]
    if rope:
        in_specs += [pl.BlockSpec((tl, 512), lambda bi, i: (i, 0))] * 2
        args += list(rope_tabs)
    out_shape = [jax.ShapeDtypeStruct((l, b * 512), F32),
                 jax.ShapeDtypeStruct((b, l, 512), BF16),
                 jax.ShapeDtypeStruct((b, l, 512), F32),
                 jax.ShapeDtypeStruct((b, l, 512), BF16),
                 jax.ShapeDtypeStruct((b, l, 512), F32),
                 jax.ShapeDtypeStruct((b, l, 512), BF16)]
    out_specs = [pl.BlockSpec((tl, 512), lambda bi, i: (i, bi)),
                 tok(512), tok(512), tok(512), tok(512), tok(512)]
    return pl.pallas_call(
        functools.partial(_inproj_even_kernel, rope),
        grid=(b, l // tl), in_specs=in_specs, out_specs=out_specs, out_shape=out_shape,
        compiler_params=_params("arbitrary", "arbitrary"),
        name="inproj_even_rope" if rope else "inproj_even",
    )(*args)


def _inproj_odd_kernel(rope, *refs):
    if rope:
        (x_ref, mod_ref, g_ref, w_ref, bd_ref, gq_ref, gk_ref, cos_ref, sin_ref,
         q_ref, kn_ref, ka_ref, v_ref, va_ref, u_ref) = refs
    else:
        (x_ref, mod_ref, g_ref, w_ref, bd_ref, gq_ref, gk_ref,
         q_ref, kn_ref, ka_ref, v_ref, va_ref, u_ref) = refs
    mod = mod_ref[0]
    h = _modulate(x_ref[0], g_ref[...], mod[0:1], mod[1:2])
    z = _dot(h.astype(BF16), w_ref[...])
    bd = bd_ref[...]
    q = _group_rms(z[:, 0:512], bd, gq_ref[...])
    k = _group_rms(z[:, 512:640], bd[0:128, 0:128], gk_ref[...])
    v = z[:, 640:768]
    kn_ref[0] = k
    v_ref[0] = v
    va_ref[0] = v.astype(BF16)
    u_ref[0] = z[:, 768:1280]
    if rope:
        q = _rope(q, cos_ref[...], sin_ref[...])
        k = _rope(k, cos_ref[:, 0:128], sin_ref[:, 0:128])
    q_ref[0] = (q * ATTN_SCALE).astype(BF16)
    ka_ref[0] = k.astype(BF16)


def _inproj_odd(x, mod, gain, w, ones_bd, gq, gk, rope_tabs, per_batch_mod):
    b, l, d = x.shape
    tl = TOKEN_TILE
    rope = rope_tabs is not None
    mod_map = (lambda bi, i: (bi, 0, 0)) if per_batch_mod else (lambda bi, i: (0, 0, 0))
    const2 = lambda bi, i: (0, 0)
    tok = lambda w_: pl.BlockSpec((1, tl, w_), lambda bi, i: (bi, i, 0))
    in_specs = [tok(d),
                pl.BlockSpec((1, 6, d), mod_map),
                pl.BlockSpec((1, d), const2),
                pl.BlockSpec(w.shape, const2),
                pl.BlockSpec(ones_bd.shape, const2),
                pl.BlockSpec((1, 512), const2),
                pl.BlockSpec((1, 128), const2)]
    args = [x, mod, gain, w, ones_bd, gq, gk]
    if rope:
        in_specs += [pl.BlockSpec((tl, 512), lambda bi, i: (i, 0))] * 2
        args += list(rope_tabs)
    out_shape = [jax.ShapeDtypeStruct((b, l, 512), BF16),
                 jax.ShapeDtypeStruct((b, l, 128), F32),
                 jax.ShapeDtypeStruct((b, l, 128), BF16),
                 jax.ShapeDtypeStruct((b, l, 128), F32),
                 jax.ShapeDtypeStruct((b, l, 128), BF16),
                 jax.ShapeDtypeStruct((b, l, 512), F32)]
    out_specs = [tok(512), tok(128), tok(128), tok(128), tok(128), tok(512)]
    return pl.pallas_call(
        functools.partial(_inproj_odd_kernel, rope),
        grid=(b, l // tl), in_specs=in_specs, out_specs=out_specs, out_shape=out_shape,
        compiler_params=_params("arbitrary", "arbitrary"),
        name="inproj_odd_rope" if rope else "inproj_odd",
    )(*args)


def _s5_kernel(uf_ref, ub_ref, bd_ref, cd_ref, a_ref, h0_ref, yf_ref, yb_ref, fin_ref,
               bu_ref, carry_ref):
    c = pl.program_id(1)
    rows = S5_CHUNK * S5_BATCH

    @pl.when(c == 0)
    def _():
        carry_ref[...] = h0_ref[...]

    for d in range(2):
        u = (uf_ref if d == 0 else ub_ref)[...].reshape(rows, MIX_HALF).astype(BF16)
        for j in range(S5_JB):
            bu_ref[d, :, S5_BLK * j:S5_BLK * (j + 1)] = _dot(u[:, 128 * j:128 * (j + 1)], bd_ref[d, j])
        for j in range(S5_JB):
            lo = S5_BLK * j
            a_re = jnp.broadcast_to(a_ref[d, 0, j:j + 1, :], (S5_BATCH, 512))
            a_im = jnp.broadcast_to(a_ref[d, 1, j:j + 1, :], (S5_BATCH, 512))

            h_re, h_im = carry_ref[d, :, lo:lo + 512], carry_ref[d, :, lo + 512:lo + 1024]
            for s in range(S5_CHUNK):
                row = (s if d == 0 else S5_CHUNK - 1 - s) * S5_BATCH
                b_re = bu_ref[d, row:row + S5_BATCH, lo:lo + 512]
                b_im = bu_ref[d, row:row + S5_BATCH, lo + 512:lo + 1024]
                h_re, h_im = a_re * h_re - a_im * h_im + b_re, a_re * h_im + a_im * h_re + b_im
                bu_ref[d, row:row + S5_BATCH, lo:lo + 512] = h_re
                bu_ref[d, row:row + S5_BATCH, lo + 512:lo + 1024] = h_im
            carry_ref[d, :, lo:lo + 512] = h_re
            carry_ref[d, :, lo + 512:lo + 1024] = h_im
        ys = [_dot(bu_ref[d, :, S5_BLK * j:S5_BLK * (j + 1)].astype(BF16), cd_ref[d, j])
              for j in range(S5_JB)]
        y = jnp.concatenate(ys, axis=1).reshape(S5_CHUNK, S5_BATCH, MIX_HALF)
        if d == 0:
            yf_ref[...] = y
        else:
            yb_ref[...] = y

    @pl.when(c == pl.num_programs(1) - 1)
    def _():
        fin_ref[...] = carry_ref[...]


def _s5_scan(u_tm, bd, cd, acoef, h0):
    l, b, _ = u_tm.shape
    nc = l // S5_CHUNK
    ng = b // S5_BATCH
    state_w = S5_JB * S5_BLK
    blk = (S5_CHUNK, S5_BATCH, MIX_HALF)
    full = lambda a: pl.BlockSpec(a.shape, lambda g, c: (0,) * a.ndim)
    return pl.pallas_call(
        _s5_kernel,
        grid=(ng, nc),
        in_specs=[pl.BlockSpec(blk, lambda g, c: (c, g, 0)),
                  pl.BlockSpec(blk, lambda g, c: (nc - 1 - c, g, 0)),
                  full(bd), full(cd), full(acoef),
                  pl.BlockSpec((2, S5_BATCH, state_w), lambda g, c: (0, g, 0))],
        out_specs=[pl.BlockSpec(blk, lambda g, c: (c, g, 0)),
                   pl.BlockSpec(blk, lambda g, c: (nc - 1 - c, g, 0)),
                   pl.BlockSpec((2, S5_BATCH, state_w), lambda g, c: (0, g, 0))],
        out_shape=[jax.ShapeDtypeStruct((l, b, MIX_HALF), F32),
                   jax.ShapeDtypeStruct((l, b, MIX_HALF), F32),
                   jax.ShapeDtypeStruct((2, b, state_w), F32)],
        scratch_shapes=[pltpu.VMEM((2, S5_CHUNK * S5_BATCH, state_w), F32),
                        pltpu.VMEM((2, S5_BATCH, state_w), F32)],
        compiler_params=_params("arbitrary", "arbitrary"),
        name="s5_scan",
    )(u_tm, u_tm, bd, cd, acoef, h0)


def _s5_tables(lam_re, lam_im, log_step, b_re, b_im, c_re, c_im):
    step = jnp.exp(log_step.astype(F32))[..., None]
    lr, li = lam_re.astype(F32), lam_im.astype(F32)
    er = jnp.exp(lr * step)
    a_re, a_im = er * jnp.cos(li * step), er * jnp.sin(li * step)
    den = lr * lr + li * li
    q_re = ((a_re - 1.0) * lr + a_im * li) / den
    q_im = (a_im * lr - (a_re - 1.0) * li) / den
    bb_re = q_re[..., None] * b_re - q_im[..., None] * b_im
    bb_im = q_re[..., None] * b_im + q_im[..., None] * b_re
    eye = jnp.eye(8, dtype=F32)
    bb = jnp.stack([bb_re, bb_im], axis=1).reshape(2, 2, S5_JB, 8, S5_STATE, S5_GROUP)
    bd = jnp.einsum('drjgpc,gh->djgcrhp', bb, eye).reshape(2, S5_JB, 128, S5_BLK).astype(BF16)
    cc = jnp.stack([c_re, -c_im], axis=1).astype(F32).reshape(2, 2, S5_JB, 8, S5_GROUP, S5_STATE)
    cd = jnp.einsum('drjgcp,gh->djrgphc', cc, eye).reshape(2, S5_JB, S5_BLK, 128).astype(BF16)
    acoef = jnp.stack([a_re, a_im], axis=1).reshape(2, 2, S5_JB, 512)
    return bd, cd, acoef


def _s5_state_to_blocks(st):
    b = st.shape[0]
    st = st.astype(F32).reshape(b, 2, 2, S5_JB, 8, S5_STATE)
    return jnp.transpose(st, (1, 0, 3, 2, 4, 5)).reshape(2, b, S5_JB * S5_BLK)


def _s5_blocks_to_state(fin):
    b = fin.shape[1]
    fin = fin.reshape(2, b, S5_JB, 2, 8, S5_STATE)
    return jnp.transpose(fin, (1, 0, 3, 2, 4, 5)).reshape(b, 2, 2, S5_GROUPS, S5_STATE)


def _diff_attn_kernel(lam_init, q_ref, k_ref, v_ref, lv_ref, o_ref):
    lv = lv_ref[...]
    lam = (jnp.exp(jnp.sum(lv[0:1] * lv[1:2], axis=-1, keepdims=True))
           - jnp.exp(jnp.sum(lv[2:3] * lv[3:4], axis=-1, keepdims=True)) + lam_init)
    lane = lax.broadcasted_iota(jnp.int32, (1, 128), 1)
    outs = []
    for h in range(DIFF_HEADS):
        es, sums = [], []
        for m in range(2):
            j = m * 2 + h // 2
            keep = (lane < 64) if h % 2 == 0 else (lane >= 64)
            qb = jnp.where(keep, q_ref[0, :, 128 * j:128 * (j + 1)], jnp.zeros((), BF16))
            s = _dot_nt(qb, k_ref[0, :, 128 * j:128 * (j + 1)])
            e = jnp.exp(s - jnp.max(s, axis=-1, keepdims=True))
            es.append(e)
            sums.append(jnp.sum(e, axis=-1, keepdims=True))
        a = es[0] - (lam * sums[0] / sums[1]) * es[1]
        outs.append(_dot(a.astype(BF16), v_ref[0, :, 128 * h:128 * (h + 1)]) / sums[0])
    o_ref[0] = jnp.concatenate(outs, axis=1)


def _diff_attention(q, k, v, lv, lam_init):
    b, lq, _ = q.shape
    lk = k.shape[1]
    tq = TOKEN_TILE
    return pl.pallas_call(
        functools.partial(_diff_attn_kernel, lam_init),
        grid=(b, lq // tq),
        in_specs=[pl.BlockSpec((1, tq, 512), lambda bi, i: (bi, i, 0)),
                  pl.BlockSpec((1, lk, 512), lambda bi, i: (bi, 0, 0)),
                  pl.BlockSpec((1, lk, 512), lambda bi, i: (bi, 0, 0)),
                  pl.BlockSpec((4, DIFF_DK), lambda bi, i: (0, 0))],
        out_specs=pl.BlockSpec((1, tq, 512), lambda bi, i: (bi, i, 0)),
        out_shape=jax.ShapeDtypeStruct((b, lq, 512), F32),
        compiler_params=_params("arbitrary", "arbitrary"),
        name="diff_attention",
    )(q, k, v, lv)


def _win_attn_kernel(local, seq_len, *refs):
    if local:
        q_ref, kc_ref, vc_ref, kl_ref, vl_ref, sink_ref, o_ref = refs
    else:
        q_ref, kc_ref, vc_ref, sink_ref, o_ref = refs
    tq = q_ref.shape[1]
    lane = lax.broadcasted_iota(jnp.int32, (1, 128), 1)
    low = lane < 64
    kc = kc_ref[0]
    vc = vc_ref[0]
    n_ctx = kc.shape[0]
    if local:
        span = tq + 2 * WINDOW
        start = pl.multiple_of(pl.program_id(1) * tq, tq)
        kl = kl_ref[0, pl.ds(start, span), :]
        vl = vl_ref[0, pl.ds(start, span), :]
        qpos = start + lax.broadcasted_iota(jnp.int32, (tq, span), 0)
        kpos = start - WINDOW + lax.broadcasted_iota(jnp.int32, (tq, span), 1)
        ok = (jnp.abs(qpos - kpos) <= WINDOW) & (kpos >= 0) & (kpos < seq_len)
    heads = {}
    for n in range(WIN_KV_HEADS):
        keep = low if n == 0 else jnp.logical_not(low)
        for g in range(WIN_GROUP):
            hd = n * WIN_GROUP + g
            j, half = hd // 2, hd % 2
            qb = q_ref[0, :, 128 * j:128 * (j + 1)]
            if half != n:
                qb = jnp.concatenate([qb[:, 64:128], qb[:, 0:64]], axis=1)
            qb = jnp.where(keep, qb, jnp.zeros((), BF16))
            sink = sink_ref[hd:hd + 1, 0:1]
            s = _dot_nt(qb, kc)
            if local:
                s = jnp.concatenate([s, jnp.where(ok, _dot_nt(qb, kl), NEG_INF)], axis=1)
            mx = jnp.maximum(jnp.max(s, axis=-1, keepdims=True), sink)
            e = jnp.exp(s - mx)
            den = jnp.sum(e, axis=-1, keepdims=True) + jnp.exp(sink - mx)
            p = e.astype(BF16)
            o = _dot(p[:, 0:n_ctx], vc)
            if local:
                o = o + _dot(p[:, n_ctx:], vl)
            heads[hd] = o / den
    blocks = []
    for j in range(4):
        ev, od = heads[2 * j], heads[2 * j + 1]
        n = (2 * j) // WIN_GROUP
        if n == 1:
            ev = pltpu.roll(ev, 64, axis=1)
        else:
            od = pltpu.roll(od, 64, axis=1)
        blocks.append(jnp.where(low, ev, od))
    o_ref[0] = jnp.concatenate(blocks, axis=1).astype(BF16)


def _win_attention(q, kc, vc, sink, k_lat=None, v_lat=None):
    b, lq, _ = q.shape
    n_ctx = kc.shape[1]
    local = k_lat is not None
    tq = TOKEN_TILE
    in_specs = [pl.BlockSpec((1, tq, 512), lambda bi, i: (bi, i, 0)),
                pl.BlockSpec((1, n_ctx, 128), lambda bi, i: (bi, 0, 0)),
                pl.BlockSpec((1, n_ctx, 128), lambda bi, i: (bi, 0, 0))]
    args = [q, kc, vc]
    if local:
        lp = k_lat.shape[1]
        in_specs += [pl.BlockSpec((1, lp, 128), lambda bi, i: (bi, 0, 0))] * 2
        args += [k_lat, v_lat]
    in_specs.append(pl.BlockSpec((8, 128), lambda bi, i: (0, 0)))
    args.append(sink)
    return pl.pallas_call(
        functools.partial(_win_attn_kernel, local, lq),
        grid=(b, lq // tq), in_specs=in_specs,
        out_specs=pl.BlockSpec((1, tq, 512), lambda bi, i: (bi, i, 0)),
        out_shape=jax.ShapeDtypeStruct((b, lq, 512), BF16),
        compiler_params=_params("arbitrary", "arbitrary"),
        name="win_attention_local" if local else "win_attention",
    )(*args)


def _pool_kernel(u_ref, w_ref, scale_ref, o_ref):
    l = u_ref.shape[1]
    lp = l + 2 * POOL_PAD
    t = lax.broadcasted_iota(jnp.int32, (l, 128), 0)
    zpad = jnp.zeros((POOL_PAD, 128), F32)

    def shifted(a, s):
        return pltpu.roll(a, s, axis=0) + pltpu.roll(a, lp - s, axis=0)

    outs = []
    for gi, wl in enumerate(POOL_WINDOWS):
        x = u_ref[0, :, 128 * gi:128 * (gi + 1)]
        xp = jnp.concatenate([zpad, x, zpad], axis=0)
        acc = xp + pltpu.roll(xp, 1, axis=0)
        if wl >= 4:
            acc = shifted(acc, 1)
        if wl >= 8:
            acc = shifted(acc, 2)
        if wl >= 16:
            acc = shifted(acc, 4)
        win = acc[POOL_PAD:POOL_PAD + l]
        cnt = (jnp.minimum(t + wl // 2, l) - jnp.maximum(t - wl // 2, 0)).astype(F32)
        pooled = win / cnt - x
        outs.append(_dot(pooled.astype(BF16), w_ref[gi]))
    o_ref[0] = (jnp.concatenate(outs, axis=1) * scale_ref[...]).astype(BF16)


def _pool_mixer(u, w, scale):
    b, l, _ = u.shape
    return pl.pallas_call(
        _pool_kernel,
        grid=(b,),
        in_specs=[pl.BlockSpec((1, l, 512), lambda bi: (bi, 0, 0)),
                  pl.BlockSpec(w.shape, lambda bi: (0, 0, 0)),
                  pl.BlockSpec((1, 512), lambda bi: (0, 0))],
        out_specs=pl.BlockSpec((1, l, 512), lambda bi: (bi, 0, 0)),
        out_shape=jax.ShapeDtypeStruct((b, l, 512), BF16),
        compiler_params=_params("arbitrary"),
        name="pool_mixer",
    )(u, w, scale)


def _mixout_even_kernel(out_scale, x_ref, mod_ref, u_ref, yf_ref, yb_ref, oa_ref, sd_ref,
                        wglu_ref, gsub_ref, wout_ref, o_ref):
    mod = mod_ref[0]
    y = sd_ref[...] * u_ref[...] + yf_ref[...] + yb_ref[...]
    g = _gelu(y)
    o_a = g * _sigmoid(_dot(g.astype(BF16), wglu_ref[...]))
    parts = [o_a.astype(BF16)]
    for h in range(DIFF_HEADS):
        blk = oa_ref[0, :, 128 * h:128 * (h + 1)]
        ms = jnp.mean(blk * blk, axis=-1, keepdims=True)
        parts.append(((blk * lax.rsqrt(ms + EPS) * gsub_ref[...]) * out_scale).astype(BF16))
    cat = jnp.concatenate(parts, axis=1)
    o_ref[0] = x_ref[0] + mod[2:3] * _dot(cat, wout_ref[...])


def _mixout_even(x, mod, u_tm2, yf2, yb2, o_attn, s5_d, w_glu, g_sub, w_out, out_scale, per_batch_mod):
    b, l, d = x.shape
    tl = TOKEN_TILE
    mod_map = (lambda bi, i: (bi, 0, 0)) if per_batch_mod else (lambda bi, i: (0, 0, 0))
    const2 = lambda bi, i: (0, 0)
    tm = pl.BlockSpec((tl, 512), lambda bi, i: (i, bi))
    return pl.pallas_call(
        functools.partial(_mixout_even_kernel, out_scale),
        grid=(b, l // tl),
        in_specs=[pl.BlockSpec((1, tl, d), lambda bi, i: (bi, i, 0)),
                  pl.BlockSpec((1, 6, d), mod_map),
                  tm, tm, tm,
                  pl.BlockSpec((1, tl, 512), lambda bi, i: (bi, i, 0)),
                  pl.BlockSpec((1, 512), const2),
                  pl.BlockSpec((512, 512), const2),
                  pl.BlockSpec((1, 128), const2),
                  pl.BlockSpec((d, d), const2)],
        out_specs=pl.BlockSpec((1, tl, d), lambda bi, i: (bi, i, 0)),
        out_shape=jax.ShapeDtypeStruct((b, l, d), F32),
        compiler_params=_params("arbitrary", "arbitrary"),
        name="mixout_even",
    )(x, mod, u_tm2, yf2, yb2, o_attn, s5_d, w_glu, g_sub, w_out)


def _mixout_odd_kernel(x_ref, mod_ref, oc_ref, od_ref, wout_ref, o_ref):
    mod = mod_ref[0]
    cat = jnp.concatenate([oc_ref[0], od_ref[0]], axis=1)
    o_ref[0] = x_ref[0] + mod[2:3] * _dot(cat, wout_ref[...])


def _mixout_odd(x, mod, o_c, o_d, w_out, per_batch_mod):
    b, l, d = x.shape
    tl = TOKEN_TILE
    mod_map = (lambda bi, i: (bi, 0, 0)) if per_batch_mod else (lambda bi, i: (0, 0, 0))
    tok = lambda w_: pl.BlockSpec((1, tl, w_), lambda bi, i: (bi, i, 0))
    return pl.pallas_call(
        _mixout_odd_kernel,
        grid=(b, l // tl),
        in_specs=[tok(d), pl.BlockSpec((1, 6, d), mod_map), tok(512), tok(512),
                  pl.BlockSpec((d, d), lambda bi, i: (0, 0))],
        out_specs=tok(d),
        out_shape=jax.ShapeDtypeStruct((b, l, d), F32),
        compiler_params=_params("arbitrary", "arbitrary"),
        name="mixout_odd",
    )(x, mod, o_c, o_d, w_out)


def _rank16(s, kio, exact):
    n = s.shape[0]
    vals = []
    if exact:
        rank = jnp.full(s.shape, float(PEER_TOPK), F32)
        for r in range(PEER_TOPK):
            m = jnp.max(s, axis=0, keepdims=True)
            ix = jnp.min(jnp.where(s == m, kio, float(n)), axis=0, keepdims=True)
            hit = kio == ix
            rank = jnp.where(hit, float(r), rank)
            s = jnp.where(hit, -jnp.inf, s)
            vals.append(m)
        return vals, rank
    for r in range(PEER_TOPK):
        m = jnp.max(s, axis=0, keepdims=True)
        s = jnp.where(s == m, -RANK_CODE * (1.0 + r / 32.0), s)
        vals.append(m)
    code = s * (-1.0 / RANK_CODE)
    return vals, jnp.where(code >= 1.0, (code - 1.0) * 32.0, float(PEER_TOPK))


_SORT16 = ((0, 1), (2, 3), (0, 2), (1, 3), (1, 2), (4, 5), (6, 7), (4, 6), (5, 7), (5, 6), (0, 4), (2, 6), (2, 4),
           (1, 5), (3, 7), (3, 5), (1, 2), (3, 4), (5, 6), (8, 9), (10, 11), (8, 10), (9, 11), (9, 10), (12, 13),
           (14, 15), (12, 14), (13, 15), (13, 14), (8, 12), (10, 14), (10, 12), (9, 13), (11, 15), (11, 13), (9, 10),
           (11, 12), (13, 14), (0, 8), (4, 12), (4, 8), (2, 10), (6, 14), (6, 10), (2, 4), (6, 8), (10, 12), (1, 9),
           (5, 13), (5, 9), (3, 11), (7, 15), (7, 11), (3, 5), (7, 9), (11, 13), (1, 2), (3, 4), (5, 6), (7, 8),
           (9, 10), (11, 12), (13, 14))


def _sorted_top16(s):
    cols = [s[8 * i:8 * (i + 1), :] for i in range(16)]
    for i, j in _SORT16:
        cols[i], cols[j] = jnp.maximum(cols[i], cols[j]), jnp.minimum(cols[i], cols[j])
    vals = []
    for r in range(PEER_TOPK):
        m = cols[0]
        for shift in (4, 2, 1):
            m = jnp.maximum(m, pltpu.roll(m, shift, axis=0))
        vals.append(m[0:1])
        hit = cols[0] == m
        for i in range(PEER_TOPK - 1 - r):
            cols[i] = jnp.where(hit, cols[i + 1], cols[i])
    return vals


def _candidate_rows(v1, v2, j):
    v1hi = jnp.concatenate(v1[8:16], axis=0)
    v2lo = jnp.concatenate(v2[0:8], axis=0)
    v2hi = jnp.concatenate(v2[8:16], axis=0)
    g3_a = jnp.where(j < 5.0, v1[2], v1[4])
    g3_b = jnp.where(j < 5.0, v2lo, pltpu.roll(v2lo, 5, axis=0))
    g4_a = jnp.where(j < 4.0, v1[3], jnp.where(j < 6.0, v1[5], v1[6]))
    g4_b = jnp.where(j < 4.0, v2lo, jnp.where(j < 6.0, pltpu.roll(v2lo, 4, axis=0), pltpu.roll(v2lo, 6, axis=0)))
    return jnp.concatenate([v1[0] + v2lo, v1[0] + v2hi, v1[1] + v2lo, g3_a + g3_b, g4_a + g4_b,
                            jnp.where(j < 2.0, v1[7] + v2lo, -jnp.inf), v1hi + v2[0]], axis=0)


def _staircase_select(cand, j):
    rest = cand
    for r in range(PEER_TOPK):
        m = jnp.max(rest, axis=0, keepdims=True)
        if r == 0:
            best = m
        rest = jnp.where(rest == m, -jnp.inf, rest)
    picked = cand >= m
    z = jnp.sum(jnp.where(picked, jnp.exp(cand - best), 0.0), axis=0, keepdims=True)
    g = [jnp.where(picked[8 * i:8 * (i + 1)], 1.0, 0.0) for i in range(7)]
    tot = lambda v: jnp.sum(v, axis=0, keepdims=True)
    n = [None] * PEER_TOPK
    n[0], n[1], n[7] = tot(g[0] + g[1]), tot(g[2]), tot(g[5])
    n[2] = tot(jnp.where(j < 5.0, g[3], 0.0))
    n[4] = tot(g[3]) - n[2]
    n[3] = tot(jnp.where(j < 4.0, g[4], 0.0))
    n[5] = tot(jnp.where(j < 6.0, g[4], 0.0)) - n[3]
    n[6] = tot(g[4]) - n[3] - n[5]
    for k in range(8):
        n[8 + k] = g[6][k:k + 1]
    total = n[0]
    for a in range(1, PEER_TOPK):
        total = total + n[a]
    return z, n, total


def _candidate_positions(j):
    return jnp.concatenate([j, 8.0 + j, 16.0 + j,
                            jnp.where(j < 5.0, 32.0 + j, 59.0 + j),
                            jnp.where(j < 4.0, 48.0 + j, jnp.where(j < 6.0, 76.0 + j, 90.0 + j)),
                            jnp.where(j < 2.0, 112.0 + j, 999.0), 128.0 + 16.0 * j], axis=0)


def _peer_topk_kernel(x_ref, mod_ref, g_ref, wq_ref, keys_ref, h_ref, n_ref, p1_ref, r2_ref, p2_ref):
    mod = mod_ref[0]
    h = _modulate(x_ref[0], g_ref[...], mod[3:4], mod[4:5]).astype(BF16)
    h_ref[0] = h
    q = _dot(h, wq_ref[...])
    tl = q.shape[0]
    lanes = tl
    kio = lax.broadcasted_iota(jnp.int32, (PEER_NKEYS, lanes), 0).astype(F32)
    jrow = lax.broadcasted_iota(jnp.int32, (8, lanes), 0).astype(F32)
    pos = _candidate_positions(jrow)

    def head(hd, s1, s2, exact, at):
        s1, s2 = s1[:, at:at + lanes], s2[:, at:at + lanes]
        if exact:
            v1, rank1 = _rank16(s1, kio, True)
        else:
            v1 = _sorted_top16(s1)
        v2, rank2 = _rank16(s2, kio, exact)
        cand = _candidate_rows(v1, v2, jrow)
        if exact:
            sel_a = []
            for r in range(PEER_TOPK):
                m = jnp.max(cand, axis=0, keepdims=True)
                px = jnp.min(jnp.where(cand == m, pos, 999.0), axis=0, keepdims=True)
                cand = jnp.where(pos == px, -jnp.inf, cand)
                sel_a.append(jnp.floor(px * (1.0 / PEER_TOPK)))
                if r == 0:
                    best0, z = m, jnp.ones_like(m)
                else:
                    z = z + jnp.exp(m - best0)
            counts = []
            for a in range(PEER_TOPK):
                n_a = jnp.zeros_like(z)
                for r in range(a, PEER_TOPK):
                    n_a = n_a + jnp.where(sel_a[r] == float(a), 1.0, 0.0)
                counts.append(n_a)
        else:
            z, counts, picked = _staircase_select(cand, jrow)
        n_e1 = jnp.zeros((PEER_NKEYS, lanes), F32)
        for a in range(PEER_TOPK):
            n_e1 = jnp.where((rank1 == float(a)) if exact else (s1 == v1[a]), counts[a], n_e1)
        n_ref[0, hd, :, at:at + lanes] = n_e1
        p1_ref[0, hd, :, at:at + lanes] = jnp.exp(s1 - v1[0]) / z
        r2_ref[0, hd, :, at:at + lanes] = rank2.astype(BF16)
        p2_ref[0, hd, :, at:at + lanes] = jnp.exp(s2 - v2[0]).astype(BF16)
        if exact:
            return None
        total = (jnp.sum(jnp.where(s1 >= v1[PEER_TOPK - 1], 1.0, 0.0), axis=0, keepdims=True)
                 + jnp.sum(jnp.where(rank2 < float(PEER_TOPK), 1.0, 0.0), axis=0, keepdims=True) + picked)
        for r in range(PEER_TOPK - 1):
            total = total + jnp.where(v1[r] <= v1[r + 1], 1.0, 0.0)
        return jnp.max(jnp.abs(total - 3.0 * PEER_TOPK))

    def scores(hd):
        return (_dot_nt(keys_ref[2 * hd], q[:, 256 * hd:256 * hd + 128].astype(BF16)),
                _dot_nt(keys_ref[2 * hd + 1], q[:, 256 * hd + 128:256 * hd + 256].astype(BF16)))

    miscounts = []
    for hd in range(PEER_HEADS):
        s1, s2 = scores(hd)
        miscounts.append([head(hd, s1, s2, False, at) for at in range(0, tl, lanes)])
    for hd in range(PEER_HEADS):
        for at, miscount in zip(range(0, tl, lanes), miscounts[hd]):
            @pl.when(miscount > 0.0)
            def _(hd=hd, at=at):
                s1, s2 = scores(hd)
                head(hd, s1, s2, True, at)


def _peer_topk(x, mod, gain, w_q, keys, per_batch_mod):
    b, l, d = x.shape
    tl = TOKEN_TILE
    mod_map = (lambda bi, i: (bi, 0, 0)) if per_batch_mod else (lambda bi, i: (0, 0, 0))
    sel = pl.BlockSpec((1, PEER_HEADS, PEER_NKEYS, tl), lambda bi, i: (bi, 0, 0, i))
    sel_f32 = jax.ShapeDtypeStruct((b, PEER_HEADS, PEER_NKEYS, l), F32)
    sel_bf16 = jax.ShapeDtypeStruct((b, PEER_HEADS, PEER_NKEYS, l), BF16)
    return pl.pallas_call(
        _peer_topk_kernel,
        grid=(b, l // tl),
        in_specs=[pl.BlockSpec((1, tl, d), lambda bi, i: (bi, i, 0)),
                  pl.BlockSpec((1, 6, d), mod_map),
                  pl.BlockSpec((1, d), lambda bi, i: (0, 0)),
                  pl.BlockSpec(w_q.shape, lambda bi, i: (0, 0)),
                  pl.BlockSpec(keys.shape, lambda bi, i: (0, 0, 0))],
        out_specs=[pl.BlockSpec((1, tl, d), lambda bi, i: (bi, i, 0)), sel, sel, sel, sel],
        out_shape=[jax.ShapeDtypeStruct((b, l, d), BF16), sel_f32, sel_f32, sel_bf16, sel_bf16],
        compiler_params=_params("arbitrary", "arbitrary"),
        name="peer_topk",
    )(x, mod, gain, w_q, keys)


def _peer_dense_kernel(x_ref, mod_ref, h_ref, u_ref, vt_ref, n_ref, p1_ref, r2_ref, p2_ref, o_ref,
                       ht_ref, acc_ref, s_ref, gate_ref):
    e = pl.program_id(2)

    @pl.when(e == 0)
    def _():
        ht_ref[...] = h_ref[0].astype(F32).T.astype(BF16)
        acc_ref[...] = jnp.zeros_like(acc_ref)

    ht = ht_ref[...]
    tm = ht.shape[1]
    bounds = [sum(PEER_SUBS[:i]) for i in range(len(PEER_SUBS) + 1)]

    def gate_rows(lo, hi):
        for row in range(lo // PEER_NKEYS, hi // PEER_NKEYS):
            gate = [None] * (PEER_NKEYS // 16)
            for hd in range(PEER_HEADS):
                n_b = jnp.broadcast_to(n_ref[0, hd, row:row + 1, :], (16, tm)).astype(BF16)
                p1_b = jnp.broadcast_to(p1_ref[0, hd, row:row + 1, :], (16, tm)).astype(BF16)
                for k in range(PEER_NKEYS // 16):
                    term = jnp.where(r2_ref[0, hd, 16 * k:16 * (k + 1), :] < n_b,
                                     p2_ref[0, hd, 16 * k:16 * (k + 1), :], jnp.zeros((), BF16)) * p1_b
                    gate[k] = term if gate[k] is None else gate[k] + term
            for k in range(PEER_NKEYS // 16):
                at = PEER_NKEYS * row + 16 * k
                gate_ref[at:at + 16, :] = gate[k]

    for lo, hi in zip(bounds[:-1], bounds[1:]):
        s_ref[lo:hi, :] = _dot(u_ref[lo:hi, :], ht).astype(BF16)
        gate_rows(lo, hi)
    ws = [gate_ref[lo:hi, :] * _gelu(s_ref[lo:hi, :])
          for lo, hi in zip(bounds[:-1], bounds[1:])]
    acc_ref[...] += _dot(vt_ref[...], jnp.concatenate(ws, axis=0))

    @pl.when(e == pl.num_programs(2) - 1)
    def _():
        o_ref[0] = x_ref[0] + mod_ref[0][5:6] * acc_ref[...].T


def _peer_dense(x, mod, h, u_tabs, vt_tabs, layer, n_e1, p1n, rank2, p2, per_batch_mod):
    b, l, d = x.shape
    tm = PEER_TOKENS
    n_exp = u_tabs.shape[1]
    rows = PEER_STEP // PEER_NKEYS
    mod_map = (lambda bi, i, e: (bi, 0, 0)) if per_batch_mod else (lambda bi, i, e: (0, 0, 0))
    tok = lambda: pl.BlockSpec((1, tm, d), lambda bi, i, e: (bi, i, 0))
    by_row = pl.BlockSpec((1, PEER_HEADS, rows, tm), lambda bi, i, e: (bi, 0, e, i))
    by_key = pl.BlockSpec((1, PEER_HEADS, PEER_NKEYS, tm), lambda bi, i, e: (bi, 0, 0, i))
    return pl.pallas_call(
        _peer_dense_kernel,
        grid=(b, l // tm, n_exp // PEER_STEP),
        in_specs=[tok(), pl.BlockSpec((1, 6, d), mod_map), tok(),
                  pl.BlockSpec((None, PEER_STEP, d), lambda bi, i, e: (layer, e, 0)),
                  pl.BlockSpec((None, d, PEER_STEP), lambda bi, i, e: (layer, 0, e)),
                  by_row, by_row, by_key, by_key],
        out_specs=tok(),
        out_shape=jax.ShapeDtypeStruct((b, l, d), F32),
        scratch_shapes=[pltpu.VMEM((d, tm), BF16), pltpu.VMEM((d, tm), F32),
                        pltpu.VMEM((PEER_STEP, tm), BF16), pltpu.VMEM((PEER_STEP, tm), BF16)],
        compiler_params=_params("arbitrary", "arbitrary", "arbitrary"),
        name="peer_dense",
    )(x, mod, h, u_tabs, vt_tabs, n_e1, p1n, rank2, p2)


def _rope_tables(n_lat, width):
    q4 = DIFF_DK // 4
    n_rows = n_lat // GRID_W
    row = jnp.repeat(jnp.arange(n_rows), GRID_W).astype(F32)
    col = jnp.tile(jnp.arange(GRID_W), n_rows).astype(F32)
    freqs = ROPE_BASE ** (-jnp.arange(q4, dtype=F32) / q4)
    ang = jnp.stack([row[:, None] * freqs, col[:, None] * freqs], axis=1)
    cos, sin = jnp.cos(ang), jnp.sin(ang)
    cos64 = jnp.stack([cos, cos], axis=2).reshape(n_lat, 64)
    sin64 = jnp.stack([-sin, sin], axis=2).reshape(n_lat, 64)
    reps = width // 64
    return jnp.tile(cos64, (1, reps)), jnp.tile(sin64, (1, reps))


def _ones_block_diag(width, group):
    idx = jnp.arange(width) // group
    return jnp.where(idx[:, None] == idx[None, :], 1.0 / group, 0.0).astype(BF16)


def _peer_layer(x, mod, W, l, per_batch_mod):
    shape = x.shape
    if not per_batch_mod:
        x = x.reshape(1, -1, shape[-1])
    h, n_e1, p1n, rank2, p2 = _peer_topk(x, mod, W['norm_ffn'][l], W['peer_w_q'][l], W['peer_keys'][l],
                                         per_batch_mod)
    y = _peer_dense(x, mod, h, W['peer_u'], W['peer_vt'], l, n_e1, p1n, rank2, p2, per_batch_mod)
    return y.reshape(shape)


def _even_layer(x, mod, W, l, cache, per_batch_mod):
    b, seq, _ = x.shape
    i = l // 2
    lam_init = 0.8 - 0.6 * math.exp(-0.3 * l)
    rope_tabs = W['rope'] if cache is not None else None
    u_tm2, q, k_norm, k_att, v, v_att = _inproj_even(
        x, mod, W['norm_mix'][l], W['even_w_in'][i], W['ones_bd'], W['diff_q_norm'][i],
        W['diff_k_norm'][i], rope_tabs, per_batch_mod)
    if cache is None:
        h0 = jnp.zeros((2, b, S5_JB * S5_BLK), F32)
        k_all, v_all = k_att, v_att
    else:
        h0 = _s5_state_to_blocks(cache['s5'][:, i])
        k_all = jnp.concatenate([cache['diff_k'][:, i].reshape(b, -1, 512).astype(BF16), k_att], axis=1)
        v_all = jnp.concatenate([cache['diff_v'][:, i].reshape(b, -1, 512).astype(BF16), v_att], axis=1)
    yf, yb, fin = _s5_scan(u_tm2.reshape(seq, b, 512), W['s5_bd'][i], W['s5_cd'][i], W['s5_a'][i], h0)
    o_attn = _diff_attention(q, k_all, v_all, W['diff_lambda'][i], lam_init)
    x = _mixout_even(x, mod, u_tm2, yf.reshape(seq, b * 512), yb.reshape(seq, b * 512), o_attn,
                     W['s5_d'][i], W['s5_w_glu'][i], W['diff_sub_norm'][i], W['even_w_out'][i],
                     1.0 - lam_init, per_batch_mod)
    return x, (k_norm, v, fin)


def _odd_layer(x, mod, W, l, cache, per_batch_mod):
    b, seq, _ = x.shape
    i = l // 2
    rope_tabs = W['rope'] if cache is not None else None
    q, k_norm, k_att, v, v_att, u = _inproj_odd(
        x, mod, W['norm_mix'][l], W['odd_w_in'][i], W['ones_bd'], W['win_q_norm'][i],
        W['win_k_norm'][i], rope_tabs, per_batch_mod)
    if cache is None:
        o_c = _win_attention(q, k_att, v_att, W['win_sink'][i])
    else:
        pad = ((0, 0), (WINDOW, WINDOW), (0, 0))
        o_c = _win_attention(q, cache['win_k'][:, i].reshape(b, -1, 128).astype(BF16),
                             cache['win_v'][:, i].reshape(b, -1, 128).astype(BF16), W['win_sink'][i],
                             jnp.pad(k_att, pad), jnp.pad(v_att, pad))
    o_d = _pool_mixer(u, W['pool_w'][i], W['pool_scale'][i])
    x = _mixout_odd(x, mod, o_c, o_d, W['odd_w_out'][i], per_batch_mod)
    return x, (k_norm, v)


def kernel(x_prompt, x_sample, cache_diff_k, cache_diff_v, state_s5, cache_win_k, cache_win_v, c, c_ctx, ada_w, ada_b, norm_mix, norm_ffn, even_w_in, even_w_out, s5_lam_re, s5_lam_im, s5_log_step, s5_b_re, s5_b_im, s5_c_re, s5_c_im, s5_d, s5_w_glu, diff_q_norm, diff_k_norm, diff_lambda, diff_sub_norm, odd_w_in, odd_w_out, win_q_norm, win_k_norm, win_sink, pool_w, pool_scale, peer_w_q, peer_sub_keys, peer_u, peer_v):
    depth = ada_w.shape[0]
    n_even, n_odd = even_w_in.shape[0], odd_w_in.shape[0]
    bsz, seq, d = x_prompt.shape
    dec_b, dec_l, _ = x_sample.shape

    cvec = jnp.zeros((16, d), F32).at[0].set(c_ctx).at[1:1 + dec_b].set(c)
    mod = _modulation(cvec, ada_w, ada_b)

    s5 = [_s5_tables(s5_lam_re[i], s5_lam_im[i], s5_log_step[i], s5_b_re[i], s5_b_im[i],
                     s5_c_re[i], s5_c_im[i]) for i in range(n_even)]
    W = dict(
        norm_mix=norm_mix.reshape(depth, 1, d), norm_ffn=norm_ffn.reshape(depth, 1, d),
        even_w_in=even_w_in.astype(BF16), even_w_out=even_w_out.astype(BF16),
        odd_w_in=odd_w_in.astype(BF16), odd_w_out=odd_w_out.astype(BF16),
        ones_bd=_ones_block_diag(512, 64),
        diff_q_norm=jnp.tile(diff_q_norm, (1, 8)).reshape(n_even, 1, 512),
        diff_k_norm=jnp.tile(diff_k_norm, (1, 8)).reshape(n_even, 1, 512),
        diff_lambda=diff_lambda, diff_sub_norm=diff_sub_norm.reshape(n_even, 1, DIFF_DV),
        s5_bd=[t[0] for t in s5], s5_cd=[t[1] for t in s5], s5_a=[t[2] for t in s5],
        s5_d=s5_d.reshape(n_even, 1, 512), s5_w_glu=s5_w_glu.astype(BF16),
        win_q_norm=jnp.tile(win_q_norm, (1, 8)).reshape(n_odd, 1, 512),
        win_k_norm=jnp.tile(win_k_norm, (1, 2)).reshape(n_odd, 1, 128),
        win_sink=jnp.broadcast_to(win_sink[:, :, None], (n_odd, 8, 128)),
        pool_w=pool_w.astype(BF16), pool_scale=pool_scale.reshape(n_odd, 1, 512),
        peer_w_q=peer_w_q.astype(BF16),
        peer_keys=peer_sub_keys.astype(BF16).reshape(depth, 2 * PEER_HEADS, PEER_NKEYS, -1),
        peer_u=peer_u.astype(BF16),
        peer_vt=jnp.swapaxes(peer_v, 1, 2).astype(BF16),
        rope=_rope_tables(dec_l, 512),
    )
    cache = {'diff_k': cache_diff_k, 'diff_v': cache_diff_v, 's5': state_s5,
             'win_k': cache_win_k, 'win_v': cache_win_v}

    def run(x, mods, cch, per_batch_mod):
        states = []
        for l in range(depth):
            if l % 2 == 0:
                x, st = _even_layer(x, mods[l], W, l, cch, per_batch_mod)
            else:
                x, st = _odd_layer(x, mods[l], W, l, cch, per_batch_mod)
            states.append(st)
            x = _peer_layer(x, mods[l], W, l, per_batch_mod)
        return x, states

    ctx_mods = [mod[l, 0:1].reshape(1, 6, d) for l in range(depth)]
    dec_mods = [mod[l, 1:1 + dec_b].reshape(dec_b, 6, d) for l in range(depth)]
    y_prompt, states = run(x_prompt, ctx_mods, None, False)
    y_sample, _ = run(x_sample, dec_mods, cache, True)

    evens = [states[l] for l in range(depth) if l % 2 == 0]
    odds = [states[l] for l in range(depth) if l % 2 == 1]
    new_diff_k = jnp.stack([s[0].reshape(bsz, seq, 2, DIFF_HEADS, DIFF_DK) for s in evens], axis=1)
    new_diff_v = jnp.stack([s[1].reshape(bsz, seq, DIFF_HEADS, DIFF_DV) for s in evens], axis=1)
    new_s5 = jnp.stack([_s5_blocks_to_state(s[2]) for s in evens], axis=1)
    new_win_k = jnp.stack([s[0].reshape(bsz, seq, WIN_KV_HEADS, WIN_DH) for s in odds], axis=1)
    new_win_v = jnp.stack([s[1].reshape(bsz, seq, WIN_KV_HEADS, WIN_DH) for s in odds], axis=1)
    return (y_prompt, y_sample, new_diff_k, new_diff_v, new_s5, new_win_k, new_win_v)
```

```python
import functools
import math

import jax
import jax.numpy as jnp
from jax import lax
from jax.experimental import pallas as pl
from jax.experimental.pallas import tpu as pltpu

F32 = jnp.float32
BF16 = jnp.bfloat16

D_MODEL = 1024
MIX_HALF = 512
GRID_W = 64
EPS = 1e-6
NEG_INF = -1e30
ROPE_BASE = 10000.0

S5_GROUPS = 32
S5_GROUP = 16
S5_STATE = 64
S5_JB = 4
S5_BLK = 1024
S5_CHUNK = 32
S5_BATCH = 8

DIFF_HEADS = 4
DIFF_DK = 64
DIFF_DV = 128
WIN_KV_HEADS = 2
WIN_GROUP = 4
WIN_DH = 64
ATTN_SCALE = DIFF_DK ** -0.5
WINDOW = 128
POOL_WINDOWS = (2, 4, 8, 16)
POOL_PAD = 8

PEER_HEADS = 8
PEER_NKEYS = 128
PEER_TOPK = 16
RANK_CODE = 2.0 ** 100
PEER_SUBS = (128, 384, 512, 1024)
PEER_STEP = sum(PEER_SUBS)
PEER_TOKENS = 512

TOKEN_TILE = 256
WIDE_TILE = 512
VMEM_LIMIT = 56 * 1024 * 1024


def _params(*sem):
    return pltpu.CompilerParams(dimension_semantics=sem, vmem_limit_bytes=VMEM_LIMIT)


def _dot(a, b):
    return jnp.dot(a, b, preferred_element_type=F32)


def _dot_nt(a, b):
    return lax.dot_general(a, b, (((1,), (1,)), ((), ())), preferred_element_type=F32)


def _sigmoid(x):
    return 1.0 / (1.0 + jnp.exp(-x))


def _gelu(x):
    c = math.sqrt(2.0 / math.pi)
    hx = 0.5 * x
    return hx + hx * jnp.tanh(x * (c + (c * 0.044715) * (x * x)))


def _modulate(x, gain, shift, scale):
    ms = jnp.mean(x * x, axis=-1, keepdims=True)
    return (x * lax.rsqrt(ms + EPS) * gain) * (1.0 + scale) + shift


def _group_rms(z, ones_bd, gain):
    ms = _dot((z * z).astype(BF16), ones_bd)
    return z * lax.rsqrt(ms + EPS) * gain


def _rope(z, cos_t, sin_t):
    lane = lax.broadcasted_iota(jnp.int32, (1, 128), 1) % 32
    first = lane < 16
    parts = []
    for j in range(z.shape[1] // 128):
        c = z[:, 128 * j:128 * (j + 1)]
        parts.append(jnp.where(first, pltpu.roll(c, 112, axis=1), pltpu.roll(c, 16, axis=1)))
    swapped = parts[0] if len(parts) == 1 else jnp.concatenate(parts, axis=1)
    return z * cos_t + swapped * sin_t


def _mod_kernel(c_ref, w_ref, b_ref, o_ref):
    c = c_ref[...]
    s = c * _sigmoid(c)
    o_ref[0] = jnp.dot(s, w_ref[0], precision=lax.Precision.HIGHEST,
                       preferred_element_type=F32) + b_ref[0]


def _modulation(cvec, ada_w, ada_b):
    depth, d, n = ada_w.shape
    tn = 1536
    return pl.pallas_call(
        _mod_kernel,
        grid=(depth, n // tn),
        in_specs=[pl.BlockSpec((16, d), lambda l, j: (0, 0)),
                  pl.BlockSpec((1, d, tn), lambda l, j: (l, 0, j)),
                  pl.BlockSpec((1, 1, tn), lambda l, j: (l, 0, j))],
        out_specs=pl.BlockSpec((1, 16, tn), lambda l, j: (l, 0, j)),
        out_shape=jax.ShapeDtypeStruct((depth, 16, n), F32),
        compiler_params=_params("arbitrary", "arbitrary"),
        name="adaln_mod",
    )(cvec, ada_w, ada_b.reshape(depth, 1, n))


def _inproj_even_kernel(rope, *refs):
    if rope:
        (x_ref, mod_ref, g_ref, w_ref, bd_ref, gq_ref, gk_ref, cos_ref, sin_ref,
         u_ref, q_ref, kn_ref, ka_ref, v_ref, va_ref) = refs
    else:
        (x_ref, mod_ref, g_ref, w_ref, bd_ref, gq_ref, gk_ref,
         u_ref, q_ref, kn_ref, ka_ref, v_ref, va_ref) = refs
    mod = mod_ref[0]
    h = _modulate(x_ref[0], g_ref[...], mod[0:1], mod[1:2])
    z = _dot(h.astype(BF16), w_ref[...])
    u = z[:, 0:512]
    q = _group_rms(z[:, 512:1024], bd_ref[...], gq_ref[...])
    k = _group_rms(z[:, 1024:1536], bd_ref[...], gk_ref[...])
    v = z[:, 1536:2048]
    u_ref[...] = u
    kn_ref[0] = k
    v_ref[0] = v
    va_ref[0] = v.astype(BF16)
    if rope:
        q = _rope(q, cos_ref[...], sin_ref[...])
        k = _rope(k, cos_ref[...], sin_ref[...])
    q_ref[0] = (q * ATTN_SCALE).astype(BF16)
    ka_ref[0] = k.astype(BF16)


def _inproj_even(x, mod, gain, w, ones_bd, gq, gk, rope_tabs, per_batch_mod):
    b, l, d = x.shape
    tl = min(l, WIDE_TILE)
    rope = rope_tabs is not None
    mod_map = (lambda bi, i: (bi, 0, 0)) if per_batch_mod else (lambda bi, i: (0, 0, 0))
    const2 = lambda bi, i: (0, 0)
    tok = lambda w_: pl.BlockSpec((1, tl, w_), lambda bi, i: (bi, i, 0))
    in_specs = [tok(d),
                pl.BlockSpec((1, 6, d), mod_map),
                pl.BlockSpec((1, d), const2),
                pl.BlockSpec(w.shape, const2),
                pl.BlockSpec(ones_bd.shape, const2),
                pl.BlockSpec((1, 512), const2),
                pl.BlockSpec((1, 512), const2)]
    args = [x, mod, gain, w, ones_bd, gq, gk]
    if rope:
        in_specs += [pl.BlockSpec((tl, 512), lambda bi, i: (i, 0))] * 2
        args += list(rope_tabs)
    out_shape = [jax.ShapeDtypeStruct((l, b * 512), F32),
                 jax.ShapeDtypeStruct((b, l, 512), BF16),
                 jax.ShapeDtypeStruct((b, l, 512), F32),
                 jax.ShapeDtypeStruct((b, l, 512), BF16),
                 jax.ShapeDtypeStruct((b, l, 512), F32),
                 jax.ShapeDtypeStruct((b, l, 512), BF16)]
    out_specs = [pl.BlockSpec((tl, 512), lambda bi, i: (i, bi)),
                 tok(512), tok(512), tok(512), tok(512), tok(512)]
    return pl.pallas_call(
        functools.partial(_inproj_even_kernel, rope),
        grid=(b, l // tl), in_specs=in_specs, out_specs=out_specs, out_shape=out_shape,
        compiler_params=_params("arbitrary", "arbitrary"),
        name="inproj_even_rope" if rope else "inproj_even",
    )(*args)


def _inproj_odd_kernel(rope, *refs):
    if rope:
        (x_ref, mod_ref, g_ref, w_ref, bd_ref, gq_ref, gk_ref, cos_ref, sin_ref,
         q_ref, kn_ref, ka_ref, v_ref, va_ref, u_ref) = refs
    else:
        (x_ref, mod_ref, g_ref, w_ref, bd_ref, gq_ref, gk_ref,
         q_ref, kn_ref, ka_ref, v_ref, va_ref, u_ref) = refs
    mod = mod_ref[0]
    h = _modulate(x_ref[0], g_ref[...], mod[0:1], mod[1:2])
    z = _dot(h.astype(BF16), w_ref[...])
    bd = bd_ref[...]
    q = _group_rms(z[:, 0:512], bd, gq_ref[...])
    k = _group_rms(z[:, 512:640], bd[0:128, 0:128], gk_ref[...])
    v = z[:, 640:768]
    kn_ref[0] = k
    v_ref[0] = v
    va_ref[0] = v.astype(BF16)
    u_ref[0] = z[:, 768:1280]
    if rope:
        q = _rope(q, cos_ref[...], sin_ref[...])
        k = _rope(k, cos_ref[:, 0:128], sin_ref[:, 0:128])
    q_ref[0] = (q * ATTN_SCALE).astype(BF16)
    ka_ref[0] = k.astype(BF16)


def _inproj_odd(x, mod, gain, w, ones_bd, gq, gk, rope_tabs, per_batch_mod):
    b, l, d = x.shape
    tl = min(l, WIDE_TILE)
    rope = rope_tabs is not None
    mod_map = (lambda bi, i: (bi, 0, 0)) if per_batch_mod else (lambda bi, i: (0, 0, 0))
    const2 = lambda bi, i: (0, 0)
    tok = lambda w_: pl.BlockSpec((1, tl, w_), lambda bi, i: (bi, i, 0))
    in_specs = [tok(d),
                pl.BlockSpec((1, 6, d), mod_map),
                pl.BlockSpec((1, d), const2),
                pl.BlockSpec(w.shape, const2),
                pl.BlockSpec(ones_bd.shape, const2),
                pl.BlockSpec((1, 512), const2),
                pl.BlockSpec((1, 128), const2)]
    args = [x, mod, gain, w, ones_bd, gq, gk]
    if rope:
        in_specs += [pl.BlockSpec((tl, 512), lambda bi, i: (i, 0))] * 2
        args += list(rope_tabs)
    out_shape = [jax.ShapeDtypeStruct((b, l, 512), BF16),
                 jax.ShapeDtypeStruct((b, l, 128), F32),
                 jax.ShapeDtypeStruct((b, l, 128), BF16),
                 jax.ShapeDtypeStruct((b, l, 128), F32),
                 jax.ShapeDtypeStruct((b, l, 128), BF16),
                 jax.ShapeDtypeStruct((b, l, 512), F32)]
    out_specs = [tok(512), tok(128), tok(128), tok(128), tok(128), tok(512)]
    return pl.pallas_call(
        functools.partial(_inproj_odd_kernel, rope),
        grid=(b, l // tl), in_specs=in_specs, out_specs=out_specs, out_shape=out_shape,
        compiler_params=_params("arbitrary", "arbitrary"),
        name="inproj_odd_rope" if rope else "inproj_odd",
    )(*args)


def _s5_kernel(uf_ref, ub_ref, bd_ref, cd_ref, a_ref, h0_ref, yf_ref, yb_ref, fin_ref,
               bu_ref, carry_ref):
    c = pl.program_id(1)
    rows = S5_CHUNK * S5_BATCH

    @pl.when(c == 0)
    def _():
        carry_ref[...] = h0_ref[...]

    for d in range(2):
        u = (uf_ref if d == 0 else ub_ref)[...].reshape(rows, MIX_HALF).astype(BF16)
        for j in range(S5_JB):
            bu_ref[d, :, S5_BLK * j:S5_BLK * (j + 1)] = _dot(u[:, 128 * j:128 * (j + 1)], bd_ref[d, j])
        for j in range(S5_JB):
            lo = S5_BLK * j
            a_re = jnp.broadcast_to(a_ref[d, 0, j:j + 1, :], (S5_BATCH, 512))
            a_im = jnp.broadcast_to(a_ref[d, 1, j:j + 1, :], (S5_BATCH, 512))

            h_re, h_im = carry_ref[d, :, lo:lo + 512], carry_ref[d, :, lo + 512:lo + 1024]
            for s in range(S5_CHUNK):
                row = (s if d == 0 else S5_CHUNK - 1 - s) * S5_BATCH
                b_re = bu_ref[d, row:row + S5_BATCH, lo:lo + 512]
                b_im = bu_ref[d, row:row + S5_BATCH, lo + 512:lo + 1024]
                h_re, h_im = a_re * h_re - a_im * h_im + b_re, a_re * h_im + a_im * h_re + b_im
                bu_ref[d, row:row + S5_BATCH, lo:lo + 512] = h_re
                bu_ref[d, row:row + S5_BATCH, lo + 512:lo + 1024] = h_im
            carry_ref[d, :, lo:lo + 512] = h_re
            carry_ref[d, :, lo + 512:lo + 1024] = h_im
        ys = [_dot(bu_ref[d, :, S5_BLK * j:S5_BLK * (j + 1)].astype(BF16), cd_ref[d, j])
              for j in range(S5_JB)]
        y = jnp.concatenate(ys, axis=1).reshape(S5_CHUNK, S5_BATCH, MIX_HALF)
        if d == 0:
            yf_ref[...] = y
        else:
            yb_ref[...] = y

    @pl.when(c == pl.num_programs(1) - 1)
    def _():
        fin_ref[...] = carry_ref[...]


def _s5_scan(u_tm, bd, cd, acoef, h0):
    l, b, _ = u_tm.shape
    nc = l // S5_CHUNK
    ng = b // S5_BATCH
    state_w = S5_JB * S5_BLK
    blk = (S5_CHUNK, S5_BATCH, MIX_HALF)
    full = lambda a: pl.BlockSpec(a.shape, lambda g, c: (0,) * a.ndim)
    return pl.pallas_call(
        _s5_kernel,
        grid=(ng, nc),
        in_specs=[pl.BlockSpec(blk, lambda g, c: (c, g, 0)),
                  pl.BlockSpec(blk, lambda g, c: (nc - 1 - c, g, 0)),
                  full(bd), full(cd), full(acoef),
                  pl.BlockSpec((2, S5_BATCH, state_w), lambda g, c: (0, g, 0))],
        out_specs=[pl.BlockSpec(blk, lambda g, c: (c, g, 0)),
                   pl.BlockSpec(blk, lambda g, c: (nc - 1 - c, g, 0)),
                   pl.BlockSpec((2, S5_BATCH, state_w), lambda g, c: (0, g, 0))],
        out_shape=[jax.ShapeDtypeStruct((l, b, MIX_HALF), F32),
                   jax.ShapeDtypeStruct((l, b, MIX_HALF), F32),
                   jax.ShapeDtypeStruct((2, b, state_w), F32)],
        scratch_shapes=[pltpu.VMEM((2, S5_CHUNK * S5_BATCH, state_w), F32),
                        pltpu.VMEM((2, S5_BATCH, state_w), F32)],
        compiler_params=_params("arbitrary", "arbitrary"),
        name="s5_scan",
    )(u_tm, u_tm, bd, cd, acoef, h0)


def _s5_tables(lam_re, lam_im, log_step, b_re, b_im, c_re, c_im):
    step = jnp.exp(log_step.astype(F32))[..., None]
    lr, li = lam_re.astype(F32), lam_im.astype(F32)
    er = jnp.exp(lr * step)
    a_re, a_im = er * jnp.cos(li * step), er * jnp.sin(li * step)
    den = lr * lr + li * li
    q_re = ((a_re - 1.0) * lr + a_im * li) / den
    q_im = (a_im * lr - (a_re - 1.0) * li) / den
    bb_re = q_re[..., None] * b_re - q_im[..., None] * b_im
    bb_im = q_re[..., None] * b_im + q_im[..., None] * b_re
    eye = jnp.eye(8, dtype=F32)
    bb = jnp.stack([bb_re, bb_im], axis=1).reshape(2, 2, S5_JB, 8, S5_STATE, S5_GROUP)
    bd = jnp.einsum('drjgpc,gh->djgcrhp', bb, eye).reshape(2, S5_JB, 128, S5_BLK).astype(BF16)
    cc = jnp.stack([c_re, -c_im], axis=1).astype(F32).reshape(2, 2, S5_JB, 8, S5_GROUP, S5_STATE)
    cd = jnp.einsum('drjgcp,gh->djrgphc', cc, eye).reshape(2, S5_JB, S5_BLK, 128).astype(BF16)
    acoef = jnp.stack([a_re, a_im], axis=1).reshape(2, 2, S5_JB, 512)
    return bd, cd, acoef


def _s5_state_to_blocks(st):
    b = st.shape[0]
    st = st.astype(F32).reshape(b, 2, 2, S5_JB, 8, S5_STATE)
    return jnp.transpose(st, (1, 0, 3, 2, 4, 5)).reshape(2, b, S5_JB * S5_BLK)


def _s5_blocks_to_state(fin):
    b = fin.shape[1]
    fin = fin.reshape(2, b, S5_JB, 2, 8, S5_STATE)
    return jnp.transpose(fin, (1, 0, 3, 2, 4, 5)).reshape(b, 2, 2, S5_GROUPS, S5_STATE)


def _diff_attn_kernel(lam_init, q_ref, k_ref, v_ref, lv_ref, o_ref):
    lv = lv_ref[...]
    lam = (jnp.exp(jnp.sum(lv[0:1] * lv[1:2], axis=-1, keepdims=True))
           - jnp.exp(jnp.sum(lv[2:3] * lv[3:4], axis=-1, keepdims=True)) + lam_init)
    lane = lax.broadcasted_iota(jnp.int32, (1, 128), 1)
    def scores(h):
        out = []
        for m in range(2):
            j = m * 2 + h // 2
            keep = (lane < 64) if h % 2 == 0 else (lane >= 64)
            qb = jnp.where(keep, q_ref[0, :, 128 * j:128 * (j + 1)], jnp.zeros((), BF16))
            out.append(_dot_nt(qb, k_ref[0, :, 128 * j:128 * (j + 1)]))
        return out

    outs = []
    nxt = scores(0)
    for h in range(DIFF_HEADS):
        cur = nxt
        if h + 1 < DIFF_HEADS:
            nxt = scores(h + 1)
        es = [jnp.exp(s - jnp.max(s, axis=-1, keepdims=True)) for s in cur]
        sums = [jnp.sum(e, axis=-1, keepdims=True) for e in es]
        a = es[0] - (lam * sums[0] / sums[1]) * es[1]
        outs.append(_dot(a.astype(BF16), v_ref[0, :, 128 * h:128 * (h + 1)]) / sums[0])
    o_ref[0] = jnp.concatenate(outs, axis=1)


def _diff_attention(q, k, v, lv, lam_init):
    b, lq, _ = q.shape
    lk = k.shape[1]
    tq = TOKEN_TILE
    return pl.pallas_call(
        functools.partial(_diff_attn_kernel, lam_init),
        grid=(b, lq // tq),
        in_specs=[pl.BlockSpec((1, tq, 512), lambda bi, i: (bi, i, 0)),
                  pl.BlockSpec((1, lk, 512), lambda bi, i: (bi, 0, 0)),
                  pl.BlockSpec((1, lk, 512), lambda bi, i: (bi, 0, 0)),
                  pl.BlockSpec((4, DIFF_DK), lambda bi, i: (0, 0))],
        out_specs=pl.BlockSpec((1, tq, 512), lambda bi, i: (bi, i, 0)),
        out_shape=jax.ShapeDtypeStruct((b, lq, 512), F32),
        compiler_params=_params("arbitrary", "arbitrary"),
        name="diff_attention",
    )(q, k, v, lv)


def _win_attn_kernel(local, seq_len, *refs):
    if local:
        q_ref, kc_ref, vc_ref, kl_ref, vl_ref, sink_ref, o_ref = refs
    else:
        q_ref, kc_ref, vc_ref, sink_ref, o_ref = refs
    tq = q_ref.shape[1]
    lane = lax.broadcasted_iota(jnp.int32, (1, 128), 1)
    low = lane < 64
    kc = kc_ref[0]
    vc = vc_ref[0]
    n_ctx = kc.shape[0]
    if local:
        span = tq + 2 * WINDOW
        start = pl.multiple_of(pl.program_id(1) * tq, tq)
        kl = kl_ref[0, pl.ds(start, span), :]
        vl = vl_ref[0, pl.ds(start, span), :]
        qpos = start + lax.broadcasted_iota(jnp.int32, (tq, span), 0)
        kpos = start - WINDOW + lax.broadcasted_iota(jnp.int32, (tq, span), 1)
        ok = (jnp.abs(qpos - kpos) <= WINDOW) & (kpos >= 0) & (kpos < seq_len)
    def scores(hd):
        n = hd // WIN_GROUP
        j, half = hd // 2, hd % 2
        qb = q_ref[0, :, 128 * j:128 * (j + 1)]
        if half != n:
            qb = jnp.concatenate([qb[:, 64:128], qb[:, 0:64]], axis=1)
        qb = jnp.where(low if n == 0 else jnp.logical_not(low), qb, jnp.zeros((), BF16))
        s = _dot_nt(qb, kc)
        if local:
            s = jnp.concatenate([s, jnp.where(ok, _dot_nt(qb, kl), NEG_INF)], axis=1)
        return s

    heads = {}
    n_heads = WIN_KV_HEADS * WIN_GROUP
    nxt = scores(0)
    for hd in range(n_heads):
        s = nxt
        if hd + 1 < n_heads:
            nxt = scores(hd + 1)
        sink = sink_ref[hd:hd + 1, 0:1]
        mx = jnp.maximum(jnp.max(s, axis=-1, keepdims=True), sink)
        e = jnp.exp(s - mx)
        den = jnp.sum(e, axis=-1, keepdims=True) + jnp.exp(sink - mx)
        p = e.astype(BF16)
        o = _dot(p[:, 0:n_ctx], vc)
        if local:
            o = o + _dot(p[:, n_ctx:], vl)
        heads[hd] = o / den
    blocks = []
    for j in range(4):
        ev, od = heads[2 * j], heads[2 * j + 1]
        n = (2 * j) // WIN_GROUP
        if n == 1:
            ev = pltpu.roll(ev, 64, axis=1)
        else:
            od = pltpu.roll(od, 64, axis=1)
        blocks.append(jnp.where(low, ev, od))
    o_ref[0] = jnp.concatenate(blocks, axis=1).astype(BF16)


def _win_attention(q, kc, vc, sink, k_lat=None, v_lat=None):
    b, lq, _ = q.shape
    n_ctx = kc.shape[1]
    local = k_lat is not None
    tq = TOKEN_TILE
    in_specs = [pl.BlockSpec((1, tq, 512), lambda bi, i: (bi, i, 0)),
                pl.BlockSpec((1, n_ctx, 128), lambda bi, i: (bi, 0, 0)),
                pl.BlockSpec((1, n_ctx, 128), lambda bi, i: (bi, 0, 0))]
    args = [q, kc, vc]
    if local:
        lp = k_lat.shape[1]
        in_specs += [pl.BlockSpec((1, lp, 128), lambda bi, i: (bi, 0, 0))] * 2
        args += [k_lat, v_lat]
    in_specs.append(pl.BlockSpec((8, 128), lambda bi, i: (0, 0)))
    args.append(sink)
    return pl.pallas_call(
        functools.partial(_win_attn_kernel, local, lq),
        grid=(b, lq // tq), in_specs=in_specs,
        out_specs=pl.BlockSpec((1, tq, 512), lambda bi, i: (bi, i, 0)),
        out_shape=jax.ShapeDtypeStruct((b, lq, 512), BF16),
        compiler_params=_params("arbitrary", "arbitrary"),
        name="win_attention_local" if local else "win_attention",
    )(*args)


def _pool_kernel(u_ref, w_ref, scale_ref, o_ref):
    l = u_ref.shape[1]
    lp = l + 2 * POOL_PAD
    t = lax.broadcasted_iota(jnp.int32, (l, 128), 0)
    zpad = jnp.zeros((POOL_PAD, 128), F32)

    def shifted(a, s):
        return pltpu.roll(a, s, axis=0) + pltpu.roll(a, lp - s, axis=0)

    outs = []
    for gi, wl in enumerate(POOL_WINDOWS):
        x = u_ref[0, :, 128 * gi:128 * (gi + 1)]
        xp = jnp.concatenate([zpad, x, zpad], axis=0)
        acc = xp + pltpu.roll(xp, 1, axis=0)
        if wl >= 4:
            acc = shifted(acc, 1)
        if wl >= 8:
            acc = shifted(acc, 2)
        if wl >= 16:
            acc = shifted(acc, 4)
        win = acc[POOL_PAD:POOL_PAD + l]
        cnt = (jnp.minimum(t + wl // 2, l) - jnp.maximum(t - wl // 2, 0)).astype(F32)
        pooled = win / cnt - x
        outs.append(_dot(pooled.astype(BF16), w_ref[gi]))
    o_ref[0] = (jnp.concatenate(outs, axis=1) * scale_ref[...]).astype(BF16)


def _pool_mixer(u, w, scale):
    b, l, _ = u.shape
    return pl.pallas_call(
        _pool_kernel,
        grid=(b,),
        in_specs=[pl.BlockSpec((1, l, 512), lambda bi: (bi, 0, 0)),
                  pl.BlockSpec(w.shape, lambda bi: (0, 0, 0)),
                  pl.BlockSpec((1, 512), lambda bi: (0, 0))],
        out_specs=pl.BlockSpec((1, l, 512), lambda bi: (bi, 0, 0)),
        out_shape=jax.ShapeDtypeStruct((b, l, 512), BF16),
        compiler_params=_params("arbitrary"),
        name="pool_mixer",
    )(u, w, scale)


def _mixout_even_kernel(out_scale, x_ref, mod_ref, u_ref, yf_ref, yb_ref, oa_ref, sd_ref,
                        wglu_ref, gsub_ref, wout_ref, o_ref):
    mod = mod_ref[0]
    y = sd_ref[...] * u_ref[...] + yf_ref[...] + yb_ref[...]
    g = _gelu(y)
    o_a = g * _sigmoid(_dot(g.astype(BF16), wglu_ref[...]))
    parts = [o_a.astype(BF16)]
    for h in range(DIFF_HEADS):
        blk = oa_ref[0, :, 128 * h:128 * (h + 1)]
        ms = jnp.mean(blk * blk, axis=-1, keepdims=True)
        parts.append(((blk * lax.rsqrt(ms + EPS) * gsub_ref[...]) * out_scale).astype(BF16))
    cat = jnp.concatenate(parts, axis=1)
    o_ref[0] = x_ref[0] + mod[2:3] * _dot(cat, wout_ref[...])


def _mixout_even(x, mod, u_tm2, yf2, yb2, o_attn, s5_d, w_glu, g_sub, w_out, out_scale, per_batch_mod):
    b, l, d = x.shape
    tl = min(l, WIDE_TILE)
    mod_map = (lambda bi, i: (bi, 0, 0)) if per_batch_mod else (lambda bi, i: (0, 0, 0))
    const2 = lambda bi, i: (0, 0)
    tm = pl.BlockSpec((tl, 512), lambda bi, i: (i, bi))
    return pl.pallas_call(
        functools.partial(_mixout_even_kernel, out_scale),
        grid=(b, l // tl),
        in_specs=[pl.BlockSpec((1, tl, d), lambda bi, i: (bi, i, 0)),
                  pl.BlockSpec((1, 6, d), mod_map),
                  tm, tm, tm,
                  pl.BlockSpec((1, tl, 512), lambda bi, i: (bi, i, 0)),
                  pl.BlockSpec((1, 512), const2),
                  pl.BlockSpec((512, 512), const2),
                  pl.BlockSpec((1, 128), const2),
                  pl.BlockSpec((d, d), const2)],
        out_specs=pl.BlockSpec((1, tl, d), lambda bi, i: (bi, i, 0)),
        out_shape=jax.ShapeDtypeStruct((b, l, d), F32),
        compiler_params=_params("arbitrary", "arbitrary"),
        name="mixout_even",
    )(x, mod, u_tm2, yf2, yb2, o_attn, s5_d, w_glu, g_sub, w_out)


def _mixout_odd_kernel(x_ref, mod_ref, oc_ref, od_ref, wout_ref, o_ref):
    mod = mod_ref[0]
    cat = jnp.concatenate([oc_ref[0], od_ref[0]], axis=1)
    o_ref[0] = x_ref[0] + mod[2:3] * _dot(cat, wout_ref[...])


def _mixout_odd(x, mod, o_c, o_d, w_out, per_batch_mod):
    b, l, d = x.shape
    tl = min(l, WIDE_TILE)
    mod_map = (lambda bi, i: (bi, 0, 0)) if per_batch_mod else (lambda bi, i: (0, 0, 0))
    tok = lambda w_: pl.BlockSpec((1, tl, w_), lambda bi, i: (bi, i, 0))
    return pl.pallas_call(
        _mixout_odd_kernel,
        grid=(b, l // tl),
        in_specs=[tok(d), pl.BlockSpec((1, 6, d), mod_map), tok(512), tok(512),
                  pl.BlockSpec((d, d), lambda bi, i: (0, 0))],
        out_specs=tok(d),
        out_shape=jax.ShapeDtypeStruct((b, l, d), F32),
        compiler_params=_params("arbitrary", "arbitrary"),
        name="mixout_odd",
    )(x, mod, o_c, o_d, w_out)


def _rank16(s, kio, exact):
    n = s.shape[0]
    vals = []
    if exact:
        rank = jnp.full(s.shape, float(PEER_TOPK), F32)
        for r in range(PEER_TOPK):
            m = jnp.max(s, axis=0, keepdims=True)
            ix = jnp.min(jnp.where(s == m, kio, float(n)), axis=0, keepdims=True)
            hit = kio == ix
            rank = jnp.where(hit, float(r), rank)
            s = jnp.where(hit, -jnp.inf, s)
            vals.append(m)
        return vals, rank
    for r in range(PEER_TOPK):
        m = jnp.max(s, axis=0, keepdims=True)
        s = jnp.where(s == m, -RANK_CODE * (1.0 + r / 32.0), s)
        vals.append(m)
    code = s * (-1.0 / RANK_CODE)
    return vals, jnp.where(code >= 1.0, (code - 1.0) * 32.0, float(PEER_TOPK))


_SORT16 = ((0, 1), (2, 3), (0, 2), (1, 3), (1, 2), (4, 5), (6, 7), (4, 6), (5, 7), (5, 6), (0, 4), (2, 6), (2, 4),
           (1, 5), (3, 7), (3, 5), (1, 2), (3, 4), (5, 6), (8, 9), (10, 11), (8, 10), (9, 11), (9, 10), (12, 13),
           (14, 15), (12, 14), (13, 15), (13, 14), (8, 12), (10, 14), (10, 12), (9, 13), (11, 15), (11, 13), (9, 10),
           (11, 12), (13, 14), (0, 8), (4, 12), (4, 8), (2, 10), (6, 14), (6, 10), (2, 4), (6, 8), (10, 12), (1, 9),
           (5, 13), (5, 9), (3, 11), (7, 15), (7, 11), (3, 5), (7, 9), (11, 13), (1, 2), (3, 4), (5, 6), (7, 8),
           (9, 10), (11, 12), (13, 14))


def _sorted_top16(s):
    cols = [s[8 * i:8 * (i + 1), :] for i in range(16)]
    for i, j in _SORT16:
        cols[i], cols[j] = jnp.maximum(cols[i], cols[j]), jnp.minimum(cols[i], cols[j])
    vals = []
    for r in range(PEER_TOPK):
        m = cols[0]
        for shift in (4, 2, 1):
            m = jnp.maximum(m, pltpu.roll(m, shift, axis=0))
        vals.append(m[0:1])
        hit = cols[0] == m
        for i in range(PEER_TOPK - 1 - r):
            cols[i] = jnp.where(hit, cols[i + 1], cols[i])
    return vals


def _candidate_rows(v1, v2, j):
    v1hi = jnp.concatenate(v1[8:16], axis=0)
    v2lo = jnp.concatenate(v2[0:8], axis=0)
    v2hi = jnp.concatenate(v2[8:16], axis=0)
    g3_a = jnp.where(j < 5.0, v1[2], v1[4])
    g3_b = jnp.where(j < 5.0, v2lo, pltpu.roll(v2lo, 5, axis=0))
    g4_a = jnp.where(j < 4.0, v1[3], jnp.where(j < 6.0, v1[5], v1[6]))
    g4_b = jnp.where(j < 4.0, v2lo, jnp.where(j < 6.0, pltpu.roll(v2lo, 4, axis=0), pltpu.roll(v2lo, 6, axis=0)))
    return jnp.concatenate([v1[0] + v2lo, v1[0] + v2hi, v1[1] + v2lo, g3_a + g3_b, g4_a + g4_b,
                            jnp.where(j < 2.0, v1[7] + v2lo, -jnp.inf), v1hi + v2[0]], axis=0)


def _staircase_select(cand, j):
    rest = cand
    for r in range(PEER_TOPK):
        m = jnp.max(rest, axis=0, keepdims=True)
        if r == 0:
            best = m
        rest = jnp.where(rest == m, -jnp.inf, rest)
    picked = cand >= m
    z = jnp.sum(jnp.where(picked, jnp.exp(cand - best), 0.0), axis=0, keepdims=True)
    g = [jnp.where(picked[8 * i:8 * (i + 1)], 1.0, 0.0) for i in range(7)]
    tot = lambda v: jnp.sum(v, axis=0, keepdims=True)
    n = [None] * PEER_TOPK
    n[0], n[1], n[7] = tot(g[0] + g[1]), tot(g[2]), tot(g[5])
    n[2] = tot(jnp.where(j < 5.0, g[3], 0.0))
    n[4] = tot(g[3]) - n[2]
    n[3] = tot(jnp.where(j < 4.0, g[4], 0.0))
    n[5] = tot(jnp.where(j < 6.0, g[4], 0.0)) - n[3]
    n[6] = tot(g[4]) - n[3] - n[5]
    for k in range(8):
        n[8 + k] = g[6][k:k + 1]
    total = n[0]
    for a in range(1, PEER_TOPK):
        total = total + n[a]
    return z, n, total


def _candidate_positions(j):
    return jnp.concatenate([j, 8.0 + j, 16.0 + j,
                            jnp.where(j < 5.0, 32.0 + j, 59.0 + j),
                            jnp.where(j < 4.0, 48.0 + j, jnp.where(j < 6.0, 76.0 + j, 90.0 + j)),
                            jnp.where(j < 2.0, 112.0 + j, 999.0), 128.0 + 16.0 * j], axis=0)


def _peer_topk_kernel(x_ref, mod_ref, g_ref, wq_ref, keys_ref, h_ref, n_ref, p1_ref, r2_ref, p2_ref):
    mod = mod_ref[0]
    h = _modulate(x_ref[0], g_ref[...], mod[3:4], mod[4:5]).astype(BF16)
    h_ref[0] = h
    q = _dot(h, wq_ref[...])
    tl = q.shape[0]
    lanes = tl
    kio = lax.broadcasted_iota(jnp.int32, (PEER_NKEYS, lanes), 0).astype(F32)
    jrow = lax.broadcasted_iota(jnp.int32, (8, lanes), 0).astype(F32)
    pos = _candidate_positions(jrow)

    def head(hd, s1, s2, exact, at):
        s1, s2 = s1[:, at:at + lanes], s2[:, at:at + lanes]
        if exact:
            v1, rank1 = _rank16(s1, kio, True)
        else:
            v1 = _sorted_top16(s1)
        v2, rank2 = _rank16(s2, kio, exact)
        cand = _candidate_rows(v1, v2, jrow)
        if exact:
            sel_a = []
            for r in range(PEER_TOPK):
                m = jnp.max(cand, axis=0, keepdims=True)
                px = jnp.min(jnp.where(cand == m, pos, 999.0), axis=0, keepdims=True)
                cand = jnp.where(pos == px, -jnp.inf, cand)
                sel_a.append(jnp.floor(px * (1.0 / PEER_TOPK)))
                if r == 0:
                    best0, z = m, jnp.ones_like(m)
                else:
                    z = z + jnp.exp(m - best0)
            counts = []
            for a in range(PEER_TOPK):
                n_a = jnp.zeros_like(z)
                for r in range(a, PEER_TOPK):
                    n_a = n_a + jnp.where(sel_a[r] == float(a), 1.0, 0.0)
                counts.append(n_a)
        else:
            z, counts, picked = _staircase_select(cand, jrow)
        n_e1 = jnp.zeros((PEER_NKEYS, lanes), F32)
        for a in range(PEER_TOPK):
            n_e1 = jnp.where((rank1 == float(a)) if exact else (s1 == v1[a]), counts[a], n_e1)
        n_ref[0, hd, :, at:at + lanes] = n_e1
        p1_ref[0, hd, :, at:at + lanes] = jnp.exp(s1 - v1[0]) / z
        r2_ref[0, hd, :, at:at + lanes] = rank2.astype(BF16)
        p2_ref[0, hd, :, at:at + lanes] = jnp.exp(s2 - v2[0]).astype(BF16)
        if exact:
            return None
        total = (jnp.sum(jnp.where(s1 >= v1[PEER_TOPK - 1], 1.0, 0.0), axis=0, keepdims=True)
                 + jnp.sum(jnp.where(rank2 < float(PEER_TOPK), 1.0, 0.0), axis=0, keepdims=True) + picked)
        for r in range(PEER_TOPK - 1):
            total = total + jnp.where(v1[r] <= v1[r + 1], 1.0, 0.0)
        return jnp.max(jnp.abs(total - 3.0 * PEER_TOPK))

    def scores(hd):
        return (_dot_nt(keys_ref[2 * hd], q[:, 256 * hd:256 * hd + 128].astype(BF16)),
                _dot_nt(keys_ref[2 * hd + 1], q[:, 256 * hd + 128:256 * hd + 256].astype(BF16)))

    miscounts = []
    for hd in range(PEER_HEADS):
        s1, s2 = scores(hd)
        miscounts.append([head(hd, s1, s2, False, at) for at in range(0, tl, lanes)])
    for hd in range(PEER_HEADS):
        for at, miscount in zip(range(0, tl, lanes), miscounts[hd]):
            @pl.when(miscount > 0.0)
            def _(hd=hd, at=at):
                s1, s2 = scores(hd)
                head(hd, s1, s2, True, at)


def _peer_topk(x, mod, gain, w_q, keys, per_batch_mod):
    b, l, d = x.shape
    tl = TOKEN_TILE
    mod_map = (lambda bi, i: (bi, 0, 0)) if per_batch_mod else (lambda bi, i: (0, 0, 0))
    sel = pl.BlockSpec((1, PEER_HEADS, PEER_NKEYS, tl), lambda bi, i: (bi, 0, 0, i))
    sel_f32 = jax.ShapeDtypeStruct((b, PEER_HEADS, PEER_NKEYS, l), F32)
    sel_bf16 = jax.ShapeDtypeStruct((b, PEER_HEADS, PEER_NKEYS, l), BF16)
    return pl.pallas_call(
        _peer_topk_kernel,
        grid=(b, l // tl),
        in_specs=[pl.BlockSpec((1, tl, d), lambda bi, i: (bi, i, 0)),
                  pl.BlockSpec((1, 6, d), mod_map),
                  pl.BlockSpec((1, d), lambda bi, i: (0, 0)),
                  pl.BlockSpec(w_q.shape, lambda bi, i: (0, 0)),
                  pl.BlockSpec(keys.shape, lambda bi, i: (0, 0, 0))],
        out_specs=[pl.BlockSpec((1, tl, d), lambda bi, i: (bi, i, 0)), sel, sel, sel, sel],
        out_shape=[jax.ShapeDtypeStruct((b, l, d), BF16), sel_f32, sel_f32, sel_bf16, sel_bf16],
        compiler_params=_params("arbitrary", "arbitrary"),
        name="peer_topk",
    )(x, mod, gain, w_q, keys)


def _peer_dense_kernel(x_ref, mod_ref, h_ref, u_ref, vt_ref, n_ref, p1_ref, r2_ref, p2_ref, o_ref,
                       ht_ref, acc_ref, s_ref, gate_ref):
    e = pl.program_id(2)

    @pl.when(e == 0)
    def _():
        ht_ref[...] = h_ref[0].astype(F32).T.astype(BF16)
        acc_ref[...] = jnp.zeros_like(acc_ref)

    ht = ht_ref[...]
    tm = ht.shape[1]
    bounds = [sum(PEER_SUBS[:i]) for i in range(len(PEER_SUBS) + 1)]

    def gate_rows(lo, hi):
        for row in range(lo // PEER_NKEYS, hi // PEER_NKEYS):
            gate = [None] * (PEER_NKEYS // 16)
            for hd in range(PEER_HEADS):
                n_b = jnp.broadcast_to(n_ref[0, hd, row:row + 1, :], (16, tm)).astype(BF16)
                p1_b = jnp.broadcast_to(p1_ref[0, hd, row:row + 1, :], (16, tm)).astype(BF16)
                for k in range(PEER_NKEYS // 16):
                    term = jnp.where(r2_ref[0, hd, 16 * k:16 * (k + 1), :] < n_b,
                                     p2_ref[0, hd, 16 * k:16 * (k + 1), :], jnp.zeros((), BF16)) * p1_b
                    gate[k] = term if gate[k] is None else gate[k] + term
            for k in range(PEER_NKEYS // 16):
                at = PEER_NKEYS * row + 16 * k
                gate_ref[at:at + 16, :] = gate[k]

    for lo, hi in zip(bounds[:-1], bounds[1:]):
        s_ref[lo:hi, :] = _dot(u_ref[lo:hi, :], ht).astype(BF16)
        gate_rows(lo, hi)
    ws = [gate_ref[lo:hi, :] * _gelu(s_ref[lo:hi, :])
          for lo, hi in zip(bounds[:-1], bounds[1:])]
    acc_ref[...] += _dot(vt_ref[...], jnp.concatenate(ws, axis=0))

    @pl.when(e == pl.num_programs(2) - 1)
    def _():
        o_ref[0] = x_ref[0] + mod_ref[0][5:6] * acc_ref[...].T


def _peer_dense(x, mod, h, u_tabs, vt_tabs, layer, n_e1, p1n, rank2, p2, per_batch_mod):
    b, l, d = x.shape
    tm = PEER_TOKENS
    n_exp = u_tabs.shape[1]
    rows = PEER_STEP // PEER_NKEYS
    mod_map = (lambda bi, i, e: (bi, 0, 0)) if per_batch_mod else (lambda bi, i, e: (0, 0, 0))
    tok = lambda: pl.BlockSpec((1, tm, d), lambda bi, i, e: (bi, i, 0))
    by_row = pl.BlockSpec((1, PEER_HEADS, rows, tm), lambda bi, i, e: (bi, 0, e, i))
    by_key = pl.BlockSpec((1, PEER_HEADS, PEER_NKEYS, tm), lambda bi, i, e: (bi, 0, 0, i))
    return pl.pallas_call(
        _peer_dense_kernel,
        grid=(b, l // tm, n_exp // PEER_STEP),
        in_specs=[tok(), pl.BlockSpec((1, 6, d), mod_map), tok(),
                  pl.BlockSpec((None, PEER_STEP, d), lambda bi, i, e: (layer, e, 0)),
                  pl.BlockSpec((None, d, PEER_STEP), lambda bi, i, e: (layer, 0, e)),
                  by_row, by_row, by_key, by_key],
        out_specs=tok(),
        out_shape=jax.ShapeDtypeStruct((b, l, d), F32),
        scratch_shapes=[pltpu.VMEM((d, tm), BF16), pltpu.VMEM((d, tm), F32),
                        pltpu.VMEM((PEER_STEP, tm), BF16), pltpu.VMEM((PEER_STEP, tm), BF16)],
        compiler_params=_params("arbitrary", "arbitrary", "arbitrary"),
        name="peer_dense",
    )(x, mod, h, u_tabs, vt_tabs, n_e1, p1n, rank2, p2)


def _rope_tables(n_lat, width):
    q4 = DIFF_DK // 4
    n_rows = n_lat // GRID_W
    row = jnp.repeat(jnp.arange(n_rows), GRID_W).astype(F32)
    col = jnp.tile(jnp.arange(GRID_W), n_rows).astype(F32)
    freqs = ROPE_BASE ** (-jnp.arange(q4, dtype=F32) / q4)
    ang = jnp.stack([row[:, None] * freqs, col[:, None] * freqs], axis=1)
    cos, sin = jnp.cos(ang), jnp.sin(ang)
    cos64 = jnp.stack([cos, cos], axis=2).reshape(n_lat, 64)
    sin64 = jnp.stack([-sin, sin], axis=2).reshape(n_lat, 64)
    reps = width // 64
    return jnp.tile(cos64, (1, reps)), jnp.tile(sin64, (1, reps))


def _ones_block_diag(width, group):
    idx = jnp.arange(width) // group
    return jnp.where(idx[:, None] == idx[None, :], 1.0 / group, 0.0).astype(BF16)


def _peer_layer(x, mod, W, l, per_batch_mod):
    shape = x.shape
    if not per_batch_mod:
        x = x.reshape(1, -1, shape[-1])
    h, n_e1, p1n, rank2, p2 = _peer_topk(x, mod, W['norm_ffn'][l], W['peer_w_q'][l], W['peer_keys'][l],
                                         per_batch_mod)
    y = _peer_dense(x, mod, h, W['peer_u'], W['peer_vt'], l, n_e1, p1n, rank2, p2, per_batch_mod)
    return y.reshape(shape)


def _even_layer(x, mod, W, l, cache, per_batch_mod):
    b, seq, _ = x.shape
    i = l // 2
    lam_init = 0.8 - 0.6 * math.exp(-0.3 * l)
    rope_tabs = W['rope'] if cache is not None else None
    u_tm2, q, k_norm, k_att, v, v_att = _inproj_even(
        x, mod, W['norm_mix'][l], W['even_w_in'][i], W['ones_bd'], W['diff_q_norm'][i],
        W['diff_k_norm'][i], rope_tabs, per_batch_mod)
    if cache is None:
        h0 = jnp.zeros((2, b, S5_JB * S5_BLK), F32)
        k_all, v_all = k_att, v_att
    else:
        h0 = _s5_state_to_blocks(cache['s5'][:, i])
        k_all = jnp.concatenate([cache['diff_k'][:, i].reshape(b, -1, 512).astype(BF16), k_att], axis=1)
        v_all = jnp.concatenate([cache['diff_v'][:, i].reshape(b, -1, 512).astype(BF16), v_att], axis=1)
    yf, yb, fin = _s5_scan(u_tm2.reshape(seq, b, 512), W['s5_bd'][i], W['s5_cd'][i], W['s5_a'][i], h0)
    o_attn = _diff_attention(q, k_all, v_all, W['diff_lambda'][i], lam_init)
    x = _mixout_even(x, mod, u_tm2, yf.reshape(seq, b * 512), yb.reshape(seq, b * 512), o_attn,
                     W['s5_d'][i], W['s5_w_glu'][i], W['diff_sub_norm'][i], W['even_w_out'][i],
                     1.0 - lam_init, per_batch_mod)
    return x, (k_norm, v, fin)


def _odd_layer(x, mod, W, l, cache, per_batch_mod):
    b, seq, _ = x.shape
    i = l // 2
    rope_tabs = W['rope'] if cache is not None else None
    q, k_norm, k_att, v, v_att, u = _inproj_odd(
        x, mod, W['norm_mix'][l], W['odd_w_in'][i], W['ones_bd'], W['win_q_norm'][i],
        W['win_k_norm'][i], rope_tabs, per_batch_mod)
    if cache is None:
        o_c = _win_attention(q, k_att, v_att, W['win_sink'][i])
    else:
        pad = ((0, 0), (WINDOW, WINDOW), (0, 0))
        o_c = _win_attention(q, cache['win_k'][:, i].reshape(b, -1, 128).astype(BF16),
                             cache['win_v'][:, i].reshape(b, -1, 128).astype(BF16), W['win_sink'][i],
                             jnp.pad(k_att, pad), jnp.pad(v_att, pad))
    o_d = _pool_mixer(u, W['pool_w'][i], W['pool_scale'][i])
    x = _mixout_odd(x, mod, o_c, o_d, W['odd_w_out'][i], per_batch_mod)
    return x, (k_norm, v)


def kernel(x_prompt, x_sample, cache_diff_k, cache_diff_v, state_s5, cache_win_k, cache_win_v, c, c_ctx, ada_w, ada_b, norm_mix, norm_ffn, even_w_in, even_w_out, s5_lam_re, s5_lam_im, s5_log_step, s5_b_re, s5_b_im, s5_c_re, s5_c_im, s5_d, s5_w_glu, diff_q_norm, diff_k_norm, diff_lambda, diff_sub_norm, odd_w_in, odd_w_out, win_q_norm, win_k_norm, win_sink, pool_w, pool_scale, peer_w_q, peer_sub_keys, peer_u, peer_v):
    depth = ada_w.shape[0]
    n_even, n_odd = even_w_in.shape[0], odd_w_in.shape[0]
    bsz, seq, d = x_prompt.shape
    dec_b, dec_l, _ = x_sample.shape

    cvec = jnp.zeros((16, d), F32).at[0].set(c_ctx).at[1:1 + dec_b].set(c)
    mod = _modulation(cvec, ada_w, ada_b)

    s5 = [_s5_tables(s5_lam_re[i], s5_lam_im[i], s5_log_step[i], s5_b_re[i], s5_b_im[i],
                     s5_c_re[i], s5_c_im[i]) for i in range(n_even)]
    W = dict(
        norm_mix=norm_mix.reshape(depth, 1, d), norm_ffn=norm_ffn.reshape(depth, 1, d),
        even_w_in=even_w_in.astype(BF16), even_w_out=even_w_out.astype(BF16),
        odd_w_in=odd_w_in.astype(BF16), odd_w_out=odd_w_out.astype(BF16),
        ones_bd=_ones_block_diag(512, 64),
        diff_q_norm=jnp.tile(diff_q_norm, (1, 8)).reshape(n_even, 1, 512),
        diff_k_norm=jnp.tile(diff_k_norm, (1, 8)).reshape(n_even, 1, 512),
        diff_lambda=diff_lambda, diff_sub_norm=diff_sub_norm.reshape(n_even, 1, DIFF_DV),
        s5_bd=[t[0] for t in s5], s5_cd=[t[1] for t in s5], s5_a=[t[2] for t in s5],
        s5_d=s5_d.reshape(n_even, 1, 512), s5_w_glu=s5_w_glu.astype(BF16),
        win_q_norm=jnp.tile(win_q_norm, (1, 8)).reshape(n_odd, 1, 512),
        win_k_norm=jnp.tile(win_k_norm, (1, 2)).reshape(n_odd, 1, 128),
        win_sink=jnp.broadcast_to(win_sink[:, :, None], (n_odd, 8, 128)),
        pool_w=pool_w.astype(BF16), pool_scale=pool_scale.reshape(n_odd, 1, 512),
        peer_w_q=peer_w_q.astype(BF16),
        peer_keys=peer_sub_keys.astype(BF16).reshape(depth, 2 * PEER_HEADS, PEER_NKEYS, -1),
        peer_u=peer_u.astype(BF16),
        peer_vt=jnp.swapaxes(peer_v, 1, 2).astype(BF16),
        rope=_rope_tables(dec_l, 512),
    )
    cache = {'diff_k': cache_diff_k, 'diff_v': cache_diff_v, 's5': state_s5,
             'win_k': cache_win_k, 'win_v': cache_win_v}

    def run(x, mods, cch, per_batch_mod):
        states = []
        for l in range(depth):
            if l % 2 == 0:
                x, st = _even_layer(x, mods[l], W, l, cch, per_batch_mod)
            else:
                x, st = _odd_layer(x, mods[l], W, l, cch, per_batch_mod)
            states.append(st)
            x = _peer_layer(x, mods[l], W, l, per_batch_mod)
        return x, states

    ctx_mods = [mod[l, 0:1].reshape(1, 6, d) for l in range(depth)]
    dec_mods = [mod[l, 1:1 + dec_b].reshape(dec_b, 6, d) for l in range(depth)]
    y_prompt, states = run(x_prompt, ctx_mods, None, False)
    y_sample, _ = run(x_sample, dec_mods, cache, True)

    evens = [states[l] for l in range(depth) if l % 2 == 0]
    odds = [states[l] for l in range(depth) if l % 2 == 1]
    new_diff_k = jnp.stack([s[0].reshape(bsz, seq, 2, DIFF_HEADS, DIFF_DK) for s in evens], axis=1)
    new_diff_v = jnp.stack([s[1].reshape(bsz, seq, DIFF_HEADS, DIFF_DV) for s in evens], axis=1)
    new_s5 = jnp.stack([_s5_blocks_to_state(s[2]) for s in evens], axis=1)
    new_win_k = jnp.stack([s[0].reshape(bsz, seq, WIN_KV_HEADS, WIN_DH) for s in odds], axis=1)
    new_win_v = jnp.stack([s[1].reshape(bsz, seq, WIN_KV_HEADS, WIN_DH) for s in odds], axis=1)
    return (y_prompt, y_sample, new_diff_k, new_diff_v, new_s5, new_win_k, new_win_v)
```

```python
import functools
import math

import jax
import jax.numpy as jnp
from jax import lax
from jax.experimental import pallas as pl
from jax.experimental.pallas import tpu as pltpu

F32 = jnp.float32
BF16 = jnp.bfloat16

D_MODEL = 1024
MIX_HALF = 512
GRID_W = 64
EPS = 1e-6
NEG_INF = -1e30
ROPE_BASE = 10000.0

S5_GROUPS = 32
S5_GROUP = 16
S5_STATE = 64
S5_JB = 4
S5_BLK = 1024
S5_CHUNK = 32
S5_BATCH = 8

DIFF_HEADS = 4
DIFF_DK = 64
DIFF_DV = 128
WIN_KV_HEADS = 2
WIN_GROUP = 4
WIN_DH = 64
ATTN_SCALE = DIFF_DK ** -0.5
WINDOW = 128
POOL_WINDOWS = (2, 4, 8, 16)
POOL_PAD = 8

PEER_HEADS = 8
PEER_NKEYS = 128
PEER_TOPK = 16
RANK_CODE = 2.0 ** 100
PEER_SUBS = (128, 384, 512, 1024)
PEER_STEP = sum(PEER_SUBS)
PEER_TOKENS = 512

TOKEN_TILE = 256
WIDE_TILE = 512
VMEM_LIMIT = 56 * 1024 * 1024


def _params(*sem):
    return pltpu.CompilerParams(dimension_semantics=sem, vmem_limit_bytes=VMEM_LIMIT)


def _dot(a, b):
    return jnp.dot(a, b, preferred_element_type=F32)


def _dot_nt(a, b):
    return lax.dot_general(a, b, (((1,), (1,)), ((), ())), preferred_element_type=F32)


def _sigmoid(x):
    return 1.0 / (1.0 + jnp.exp(-x))


def _gelu(x):
    c = math.sqrt(2.0 / math.pi)
    hx = 0.5 * x
    return hx + hx * jnp.tanh(x * (c + (c * 0.044715) * (x * x)))


def _modulate(x, gain, shift, scale):
    ms = jnp.mean(x * x, axis=-1, keepdims=True)
    return (x * lax.rsqrt(ms + EPS) * gain) * (1.0 + scale) + shift


def _group_rms(z, ones_bd, gain):
    ms = _dot((z * z).astype(BF16), ones_bd)
    return z * lax.rsqrt(ms + EPS) * gain


def _rope(z, cos_t, sin_t):
    lane = lax.broadcasted_iota(jnp.int32, (1, 128), 1) % 32
    first = lane < 16
    parts = []
    for j in range(z.shape[1] // 128):
        c = z[:, 128 * j:128 * (j + 1)]
        parts.append(jnp.where(first, pltpu.roll(c, 112, axis=1), pltpu.roll(c, 16, axis=1)))
    swapped = parts[0] if len(parts) == 1 else jnp.concatenate(parts, axis=1)
    return z * cos_t + swapped * sin_t


def _mod_kernel(c_ref, w_ref, b_ref, o_ref):
    c = c_ref[...]
    s = c * _sigmoid(c)
    o_ref[0] = jnp.dot(s, w_ref[0], precision=lax.Precision.HIGHEST,
                       preferred_element_type=F32) + b_ref[0]


def _modulation(cvec, ada_w, ada_b):
    depth, d, n = ada_w.shape
    tn = 1536
    return pl.pallas_call(
        _mod_kernel,
        grid=(depth, n // tn),
        in_specs=[pl.BlockSpec((16, d), lambda l, j: (0, 0)),
                  pl.BlockSpec((1, d, tn), lambda l, j: (l, 0, j)),
                  pl.BlockSpec((1, 1, tn), lambda l, j: (l, 0, j))],
        out_specs=pl.BlockSpec((1, 16, tn), lambda l, j: (l, 0, j)),
        out_shape=jax.ShapeDtypeStruct((depth, 16, n), F32),
        compiler_params=_params("arbitrary", "arbitrary"),
        name="adaln_mod",
    )(cvec, ada_w, ada_b.reshape(depth, 1, n))


def _inproj_even_kernel(rope, *refs):
    if rope:
        (x_ref, mod_ref, g_ref, w_ref, bd_ref, gq_ref, gk_ref, cos_ref, sin_ref,
         u_ref, q_ref, kn_ref, ka_ref, v_ref, va_ref) = refs
    else:
        (x_ref, mod_ref, g_ref, w_ref, bd_ref, gq_ref, gk_ref,
         u_ref, q_ref, kn_ref, ka_ref, v_ref, va_ref) = refs
    mod = mod_ref[0]
    h = _modulate(x_ref[0], g_ref[...], mod[0:1], mod[1:2])
    z = _dot(h.astype(BF16), w_ref[...])
    u = z[:, 0:512]
    q = _group_rms(z[:, 512:1024], bd_ref[...], gq_ref[...])
    k = _group_rms(z[:, 1024:1536], bd_ref[...], gk_ref[...])
    v = z[:, 1536:2048]
    u_ref[...] = u
    kn_ref[0] = k
    v_ref[0] = v
    va_ref[0] = v.astype(BF16)
    if rope:
        q = _rope(q, cos_ref[...], sin_ref[...])
        k = _rope(k, cos_ref[...], sin_ref[...])
    q_ref[0] = (q * ATTN_SCALE).astype(BF16)
    ka_ref[0] = k.astype(BF16)


def _inproj_even(x, mod, gain, w, ones_bd, gq, gk, rope_tabs, per_batch_mod):
    b, l, d = x.shape
    tl = min(l, WIDE_TILE)
    rope = rope_tabs is not None
    mod_map = (lambda bi, i: (bi, 0, 0)) if per_batch_mod else (lambda bi, i: (0, 0, 0))
    const2 = lambda bi, i: (0, 0)
    tok = lambda w_: pl.BlockSpec((1, tl, w_), lambda bi, i: (bi, i, 0))
    in_specs = [tok(d),
                pl.BlockSpec((1, 6, d), mod_map),
                pl.BlockSpec((1, d), const2),
                pl.BlockSpec(w.shape, const2),
                pl.BlockSpec(ones_bd.shape, const2),
                pl.BlockSpec((1, 512), const2),
                pl.BlockSpec((1, 512), const2)]
    args = [x, mod, gain, w, ones_bd, gq, gk]
    if rope:
        in_specs += [pl.BlockSpec((tl, 512), lambda bi, i: (i, 0))] * 2
        args += list(rope_tabs)
    out_shape = [jax.ShapeDtypeStruct((l, b * 512), F32),
                 jax.ShapeDtypeStruct((b, l, 512), BF16),
                 jax.ShapeDtypeStruct((b, l, 512), F32),
                 jax.ShapeDtypeStruct((b, l, 512), BF16),
                 jax.ShapeDtypeStruct((b, l, 512), F32),
                 jax.ShapeDtypeStruct((b, l, 512), BF16)]
    out_specs = [pl.BlockSpec((tl, 512), lambda bi, i: (i, bi)),
                 tok(512), tok(512), tok(512), tok(512), tok(512)]
    return pl.pallas_call(
        functools.partial(_inproj_even_kernel, rope),
        grid=(b, l // tl), in_specs=in_specs, out_specs=out_specs, out_shape=out_shape,
        compiler_params=_params("arbitrary", "arbitrary"),
        name="inproj_even_rope" if rope else "inproj_even",
    )(*args)


def _inproj_odd_kernel(rope, *refs):
    if rope:
        (x_ref, mod_ref, g_ref, w_ref, bd_ref, gq_ref, gk_ref, cos_ref, sin_ref,
         q_ref, kn_ref, ka_ref, v_ref, va_ref, u_ref) = refs
    else:
        (x_ref, mod_ref, g_ref, w_ref, bd_ref, gq_ref, gk_ref,
         q_ref, kn_ref, ka_ref, v_ref, va_ref, u_ref) = refs
    mod = mod_ref[0]
    h = _modulate(x_ref[0], g_ref[...], mod[0:1], mod[1:2])
    z = _dot(h.astype(BF16), w_ref[...])
    bd = bd_ref[...]
    q = _group_rms(z[:, 0:512], bd, gq_ref[...])
    k = _group_rms(z[:, 512:640], bd[0:128, 0:128], gk_ref[...])
    v = z[:, 640:768]
    kn_ref[0] = k
    v_ref[0] = v
    va_ref[0] = v.astype(BF16)
    u_ref[0] = z[:, 768:1280]
    if rope:
        q = _rope(q, cos_ref[...], sin_ref[...])
        k = _rope(k, cos_ref[:, 0:128], sin_ref[:, 0:128])
    q_ref[0] = (q * ATTN_SCALE).astype(BF16)
    ka_ref[0] = k.astype(BF16)


def _inproj_odd(x, mod, gain, w, ones_bd, gq, gk, rope_tabs, per_batch_mod):
    b, l, d = x.shape
    tl = min(l, WIDE_TILE)
    rope = rope_tabs is not None
    mod_map = (lambda bi, i: (bi, 0, 0)) if per_batch_mod else (lambda bi, i: (0, 0, 0))
    const2 = lambda bi, i: (0, 0)
    tok = lambda w_: pl.BlockSpec((1, tl, w_), lambda bi, i: (bi, i, 0))
    in_specs = [tok(d),
                pl.BlockSpec((1, 6, d), mod_map),
                pl.BlockSpec((1, d), const2),
                pl.BlockSpec(w.shape, const2),
                pl.BlockSpec(ones_bd.shape, const2),
                pl.BlockSpec((1, 512), const2),
                pl.BlockSpec((1, 128), const2)]
    args = [x, mod, gain, w, ones_bd, gq, gk]
    if rope:
        in_specs += [pl.BlockSpec((tl, 512), lambda bi, i: (i, 0))] * 2
        args += list(rope_tabs)
    out_shape = [jax.ShapeDtypeStruct((b, l, 512), BF16),
                 jax.ShapeDtypeStruct((b, l, 128), F32),
                 jax.ShapeDtypeStruct((b, l, 128), BF16),
                 jax.ShapeDtypeStruct((b, l, 128), F32),
                 jax.ShapeDtypeStruct((b, l, 128), BF16),
                 jax.ShapeDtypeStruct((b, l, 512), F32)]
    out_specs = [tok(512), tok(128), tok(128), tok(128), tok(128), tok(512)]
    return pl.pallas_call(
        functools.partial(_inproj_odd_kernel, rope),
        grid=(b, l // tl), in_specs=in_specs, out_specs=out_specs, out_shape=out_shape,
        compiler_params=_params("arbitrary", "arbitrary"),
        name="inproj_odd_rope" if rope else "inproj_odd",
    )(*args)


def _s5_kernel(uf_ref, ub_ref, bd_ref, cd_ref, a_ref, h0_ref, yf_ref, yb_ref, fin_ref,
               bu_ref, carry_ref):
    c = pl.program_id(1)
    rows = S5_CHUNK * S5_BATCH

    @pl.when(c == 0)
    def _():
        carry_ref[...] = h0_ref[...]

    for d in range(2):
        u = (uf_ref if d == 0 else ub_ref)[...].reshape(rows, MIX_HALF).astype(BF16)
        for j in range(S5_JB):
            bu_ref[d, :, S5_BLK * j:S5_BLK * (j + 1)] = _dot(u[:, 128 * j:128 * (j + 1)], bd_ref[d, j])
        for j in range(S5_JB):
            lo = S5_BLK * j
            a_re = jnp.broadcast_to(a_ref[d, 0, j:j + 1, :], (S5_BATCH, 512))
            a_im = jnp.broadcast_to(a_ref[d, 1, j:j + 1, :], (S5_BATCH, 512))

            h_re, h_im = carry_ref[d, :, lo:lo + 512], carry_ref[d, :, lo + 512:lo + 1024]
            for s in range(S5_CHUNK):
                row = (s if d == 0 else S5_CHUNK - 1 - s) * S5_BATCH
                b_re = bu_ref[d, row:row + S5_BATCH, lo:lo + 512]
                b_im = bu_ref[d, row:row + S5_BATCH, lo + 512:lo + 1024]
                h_re, h_im = a_re * h_re - a_im * h_im + b_re, a_re * h_im + a_im * h_re + b_im
                bu_ref[d, row:row + S5_BATCH, lo:lo + 512] = h_re
                bu_ref[d, row:row + S5_BATCH, lo + 512:lo + 1024] = h_im
            carry_ref[d, :, lo:lo + 512] = h_re
            carry_ref[d, :, lo + 512:lo + 1024] = h_im
        ys = [_dot(bu_ref[d, :, S5_BLK * j:S5_BLK * (j + 1)].astype(BF16), cd_ref[d, j])
              for j in range(S5_JB)]
        y = jnp.concatenate(ys, axis=1).reshape(S5_CHUNK, S5_BATCH, MIX_HALF)
        if d == 0:
            yf_ref[...] = y
        else:
            yb_ref[...] = y

    @pl.when(c == pl.num_programs(1) - 1)
    def _():
        fin_ref[...] = carry_ref[...]


def _s5_scan(u_tm, bd, cd, acoef, h0):
    l, b, _ = u_tm.shape
    nc = l // S5_CHUNK
    ng = b // S5_BATCH
    state_w = S5_JB * S5_BLK
    blk = (S5_CHUNK, S5_BATCH, MIX_HALF)
    full = lambda a: pl.BlockSpec(a.shape, lambda g, c: (0,) * a.ndim)
    return pl.pallas_call(
        _s5_kernel,
        grid=(ng, nc),
        in_specs=[pl.BlockSpec(blk, lambda g, c: (c, g, 0)),
                  pl.BlockSpec(blk, lambda g, c: (nc - 1 - c, g, 0)),
                  full(bd), full(cd), full(acoef),
                  pl.BlockSpec((2, S5_BATCH, state_w), lambda g, c: (0, g, 0))],
        out_specs=[pl.BlockSpec(blk, lambda g, c: (c, g, 0)),
                   pl.BlockSpec(blk, lambda g, c: (nc - 1 - c, g, 0)),
                   pl.BlockSpec((2, S5_BATCH, state_w), lambda g, c: (0, g, 0))],
        out_shape=[jax.ShapeDtypeStruct((l, b, MIX_HALF), F32),
                   jax.ShapeDtypeStruct((l, b, MIX_HALF), F32),
                   jax.ShapeDtypeStruct((2, b, state_w), F32)],
        scratch_shapes=[pltpu.VMEM((2, S5_CHUNK * S5_BATCH, state_w), F32),
                        pltpu.VMEM((2, S5_BATCH, state_w), F32)],
        compiler_params=_params("arbitrary", "arbitrary"),
        name="s5_scan",
    )(u_tm, u_tm, bd, cd, acoef, h0)


def _s5_tables(lam_re, lam_im, log_step, b_re, b_im, c_re, c_im):
    step = jnp.exp(log_step.astype(F32))[..., None]
    lr, li = lam_re.astype(F32), lam_im.astype(F32)
    er = jnp.exp(lr * step)
    a_re, a_im = er * jnp.cos(li * step), er * jnp.sin(li * step)
    den = lr * lr + li * li
    q_re = ((a_re - 1.0) * lr + a_im * li) / den
    q_im = (a_im * lr - (a_re - 1.0) * li) / den
    bb_re = q_re[..., None] * b_re - q_im[..., None] * b_im
    bb_im = q_re[..., None] * b_im + q_im[..., None] * b_re
    eye = jnp.eye(8, dtype=F32)
    bb = jnp.stack([bb_re, bb_im], axis=1).reshape(2, 2, S5_JB, 8, S5_STATE, S5_GROUP)
    bd = jnp.einsum('drjgpc,gh->djgcrhp', bb, eye).reshape(2, S5_JB, 128, S5_BLK).astype(BF16)
    cc = jnp.stack([c_re, -c_im], axis=1).astype(F32).reshape(2, 2, S5_JB, 8, S5_GROUP, S5_STATE)
    cd = jnp.einsum('drjgcp,gh->djrgphc', cc, eye).reshape(2, S5_JB, S5_BLK, 128).astype(BF16)
    acoef = jnp.stack([a_re, a_im], axis=1).reshape(2, 2, S5_JB, 512)
    return bd, cd, acoef


def _s5_state_to_blocks(st):
    b = st.shape[0]
    st = st.astype(F32).reshape(b, 2, 2, S5_JB, 8, S5_STATE)
    return jnp.transpose(st, (1, 0, 3, 2, 4, 5)).reshape(2, b, S5_JB * S5_BLK)


def _s5_blocks_to_state(fin):
    b = fin.shape[1]
    fin = fin.reshape(2, b, S5_JB, 2, 8, S5_STATE)
    return jnp.transpose(fin, (1, 0, 3, 2, 4, 5)).reshape(b, 2, 2, S5_GROUPS, S5_STATE)


def _diff_attn_kernel(lam_init, q_ref, k_ref, v_ref, lv_ref, o_ref):
    lv = lv_ref[...]
    lam = (jnp.exp(jnp.sum(lv[0:1] * lv[1:2], axis=-1, keepdims=True))
           - jnp.exp(jnp.sum(lv[2:3] * lv[3:4], axis=-1, keepdims=True)) + lam_init)
    lane = lax.broadcasted_iota(jnp.int32, (1, 128), 1)
    def scores(h):
        out = []
        for m in range(2):
            j = m * 2 + h // 2
            keep = (lane < 64) if h % 2 == 0 else (lane >= 64)
            qb = jnp.where(keep, q_ref[0, :, 128 * j:128 * (j + 1)], jnp.zeros((), BF16))
            out.append(_dot_nt(qb, k_ref[0, :, 128 * j:128 * (j + 1)]))
        return out

    outs = []
    nxt = scores(0)
    for h in range(DIFF_HEADS):
        cur = nxt
        if h + 1 < DIFF_HEADS:
            nxt = scores(h + 1)
        es = [jnp.exp(s - jnp.max(s, axis=-1, keepdims=True)) for s in cur]
        sums = [jnp.sum(e, axis=-1, keepdims=True) for e in es]
        a = es[0] - (lam * sums[0] / sums[1]) * es[1]
        outs.append(_dot(a.astype(BF16), v_ref[0, :, 128 * h:128 * (h + 1)]) / sums[0])
    o_ref[0] = jnp.concatenate(outs, axis=1)


def _diff_attention(q, k, v, lv, lam_init):
    b, lq, _ = q.shape
    lk = k.shape[1]
    tq = TOKEN_TILE
    return pl.pallas_call(
        functools.partial(_diff_attn_kernel, lam_init),
        grid=(b, lq // tq),
        in_specs=[pl.BlockSpec((1, tq, 512), lambda bi, i: (bi, i, 0)),
                  pl.BlockSpec((1, lk, 512), lambda bi, i: (bi, 0, 0)),
                  pl.BlockSpec((1, lk, 512), lambda bi, i: (bi, 0, 0)),
                  pl.BlockSpec((4, DIFF_DK), lambda bi, i: (0, 0))],
        out_specs=pl.BlockSpec((1, tq, 512), lambda bi, i: (bi, i, 0)),
        out_shape=jax.ShapeDtypeStruct((b, lq, 512), F32),
        compiler_params=_params("arbitrary", "arbitrary"),
        name="diff_attention",
    )(q, k, v, lv)


def _win_attn_kernel(local, seq_len, *refs):
    if local:
        q_ref, kc_ref, vc_ref, kl_ref, vl_ref, sink_ref, o_ref = refs
    else:
        q_ref, kc_ref, vc_ref, sink_ref, o_ref = refs
    tq = q_ref.shape[1]
    lane = lax.broadcasted_iota(jnp.int32, (1, 128), 1)
    low = lane < 64
    kc = kc_ref[0]
    vc = vc_ref[0]
    n_ctx = kc.shape[0]
    if local:
        span = tq + 2 * WINDOW
        start = pl.multiple_of(pl.program_id(1) * tq, tq)
        kl = kl_ref[0, pl.ds(start, span), :]
        vl = vl_ref[0, pl.ds(start, span), :]
        qpos = start + lax.broadcasted_iota(jnp.int32, (tq, span), 0)
        kpos = start - WINDOW + lax.broadcasted_iota(jnp.int32, (tq, span), 1)
        ok = (jnp.abs(qpos - kpos) <= WINDOW) & (kpos >= 0) & (kpos < seq_len)
    def scores(hd):
        n = hd // WIN_GROUP
        j, half = hd // 2, hd % 2
        qb = q_ref[0, :, 128 * j:128 * (j + 1)]
        if half != n:
            qb = jnp.concatenate([qb[:, 64:128], qb[:, 0:64]], axis=1)
        qb = jnp.where(low if n == 0 else jnp.logical_not(low), qb, jnp.zeros((), BF16))
        s = _dot_nt(qb, kc)
        if local:
            s = jnp.concatenate([s, jnp.where(ok, _dot_nt(qb, kl), NEG_INF)], axis=1)
        return s

    heads = {}
    n_heads = WIN_KV_HEADS * WIN_GROUP
    nxt = scores(0)
    for hd in range(n_heads):
        s = nxt
        if hd + 1 < n_heads:
            nxt = scores(hd + 1)
        sink = sink_ref[hd:hd + 1, 0:1]
        mx = jnp.maximum(jnp.max(s, axis=-1, keepdims=True), sink)
        e = jnp.exp(s - mx)
        den = jnp.sum(e, axis=-1, keepdims=True) + jnp.exp(sink - mx)
        p = e.astype(BF16)
        o = _dot(p[:, 0:n_ctx], vc)
        if local:
            o = o + _dot(p[:, n_ctx:], vl)
        heads[hd] = o / den
    blocks = []
    for j in range(4):
        ev, od = heads[2 * j], heads[2 * j + 1]
        n = (2 * j) // WIN_GROUP
        if n == 1:
            ev = pltpu.roll(ev, 64, axis=1)
        else:
            od = pltpu.roll(od, 64, axis=1)
        blocks.append(jnp.where(low, ev, od))
    o_ref[0] = jnp.concatenate(blocks, axis=1).astype(BF16)


def _win_attention(q, kc, vc, sink, k_lat=None, v_lat=None):
    b, lq, _ = q.shape
    n_ctx = kc.shape[1]
    local = k_lat is not None
    tq = TOKEN_TILE
    in_specs = [pl.BlockSpec((1, tq, 512), lambda bi, i: (bi, i, 0)),
                pl.BlockSpec((1, n_ctx, 128), lambda bi, i: (bi, 0, 0)),
                pl.BlockSpec((1, n_ctx, 128), lambda bi, i: (bi, 0, 0))]
    args = [q, kc, vc]
    if local:
        lp = k_lat.shape[1]
        in_specs += [pl.BlockSpec((1, lp, 128), lambda bi, i: (bi, 0, 0))] * 2
        args += [k_lat, v_lat]
    in_specs.append(pl.BlockSpec((8, 128), lambda bi, i: (0, 0)))
    args.append(sink)
    return pl.pallas_call(
        functools.partial(_win_attn_kernel, local, lq),
        grid=(b, lq // tq), in_specs=in_specs,
        out_specs=pl.BlockSpec((1, tq, 512), lambda bi, i: (bi, i, 0)),
        out_shape=jax.ShapeDtypeStruct((b, lq, 512), BF16),
        compiler_params=_params("arbitrary", "arbitrary"),
        name="win_attention_local" if local else "win_attention",
    )(*args)


def _pool_kernel(u_ref, w_ref, scale_ref, o_ref):
    l = u_ref.shape[1]
    lp = l + 2 * POOL_PAD
    t = lax.broadcasted_iota(jnp.int32, (l, 128), 0)
    zpad = jnp.zeros((POOL_PAD, 128), F32)

    def shifted(a, s):
        return pltpu.roll(a, s, axis=0) + pltpu.roll(a, lp - s, axis=0)

    outs = []
    for gi, wl in enumerate(POOL_WINDOWS):
        x = u_ref[0, :, 128 * gi:128 * (gi + 1)]
        xp = jnp.concatenate([zpad, x, zpad], axis=0)
        acc = xp + pltpu.roll(xp, 1, axis=0)
        if wl >= 4:
            acc = shifted(acc, 1)
        if wl >= 8:
            acc = shifted(acc, 2)
        if wl >= 16:
            acc = shifted(acc, 4)
        win = acc[POOL_PAD:POOL_PAD + l]
        cnt = (jnp.minimum(t + wl // 2, l) - jnp.maximum(t - wl // 2, 0)).astype(F32)
        pooled = win / cnt - x
        outs.append(_dot(pooled.astype(BF16), w_ref[gi]))
    o_ref[0] = (jnp.concatenate(outs, axis=1) * scale_ref[...]).astype(BF16)


def _pool_mixer(u, w, scale):
    b, l, _ = u.shape
    return pl.pallas_call(
        _pool_kernel,
        grid=(b,),
        in_specs=[pl.BlockSpec((1, l, 512), lambda bi: (bi, 0, 0)),
                  pl.BlockSpec(w.shape, lambda bi: (0, 0, 0)),
                  pl.BlockSpec((1, 512), lambda bi: (0, 0))],
        out_specs=pl.BlockSpec((1, l, 512), lambda bi: (bi, 0, 0)),
        out_shape=jax.ShapeDtypeStruct((b, l, 512), BF16),
        compiler_params=_params("arbitrary"),
        name="pool_mixer",
    )(u, w, scale)


def _mixout_even_kernel(out_scale, x_ref, mod_ref, u_ref, yf_ref, yb_ref, oa_ref, sd_ref,
                        wglu_ref, gsub_ref, wout_ref, o_ref):
    mod = mod_ref[0]
    y = sd_ref[...] * u_ref[...] + yf_ref[...] + yb_ref[...]
    g = _gelu(y)
    o_a = g * _sigmoid(_dot(g.astype(BF16), wglu_ref[...]))
    parts = [o_a.astype(BF16)]
    for h in range(DIFF_HEADS):
        blk = oa_ref[0, :, 128 * h:128 * (h + 1)]
        ms = jnp.mean(blk * blk, axis=-1, keepdims=True)
        parts.append(((blk * lax.rsqrt(ms + EPS) * gsub_ref[...]) * out_scale).astype(BF16))
    cat = jnp.concatenate(parts, axis=1)
    o_ref[0] = x_ref[0] + mod[2:3] * _dot(cat, wout_ref[...])


def _mixout_even(x, mod, u_tm2, yf2, yb2, o_attn, s5_d, w_glu, g_sub, w_out, out_scale, per_batch_mod):
    b, l, d = x.shape
    tl = min(l, WIDE_TILE)
    mod_map = (lambda bi, i: (bi, 0, 0)) if per_batch_mod else (lambda bi, i: (0, 0, 0))
    const2 = lambda bi, i: (0, 0)
    tm = pl.BlockSpec((tl, 512), lambda bi, i: (i, bi))
    return pl.pallas_call(
        functools.partial(_mixout_even_kernel, out_scale),
        grid=(b, l // tl),
        in_specs=[pl.BlockSpec((1, tl, d), lambda bi, i: (bi, i, 0)),
                  pl.BlockSpec((1, 6, d), mod_map),
                  tm, tm, tm,
                  pl.BlockSpec((1, tl, 512), lambda bi, i: (bi, i, 0)),
                  pl.BlockSpec((1, 512), const2),
                  pl.BlockSpec((512, 512), const2),
                  pl.BlockSpec((1, 128), const2),
                  pl.BlockSpec((d, d), const2)],
        out_specs=pl.BlockSpec((1, tl, d), lambda bi, i: (bi, i, 0)),
        out_shape=jax.ShapeDtypeStruct((b, l, d), F32),
        compiler_params=_params("arbitrary", "arbitrary"),
        name="mixout_even",
    )(x, mod, u_tm2, yf2, yb2, o_attn, s5_d, w_glu, g_sub, w_out)


def _mixout_odd_kernel(x_ref, mod_ref, oc_ref, od_ref, wout_ref, o_ref):
    mod = mod_ref[0]
    cat = jnp.concatenate([oc_ref[0], od_ref[0]], axis=1)
    o_ref[0] = x_ref[0] + mod[2:3] * _dot(cat, wout_ref[...])


def _mixout_odd(x, mod, o_c, o_d, w_out, per_batch_mod):
    b, l, d = x.shape
    tl = min(l, WIDE_TILE)
    mod_map = (lambda bi, i: (bi, 0, 0)) if per_batch_mod else (lambda bi, i: (0, 0, 0))
    tok = lambda w_: pl.BlockSpec((1, tl, w_), lambda bi, i: (bi, i, 0))
    return pl.pallas_call(
        _mixout_odd_kernel,
        grid=(b, l // tl),
        in_specs=[tok(d), pl.BlockSpec((1, 6, d), mod_map), tok(512), tok(512),
                  pl.BlockSpec((d, d), lambda bi, i: (0, 0))],
        out_specs=tok(d),
        out_shape=jax.ShapeDtypeStruct((b, l, d), F32),
        compiler_params=_params("arbitrary", "arbitrary"),
        name="mixout_odd",
    )(x, mod, o_c, o_d, w_out)


def _rank16(s, kio, exact):
    n = s.shape[0]
    vals = []
    if exact:
        rank = jnp.full(s.shape, float(PEER_TOPK), F32)
        for r in range(PEER_TOPK):
            m = jnp.max(s, axis=0, keepdims=True)
            ix = jnp.min(jnp.where(s == m, kio, float(n)), axis=0, keepdims=True)
            hit = kio == ix
            rank = jnp.where(hit, float(r), rank)
            s = jnp.where(hit, -jnp.inf, s)
            vals.append(m)
        return vals, rank
    for r in range(PEER_TOPK):
        m = jnp.max(s, axis=0, keepdims=True)
        s = jnp.where(s == m, -RANK_CODE * (1.0 + r / 32.0), s)
        vals.append(m)
    code = s * (-1.0 / RANK_CODE)
    return vals, jnp.where(code >= 1.0, (code - 1.0) * 32.0, float(PEER_TOPK))


_SORT16 = ((0, 1), (2, 3), (0, 2), (1, 3), (1, 2), (4, 5), (6, 7), (4, 6), (5, 7), (5, 6), (0, 4), (2, 6), (2, 4),
           (1, 5), (3, 7), (3, 5), (1, 2), (3, 4), (5, 6), (8, 9), (10, 11), (8, 10), (9, 11), (9, 10), (12, 13),
           (14, 15), (12, 14), (13, 15), (13, 14), (8, 12), (10, 14), (10, 12), (9, 13), (11, 15), (11, 13), (9, 10),
           (11, 12), (13, 14), (0, 8), (4, 12), (4, 8), (2, 10), (6, 14), (6, 10), (2, 4), (6, 8), (10, 12), (1, 9),
           (5, 13), (5, 9), (3, 11), (7, 15), (7, 11), (3, 5), (7, 9), (11, 13), (1, 2), (3, 4), (5, 6), (7, 8),
           (9, 10), (11, 12), (13, 14))


def _sorted_top16(s):
    cols = [s[8 * i:8 * (i + 1), :] for i in range(16)]
    for i, j in _SORT16:
        cols[i], cols[j] = jnp.maximum(cols[i], cols[j]), jnp.minimum(cols[i], cols[j])
    vals = []
    for r in range(PEER_TOPK):
        m = cols[0]
        for shift in (4, 2, 1):
            m = jnp.maximum(m, pltpu.roll(m, shift, axis=0))
        vals.append(m[0:1])
        hit = cols[0] == m
        for i in range(PEER_TOPK - 1 - r):
            cols[i] = jnp.where(hit, cols[i + 1], cols[i])
    return vals


def _candidate_rows(v1, v2, j):
    v1hi = jnp.concatenate(v1[8:16], axis=0)
    v2lo = jnp.concatenate(v2[0:8], axis=0)
    v2hi = jnp.concatenate(v2[8:16], axis=0)
    g3_a = jnp.where(j < 5.0, v1[2], v1[4])
    g3_b = jnp.where(j < 5.0, v2lo, pltpu.roll(v2lo, 5, axis=0))
    g4_a = jnp.where(j < 4.0, v1[3], jnp.where(j < 6.0, v1[5], v1[6]))
    g4_b = jnp.where(j < 4.0, v2lo, jnp.where(j < 6.0, pltpu.roll(v2lo, 4, axis=0), pltpu.roll(v2lo, 6, axis=0)))
    return jnp.concatenate([v1[0] + v2lo, v1[0] + v2hi, v1[1] + v2lo, g3_a + g3_b, g4_a + g4_b,
                            jnp.where(j < 2.0, v1[7] + v2lo, -jnp.inf), v1hi + v2[0]], axis=0)


def _staircase_select(cand, j):
    rest = cand
    for r in range(PEER_TOPK):
        m = jnp.max(rest, axis=0, keepdims=True)
        if r == 0:
            best = m
        rest = jnp.where(rest == m, -jnp.inf, rest)
    picked = cand >= m
    z = jnp.sum(jnp.where(picked, jnp.exp(cand - best), 0.0), axis=0, keepdims=True)
    g = [jnp.where(picked[8 * i:8 * (i + 1)], 1.0, 0.0) for i in range(7)]
    tot = lambda v: jnp.sum(v, axis=0, keepdims=True)
    n = [None] * PEER_TOPK
    n[0], n[1], n[7] = tot(g[0] + g[1]), tot(g[2]), tot(g[5])
    n[2] = tot(jnp.where(j < 5.0, g[3], 0.0))
    n[4] = tot(g[3]) - n[2]
    n[3] = tot(jnp.where(j < 4.0, g[4], 0.0))
    n[5] = tot(jnp.where(j < 6.0, g[4], 0.0)) - n[3]
    n[6] = tot(g[4]) - n[3] - n[5]
    for k in range(8):
        n[8 + k] = g[6][k:k + 1]
    total = n[0]
    for a in range(1, PEER_TOPK):
        total = total + n[a]
    return z, n, total


def _candidate_positions(j):
    return jnp.concatenate([j, 8.0 + j, 16.0 + j,
                            jnp.where(j < 5.0, 32.0 + j, 59.0 + j),
                            jnp.where(j < 4.0, 48.0 + j, jnp.where(j < 6.0, 76.0 + j, 90.0 + j)),
                            jnp.where(j < 2.0, 112.0 + j, 999.0), 128.0 + 16.0 * j], axis=0)


def _peer_topk_kernel(x_ref, mod_ref, g_ref, wq_ref, keys_ref, h_ref, n_ref, p1_ref, r2_ref, p2_ref):
    mod = mod_ref[0]
    h = _modulate(x_ref[0], g_ref[...], mod[3:4], mod[4:5]).astype(BF16)
    h_ref[0] = h
    q = _dot(h, wq_ref[...])
    tl = q.shape[0]
    lanes = tl
    kio = lax.broadcasted_iota(jnp.int32, (PEER_NKEYS, lanes), 0).astype(F32)
    jrow = lax.broadcasted_iota(jnp.int32, (8, lanes), 0).astype(F32)
    pos = _candidate_positions(jrow)

    def head(hd, s1, s2, exact, at):
        s1, s2 = s1[:, at:at + lanes], s2[:, at:at + lanes]
        if exact:
            v1, rank1 = _rank16(s1, kio, True)
        else:
            v1 = _sorted_top16(s1)
        v2, rank2 = _rank16(s2, kio, exact)
        cand = _candidate_rows(v1, v2, jrow)
        if exact:
            sel_a = []
            for r in range(PEER_TOPK):
                m = jnp.max(cand, axis=0, keepdims=True)
                px = jnp.min(jnp.where(cand == m, pos, 999.0), axis=0, keepdims=True)
                cand = jnp.where(pos == px, -jnp.inf, cand)
                sel_a.append(jnp.floor(px * (1.0 / PEER_TOPK)))
                if r == 0:
                    best0, z = m, jnp.ones_like(m)
                else:
                    z = z + jnp.exp(m - best0)
            counts = []
            for a in range(PEER_TOPK):
                n_a = jnp.zeros_like(z)
                for r in range(a, PEER_TOPK):
                    n_a = n_a + jnp.where(sel_a[r] == float(a), 1.0, 0.0)
                counts.append(n_a)
        else:
            z, counts, picked = _staircase_select(cand, jrow)
        if exact:
            n_e1 = jnp.zeros((PEER_NKEYS, lanes), F32)
            for a in range(PEER_TOPK):
                n_e1 = jnp.where(rank1 == float(a), counts[a], n_e1)
        else:
            c8 = counts[8]
            for a in range(9, PEER_TOPK):
                c8 = c8 + counts[a]
            floor8 = v1[7]
            for k in range(1, 9):
                floor8 = jnp.where(c8 == float(k), v1[7 + k], floor8)
            n_e1 = jnp.where(s1 >= floor8, 1.0, 0.0) - jnp.where(s1 >= v1[7], 1.0, 0.0)
            for a in range(8):
                n_e1 = jnp.where(s1 == v1[a], counts[a], n_e1)
        n_ref[0, hd, :, at:at + lanes] = n_e1
        p1_ref[0, hd, :, at:at + lanes] = jnp.exp(s1 - v1[0]) / z
        r2_ref[0, hd, :, at:at + lanes] = rank2.astype(BF16)
        p2_ref[0, hd, :, at:at + lanes] = jnp.exp(s2 - v2[0]).astype(BF16)
        if exact:
            return None
        total = (jnp.sum(jnp.where(s1 >= v1[PEER_TOPK - 1], 1.0, 0.0), axis=0, keepdims=True)
                 + jnp.sum(jnp.where(rank2 < float(PEER_TOPK), 1.0, 0.0), axis=0, keepdims=True) + picked)
        for r in range(PEER_TOPK - 1):
            total = total + jnp.where(v1[r] <= v1[r + 1], 1.0, 0.0)
        return jnp.max(jnp.abs(total - 3.0 * PEER_TOPK))

    def scores(hd):
        return (_dot_nt(keys_ref[2 * hd], q[:, 256 * hd:256 * hd + 128].astype(BF16)),
                _dot_nt(keys_ref[2 * hd + 1], q[:, 256 * hd + 128:256 * hd + 256].astype(BF16)))

    miscounts = []
    for hd in range(PEER_HEADS):
        s1, s2 = scores(hd)
        miscounts.append([head(hd, s1, s2, False, at) for at in range(0, tl, lanes)])
    for hd in range(PEER_HEADS):
        for at, miscount in zip(range(0, tl, lanes), miscounts[hd]):
            @pl.when(miscount > 0.0)
            def _(hd=hd, at=at):
                s1, s2 = scores(hd)
                head(hd, s1, s2, True, at)


def _peer_topk(x, mod, gain, w_q, keys, per_batch_mod):
    b, l, d = x.shape
    tl = TOKEN_TILE
    mod_map = (lambda bi, i: (bi, 0, 0)) if per_batch_mod else (lambda bi, i: (0, 0, 0))
    sel = pl.BlockSpec((1, PEER_HEADS, PEER_NKEYS, tl), lambda bi, i: (bi, 0, 0, i))
    sel_f32 = jax.ShapeDtypeStruct((b, PEER_HEADS, PEER_NKEYS, l), F32)
    sel_bf16 = jax.ShapeDtypeStruct((b, PEER_HEADS, PEER_NKEYS, l), BF16)
    return pl.pallas_call(
        _peer_topk_kernel,
        grid=(b, l // tl),
        in_specs=[pl.BlockSpec((1, tl, d), lambda bi, i: (bi, i, 0)),
                  pl.BlockSpec((1, 6, d), mod_map),
                  pl.BlockSpec((1, d), lambda bi, i: (0, 0)),
                  pl.BlockSpec(w_q.shape, lambda bi, i: (0, 0)),
                  pl.BlockSpec(keys.shape, lambda bi, i: (0, 0, 0))],
        out_specs=[pl.BlockSpec((1, tl, d), lambda bi, i: (bi, i, 0)), sel, sel, sel, sel],
        out_shape=[jax.ShapeDtypeStruct((b, l, d), BF16), sel_f32, sel_f32, sel_bf16, sel_bf16],
        compiler_params=_params("arbitrary", "arbitrary"),
        name="peer_topk",
    )(x, mod, gain, w_q, keys)


def _peer_dense_kernel(x_ref, mod_ref, h_ref, u_ref, vt_ref, n_ref, p1_ref, r2_ref, p2_ref, o_ref,
                       ht_ref, acc_ref, s_ref, gate_ref):
    e = pl.program_id(2)

    @pl.when(e == 0)
    def _():
        ht_ref[...] = h_ref[0].astype(F32).T.astype(BF16)
        acc_ref[...] = jnp.zeros_like(acc_ref)

    ht = ht_ref[...]
    tm = ht.shape[1]
    bounds = [sum(PEER_SUBS[:i]) for i in range(len(PEER_SUBS) + 1)]

    def gate_rows(lo, hi):
        for row in range(lo // PEER_NKEYS, hi // PEER_NKEYS):
            gate = [None] * (PEER_NKEYS // 16)
            for hd in range(PEER_HEADS):
                n_b = jnp.broadcast_to(n_ref[0, hd, row:row + 1, :], (16, tm)).astype(BF16)
                p1_b = jnp.broadcast_to(p1_ref[0, hd, row:row + 1, :], (16, tm)).astype(BF16)
                for k in range(PEER_NKEYS // 16):
                    term = jnp.where(r2_ref[0, hd, 16 * k:16 * (k + 1), :] < n_b,
                                     p2_ref[0, hd, 16 * k:16 * (k + 1), :], jnp.zeros((), BF16)) * p1_b
                    gate[k] = term if gate[k] is None else gate[k] + term
            for k in range(PEER_NKEYS // 16):
                at = PEER_NKEYS * row + 16 * k
                gate_ref[at:at + 16, :] = gate[k]

    for lo, hi in zip(bounds[:-1], bounds[1:]):
        s_ref[lo:hi, :] = _dot(u_ref[lo:hi, :], ht).astype(BF16)
        gate_rows(lo, hi)
    ws = [gate_ref[lo:hi, :] * _gelu(s_ref[lo:hi, :])
          for lo, hi in zip(bounds[:-1], bounds[1:])]
    acc_ref[...] += _dot(vt_ref[...], jnp.concatenate(ws, axis=0))

    @pl.when(e == pl.num_programs(2) - 1)
    def _():
        o_ref[0] = x_ref[0] + mod_ref[0][5:6] * acc_ref[...].T


def _peer_dense(x, mod, h, u_tabs, vt_tabs, layer, n_e1, p1n, rank2, p2, per_batch_mod):
    b, l, d = x.shape
    tm = PEER_TOKENS
    n_exp = u_tabs.shape[1]
    rows = PEER_STEP // PEER_NKEYS
    mod_map = (lambda bi, i, e: (bi, 0, 0)) if per_batch_mod else (lambda bi, i, e: (0, 0, 0))
    tok = lambda: pl.BlockSpec((1, tm, d), lambda bi, i, e: (bi, i, 0))
    by_row = pl.BlockSpec((1, PEER_HEADS, rows, tm), lambda bi, i, e: (bi, 0, e, i))
    by_key = pl.BlockSpec((1, PEER_HEADS, PEER_NKEYS, tm), lambda bi, i, e: (bi, 0, 0, i))
    return pl.pallas_call(
        _peer_dense_kernel,
        grid=(b, l // tm, n_exp // PEER_STEP),
        in_specs=[tok(), pl.BlockSpec((1, 6, d), mod_map), tok(),
                  pl.BlockSpec((None, PEER_STEP, d), lambda bi, i, e: (layer, e, 0)),
                  pl.BlockSpec((None, d, PEER_STEP), lambda bi, i, e: (layer, 0, e)),
                  by_row, by_row, by_key, by_key],
        out_specs=tok(),
        out_shape=jax.ShapeDtypeStruct((b, l, d), F32),
        scratch_shapes=[pltpu.VMEM((d, tm), BF16), pltpu.VMEM((d, tm), F32),
                        pltpu.VMEM((PEER_STEP, tm), BF16), pltpu.VMEM((PEER_STEP, tm), BF16)],
        compiler_params=_params("arbitrary", "arbitrary", "arbitrary"),
        name="peer_dense",
    )(x, mod, h, u_tabs, vt_tabs, n_e1, p1n, rank2, p2)


def _rope_tables(n_lat, width):
    q4 = DIFF_DK // 4
    n_rows = n_lat // GRID_W
    row = jnp.repeat(jnp.arange(n_rows), GRID_W).astype(F32)
    col = jnp.tile(jnp.arange(GRID_W), n_rows).astype(F32)
    freqs = ROPE_BASE ** (-jnp.arange(q4, dtype=F32) / q4)
    ang = jnp.stack([row[:, None] * freqs, col[:, None] * freqs], axis=1)
    cos, sin = jnp.cos(ang), jnp.sin(ang)
    cos64 = jnp.stack([cos, cos], axis=2).reshape(n_lat, 64)
    sin64 = jnp.stack([-sin, sin], axis=2).reshape(n_lat, 64)
    reps = width // 64
    return jnp.tile(cos64, (1, reps)), jnp.tile(sin64, (1, reps))


def _ones_block_diag(width, group):
    idx = jnp.arange(width) // group
    return jnp.where(idx[:, None] == idx[None, :], 1.0 / group, 0.0).astype(BF16)


def _peer_layer(x, mod, W, l, per_batch_mod):
    shape = x.shape
    if not per_batch_mod:
        x = x.reshape(1, -1, shape[-1])
    h, n_e1, p1n, rank2, p2 = _peer_topk(x, mod, W['norm_ffn'][l], W['peer_w_q'][l], W['peer_keys'][l],
                                         per_batch_mod)
    y = _peer_dense(x, mod, h, W['peer_u'], W['peer_vt'], l, n_e1, p1n, rank2, p2, per_batch_mod)
    return y.reshape(shape)


def _even_layer(x, mod, W, l, cache, per_batch_mod):
    b, seq, _ = x.shape
    i = l // 2
    lam_init = 0.8 - 0.6 * math.exp(-0.3 * l)
    rope_tabs = W['rope'] if cache is not None else None
    u_tm2, q, k_norm, k_att, v, v_att = _inproj_even(
        x, mod, W['norm_mix'][l], W['even_w_in'][i], W['ones_bd'], W['diff_q_norm'][i],
        W['diff_k_norm'][i], rope_tabs, per_batch_mod)
    if cache is None:
        h0 = jnp.zeros((2, b, S5_JB * S5_BLK), F32)
        k_all, v_all = k_att, v_att
    else:
        h0 = _s5_state_to_blocks(cache['s5'][:, i])
        k_all = jnp.concatenate([cache['diff_k'][:, i].reshape(b, -1, 512).astype(BF16), k_att], axis=1)
        v_all = jnp.concatenate([cache['diff_v'][:, i].reshape(b, -1, 512).astype(BF16), v_att], axis=1)
    yf, yb, fin = _s5_scan(u_tm2.reshape(seq, b, 512), W['s5_bd'][i], W['s5_cd'][i], W['s5_a'][i], h0)
    o_attn = _diff_attention(q, k_all, v_all, W['diff_lambda'][i], lam_init)
    x = _mixout_even(x, mod, u_tm2, yf.reshape(seq, b * 512), yb.reshape(seq, b * 512), o_attn,
                     W['s5_d'][i], W['s5_w_glu'][i], W['diff_sub_norm'][i], W['even_w_out'][i],
                     1.0 - lam_init, per_batch_mod)
    return x, (k_norm, v, fin)


def _odd_layer(x, mod, W, l, cache, per_batch_mod):
    b, seq, _ = x.shape
    i = l // 2
    rope_tabs = W['rope'] if cache is not None else None
    q, k_norm, k_att, v, v_att, u = _inproj_odd(
        x, mod, W['norm_mix'][l], W['odd_w_in'][i], W['ones_bd'], W['win_q_norm'][i],
        W['win_k_norm'][i], rope_tabs, per_batch_mod)
    if cache is None:
        o_c = _win_attention(q, k_att, v_att, W['win_sink'][i])
    else:
        pad = ((0, 0), (WINDOW, WINDOW), (0, 0))
        o_c = _win_attention(q, cache['win_k'][:, i].reshape(b, -1, 128).astype(BF16),
                             cache['win_v'][:, i].reshape(b, -1, 128).astype(BF16), W['win_sink'][i],
                             jnp.pad(k_att, pad), jnp.pad(v_att, pad))
    o_d = _pool_mixer(u, W['pool_w'][i], W['pool_scale'][i])
    x = _mixout_odd(x, mod, o_c, o_d, W['odd_w_out'][i], per_batch_mod)
    return x, (k_norm, v)


def kernel(x_prompt, x_sample, cache_diff_k, cache_diff_v, state_s5, cache_win_k, cache_win_v, c, c_ctx, ada_w, ada_b, norm_mix, norm_ffn, even_w_in, even_w_out, s5_lam_re, s5_lam_im, s5_log_step, s5_b_re, s5_b_im, s5_c_re, s5_c_im, s5_d, s5_w_glu, diff_q_norm, diff_k_norm, diff_lambda, diff_sub_norm, odd_w_in, odd_w_out, win_q_norm, win_k_norm, win_sink, pool_w, pool_scale, peer_w_q, peer_sub_keys, peer_u, peer_v):
    depth = ada_w.shape[0]
    n_even, n_odd = even_w_in.shape[0], odd_w_in.shape[0]
    bsz, seq, d = x_prompt.shape
    dec_b, dec_l, _ = x_sample.shape

    cvec = jnp.zeros((16, d), F32).at[0].set(c_ctx).at[1:1 + dec_b].set(c)
    mod = _modulation(cvec, ada_w, ada_b)

    s5 = [_s5_tables(s5_lam_re[i], s5_lam_im[i], s5_log_step[i], s5_b_re[i], s5_b_im[i],
                     s5_c_re[i], s5_c_im[i]) for i in range(n_even)]
    W = dict(
        norm_mix=norm_mix.reshape(depth, 1, d), norm_ffn=norm_ffn.reshape(depth, 1, d),
        even_w_in=even_w_in.astype(BF16), even_w_out=even_w_out.astype(BF16),
        odd_w_in=odd_w_in.astype(BF16), odd_w_out=odd_w_out.astype(BF16),
        ones_bd=_ones_block_diag(512, 64),
        diff_q_norm=jnp.tile(diff_q_norm, (1, 8)).reshape(n_even, 1, 512),
        diff_k_norm=jnp.tile(diff_k_norm, (1, 8)).reshape(n_even, 1, 512),
        diff_lambda=diff_lambda, diff_sub_norm=diff_sub_norm.reshape(n_even, 1, DIFF_DV),
        s5_bd=[t[0] for t in s5], s5_cd=[t[1] for t in s5], s5_a=[t[2] for t in s5],
        s5_d=s5_d.reshape(n_even, 1, 512), s5_w_glu=s5_w_glu.astype(BF16),
        win_q_norm=jnp.tile(win_q_norm, (1, 8)).reshape(n_odd, 1, 512),
        win_k_norm=jnp.tile(win_k_norm, (1, 2)).reshape(n_odd, 1, 128),
        win_sink=jnp.broadcast_to(win_sink[:, :, None], (n_odd, 8, 128)),
        pool_w=pool_w.astype(BF16), pool_scale=pool_scale.reshape(n_odd, 1, 512),
        peer_w_q=peer_w_q.astype(BF16),
        peer_keys=peer_sub_keys.astype(BF16).reshape(depth, 2 * PEER_HEADS, PEER_NKEYS, -1),
        peer_u=peer_u.astype(BF16),
        peer_vt=jnp.swapaxes(peer_v, 1, 2).astype(BF16),
        rope=_rope_tables(dec_l, 512),
    )
    cache = {'diff_k': cache_diff_k, 'diff_v': cache_diff_v, 's5': state_s5,
             'win_k': cache_win_k, 'win_v': cache_win_v}

    def run(x, mods, cch, per_batch_mod):
        states = []
        for l in range(depth):
            if l % 2 == 0:
                x, st = _even_layer(x, mods[l], W, l, cch, per_batch_mod)
            else:
                x, st = _odd_layer(x, mods[l], W, l, cch, per_batch_mod)
            states.append(st)
            x = _peer_layer(x, mods[l], W, l, per_batch_mod)
        return x, states

    ctx_mods = [mod[l, 0:1].reshape(1, 6, d) for l in range(depth)]
    dec_mods = [mod[l, 1:1 + dec_b].reshape(dec_b, 6, d) for l in range(depth)]
    y_prompt, states = run(x_prompt, ctx_mods, None, False)
    y_sample, _ = run(x_sample, dec_mods, cache, True)

    evens = [states[l] for l in range(depth) if l % 2 == 0]
    odds = [states[l] for l in range(depth) if l % 2 == 1]
    new_diff_k = jnp.stack([s[0].reshape(bsz, seq, 2, DIFF_HEADS, DIFF_DK) for s in evens], axis=1)
    new_diff_v = jnp.stack([s[1].reshape(bsz, seq, DIFF_HEADS, DIFF_DV) for s in evens], axis=1)
    new_s5 = jnp.stack([_s5_blocks_to_state(s[2]) for s in evens], axis=1)
    new_win_k = jnp.stack([s[0].reshape(bsz, seq, WIN_KV_HEADS, WIN_DH) for s in odds], axis=1)
    new_win_v = jnp.stack([s[1].reshape(bsz, seq, WIN_KV_HEADS, WIN_DH) for s in odds], axis=1)
    return (y_prompt, y_sample, new_diff_k, new_diff_v, new_s5, new_win_k, new_win_v)
```

```python
import functools
import math

import jax
import jax.numpy as jnp
from jax import lax
from jax.experimental import pallas as pl
from jax.experimental.pallas import tpu as pltpu

F32 = jnp.float32
BF16 = jnp.bfloat16

D_MODEL = 1024
MIX_HALF = 512
GRID_W = 64
EPS = 1e-6
NEG_INF = -1e30
ROPE_BASE = 10000.0

S5_GROUPS = 32
S5_GROUP = 16
S5_STATE = 64
S5_JB = 4
S5_BLK = 1024
S5_CHUNK = 32
S5_BATCH = 8

DIFF_HEADS = 4
DIFF_DK = 64
DIFF_DV = 128
WIN_KV_HEADS = 2
WIN_GROUP = 4
WIN_DH = 64
ATTN_SCALE = DIFF_DK ** -0.5
WINDOW = 128
POOL_WINDOWS = (2, 4, 8, 16)
POOL_PAD = 8

PEER_HEADS = 8
PEER_NKEYS = 128
PEER_TOPK = 16
RANK_CODE = 2.0 ** 100
PEER_SUBS = (128, 384, 512, 1024)
PEER_STEP = sum(PEER_SUBS)
PEER_TOKENS = 512

TOKEN_TILE = 256
WIDE_TILE = 512
VMEM_LIMIT = 56 * 1024 * 1024


def _params(*sem):
    return pltpu.CompilerParams(dimension_semantics=sem, vmem_limit_bytes=VMEM_LIMIT)


def _dot(a, b):
    return jnp.dot(a, b, preferred_element_type=F32)


def _dot_nt(a, b):
    return lax.dot_general(a, b, (((1,), (1,)), ((), ())), preferred_element_type=F32)


def _sigmoid(x):
    return 1.0 / (1.0 + jnp.exp(-x))


def _gelu(x):
    c = math.sqrt(2.0 / math.pi)
    hx = 0.5 * x
    return hx + hx * jnp.tanh(x * (c + (c * 0.044715) * (x * x)))


def _modulate(x, gain, shift, scale):
    ms = jnp.mean(x * x, axis=-1, keepdims=True)
    return (x * lax.rsqrt(ms + EPS) * gain) * (1.0 + scale) + shift


def _group_rms(z, ones_bd, gain):
    ms = _dot((z * z).astype(BF16), ones_bd)
    return z * lax.rsqrt(ms + EPS) * gain


def _rope(z, cos_t, sin_t):
    lane = lax.broadcasted_iota(jnp.int32, (1, 128), 1) % 32
    first = lane < 16
    parts = []
    for j in range(z.shape[1] // 128):
        c = z[:, 128 * j:128 * (j + 1)]
        parts.append(jnp.where(first, pltpu.roll(c, 112, axis=1), pltpu.roll(c, 16, axis=1)))
    swapped = parts[0] if len(parts) == 1 else jnp.concatenate(parts, axis=1)
    return z * cos_t + swapped * sin_t


def _mod_kernel(c_ref, w_ref, b_ref, o_ref):
    c = c_ref[...]
    s = c * _sigmoid(c)
    o_ref[0] = jnp.dot(s, w_ref[0], precision=lax.Precision.HIGHEST,
                       preferred_element_type=F32) + b_ref[0]


def _modulation(cvec, ada_w, ada_b):
    depth, d, n = ada_w.shape
    tn = 1536
    return pl.pallas_call(
        _mod_kernel,
        grid=(depth, n // tn),
        in_specs=[pl.BlockSpec((16, d), lambda l, j: (0, 0)),
                  pl.BlockSpec((1, d, tn), lambda l, j: (l, 0, j)),
                  pl.BlockSpec((1, 1, tn), lambda l, j: (l, 0, j))],
        out_specs=pl.BlockSpec((1, 16, tn), lambda l, j: (l, 0, j)),
        out_shape=jax.ShapeDtypeStruct((depth, 16, n), F32),
        compiler_params=_params("arbitrary", "arbitrary"),
        name="adaln_mod",
    )(cvec, ada_w, ada_b.reshape(depth, 1, n))


def _inproj_even_kernel(rope, *refs):
    if rope:
        (x_ref, mod_ref, g_ref, w_ref, bd_ref, gq_ref, gk_ref, cos_ref, sin_ref,
         u_ref, q_ref, kn_ref, ka_ref, v_ref, va_ref) = refs
    else:
        (x_ref, mod_ref, g_ref, w_ref, bd_ref, gq_ref, gk_ref,
         u_ref, q_ref, kn_ref, ka_ref, v_ref, va_ref) = refs
    mod = mod_ref[0]
    h = _modulate(x_ref[0], g_ref[...], mod[0:1], mod[1:2])
    z = _dot(h.astype(BF16), w_ref[...])
    u = z[:, 0:512]
    q = _group_rms(z[:, 512:1024], bd_ref[...], gq_ref[...])
    k = _group_rms(z[:, 1024:1536], bd_ref[...], gk_ref[...])
    v = z[:, 1536:2048]
    u_ref[...] = u
    kn_ref[0] = k
    v_ref[0] = v
    va_ref[0] = v.astype(BF16)
    if rope:
        q = _rope(q, cos_ref[...], sin_ref[...])
        k = _rope(k, cos_ref[...], sin_ref[...])
    q_ref[0] = (q * ATTN_SCALE).astype(BF16)
    ka_ref[0] = k.astype(BF16)


def _inproj_even(x, mod, gain, w, ones_bd, gq, gk, rope_tabs, per_batch_mod):
    b, l, d = x.shape
    tl = min(l, WIDE_TILE)
    rope = rope_tabs is not None
    mod_map = (lambda bi, i: (bi, 0, 0)) if per_batch_mod else (lambda bi, i: (0, 0, 0))
    const2 = lambda bi, i: (0, 0)
    tok = lambda w_: pl.BlockSpec((1, tl, w_), lambda bi, i: (bi, i, 0))
    in_specs = [tok(d),
                pl.BlockSpec((1, 6, d), mod_map),
                pl.BlockSpec((1, d), const2),
                pl.BlockSpec(w.shape, const2),
                pl.BlockSpec(ones_bd.shape, const2),
                pl.BlockSpec((1, 512), const2),
                pl.BlockSpec((1, 512), const2)]
    args = [x, mod, gain, w, ones_bd, gq, gk]
    if rope:
        in_specs += [pl.BlockSpec((tl, 512), lambda bi, i: (i, 0))] * 2
        args += list(rope_tabs)
    out_shape = [jax.ShapeDtypeStruct((l, b * 512), F32),
                 jax.ShapeDtypeStruct((b, l, 512), BF16),
                 jax.ShapeDtypeStruct((b, l, 512), F32),
                 jax.ShapeDtypeStruct((b, l, 512), BF16),
                 jax.ShapeDtypeStruct((b, l, 512), F32),
                 jax.ShapeDtypeStruct((b, l, 512), BF16)]
    out_specs = [pl.BlockSpec((tl, 512), lambda bi, i: (i, bi)),
                 tok(512), tok(512), tok(512), tok(512), tok(512)]
    return pl.pallas_call(
        functools.partial(_inproj_even_kernel, rope),
        grid=(b, l // tl), in_specs=in_specs, out_specs=out_specs, out_shape=out_shape,
        compiler_params=_params("arbitrary", "arbitrary"),
        name="inproj_even_rope" if rope else "inproj_even",
    )(*args)


def _inproj_odd_kernel(rope, *refs):
    if rope:
        (x_ref, mod_ref, g_ref, w_ref, bd_ref, gq_ref, gk_ref, cos_ref, sin_ref,
         q_ref, kn_ref, ka_ref, v_ref, va_ref, u_ref) = refs
    else:
        (x_ref, mod_ref, g_ref, w_ref, bd_ref, gq_ref, gk_ref,
         q_ref, kn_ref, ka_ref, v_ref, va_ref, u_ref) = refs
    mod = mod_ref[0]
    h = _modulate(x_ref[0], g_ref[...], mod[0:1], mod[1:2])
    z = _dot(h.astype(BF16), w_ref[...])
    bd = bd_ref[...]
    q = _group_rms(z[:, 0:512], bd, gq_ref[...])
    k = _group_rms(z[:, 512:640], bd[0:128, 0:128], gk_ref[...])
    v = z[:, 640:768]
    kn_ref[0] = k
    v_ref[0] = v
    va_ref[0] = v.astype(BF16)
    u_ref[0] = z[:, 768:1280]
    if rope:
        q = _rope(q, cos_ref[...], sin_ref[...])
        k = _rope(k, cos_ref[:, 0:128], sin_ref[:, 0:128])
    q_ref[0] = (q * ATTN_SCALE).astype(BF16)
    ka_ref[0] = k.astype(BF16)


def _inproj_odd(x, mod, gain, w, ones_bd, gq, gk, rope_tabs, per_batch_mod):
    b, l, d = x.shape
    tl = min(l, WIDE_TILE)
    rope = rope_tabs is not None
    mod_map = (lambda bi, i: (bi, 0, 0)) if per_batch_mod else (lambda bi, i: (0, 0, 0))
    const2 = lambda bi, i: (0, 0)
    tok = lambda w_: pl.BlockSpec((1, tl, w_), lambda bi, i: (bi, i, 0))
    in_specs = [tok(d),
                pl.BlockSpec((1, 6, d), mod_map),
                pl.BlockSpec((1, d), const2),
                pl.BlockSpec(w.shape, const2),
                pl.BlockSpec(ones_bd.shape, const2),
                pl.BlockSpec((1, 512), const2),
                pl.BlockSpec((1, 128), const2)]
    args = [x, mod, gain, w, ones_bd, gq, gk]
    if rope:
        in_specs += [pl.BlockSpec((tl, 512), lambda bi, i: (i, 0))] * 2
        args += list(rope_tabs)
    out_shape = [jax.ShapeDtypeStruct((b, l, 512), BF16),
                 jax.ShapeDtypeStruct((b, l, 128), F32),
                 jax.ShapeDtypeStruct((b, l, 128), BF16),
                 jax.ShapeDtypeStruct((b, l, 128), F32),
                 jax.ShapeDtypeStruct((b, l, 128), BF16),
                 jax.ShapeDtypeStruct((b, l, 512), F32)]
    out_specs = [tok(512), tok(128), tok(128), tok(128), tok(128), tok(512)]
    return pl.pallas_call(
        functools.partial(_inproj_odd_kernel, rope),
        grid=(b, l // tl), in_specs=in_specs, out_specs=out_specs, out_shape=out_shape,
        compiler_params=_params("arbitrary", "arbitrary"),
        name="inproj_odd_rope" if rope else "inproj_odd",
    )(*args)


def _s5_kernel(uf_ref, ub_ref, bd_ref, cd_ref, a_ref, h0_ref, yf_ref, yb_ref, fin_ref,
               bu_ref, carry_ref):
    c = pl.program_id(1)
    rows = S5_CHUNK * S5_BATCH

    @pl.when(c == 0)
    def _():
        carry_ref[...] = h0_ref[...]

    for d in range(2):
        u = (uf_ref if d == 0 else ub_ref)[...].reshape(rows, MIX_HALF).astype(BF16)
        for j in range(S5_JB):
            bu_ref[d, :, S5_BLK * j:S5_BLK * (j + 1)] = _dot(u[:, 128 * j:128 * (j + 1)], bd_ref[d, j])
        for j in range(S5_JB):
            lo = S5_BLK * j
            a_re = jnp.broadcast_to(a_ref[d, 0, j:j + 1, :], (S5_BATCH, 512))
            a_im = jnp.broadcast_to(a_ref[d, 1, j:j + 1, :], (S5_BATCH, 512))

            h_re, h_im = carry_ref[d, :, lo:lo + 512], carry_ref[d, :, lo + 512:lo + 1024]
            for s in range(S5_CHUNK):
                row = (s if d == 0 else S5_CHUNK - 1 - s) * S5_BATCH
                b_re = bu_ref[d, row:row + S5_BATCH, lo:lo + 512]
                b_im = bu_ref[d, row:row + S5_BATCH, lo + 512:lo + 1024]
                h_re, h_im = a_re * h_re - a_im * h_im + b_re, a_re * h_im + a_im * h_re + b_im
                bu_ref[d, row:row + S5_BATCH, lo:lo + 512] = h_re
                bu_ref[d, row:row + S5_BATCH, lo + 512:lo + 1024] = h_im
            carry_ref[d, :, lo:lo + 512] = h_re
            carry_ref[d, :, lo + 512:lo + 1024] = h_im
        ys = [_dot(bu_ref[d, :, S5_BLK * j:S5_BLK * (j + 1)].astype(BF16), cd_ref[d, j])
              for j in range(S5_JB)]
        y = jnp.concatenate(ys, axis=1).reshape(S5_CHUNK, S5_BATCH, MIX_HALF)
        if d == 0:
            yf_ref[...] = y
        else:
            yb_ref[...] = y

    @pl.when(c == pl.num_programs(1) - 1)
    def _():
        fin_ref[...] = carry_ref[...]


def _s5_scan(u_tm, bd, cd, acoef, h0):
    l, b, _ = u_tm.shape
    nc = l // S5_CHUNK
    ng = b // S5_BATCH
    state_w = S5_JB * S5_BLK
    blk = (S5_CHUNK, S5_BATCH, MIX_HALF)
    full = lambda a: pl.BlockSpec(a.shape, lambda g, c: (0,) * a.ndim)
    return pl.pallas_call(
        _s5_kernel,
        grid=(ng, nc),
        in_specs=[pl.BlockSpec(blk, lambda g, c: (c, g, 0)),
                  pl.BlockSpec(blk, lambda g, c: (nc - 1 - c, g, 0)),
                  full(bd), full(cd), full(acoef),
                  pl.BlockSpec((2, S5_BATCH, state_w), lambda g, c: (0, g, 0))],
        out_specs=[pl.BlockSpec(blk, lambda g, c: (c, g, 0)),
                   pl.BlockSpec(blk, lambda g, c: (nc - 1 - c, g, 0)),
                   pl.BlockSpec((2, S5_BATCH, state_w), lambda g, c: (0, g, 0))],
        out_shape=[jax.ShapeDtypeStruct((l, b, MIX_HALF), F32),
                   jax.ShapeDtypeStruct((l, b, MIX_HALF), F32),
                   jax.ShapeDtypeStruct((2, b, state_w), F32)],
        scratch_shapes=[pltpu.VMEM((2, S5_CHUNK * S5_BATCH, state_w), F32),
                        pltpu.VMEM((2, S5_BATCH, state_w), F32)],
        compiler_params=_params("arbitrary", "arbitrary"),
        name="s5_scan",
    )(u_tm, u_tm, bd, cd, acoef, h0)


def _s5_tables(lam_re, lam_im, log_step, b_re, b_im, c_re, c_im):
    step = jnp.exp(log_step.astype(F32))[..., None]
    lr, li = lam_re.astype(F32), lam_im.astype(F32)
    er = jnp.exp(lr * step)
    a_re, a_im = er * jnp.cos(li * step), er * jnp.sin(li * step)
    den = lr * lr + li * li
    q_re = ((a_re - 1.0) * lr + a_im * li) / den
    q_im = (a_im * lr - (a_re - 1.0) * li) / den
    bb_re = q_re[..., None] * b_re - q_im[..., None] * b_im
    bb_im = q_re[..., None] * b_im + q_im[..., None] * b_re
    eye = jnp.eye(8, dtype=F32)
    bb = jnp.stack([bb_re, bb_im], axis=1).reshape(2, 2, S5_JB, 8, S5_STATE, S5_GROUP)
    bd = jnp.einsum('drjgpc,gh->djgcrhp', bb, eye).reshape(2, S5_JB, 128, S5_BLK).astype(BF16)
    cc = jnp.stack([c_re, -c_im], axis=1).astype(F32).reshape(2, 2, S5_JB, 8, S5_GROUP, S5_STATE)
    cd = jnp.einsum('drjgcp,gh->djrgphc', cc, eye).reshape(2, S5_JB, S5_BLK, 128).astype(BF16)
    acoef = jnp.stack([a_re, a_im], axis=1).reshape(2, 2, S5_JB, 512)
    return bd, cd, acoef


def _s5_state_to_blocks(st):
    b = st.shape[0]
    st = st.astype(F32).reshape(b, 2, 2, S5_JB, 8, S5_STATE)
    return jnp.transpose(st, (1, 0, 3, 2, 4, 5)).reshape(2, b, S5_JB * S5_BLK)


def _s5_blocks_to_state(fin):
    b = fin.shape[1]
    fin = fin.reshape(2, b, S5_JB, 2, 8, S5_STATE)
    return jnp.transpose(fin, (1, 0, 3, 2, 4, 5)).reshape(b, 2, 2, S5_GROUPS, S5_STATE)


def _diff_attn_kernel(lam_init, q_ref, k_ref, v_ref, lv_ref, o_ref):
    lv = lv_ref[...]
    lam = (jnp.exp(jnp.sum(lv[0:1] * lv[1:2], axis=-1, keepdims=True))
           - jnp.exp(jnp.sum(lv[2:3] * lv[3:4], axis=-1, keepdims=True)) + lam_init)
    lane = lax.broadcasted_iota(jnp.int32, (1, 128), 1)
    def scores(h):
        out = []
        for m in range(2):
            j = m * 2 + h // 2
            keep = (lane < 64) if h % 2 == 0 else (lane >= 64)
            qb = jnp.where(keep, q_ref[0, :, 128 * j:128 * (j + 1)], jnp.zeros((), BF16))
            out.append(_dot_nt(qb, k_ref[0, :, 128 * j:128 * (j + 1)]))
        return out

    outs = []
    pending = [scores(0), scores(1)]
    for h in range(DIFF_HEADS):
        cur = pending.pop(0)
        es = [jnp.exp(s - jnp.max(s, axis=-1, keepdims=True)) for s in cur]
        sums = [jnp.sum(e, axis=-1, keepdims=True) for e in es]
        a = (es[0] - (lam * sums[0] / sums[1]) * es[1]).astype(BF16)
        if h + 2 < DIFF_HEADS:
            pending.append(scores(h + 2))
        outs.append(_dot(a, v_ref[0, :, 128 * h:128 * (h + 1)]) / sums[0])
    o_ref[0] = jnp.concatenate(outs, axis=1)


def _diff_attention(q, k, v, lv, lam_init):
    b, lq, _ = q.shape
    lk = k.shape[1]
    tq = TOKEN_TILE
    return pl.pallas_call(
        functools.partial(_diff_attn_kernel, lam_init),
        grid=(b, lq // tq),
        in_specs=[pl.BlockSpec((1, tq, 512), lambda bi, i: (bi, i, 0)),
                  pl.BlockSpec((1, lk, 512), lambda bi, i: (bi, 0, 0)),
                  pl.BlockSpec((1, lk, 512), lambda bi, i: (bi, 0, 0)),
                  pl.BlockSpec((4, DIFF_DK), lambda bi, i: (0, 0))],
        out_specs=pl.BlockSpec((1, tq, 512), lambda bi, i: (bi, i, 0)),
        out_shape=jax.ShapeDtypeStruct((b, lq, 512), F32),
        compiler_params=_params("arbitrary", "arbitrary"),
        name="diff_attention",
    )(q, k, v, lv)


def _win_attn_kernel(local, seq_len, *refs):
    if local:
        q_ref, kc_ref, vc_ref, kl_ref, vl_ref, sink_ref, o_ref = refs
    else:
        q_ref, kc_ref, vc_ref, sink_ref, o_ref = refs
    tq = q_ref.shape[1]
    lane = lax.broadcasted_iota(jnp.int32, (1, 128), 1)
    low = lane < 64
    kc = kc_ref[0]
    vc = vc_ref[0]
    n_ctx = kc.shape[0]
    if local:
        span = tq + 2 * WINDOW
        start = pl.multiple_of(pl.program_id(1) * tq, tq)
        kl = kl_ref[0, pl.ds(start, span), :]
        vl = vl_ref[0, pl.ds(start, span), :]
        qpos = start + lax.broadcasted_iota(jnp.int32, (tq, span), 0)
        kpos = start - WINDOW + lax.broadcasted_iota(jnp.int32, (tq, span), 1)
        ok = (jnp.abs(qpos - kpos) <= WINDOW) & (kpos >= 0) & (kpos < seq_len)
    def scores(hd):
        n = hd // WIN_GROUP
        j, half = hd // 2, hd % 2
        qb = q_ref[0, :, 128 * j:128 * (j + 1)]
        if half != n:
            qb = jnp.concatenate([qb[:, 64:128], qb[:, 0:64]], axis=1)
        qb = jnp.where(low if n == 0 else jnp.logical_not(low), qb, jnp.zeros((), BF16))
        s = _dot_nt(qb, kc)
        if local:
            s = jnp.concatenate([s, jnp.where(ok, _dot_nt(qb, kl), NEG_INF)], axis=1)
        return s

    heads = {}
    n_heads = WIN_KV_HEADS * WIN_GROUP
    nxt = scores(0)
    for hd in range(n_heads):
        s = nxt
        if hd + 1 < n_heads:
            nxt = scores(hd + 1)
        sink = sink_ref[hd:hd + 1, 0:1]
        mx = jnp.maximum(jnp.max(s, axis=-1, keepdims=True), sink)
        e = jnp.exp(s - mx)
        den = jnp.sum(e, axis=-1, keepdims=True) + jnp.exp(sink - mx)
        p = e.astype(BF16)
        o = _dot(p[:, 0:n_ctx], vc)
        if local:
            o = o + _dot(p[:, n_ctx:], vl)
        heads[hd] = o / den
    blocks = []
    for j in range(4):
        ev, od = heads[2 * j], heads[2 * j + 1]
        n = (2 * j) // WIN_GROUP
        if n == 1:
            ev = pltpu.roll(ev, 64, axis=1)
        else:
            od = pltpu.roll(od, 64, axis=1)
        blocks.append(jnp.where(low, ev, od))
    o_ref[0] = jnp.concatenate(blocks, axis=1).astype(BF16)


def _win_attention(q, kc, vc, sink, k_lat=None, v_lat=None):
    b, lq, _ = q.shape
    n_ctx = kc.shape[1]
    local = k_lat is not None
    tq = TOKEN_TILE
    in_specs = [pl.BlockSpec((1, tq, 512), lambda bi, i: (bi, i, 0)),
                pl.BlockSpec((1, n_ctx, 128), lambda bi, i: (bi, 0, 0)),
                pl.BlockSpec((1, n_ctx, 128), lambda bi, i: (bi, 0, 0))]
    args = [q, kc, vc]
    if local:
        lp = k_lat.shape[1]
        in_specs += [pl.BlockSpec((1, lp, 128), lambda bi, i: (bi, 0, 0))] * 2
        args += [k_lat, v_lat]
    in_specs.append(pl.BlockSpec((8, 128), lambda bi, i: (0, 0)))
    args.append(sink)
    return pl.pallas_call(
        functools.partial(_win_attn_kernel, local, lq),
        grid=(b, lq // tq), in_specs=in_specs,
        out_specs=pl.BlockSpec((1, tq, 512), lambda bi, i: (bi, i, 0)),
        out_shape=jax.ShapeDtypeStruct((b, lq, 512), BF16),
        compiler_params=_params("arbitrary", "arbitrary"),
        name="win_attention_local" if local else "win_attention",
    )(*args)


def _pool_kernel(u_ref, w_ref, scale_ref, o_ref):
    l = u_ref.shape[1]
    lp = l + 2 * POOL_PAD
    t = lax.broadcasted_iota(jnp.int32, (l, 128), 0)
    zpad = jnp.zeros((POOL_PAD, 128), F32)

    def shifted(a, s):
        return pltpu.roll(a, s, axis=0) + pltpu.roll(a, lp - s, axis=0)

    outs = []
    for gi, wl in enumerate(POOL_WINDOWS):
        x = u_ref[0, :, 128 * gi:128 * (gi + 1)]
        xp = jnp.concatenate([zpad, x, zpad], axis=0)
        acc = xp + pltpu.roll(xp, 1, axis=0)
        if wl >= 4:
            acc = shifted(acc, 1)
        if wl >= 8:
            acc = shifted(acc, 2)
        if wl >= 16:
            acc = shifted(acc, 4)
        win = acc[POOL_PAD:POOL_PAD + l]
        cnt = (jnp.minimum(t + wl // 2, l) - jnp.maximum(t - wl // 2, 0)).astype(F32)
        pooled = win / cnt - x
        outs.append(_dot(pooled.astype(BF16), w_ref[gi]))
    o_ref[0] = (jnp.concatenate(outs, axis=1) * scale_ref[...]).astype(BF16)


def _pool_mixer(u, w, scale):
    b, l, _ = u.shape
    return pl.pallas_call(
        _pool_kernel,
        grid=(b,),
        in_specs=[pl.BlockSpec((1, l, 512), lambda bi: (bi, 0, 0)),
                  pl.BlockSpec(w.shape, lambda bi: (0, 0, 0)),
                  pl.BlockSpec((1, 512), lambda bi: (0, 0))],
        out_specs=pl.BlockSpec((1, l, 512), lambda bi: (bi, 0, 0)),
        out_shape=jax.ShapeDtypeStruct((b, l, 512), BF16),
        compiler_params=_params("arbitrary"),
        name="pool_mixer",
    )(u, w, scale)


def _mixout_even_kernel(out_scale, x_ref, mod_ref, u_ref, yf_ref, yb_ref, oa_ref, sd_ref,
                        wglu_ref, gsub_ref, wout_ref, o_ref):
    mod = mod_ref[0]
    y = sd_ref[...] * u_ref[...] + yf_ref[...] + yb_ref[...]
    g = _gelu(y)
    o_a = g * _sigmoid(_dot(g.astype(BF16), wglu_ref[...]))
    parts = [o_a.astype(BF16)]
    for h in range(DIFF_HEADS):
        blk = oa_ref[0, :, 128 * h:128 * (h + 1)]
        ms = jnp.mean(blk * blk, axis=-1, keepdims=True)
        parts.append(((blk * lax.rsqrt(ms + EPS) * gsub_ref[...]) * out_scale).astype(BF16))
    cat = jnp.concatenate(parts, axis=1)
    o_ref[0] = x_ref[0] + mod[2:3] * _dot(cat, wout_ref[...])


def _mixout_even(x, mod, u_tm2, yf2, yb2, o_attn, s5_d, w_glu, g_sub, w_out, out_scale, per_batch_mod):
    b, l, d = x.shape
    tl = min(l, WIDE_TILE)
    mod_map = (lambda bi, i: (bi, 0, 0)) if per_batch_mod else (lambda bi, i: (0, 0, 0))
    const2 = lambda bi, i: (0, 0)
    tm = pl.BlockSpec((tl, 512), lambda bi, i: (i, bi))
    return pl.pallas_call(
        functools.partial(_mixout_even_kernel, out_scale),
        grid=(b, l // tl),
        in_specs=[pl.BlockSpec((1, tl, d), lambda bi, i: (bi, i, 0)),
                  pl.BlockSpec((1, 6, d), mod_map),
                  tm, tm, tm,
                  pl.BlockSpec((1, tl, 512), lambda bi, i: (bi, i, 0)),
                  pl.BlockSpec((1, 512), const2),
                  pl.BlockSpec((512, 512), const2),
                  pl.BlockSpec((1, 128), const2),
                  pl.BlockSpec((d, d), const2)],
        out_specs=pl.BlockSpec((1, tl, d), lambda bi, i: (bi, i, 0)),
        out_shape=jax.ShapeDtypeStruct((b, l, d), F32),
        compiler_params=_params("arbitrary", "arbitrary"),
        name="mixout_even",
    )(x, mod, u_tm2, yf2, yb2, o_attn, s5_d, w_glu, g_sub, w_out)


def _mixout_odd_kernel(x_ref, mod_ref, oc_ref, od_ref, wout_ref, o_ref):
    mod = mod_ref[0]
    cat = jnp.concatenate([oc_ref[0], od_ref[0]], axis=1)
    o_ref[0] = x_ref[0] + mod[2:3] * _dot(cat, wout_ref[...])


def _mixout_odd(x, mod, o_c, o_d, w_out, per_batch_mod):
    b, l, d = x.shape
    tl = min(l, WIDE_TILE)
    mod_map = (lambda bi, i: (bi, 0, 0)) if per_batch_mod else (lambda bi, i: (0, 0, 0))
    tok = lambda w_: pl.BlockSpec((1, tl, w_), lambda bi, i: (bi, i, 0))
    return pl.pallas_call(
        _mixout_odd_kernel,
        grid=(b, l // tl),
        in_specs=[tok(d), pl.BlockSpec((1, 6, d), mod_map), tok(512), tok(512),
                  pl.BlockSpec((d, d), lambda bi, i: (0, 0))],
        out_specs=tok(d),
        out_shape=jax.ShapeDtypeStruct((b, l, d), F32),
        compiler_params=_params("arbitrary", "arbitrary"),
        name="mixout_odd",
    )(x, mod, o_c, o_d, w_out)


def _rank16(s, kio, exact):
    n = s.shape[0]
    vals = []
    if exact:
        rank = jnp.full(s.shape, float(PEER_TOPK), F32)
        for r in range(PEER_TOPK):
            m = jnp.max(s, axis=0, keepdims=True)
            ix = jnp.min(jnp.where(s == m, kio, float(n)), axis=0, keepdims=True)
            hit = kio == ix
            rank = jnp.where(hit, float(r), rank)
            s = jnp.where(hit, -jnp.inf, s)
            vals.append(m)
        return vals, rank
    for r in range(PEER_TOPK):
        m = jnp.max(s, axis=0, keepdims=True)
        s = jnp.where(s == m, -RANK_CODE * (1.0 + r / 32.0), s)
        vals.append(m)
    code = s * (-1.0 / RANK_CODE)
    return vals, jnp.where(code >= 1.0, (code - 1.0) * 32.0, float(PEER_TOPK))


_SORT16 = ((0, 1), (2, 3), (0, 2), (1, 3), (1, 2), (4, 5), (6, 7), (4, 6), (5, 7), (5, 6), (0, 4), (2, 6), (2, 4),
           (1, 5), (3, 7), (3, 5), (1, 2), (3, 4), (5, 6), (8, 9), (10, 11), (8, 10), (9, 11), (9, 10), (12, 13),
           (14, 15), (12, 14), (13, 15), (13, 14), (8, 12), (10, 14), (10, 12), (9, 13), (11, 15), (11, 13), (9, 10),
           (11, 12), (13, 14), (0, 8), (4, 12), (4, 8), (2, 10), (6, 14), (6, 10), (2, 4), (6, 8), (10, 12), (1, 9),
           (5, 13), (5, 9), (3, 11), (7, 15), (7, 11), (3, 5), (7, 9), (11, 13), (1, 2), (3, 4), (5, 6), (7, 8),
           (9, 10), (11, 12), (13, 14))


def _sorted_top16(s):
    cols = [s[8 * i:8 * (i + 1), :] for i in range(16)]
    for i, j in _SORT16:
        cols[i], cols[j] = jnp.maximum(cols[i], cols[j]), jnp.minimum(cols[i], cols[j])
    vals = []
    for r in range(PEER_TOPK):
        m = cols[0]
        for shift in (4, 2, 1):
            m = jnp.maximum(m, pltpu.roll(m, shift, axis=0))
        vals.append(m[0:1])
        hit = cols[0] == m
        for i in range(PEER_TOPK - 1 - r):
            cols[i] = jnp.where(hit, cols[i + 1], cols[i])
    return vals


def _candidate_rows(v1, v2, j):
    v1hi = jnp.concatenate(v1[8:16], axis=0)
    v2lo = jnp.concatenate(v2[0:8], axis=0)
    v2hi = jnp.concatenate(v2[8:16], axis=0)
    g3_a = jnp.where(j < 5.0, v1[2], v1[4])
    g3_b = jnp.where(j < 5.0, v2lo, pltpu.roll(v2lo, 5, axis=0))
    g4_a = jnp.where(j < 4.0, v1[3], jnp.where(j < 6.0, v1[5], v1[6]))
    g4_b = jnp.where(j < 4.0, v2lo, jnp.where(j < 6.0, pltpu.roll(v2lo, 4, axis=0), pltpu.roll(v2lo, 6, axis=0)))
    return jnp.concatenate([v1[0] + v2lo, v1[0] + v2hi, v1[1] + v2lo, g3_a + g3_b, g4_a + g4_b,
                            jnp.where(j < 2.0, v1[7] + v2lo, -jnp.inf), v1hi + v2[0]], axis=0)


def _staircase_select(cand, j):
    rest = cand
    for r in range(PEER_TOPK):
        m = jnp.max(rest, axis=0, keepdims=True)
        if r == 0:
            best = m
        rest = jnp.where(rest == m, -jnp.inf, rest)
    picked = cand >= m
    z = jnp.sum(jnp.where(picked, jnp.exp(cand - best), 0.0), axis=0, keepdims=True)
    g = [jnp.where(picked[8 * i:8 * (i + 1)], 1.0, 0.0) for i in range(7)]
    tot = lambda v: jnp.sum(v, axis=0, keepdims=True)
    n = [None] * PEER_TOPK
    n[0], n[1], n[7] = tot(g[0] + g[1]), tot(g[2]), tot(g[5])
    n[2] = tot(jnp.where(j < 5.0, g[3], 0.0))
    n[4] = tot(g[3]) - n[2]
    n[3] = tot(jnp.where(j < 4.0, g[4], 0.0))
    n[5] = tot(jnp.where(j < 6.0, g[4], 0.0)) - n[3]
    n[6] = tot(g[4]) - n[3] - n[5]
    for k in range(8):
        n[8 + k] = g[6][k:k + 1]
    total = n[0]
    for a in range(1, PEER_TOPK):
        total = total + n[a]
    return z, n, total


def _candidate_positions(j):
    return jnp.concatenate([j, 8.0 + j, 16.0 + j,
                            jnp.where(j < 5.0, 32.0 + j, 59.0 + j),
                            jnp.where(j < 4.0, 48.0 + j, jnp.where(j < 6.0, 76.0 + j, 90.0 + j)),
                            jnp.where(j < 2.0, 112.0 + j, 999.0), 128.0 + 16.0 * j], axis=0)


def _peer_topk_kernel(x_ref, mod_ref, g_ref, wq_ref, keys_ref, h_ref, n_ref, p1_ref, r2_ref, p2_ref):
    mod = mod_ref[0]
    h = _modulate(x_ref[0], g_ref[...], mod[3:4], mod[4:5]).astype(BF16)
    h_ref[0] = h
    q = _dot(h, wq_ref[...])
    tl = q.shape[0]
    lanes = tl
    kio = lax.broadcasted_iota(jnp.int32, (PEER_NKEYS, lanes), 0).astype(F32)
    jrow = lax.broadcasted_iota(jnp.int32, (8, lanes), 0).astype(F32)
    pos = _candidate_positions(jrow)

    def head(hd, s1, s2, exact, at):
        s1, s2 = s1[:, at:at + lanes], s2[:, at:at + lanes]
        if exact:
            v1, rank1 = _rank16(s1, kio, True)
        else:
            v1 = _sorted_top16(s1)
        v2, rank2 = _rank16(s2, kio, exact)
        cand = _candidate_rows(v1, v2, jrow)
        if exact:
            sel_a = []
            for r in range(PEER_TOPK):
                m = jnp.max(cand, axis=0, keepdims=True)
                px = jnp.min(jnp.where(cand == m, pos, 999.0), axis=0, keepdims=True)
                cand = jnp.where(pos == px, -jnp.inf, cand)
                sel_a.append(jnp.floor(px * (1.0 / PEER_TOPK)))
                if r == 0:
                    best0, z = m, jnp.ones_like(m)
                else:
                    z = z + jnp.exp(m - best0)
            counts = []
            for a in range(PEER_TOPK):
                n_a = jnp.zeros_like(z)
                for r in range(a, PEER_TOPK):
                    n_a = n_a + jnp.where(sel_a[r] == float(a), 1.0, 0.0)
                counts.append(n_a)
        else:
            z, counts, picked = _staircase_select(cand, jrow)
        if exact:
            n_e1 = jnp.zeros((PEER_NKEYS, lanes), F32)
            for a in range(PEER_TOPK):
                n_e1 = jnp.where(rank1 == float(a), counts[a], n_e1)
        else:
            c8 = counts[8]
            for a in range(9, PEER_TOPK):
                c8 = c8 + counts[a]
            floor8 = v1[7]
            for k in range(1, 9):
                floor8 = jnp.where(c8 == float(k), v1[7 + k], floor8)
            n_e1 = jnp.where(s1 >= floor8, 1.0, 0.0) - jnp.where(s1 >= v1[7], 1.0, 0.0)
            for a in range(8):
                n_e1 = jnp.where(s1 == v1[a], counts[a], n_e1)
        n_ref[0, hd, :, at:at + lanes] = n_e1
        p1_ref[0, hd, :, at:at + lanes] = jnp.exp(s1 - v1[0]) * (1.0 / z)
        r2_ref[0, hd, :, at:at + lanes] = rank2.astype(BF16)
        p2_ref[0, hd, :, at:at + lanes] = jnp.exp(s2 - v2[0]).astype(BF16)
        if exact:
            return None
        total = (jnp.sum(jnp.where(s1 >= v1[PEER_TOPK - 1], 1.0, 0.0), axis=0, keepdims=True)
                 + jnp.sum(jnp.where(rank2 < float(PEER_TOPK), 1.0, 0.0), axis=0, keepdims=True) + picked)
        for r in range(PEER_TOPK - 1):
            total = total + jnp.where(v1[r] <= v1[r + 1], 1.0, 0.0)
        return jnp.max(jnp.abs(total - 3.0 * PEER_TOPK))

    def scores(hd):
        return (_dot_nt(keys_ref[2 * hd], q[:, 256 * hd:256 * hd + 128].astype(BF16)),
                _dot_nt(keys_ref[2 * hd + 1], q[:, 256 * hd + 128:256 * hd + 256].astype(BF16)))

    miscounts = []
    for hd in range(PEER_HEADS):
        s1, s2 = scores(hd)
        miscounts.append([head(hd, s1, s2, False, at) for at in range(0, tl, lanes)])
    for hd in range(PEER_HEADS):
        for at, miscount in zip(range(0, tl, lanes), miscounts[hd]):
            @pl.when(miscount > 0.0)
            def _(hd=hd, at=at):
                s1, s2 = scores(hd)
                head(hd, s1, s2, True, at)


def _peer_topk(x, mod, gain, w_q, keys, per_batch_mod):
    b, l, d = x.shape
    tl = TOKEN_TILE
    mod_map = (lambda bi, i: (bi, 0, 0)) if per_batch_mod else (lambda bi, i: (0, 0, 0))
    sel = pl.BlockSpec((1, PEER_HEADS, PEER_NKEYS, tl), lambda bi, i: (bi, 0, 0, i))
    sel_f32 = jax.ShapeDtypeStruct((b, PEER_HEADS, PEER_NKEYS, l), F32)
    sel_bf16 = jax.ShapeDtypeStruct((b, PEER_HEADS, PEER_NKEYS, l), BF16)
    return pl.pallas_call(
        _peer_topk_kernel,
        grid=(b, l // tl),
        in_specs=[pl.BlockSpec((1, tl, d), lambda bi, i: (bi, i, 0)),
                  pl.BlockSpec((1, 6, d), mod_map),
                  pl.BlockSpec((1, d), lambda bi, i: (0, 0)),
                  pl.BlockSpec(w_q.shape, lambda bi, i: (0, 0)),
                  pl.BlockSpec(keys.shape, lambda bi, i: (0, 0, 0))],
        out_specs=[pl.BlockSpec((1, tl, d), lambda bi, i: (bi, i, 0)), sel, sel, sel, sel],
        out_shape=[jax.ShapeDtypeStruct((b, l, d), BF16), sel_f32, sel_f32, sel_bf16, sel_bf16],
        compiler_params=_params("arbitrary", "arbitrary"),
        name="peer_topk",
    )(x, mod, gain, w_q, keys)


def _peer_dense_kernel(x_ref, mod_ref, h_ref, u_ref, vt_ref, n_ref, p1_ref, r2_ref, p2_ref, o_ref,
                       ht_ref, acc_ref, s_ref, gate_ref):
    e = pl.program_id(2)

    @pl.when(e == 0)
    def _():
        ht_ref[...] = h_ref[0].astype(F32).T.astype(BF16)
        acc_ref[...] = jnp.zeros_like(acc_ref)

    ht = ht_ref[...]
    tm = ht.shape[1]
    bounds = [sum(PEER_SUBS[:i]) for i in range(len(PEER_SUBS) + 1)]

    def gate_rows(lo, hi):
        for row in range(lo // PEER_NKEYS, hi // PEER_NKEYS):
            gate = [None] * (PEER_NKEYS // 16)
            for hd in range(PEER_HEADS):
                n_b = jnp.broadcast_to(n_ref[0, hd, row:row + 1, :], (16, tm)).astype(BF16)
                p1_b = jnp.broadcast_to(p1_ref[0, hd, row:row + 1, :], (16, tm)).astype(BF16)
                for k in range(PEER_NKEYS // 16):
                    term = jnp.where(r2_ref[0, hd, 16 * k:16 * (k + 1), :] < n_b,
                                     p2_ref[0, hd, 16 * k:16 * (k + 1), :], jnp.zeros((), BF16)) * p1_b
                    gate[k] = term if gate[k] is None else gate[k] + term
            for k in range(PEER_NKEYS // 16):
                at = PEER_NKEYS * row + 16 * k
                gate_ref[at:at + 16, :] = gate[k]

    for lo, hi in zip(bounds[:-1], bounds[1:]):
        s_ref[lo:hi, :] = _dot(u_ref[lo:hi, :], ht).astype(BF16)
        gate_rows(lo, hi)
    ws = [gate_ref[lo:hi, :] * _gelu(s_ref[lo:hi, :])
          for lo, hi in zip(bounds[:-1], bounds[1:])]
    acc_ref[...] += _dot(vt_ref[...], jnp.concatenate(ws, axis=0))

    @pl.when(e == pl.num_programs(2) - 1)
    def _():
        o_ref[0] = x_ref[0] + mod_ref[0][5:6] * acc_ref[...].T


def _peer_dense(x, mod, h, u_tabs, vt_tabs, layer, n_e1, p1n, rank2, p2, per_batch_mod):
    b, l, d = x.shape
    tm = PEER_TOKENS
    n_exp = u_tabs.shape[1]
    rows = PEER_STEP // PEER_NKEYS
    mod_map = (lambda bi, i, e: (bi, 0, 0)) if per_batch_mod else (lambda bi, i, e: (0, 0, 0))
    tok = lambda: pl.BlockSpec((1, tm, d), lambda bi, i, e: (bi, i, 0))
    by_row = pl.BlockSpec((1, PEER_HEADS, rows, tm), lambda bi, i, e: (bi, 0, e, i))
    by_key = pl.BlockSpec((1, PEER_HEADS, PEER_NKEYS, tm), lambda bi, i, e: (bi, 0, 0, i))
    return pl.pallas_call(
        _peer_dense_kernel,
        grid=(b, l // tm, n_exp // PEER_STEP),
        in_specs=[tok(), pl.BlockSpec((1, 6, d), mod_map), tok(),
                  pl.BlockSpec((None, PEER_STEP, d), lambda bi, i, e: (layer, e, 0)),
                  pl.BlockSpec((None, d, PEER_STEP), lambda bi, i, e: (layer, 0, e)),
                  by_row, by_row, by_key, by_key],
        out_specs=tok(),
        out_shape=jax.ShapeDtypeStruct((b, l, d), F32),
        scratch_shapes=[pltpu.VMEM((d, tm), BF16), pltpu.VMEM((d, tm), F32),
                        pltpu.VMEM((PEER_STEP, tm), BF16), pltpu.VMEM((PEER_STEP, tm), BF16)],
        compiler_params=_params("arbitrary", "arbitrary", "arbitrary"),
        name="peer_dense",
    )(x, mod, h, u_tabs, vt_tabs, n_e1, p1n, rank2, p2)


def _rope_tables(n_lat, width):
    q4 = DIFF_DK // 4
    n_rows = n_lat // GRID_W
    row = jnp.repeat(jnp.arange(n_rows), GRID_W).astype(F32)
    col = jnp.tile(jnp.arange(GRID_W), n_rows).astype(F32)
    freqs = ROPE_BASE ** (-jnp.arange(q4, dtype=F32) / q4)
    ang = jnp.stack([row[:, None] * freqs, col[:, None] * freqs], axis=1)
    cos, sin = jnp.cos(ang), jnp.sin(ang)
    cos64 = jnp.stack([cos, cos], axis=2).reshape(n_lat, 64)
    sin64 = jnp.stack([-sin, sin], axis=2).reshape(n_lat, 64)
    reps = width // 64
    return jnp.tile(cos64, (1, reps)), jnp.tile(sin64, (1, reps))


def _ones_block_diag(width, group):
    idx = jnp.arange(width) // group
    return jnp.where(idx[:, None] == idx[None, :], 1.0 / group, 0.0).astype(BF16)


def _peer_layer(x, mod, W, l, per_batch_mod):
    shape = x.shape
    if not per_batch_mod:
        x = x.reshape(1, -1, shape[-1])
    h, n_e1, p1n, rank2, p2 = _peer_topk(x, mod, W['norm_ffn'][l], W['peer_w_q'][l], W['peer_keys'][l],
                                         per_batch_mod)
    y = _peer_dense(x, mod, h, W['peer_u'], W['peer_vt'], l, n_e1, p1n, rank2, p2, per_batch_mod)
    return y.reshape(shape)


def _even_layer(x, mod, W, l, cache, per_batch_mod):
    b, seq, _ = x.shape
    i = l // 2
    lam_init = 0.8 - 0.6 * math.exp(-0.3 * l)
    rope_tabs = W['rope'] if cache is not None else None
    u_tm2, q, k_norm, k_att, v, v_att = _inproj_even(
        x, mod, W['norm_mix'][l], W['even_w_in'][i], W['ones_bd'], W['diff_q_norm'][i],
        W['diff_k_norm'][i], rope_tabs, per_batch_mod)
    if cache is None:
        h0 = jnp.zeros((2, b, S5_JB * S5_BLK), F32)
        k_all, v_all = k_att, v_att
    else:
        h0 = _s5_state_to_blocks(cache['s5'][:, i])
        k_all = jnp.concatenate([cache['diff_k'][:, i].reshape(b, -1, 512).astype(BF16), k_att], axis=1)
        v_all = jnp.concatenate([cache['diff_v'][:, i].reshape(b, -1, 512).astype(BF16), v_att], axis=1)
    yf, yb, fin = _s5_scan(u_tm2.reshape(seq, b, 512), W['s5_bd'][i], W['s5_cd'][i], W['s5_a'][i], h0)
    o_attn = _diff_attention(q, k_all, v_all, W['diff_lambda'][i], lam_init)
    x = _mixout_even(x, mod, u_tm2, yf.reshape(seq, b * 512), yb.reshape(seq, b * 512), o_attn,
                     W['s5_d'][i], W['s5_w_glu'][i], W['diff_sub_norm'][i], W['even_w_out'][i],
                     1.0 - lam_init, per_batch_mod)
    return x, (k_norm, v, fin)


def _odd_layer(x, mod, W, l, cache, per_batch_mod):
    b, seq, _ = x.shape
    i = l // 2
    rope_tabs = W['rope'] if cache is not None else None
    q, k_norm, k_att, v, v_att, u = _inproj_odd(
        x, mod, W['norm_mix'][l], W['odd_w_in'][i], W['ones_bd'], W['win_q_norm'][i],
        W['win_k_norm'][i], rope_tabs, per_batch_mod)
    if cache is None:
        o_c = _win_attention(q, k_att, v_att, W['win_sink'][i])
    else:
        pad = ((0, 0), (WINDOW, WINDOW), (0, 0))
        o_c = _win_attention(q, cache['win_k'][:, i].reshape(b, -1, 128).astype(BF16),
                             cache['win_v'][:, i].reshape(b, -1, 128).astype(BF16), W['win_sink'][i],
                             jnp.pad(k_att, pad), jnp.pad(v_att, pad))
    o_d = _pool_mixer(u, W['pool_w'][i], W['pool_scale'][i])
    x = _mixout_odd(x, mod, o_c, o_d, W['odd_w_out'][i], per_batch_mod)
    return x, (k_norm, v)


def kernel(x_prompt, x_sample, cache_diff_k, cache_diff_v, state_s5, cache_win_k, cache_win_v, c, c_ctx, ada_w, ada_b, norm_mix, norm_ffn, even_w_in, even_w_out, s5_lam_re, s5_lam_im, s5_log_step, s5_b_re, s5_b_im, s5_c_re, s5_c_im, s5_d, s5_w_glu, diff_q_norm, diff_k_norm, diff_lambda, diff_sub_norm, odd_w_in, odd_w_out, win_q_norm, win_k_norm, win_sink, pool_w, pool_scale, peer_w_q, peer_sub_keys, peer_u, peer_v):
    depth = ada_w.shape[0]
    n_even, n_odd = even_w_in.shape[0], odd_w_in.shape[0]
    bsz, seq, d = x_prompt.shape
    dec_b, dec_l, _ = x_sample.shape

    cvec = jnp.zeros((16, d), F32).at[0].set(c_ctx).at[1:1 + dec_b].set(c)
    mod = _modulation(cvec, ada_w, ada_b)

    s5 = [_s5_tables(s5_lam_re[i], s5_lam_im[i], s5_log_step[i], s5_b_re[i], s5_b_im[i],
                     s5_c_re[i], s5_c_im[i]) for i in range(n_even)]
    W = dict(
        norm_mix=norm_mix.reshape(depth, 1, d), norm_ffn=norm_ffn.reshape(depth, 1, d),
        even_w_in=even_w_in.astype(BF16), even_w_out=even_w_out.astype(BF16),
        odd_w_in=odd_w_in.astype(BF16), odd_w_out=odd_w_out.astype(BF16),
        ones_bd=_ones_block_diag(512, 64),
        diff_q_norm=jnp.tile(diff_q_norm, (1, 8)).reshape(n_even, 1, 512),
        diff_k_norm=jnp.tile(diff_k_norm, (1, 8)).reshape(n_even, 1, 512),
        diff_lambda=diff_lambda, diff_sub_norm=diff_sub_norm.reshape(n_even, 1, DIFF_DV),
        s5_bd=[t[0] for t in s5], s5_cd=[t[1] for t in s5], s5_a=[t[2] for t in s5],
        s5_d=s5_d.reshape(n_even, 1, 512), s5_w_glu=s5_w_glu.astype(BF16),
        win_q_norm=jnp.tile(win_q_norm, (1, 8)).reshape(n_odd, 1, 512),
        win_k_norm=jnp.tile(win_k_norm, (1, 2)).reshape(n_odd, 1, 128),
        win_sink=jnp.broadcast_to(win_sink[:, :, None], (n_odd, 8, 128)),
        pool_w=pool_w.astype(BF16), pool_scale=pool_scale.reshape(n_odd, 1, 512),
        peer_w_q=peer_w_q.astype(BF16),
        peer_keys=peer_sub_keys.astype(BF16).reshape(depth, 2 * PEER_HEADS, PEER_NKEYS, -1),
        peer_u=peer_u.astype(BF16),
        peer_vt=jnp.swapaxes(peer_v, 1, 2).astype(BF16),
        rope=_rope_tables(dec_l, 512),
    )
    cache = {'diff_k': cache_diff_k, 'diff_v': cache_diff_v, 's5': state_s5,
             'win_k': cache_win_k, 'win_v': cache_win_v}

    def run(x, mods, cch, per_batch_mod):
        states = []
        for l in range(depth):
            if l % 2 == 0:
                x, st = _even_layer(x, mods[l], W, l, cch, per_batch_mod)
            else:
                x, st = _odd_layer(x, mods[l], W, l, cch, per_batch_mod)
            states.append(st)
            x = _peer_layer(x, mods[l], W, l, per_batch_mod)
        return x, states

    ctx_mods = [mod[l, 0:1].reshape(1, 6, d) for l in range(depth)]
    dec_mods = [mod[l, 1:1 + dec_b].reshape(dec_b, 6, d) for l in range(depth)]
    y_prompt, states = run(x_prompt, ctx_mods, None, False)
    y_sample, _ = run(x_sample, dec_mods, cache, True)

    evens = [states[l] for l in range(depth) if l % 2 == 0]
    odds = [states[l] for l in range(depth) if l % 2 == 1]
    new_diff_k = jnp.stack([s[0].reshape(bsz, seq, 2, DIFF_HEADS, DIFF_DK) for s in evens], axis=1)
    new_diff_v = jnp.stack([s[1].reshape(bsz, seq, DIFF_HEADS, DIFF_DV) for s in evens], axis=1)
    new_s5 = jnp.stack([_s5_blocks_to_state(s[2]) for s in evens], axis=1)
    new_win_k = jnp.stack([s[0].reshape(bsz, seq, WIN_KV_HEADS, WIN_DH) for s in odds], axis=1)
    new_win_v = jnp.stack([s[1].reshape(bsz, seq, WIN_KV_HEADS, WIN_DH) for s in odds], axis=1)
    return (y_prompt, y_sample, new_diff_k, new_diff_v, new_s5, new_win_k, new_win_v)
```

```python
import functools
import math

import jax
import jax.numpy as jnp
from jax import lax
from jax.experimental import pallas as pl
from jax.experimental.pallas import tpu as pltpu

F32 = jnp.float32
BF16 = jnp.bfloat16

D_MODEL = 1024
MIX_HALF = 512
GRID_W = 64
EPS = 1e-6
NEG_INF = -1e30
ROPE_BASE = 10000.0

S5_GROUPS = 32
S5_GROUP = 16
S5_STATE = 64
S5_JB = 4
S5_BLK = 1024
S5_CHUNK = 32
S5_BATCH = 8

DIFF_HEADS = 4
DIFF_DK = 64
DIFF_DV = 128
WIN_KV_HEADS = 2
WIN_GROUP = 4
WIN_DH = 64
ATTN_SCALE = DIFF_DK ** -0.5
WINDOW = 128
POOL_WINDOWS = (2, 4, 8, 16)
POOL_PAD = 8

PEER_HEADS = 8
PEER_NKEYS = 128
PEER_TOPK = 16
RANK_CODE = 2.0 ** 100
PEER_SUBS = (128, 384, 512, 1024)
PEER_STEP = sum(PEER_SUBS)
PEER_TOKENS = 512

TOKEN_TILE = 256
WIDE_TILE = 512
VMEM_LIMIT = 56 * 1024 * 1024


def _params(*sem):
    return pltpu.CompilerParams(dimension_semantics=sem, vmem_limit_bytes=VMEM_LIMIT)


def _dot(a, b):
    return jnp.dot(a, b, preferred_element_type=F32)


def _dot_nt(a, b):
    return lax.dot_general(a, b, (((1,), (1,)), ((), ())), preferred_element_type=F32)


def _sigmoid(x):
    return 1.0 / (1.0 + jnp.exp(-x))


def _gelu(x):
    c = math.sqrt(2.0 / math.pi)
    hx = 0.5 * x
    return hx + hx * jnp.tanh(x * (c + (c * 0.044715) * (x * x)))


def _modulate(x, gain, shift, scale):
    ms = jnp.mean(x * x, axis=-1, keepdims=True)
    return (x * lax.rsqrt(ms + EPS) * gain) * (1.0 + scale) + shift


def _group_rms(z, ones_bd, gain):
    ms = _dot((z * z).astype(BF16), ones_bd)
    return z * lax.rsqrt(ms + EPS) * gain


def _rope(z, cos_t, sin_t):
    lane = lax.broadcasted_iota(jnp.int32, (1, 128), 1) % 32
    first = lane < 16
    parts = []
    for j in range(z.shape[1] // 128):
        c = z[:, 128 * j:128 * (j + 1)]
        parts.append(jnp.where(first, pltpu.roll(c, 112, axis=1), pltpu.roll(c, 16, axis=1)))
    swapped = parts[0] if len(parts) == 1 else jnp.concatenate(parts, axis=1)
    return z * cos_t + swapped * sin_t


def _mod_kernel(c_ref, w_ref, b_ref, o_ref):
    c = c_ref[...]
    s = c * _sigmoid(c)
    o_ref[0] = jnp.dot(s, w_ref[0], precision=lax.Precision.HIGHEST,
                       preferred_element_type=F32) + b_ref[0]


def _modulation(cvec, ada_w, ada_b):
    depth, d, n = ada_w.shape
    tn = 1536
    return pl.pallas_call(
        _mod_kernel,
        grid=(depth, n // tn),
        in_specs=[pl.BlockSpec((16, d), lambda l, j: (0, 0)),
                  pl.BlockSpec((1, d, tn), lambda l, j: (l, 0, j)),
                  pl.BlockSpec((1, 1, tn), lambda l, j: (l, 0, j))],
        out_specs=pl.BlockSpec((1, 16, tn), lambda l, j: (l, 0, j)),
        out_shape=jax.ShapeDtypeStruct((depth, 16, n), F32),
        compiler_params=_params("arbitrary", "arbitrary"),
        name="adaln_mod",
    )(cvec, ada_w, ada_b.reshape(depth, 1, n))


def _inproj_even_kernel(rope, *refs):
    if rope:
        (x_ref, mod_ref, g_ref, w_ref, bd_ref, gq_ref, gk_ref, cos_ref, sin_ref,
         u_ref, q_ref, kn_ref, ka_ref, v_ref, va_ref) = refs
    else:
        (x_ref, mod_ref, g_ref, w_ref, bd_ref, gq_ref, gk_ref,
         u_ref, q_ref, kn_ref, ka_ref, v_ref, va_ref) = refs
    mod = mod_ref[0]
    h = _modulate(x_ref[0], g_ref[...], mod[0:1], mod[1:2])
    z = _dot(h.astype(BF16), w_ref[...])
    u = z[:, 0:512]
    q = _group_rms(z[:, 512:1024], bd_ref[...], gq_ref[...])
    k = _group_rms(z[:, 1024:1536], bd_ref[...], gk_ref[...])
    v = z[:, 1536:2048]
    u_ref[...] = u
    kn_ref[0] = k
    v_ref[0] = v
    va_ref[0] = v.astype(BF16)
    if rope:
        q = _rope(q, cos_ref[...], sin_ref[...])
        k = _rope(k, cos_ref[...], sin_ref[...])
    q_ref[0] = (q * ATTN_SCALE).astype(BF16)
    ka_ref[0] = k.astype(BF16)


def _inproj_even(x, mod, gain, w, ones_bd, gq, gk, rope_tabs, per_batch_mod):
    b, l, d = x.shape
    tl = min(l, WIDE_TILE)
    rope = rope_tabs is not None
    mod_map = (lambda bi, i: (bi, 0, 0)) if per_batch_mod else (lambda bi, i: (0, 0, 0))
    const2 = lambda bi, i: (0, 0)
    tok = lambda w_: pl.BlockSpec((1, tl, w_), lambda bi, i: (bi, i, 0))
    in_specs = [tok(d),
                pl.BlockSpec((1, 6, d), mod_map),
                pl.BlockSpec((1, d), const2),
                pl.BlockSpec(w.shape, const2),
                pl.BlockSpec(ones_bd.shape, const2),
                pl.BlockSpec((1, 512), const2),
                pl.BlockSpec((1, 512), const2)]
    args = [x, mod, gain, w, ones_bd, gq, gk]
    if rope:
        in_specs += [pl.BlockSpec((tl, 512), lambda bi, i: (i, 0))] * 2
        args += list(rope_tabs)
    out_shape = [jax.ShapeDtypeStruct((l, b * 512), F32),
                 jax.ShapeDtypeStruct((b, l, 512), BF16),
                 jax.ShapeDtypeStruct((b, l, 512), F32),
                 jax.ShapeDtypeStruct((b, l, 512), BF16),
                 jax.ShapeDtypeStruct((b, l, 512), F32),
                 jax.ShapeDtypeStruct((b, l, 512), BF16)]
    out_specs = [pl.BlockSpec((tl, 512), lambda bi, i: (i, bi)),
                 tok(512), tok(512), tok(512), tok(512), tok(512)]
    return pl.pallas_call(
        functools.partial(_inproj_even_kernel, rope),
        grid=(b, l // tl), in_specs=in_specs, out_specs=out_specs, out_shape=out_shape,
        compiler_params=_params("arbitrary", "arbitrary"),
        name="inproj_even_rope" if rope else "inproj_even",
    )(*args)


def _inproj_odd_kernel(rope, *refs):
    if rope:
        (x_ref, mod_ref, g_ref, w_ref, bd_ref, gq_ref, gk_ref, cos_ref, sin_ref,
         q_ref, kn_ref, ka_ref, v_ref, va_ref, u_ref) = refs
    else:
        (x_ref, mod_ref, g_ref, w_ref, bd_ref, gq_ref, gk_ref,
         q_ref, kn_ref, ka_ref, v_ref, va_ref, u_ref) = refs
    mod = mod_ref[0]
    h = _modulate(x_ref[0], g_ref[...], mod[0:1], mod[1:2])
    z = _dot(h.astype(BF16), w_ref[...])
    bd = bd_ref[...]
    q = _group_rms(z[:, 0:512], bd, gq_ref[...])
    k = _group_rms(z[:, 512:640], bd[0:128, 0:128], gk_ref[...])
    v = z[:, 640:768]
    kn_ref[0] = k
    v_ref[0] = v
    va_ref[0] = v.astype(BF16)
    u_ref[0] = z[:, 768:1280]
    if rope:
        q = _rope(q, cos_ref[...], sin_ref[...])
        k = _rope(k, cos_ref[:, 0:128], sin_ref[:, 0:128])
    q_ref[0] = (q * ATTN_SCALE).astype(BF16)
    ka_ref[0] = k.astype(BF16)


def _inproj_odd(x, mod, gain, w, ones_bd, gq, gk, rope_tabs, per_batch_mod):
    b, l, d = x.shape
    tl = min(l, WIDE_TILE)
    rope = rope_tabs is not None
    mod_map = (lambda bi, i: (bi, 0, 0)) if per_batch_mod else (lambda bi, i: (0, 0, 0))
    const2 = lambda bi, i: (0, 0)
    tok = lambda w_: pl.BlockSpec((1, tl, w_), lambda bi, i: (bi, i, 0))
    in_specs = [tok(d),
                pl.BlockSpec((1, 6, d), mod_map),
                pl.BlockSpec((1, d), const2),
                pl.BlockSpec(w.shape, const2),
                pl.BlockSpec(ones_bd.shape, const2),
                pl.BlockSpec((1, 512), const2),
                pl.BlockSpec((1, 128), const2)]
    args = [x, mod, gain, w, ones_bd, gq, gk]
    if rope:
        in_specs += [pl.BlockSpec((tl, 512), lambda bi, i: (i, 0))] * 2
        args += list(rope_tabs)
    out_shape = [jax.ShapeDtypeStruct((b, l, 512), BF16),
                 jax.ShapeDtypeStruct((b, l, 128), F32),
                 jax.ShapeDtypeStruct((b, l, 128), BF16),
                 jax.ShapeDtypeStruct((b, l, 128), F32),
                 jax.ShapeDtypeStruct((b, l, 128), BF16),
                 jax.ShapeDtypeStruct((b, l, 512), F32)]
    out_specs = [tok(512), tok(128), tok(128), tok(128), tok(128), tok(512)]
    return pl.pallas_call(
        functools.partial(_inproj_odd_kernel, rope),
        grid=(b, l // tl), in_specs=in_specs, out_specs=out_specs, out_shape=out_shape,
        compiler_params=_params("arbitrary", "arbitrary"),
        name="inproj_odd_rope" if rope else "inproj_odd",
    )(*args)


def _s5_kernel(uf_ref, ub_ref, bd_ref, cd_ref, a_ref, h0_ref, yf_ref, yb_ref, fin_ref,
               bu_ref, carry_ref):
    c = pl.program_id(1)
    rows = S5_CHUNK * S5_BATCH

    @pl.when(c == 0)
    def _():
        carry_ref[...] = h0_ref[...]

    for d in range(2):
        u = (uf_ref if d == 0 else ub_ref)[...].reshape(rows, MIX_HALF).astype(BF16)
        for j in range(S5_JB):
            bu_ref[d, :, S5_BLK * j:S5_BLK * (j + 1)] = _dot(u[:, 128 * j:128 * (j + 1)], bd_ref[d, j])
        for j in range(S5_JB):
            lo = S5_BLK * j
            a_re = jnp.broadcast_to(a_ref[d, 0, j:j + 1, :], (S5_BATCH, 512))
            a_im = jnp.broadcast_to(a_ref[d, 1, j:j + 1, :], (S5_BATCH, 512))

            h_re, h_im = carry_ref[d, :, lo:lo + 512], carry_ref[d, :, lo + 512:lo + 1024]
            for s in range(S5_CHUNK):
                row = (s if d == 0 else S5_CHUNK - 1 - s) * S5_BATCH
                b_re = bu_ref[d, row:row + S5_BATCH, lo:lo + 512]
                b_im = bu_ref[d, row:row + S5_BATCH, lo + 512:lo + 1024]
                h_re, h_im = a_re * h_re - a_im * h_im + b_re, a_re * h_im + a_im * h_re + b_im
                bu_ref[d, row:row + S5_BATCH, lo:lo + 512] = h_re
                bu_ref[d, row:row + S5_BATCH, lo + 512:lo + 1024] = h_im
            carry_ref[d, :, lo:lo + 512] = h_re
            carry_ref[d, :, lo + 512:lo + 1024] = h_im
        ys = [_dot(bu_ref[d, :, S5_BLK * j:S5_BLK * (j + 1)].astype(BF16), cd_ref[d, j])
              for j in range(S5_JB)]
        y = jnp.concatenate(ys, axis=1).reshape(S5_CHUNK, S5_BATCH, MIX_HALF)
        if d == 0:
            yf_ref[...] = y
        else:
            yb_ref[...] = y

    @pl.when(c == pl.num_programs(1) - 1)
    def _():
        fin_ref[...] = carry_ref[...]


def _s5_scan(u_tm, bd, cd, acoef, h0):
    l, b, _ = u_tm.shape
    nc = l // S5_CHUNK
    ng = b // S5_BATCH
    state_w = S5_JB * S5_BLK
    blk = (S5_CHUNK, S5_BATCH, MIX_HALF)
    full = lambda a: pl.BlockSpec(a.shape, lambda g, c: (0,) * a.ndim)
    return pl.pallas_call(
        _s5_kernel,
        grid=(ng, nc),
        in_specs=[pl.BlockSpec(blk, lambda g, c: (c, g, 0)),
                  pl.BlockSpec(blk, lambda g, c: (nc - 1 - c, g, 0)),
                  full(bd), full(cd), full(acoef),
                  pl.BlockSpec((2, S5_BATCH, state_w), lambda g, c: (0, g, 0))],
        out_specs=[pl.BlockSpec(blk, lambda g, c: (c, g, 0)),
                   pl.BlockSpec(blk, lambda g, c: (nc - 1 - c, g, 0)),
                   pl.BlockSpec((2, S5_BATCH, state_w), lambda g, c: (0, g, 0))],
        out_shape=[jax.ShapeDtypeStruct((l, b, MIX_HALF), F32),
                   jax.ShapeDtypeStruct((l, b, MIX_HALF), F32),
                   jax.ShapeDtypeStruct((2, b, state_w), F32)],
        scratch_shapes=[pltpu.VMEM((2, S5_CHUNK * S5_BATCH, state_w), F32),
                        pltpu.VMEM((2, S5_BATCH, state_w), F32)],
        compiler_params=_params("arbitrary", "arbitrary"),
        name="s5_scan",
    )(u_tm, u_tm, bd, cd, acoef, h0)


def _s5_tables(lam_re, lam_im, log_step, b_re, b_im, c_re, c_im):
    step = jnp.exp(log_step.astype(F32))[..., None]
    lr, li = lam_re.astype(F32), lam_im.astype(F32)
    er = jnp.exp(lr * step)
    a_re, a_im = er * jnp.cos(li * step), er * jnp.sin(li * step)
    den = lr * lr + li * li
    q_re = ((a_re - 1.0) * lr + a_im * li) / den
    q_im = (a_im * lr - (a_re - 1.0) * li) / den
    bb_re = q_re[..., None] * b_re - q_im[..., None] * b_im
    bb_im = q_re[..., None] * b_im + q_im[..., None] * b_re
    eye = jnp.eye(8, dtype=F32)
    bb = jnp.stack([bb_re, bb_im], axis=1).reshape(2, 2, S5_JB, 8, S5_STATE, S5_GROUP)
    bd = jnp.einsum('drjgpc,gh->djgcrhp', bb, eye).reshape(2, S5_JB, 128, S5_BLK).astype(BF16)
    cc = jnp.stack([c_re, -c_im], axis=1).astype(F32).reshape(2, 2, S5_JB, 8, S5_GROUP, S5_STATE)
    cd = jnp.einsum('drjgcp,gh->djrgphc', cc, eye).reshape(2, S5_JB, S5_BLK, 128).astype(BF16)
    acoef = jnp.stack([a_re, a_im], axis=1).reshape(2, 2, S5_JB, 512)
    return bd, cd, acoef


def _s5_state_to_blocks(st):
    b = st.shape[0]
    st = st.astype(F32).reshape(b, 2, 2, S5_JB, 8, S5_STATE)
    return jnp.transpose(st, (1, 0, 3, 2, 4, 5)).reshape(2, b, S5_JB * S5_BLK)


def _s5_blocks_to_state(fin):
    b = fin.shape[1]
    fin = fin.reshape(2, b, S5_JB, 2, 8, S5_STATE)
    return jnp.transpose(fin, (1, 0, 3, 2, 4, 5)).reshape(b, 2, 2, S5_GROUPS, S5_STATE)


def _diff_attn_kernel(lam_init, q_ref, k_ref, v_ref, lv_ref, o_ref):
    lv = lv_ref[...]
    lam = (jnp.exp(jnp.sum(lv[0:1] * lv[1:2], axis=-1, keepdims=True))
           - jnp.exp(jnp.sum(lv[2:3] * lv[3:4], axis=-1, keepdims=True)) + lam_init)
    lane = lax.broadcasted_iota(jnp.int32, (1, 128), 1)
    def scores(h):
        out = []
        for m in range(2):
            j = m * 2 + h // 2
            keep = (lane < 64) if h % 2 == 0 else (lane >= 64)
            qb = jnp.where(keep, q_ref[0, :, 128 * j:128 * (j + 1)], jnp.zeros((), BF16))
            out.append(_dot_nt(qb, k_ref[0, :, 128 * j:128 * (j + 1)]))
        return out

    outs = []
    pending = [scores(0), scores(1)]
    for h in range(DIFF_HEADS):
        cur = pending.pop(0)
        es = [jnp.exp(s - jnp.max(s, axis=-1, keepdims=True)) for s in cur]
        sums = [jnp.sum(e, axis=-1, keepdims=True) for e in es]
        a = (es[0] - (lam * sums[0] / sums[1]) * es[1]).astype(BF16)
        if h + 2 < DIFF_HEADS:
            pending.append(scores(h + 2))
        outs.append(_dot(a, v_ref[0, :, 128 * h:128 * (h + 1)]) / sums[0])
    o_ref[0] = jnp.concatenate(outs, axis=1)


def _diff_attention(q, k, v, lv, lam_init):
    b, lq, _ = q.shape
    lk = k.shape[1]
    tq = TOKEN_TILE
    return pl.pallas_call(
        functools.partial(_diff_attn_kernel, lam_init),
        grid=(b, lq // tq),
        in_specs=[pl.BlockSpec((1, tq, 512), lambda bi, i: (bi, i, 0)),
                  pl.BlockSpec((1, lk, 512), lambda bi, i: (bi, 0, 0)),
                  pl.BlockSpec((1, lk, 512), lambda bi, i: (bi, 0, 0)),
                  pl.BlockSpec((4, DIFF_DK), lambda bi, i: (0, 0))],
        out_specs=pl.BlockSpec((1, tq, 512), lambda bi, i: (bi, i, 0)),
        out_shape=jax.ShapeDtypeStruct((b, lq, 512), F32),
        compiler_params=_params("arbitrary", "arbitrary"),
        name="diff_attention",
    )(q, k, v, lv)


def _win_attn_kernel(local, seq_len, *refs):
    if local:
        q_ref, kc_ref, vc_ref, kl_ref, vl_ref, sink_ref, o_ref = refs
    else:
        q_ref, kc_ref, vc_ref, sink_ref, o_ref = refs
    tq = q_ref.shape[1]
    lane = lax.broadcasted_iota(jnp.int32, (1, 128), 1)
    low = lane < 64
    kc = kc_ref[0]
    vc = vc_ref[0]
    n_ctx = kc.shape[0]
    if local:
        span = tq + 2 * WINDOW
        start = pl.multiple_of(pl.program_id(1) * tq, tq)
        kl = kl_ref[0, pl.ds(start, span), :]
        vl = vl_ref[0, pl.ds(start, span), :]
        qpos = start + lax.broadcasted_iota(jnp.int32, (tq, span), 0)
        kpos = start - WINDOW + lax.broadcasted_iota(jnp.int32, (tq, span), 1)
        ok = (jnp.abs(qpos - kpos) <= WINDOW) & (kpos >= 0) & (kpos < seq_len)
    def scores(hd):
        n = hd // WIN_GROUP
        j, half = hd // 2, hd % 2
        qb = q_ref[0, :, 128 * j:128 * (j + 1)]
        if half != n:
            qb = jnp.concatenate([qb[:, 64:128], qb[:, 0:64]], axis=1)
        qb = jnp.where(low if n == 0 else jnp.logical_not(low), qb, jnp.zeros((), BF16))
        s = _dot_nt(qb, kc)
        if local:
            s = jnp.concatenate([s, jnp.where(ok, _dot_nt(qb, kl), NEG_INF)], axis=1)
        return s

    heads = {}
    n_heads = WIN_KV_HEADS * WIN_GROUP
    nxt = scores(0)
    for hd in range(n_heads):
        s = nxt
        if hd + 1 < n_heads:
            nxt = scores(hd + 1)
        sink = sink_ref[hd:hd + 1, 0:1]
        mx = jnp.maximum(jnp.max(s, axis=-1, keepdims=True), sink)
        e = jnp.exp(s - mx)
        den = jnp.sum(e, axis=-1, keepdims=True) + jnp.exp(sink - mx)
        p = e.astype(BF16)
        o = _dot(p[:, 0:n_ctx], vc)
        if local:
            o = o + _dot(p[:, n_ctx:], vl)
        heads[hd] = o / den
    blocks = []
    for j in range(4):
        ev, od = heads[2 * j], heads[2 * j + 1]
        n = (2 * j) // WIN_GROUP
        if n == 1:
            ev = pltpu.roll(ev, 64, axis=1)
        else:
            od = pltpu.roll(od, 64, axis=1)
        blocks.append(jnp.where(low, ev, od))
    o_ref[0] = jnp.concatenate(blocks, axis=1).astype(BF16)


def _win_attention(q, kc, vc, sink, k_lat=None, v_lat=None):
    b, lq, _ = q.shape
    n_ctx = kc.shape[1]
    local = k_lat is not None
    tq = TOKEN_TILE
    in_specs = [pl.BlockSpec((1, tq, 512), lambda bi, i: (bi, i, 0)),
                pl.BlockSpec((1, n_ctx, 128), lambda bi, i: (bi, 0, 0)),
                pl.BlockSpec((1, n_ctx, 128), lambda bi, i: (bi, 0, 0))]
    args = [q, kc, vc]
    if local:
        lp = k_lat.shape[1]
        in_specs += [pl.BlockSpec((1, lp, 128), lambda bi, i: (bi, 0, 0))] * 2
        args += [k_lat, v_lat]
    in_specs.append(pl.BlockSpec((8, 128), lambda bi, i: (0, 0)))
    args.append(sink)
    return pl.pallas_call(
        functools.partial(_win_attn_kernel, local, lq),
        grid=(b, lq // tq), in_specs=in_specs,
        out_specs=pl.BlockSpec((1, tq, 512), lambda bi, i: (bi, i, 0)),
        out_shape=jax.ShapeDtypeStruct((b, lq, 512), BF16),
        compiler_params=_params("arbitrary", "arbitrary"),
        name="win_attention_local" if local else "win_attention",
    )(*args)


def _pool_kernel(u_ref, w_ref, scale_ref, o_ref):
    l = u_ref.shape[1]
    lp = l + 2 * POOL_PAD
    t = lax.broadcasted_iota(jnp.int32, (l, 128), 0)
    zpad = jnp.zeros((POOL_PAD, 128), F32)

    def shifted(a, s):
        return pltpu.roll(a, s, axis=0) + pltpu.roll(a, lp - s, axis=0)

    outs = []
    for gi, wl in enumerate(POOL_WINDOWS):
        x = u_ref[0, :, 128 * gi:128 * (gi + 1)]
        xp = jnp.concatenate([zpad, x, zpad], axis=0)
        acc = xp + pltpu.roll(xp, 1, axis=0)
        if wl >= 4:
            acc = shifted(acc, 1)
        if wl >= 8:
            acc = shifted(acc, 2)
        if wl >= 16:
            acc = shifted(acc, 4)
        win = acc[POOL_PAD:POOL_PAD + l]
        cnt = (jnp.minimum(t + wl // 2, l) - jnp.maximum(t - wl // 2, 0)).astype(F32)
        pooled = win / cnt - x
        outs.append(_dot(pooled.astype(BF16), w_ref[gi]))
    o_ref[0] = (jnp.concatenate(outs, axis=1) * scale_ref[...]).astype(BF16)


def _pool_mixer(u, w, scale):
    b, l, _ = u.shape
    return pl.pallas_call(
        _pool_kernel,
        grid=(b,),
        in_specs=[pl.BlockSpec((1, l, 512), lambda bi: (bi, 0, 0)),
                  pl.BlockSpec(w.shape, lambda bi: (0, 0, 0)),
                  pl.BlockSpec((1, 512), lambda bi: (0, 0))],
        out_specs=pl.BlockSpec((1, l, 512), lambda bi: (bi, 0, 0)),
        out_shape=jax.ShapeDtypeStruct((b, l, 512), BF16),
        compiler_params=_params("arbitrary"),
        name="pool_mixer",
    )(u, w, scale)


def _mixout_even_kernel(out_scale, x_ref, mod_ref, u_ref, yf_ref, yb_ref, oa_ref, sd_ref,
                        wglu_ref, gsub_ref, wout_ref, o_ref):
    mod = mod_ref[0]
    y = sd_ref[...] * u_ref[...] + yf_ref[...] + yb_ref[...]
    g = _gelu(y)
    o_a = g * _sigmoid(_dot(g.astype(BF16), wglu_ref[...]))
    parts = [o_a.astype(BF16)]
    for h in range(DIFF_HEADS):
        blk = oa_ref[0, :, 128 * h:128 * (h + 1)]
        ms = jnp.mean(blk * blk, axis=-1, keepdims=True)
        parts.append(((blk * lax.rsqrt(ms + EPS) * gsub_ref[...]) * out_scale).astype(BF16))
    cat = jnp.concatenate(parts, axis=1)
    o_ref[0] = x_ref[0] + mod[2:3] * _dot(cat, wout_ref[...])


def _mixout_even(x, mod, u_tm2, yf2, yb2, o_attn, s5_d, w_glu, g_sub, w_out, out_scale, per_batch_mod):
    b, l, d = x.shape
    tl = min(l, WIDE_TILE)
    mod_map = (lambda bi, i: (bi, 0, 0)) if per_batch_mod else (lambda bi, i: (0, 0, 0))
    const2 = lambda bi, i: (0, 0)
    tm = pl.BlockSpec((tl, 512), lambda bi, i: (i, bi))
    return pl.pallas_call(
        functools.partial(_mixout_even_kernel, out_scale),
        grid=(b, l // tl),
        in_specs=[pl.BlockSpec((1, tl, d), lambda bi, i: (bi, i, 0)),
                  pl.BlockSpec((1, 6, d), mod_map),
                  tm, tm, tm,
                  pl.BlockSpec((1, tl, 512), lambda bi, i: (bi, i, 0)),
                  pl.BlockSpec((1, 512), const2),
                  pl.BlockSpec((512, 512), const2),
                  pl.BlockSpec((1, 128), const2),
                  pl.BlockSpec((d, d), const2)],
        out_specs=pl.BlockSpec((1, tl, d), lambda bi, i: (bi, i, 0)),
        out_shape=jax.ShapeDtypeStruct((b, l, d), F32),
        compiler_params=_params("arbitrary", "arbitrary"),
        name="mixout_even",
    )(x, mod, u_tm2, yf2, yb2, o_attn, s5_d, w_glu, g_sub, w_out)


def _mixout_odd_kernel(x_ref, mod_ref, oc_ref, od_ref, wout_ref, o_ref):
    mod = mod_ref[0]
    cat = jnp.concatenate([oc_ref[0], od_ref[0]], axis=1)
    o_ref[0] = x_ref[0] + mod[2:3] * _dot(cat, wout_ref[...])


def _mixout_odd(x, mod, o_c, o_d, w_out, per_batch_mod):
    b, l, d = x.shape
    tl = min(l, WIDE_TILE)
    mod_map = (lambda bi, i: (bi, 0, 0)) if per_batch_mod else (lambda bi, i: (0, 0, 0))
    tok = lambda w_: pl.BlockSpec((1, tl, w_), lambda bi, i: (bi, i, 0))
    return pl.pallas_call(
        _mixout_odd_kernel,
        grid=(b, l // tl),
        in_specs=[tok(d), pl.BlockSpec((1, 6, d), mod_map), tok(512), tok(512),
                  pl.BlockSpec((d, d), lambda bi, i: (0, 0))],
        out_specs=tok(d),
        out_shape=jax.ShapeDtypeStruct((b, l, d), F32),
        compiler_params=_params("arbitrary", "arbitrary"),
        name="mixout_odd",
    )(x, mod, o_c, o_d, w_out)


def _rank16(s, kio, exact):
    n = s.shape[0]
    vals = []
    if exact:
        rank = jnp.full(s.shape, float(PEER_TOPK), F32)
        for r in range(PEER_TOPK):
            m = jnp.max(s, axis=0, keepdims=True)
            ix = jnp.min(jnp.where(s == m, kio, float(n)), axis=0, keepdims=True)
            hit = kio == ix
            rank = jnp.where(hit, float(r), rank)
            s = jnp.where(hit, -jnp.inf, s)
            vals.append(m)
        return vals, rank
    for r in range(PEER_TOPK):
        m = jnp.max(s, axis=0, keepdims=True)
        s = jnp.where(s == m, -RANK_CODE * (1.0 + r / 32.0), s)
        vals.append(m)
    code = s * (-1.0 / RANK_CODE)
    return vals, jnp.where(code >= 1.0, (code - 1.0) * 32.0, float(PEER_TOPK))


_SORT16 = ((0, 1), (2, 3), (0, 2), (1, 3), (1, 2), (4, 5), (6, 7), (4, 6), (5, 7), (5, 6), (0, 4), (2, 6), (2, 4),
           (1, 5), (3, 7), (3, 5), (1, 2), (3, 4), (5, 6), (8, 9), (10, 11), (8, 10), (9, 11), (9, 10), (12, 13),
           (14, 15), (12, 14), (13, 15), (13, 14), (8, 12), (10, 14), (10, 12), (9, 13), (11, 15), (11, 13), (9, 10),
           (11, 12), (13, 14), (0, 8), (4, 12), (4, 8), (2, 10), (6, 14), (6, 10), (2, 4), (6, 8), (10, 12), (1, 9),
           (5, 13), (5, 9), (3, 11), (7, 15), (7, 11), (3, 5), (7, 9), (11, 13), (1, 2), (3, 4), (5, 6), (7, 8),
           (9, 10), (11, 12), (13, 14))


def _sorted_top16(s):
    cols = [s[8 * i:8 * (i + 1), :] for i in range(16)]
    for i, j in _SORT16:
        cols[i], cols[j] = jnp.maximum(cols[i], cols[j]), jnp.minimum(cols[i], cols[j])
    vals = []
    for r in range(PEER_TOPK):
        m = cols[0]
        for shift in (4, 2, 1):
            m = jnp.maximum(m, pltpu.roll(m, shift, axis=0))
        vals.append(m[0:1])
        hit = cols[0] == m
        for i in range(PEER_TOPK - 1 - r):
            cols[i] = jnp.where(hit, cols[i + 1], cols[i])
    return vals


def _candidate_rows(v1, v2, j):
    v1hi = jnp.concatenate(v1[8:16], axis=0)
    v2lo = jnp.concatenate(v2[0:8], axis=0)
    v2hi = jnp.concatenate(v2[8:16], axis=0)
    g3_a = jnp.where(j < 5.0, v1[2], v1[4])
    g3_b = jnp.where(j < 5.0, v2lo, pltpu.roll(v2lo, 5, axis=0))
    g4_a = jnp.where(j < 4.0, v1[3], jnp.where(j < 6.0, v1[5], v1[6]))
    g4_b = jnp.where(j < 4.0, v2lo, jnp.where(j < 6.0, pltpu.roll(v2lo, 4, axis=0), pltpu.roll(v2lo, 6, axis=0)))
    return jnp.concatenate([v1[0] + v2lo, v1[0] + v2hi, v1[1] + v2lo, g3_a + g3_b, g4_a + g4_b,
                            jnp.where(j < 2.0, v1[7] + v2lo, -jnp.inf), v1hi + v2[0]], axis=0)


def _staircase_select(cand, j):
    rest = cand
    for r in range(PEER_TOPK):
        m = jnp.max(rest, axis=0, keepdims=True)
        if r == 0:
            best = m
        rest = jnp.where(rest == m, -jnp.inf, rest)
    picked = cand >= m
    z = jnp.sum(jnp.where(picked, jnp.exp(cand - best), 0.0), axis=0, keepdims=True)
    g = [jnp.where(picked[8 * i:8 * (i + 1)], 1.0, 0.0) for i in range(7)]
    tot = lambda v: jnp.sum(v, axis=0, keepdims=True)
    n = [None] * PEER_TOPK
    n[0], n[1], n[7] = tot(g[0] + g[1]), tot(g[2]), tot(g[5])
    n[2] = tot(jnp.where(j < 5.0, g[3], 0.0))
    n[4] = tot(g[3]) - n[2]
    n[3] = tot(jnp.where(j < 4.0, g[4], 0.0))
    n[5] = tot(jnp.where(j < 6.0, g[4], 0.0)) - n[3]
    n[6] = tot(g[4]) - n[3] - n[5]
    for k in range(8):
        n[8 + k] = g[6][k:k + 1]
    total = n[0]
    for a in range(1, PEER_TOPK):
        total = total + n[a]
    return z, n, total


def _candidate_positions(j):
    return jnp.concatenate([j, 8.0 + j, 16.0 + j,
                            jnp.where(j < 5.0, 32.0 + j, 59.0 + j),
                            jnp.where(j < 4.0, 48.0 + j, jnp.where(j < 6.0, 76.0 + j, 90.0 + j)),
                            jnp.where(j < 2.0, 112.0 + j, 999.0), 128.0 + 16.0 * j], axis=0)


def _peer_topk_kernel(x_ref, mod_ref, g_ref, wq_ref, keys_ref, h_ref, n_ref, p1_ref, r2_ref, p2_ref):
    mod = mod_ref[0]
    h = _modulate(x_ref[0], g_ref[...], mod[3:4], mod[4:5]).astype(BF16)
    h_ref[0] = h
    q = _dot(h, wq_ref[...])
    tl = q.shape[0]
    lanes = tl
    kio = lax.broadcasted_iota(jnp.int32, (PEER_NKEYS, lanes), 0).astype(F32)
    jrow = lax.broadcasted_iota(jnp.int32, (8, lanes), 0).astype(F32)
    pos = _candidate_positions(jrow)

    def head(hd, s1, s2, exact, at):
        s1, s2 = s1[:, at:at + lanes], s2[:, at:at + lanes]
        if exact:
            v1, rank1 = _rank16(s1, kio, True)
        else:
            v1 = _sorted_top16(s1)
        v2, rank2 = _rank16(s2, kio, exact)
        cand = _candidate_rows(v1, v2, jrow)
        if exact:
            sel_a = []
            for r in range(PEER_TOPK):
                m = jnp.max(cand, axis=0, keepdims=True)
                px = jnp.min(jnp.where(cand == m, pos, 999.0), axis=0, keepdims=True)
                cand = jnp.where(pos == px, -jnp.inf, cand)
                sel_a.append(jnp.floor(px * (1.0 / PEER_TOPK)))
                if r == 0:
                    best0, z = m, jnp.ones_like(m)
                else:
                    z = z + jnp.exp(m - best0)
            counts = []
            for a in range(PEER_TOPK):
                n_a = jnp.zeros_like(z)
                for r in range(a, PEER_TOPK):
                    n_a = n_a + jnp.where(sel_a[r] == float(a), 1.0, 0.0)
                counts.append(n_a)
        else:
            z, counts, picked = _staircase_select(cand, jrow)
        if exact:
            n_e1 = jnp.zeros((PEER_NKEYS, lanes), F32)
            for a in range(PEER_TOPK):
                n_e1 = jnp.where(rank1 == float(a), counts[a], n_e1)
        else:
            c8 = counts[8]
            for a in range(9, PEER_TOPK):
                c8 = c8 + counts[a]
            floor8 = v1[7]
            for k in range(1, 9):
                floor8 = jnp.where(c8 == float(k), v1[7 + k], floor8)
            n_e1 = jnp.where(s1 >= floor8, 1.0, 0.0) - jnp.where(s1 >= v1[7], 1.0, 0.0)
            for a in range(8):
                n_e1 = jnp.where(s1 == v1[a], counts[a], n_e1)
        n_ref[0, hd, :, at:at + lanes] = n_e1
        p1_ref[0, hd, :, at:at + lanes] = jnp.exp(s1 - v1[0]) * (1.0 / z)
        r2_ref[0, hd, :, at:at + lanes] = rank2.astype(BF16)
        p2_ref[0, hd, :, at:at + lanes] = jnp.exp(s2 - v2[0]).astype(BF16)
        if exact:
            return None
        total = (jnp.sum(jnp.where(s1 >= v1[PEER_TOPK - 1], 1.0, 0.0), axis=0, keepdims=True)
                 + jnp.sum(jnp.where(rank2 < float(PEER_TOPK), 1.0, 0.0), axis=0, keepdims=True) + picked)
        for r in range(PEER_TOPK - 1):
            total = total + jnp.where(v1[r] <= v1[r + 1], 1.0, 0.0)
        return jnp.max(jnp.abs(total - 3.0 * PEER_TOPK))

    def scores(hd):
        return (_dot_nt(keys_ref[2 * hd], q[:, 256 * hd:256 * hd + 128].astype(BF16)),
                _dot_nt(keys_ref[2 * hd + 1], q[:, 256 * hd + 128:256 * hd + 256].astype(BF16)))

    miscounts = []
    for hd in range(PEER_HEADS):
        s1, s2 = scores(hd)
        miscounts.append([head(hd, s1, s2, False, at) for at in range(0, tl, lanes)])
    for hd in range(PEER_HEADS):
        for at, miscount in zip(range(0, tl, lanes), miscounts[hd]):
            @pl.when(miscount > 0.0)
            def _(hd=hd, at=at):
                s1, s2 = scores(hd)
                head(hd, s1, s2, True, at)


def _peer_topk(x, mod, gain, w_q, keys, per_batch_mod):
    b, l, d = x.shape
    tl = TOKEN_TILE
    mod_map = (lambda bi, i: (bi, 0, 0)) if per_batch_mod else (lambda bi, i: (0, 0, 0))
    sel = pl.BlockSpec((1, PEER_HEADS, PEER_NKEYS, tl), lambda bi, i: (bi, 0, 0, i))
    sel_f32 = jax.ShapeDtypeStruct((b, PEER_HEADS, PEER_NKEYS, l), F32)
    sel_bf16 = jax.ShapeDtypeStruct((b, PEER_HEADS, PEER_NKEYS, l), BF16)
    return pl.pallas_call(
        _peer_topk_kernel,
        grid=(b, l // tl),
        in_specs=[pl.BlockSpec((1, tl, d), lambda bi, i: (bi, i, 0)),
                  pl.BlockSpec((1, 6, d), mod_map),
                  pl.BlockSpec((1, d), lambda bi, i: (0, 0)),
                  pl.BlockSpec(w_q.shape, lambda bi, i: (0, 0)),
                  pl.BlockSpec(keys.shape, lambda bi, i: (0, 0, 0))],
        out_specs=[pl.BlockSpec((1, tl, d), lambda bi, i: (bi, i, 0)), sel, sel, sel, sel],
        out_shape=[jax.ShapeDtypeStruct((b, l, d), BF16), sel_f32, sel_f32, sel_bf16, sel_bf16],
        compiler_params=_params("arbitrary", "arbitrary"),
        name="peer_topk",
    )(x, mod, gain, w_q, keys)


def _peer_dense_kernel(x_ref, mod_ref, h_ref, u_ref, vt_ref, n_ref, p1_ref, r2_ref, p2_ref, o_ref,
                       ht_ref, acc_ref, s_ref, gate_ref):
    e = pl.program_id(2)

    @pl.when(e == 0)
    def _():
        ht_ref[...] = h_ref[0].astype(F32).T.astype(BF16)
        acc_ref[...] = jnp.zeros_like(acc_ref)

    ht = ht_ref[...]
    tm = ht.shape[1]
    bounds = [sum(PEER_SUBS[:i]) for i in range(len(PEER_SUBS) + 1)]

    def gate_rows(lo, hi):
        for row in range(lo // PEER_NKEYS, hi // PEER_NKEYS):
            gate = [None] * (PEER_NKEYS // 16)
            e1 = e * (PEER_STEP // PEER_NKEYS) + row
            for hd in range(PEER_HEADS):
                n_b = jnp.broadcast_to(n_ref[0, hd, pl.ds(e1, 1), :], (16, tm)).astype(BF16)
                p1_b = jnp.broadcast_to(p1_ref[0, hd, pl.ds(e1, 1), :], (16, tm)).astype(BF16)
                for k in range(PEER_NKEYS // 16):
                    term = jnp.where(r2_ref[0, hd, 16 * k:16 * (k + 1), :] < n_b,
                                     p2_ref[0, hd, 16 * k:16 * (k + 1), :], jnp.zeros((), BF16)) * p1_b
                    gate[k] = term if gate[k] is None else gate[k] + term
            for k in range(PEER_NKEYS // 16):
                at = PEER_NKEYS * row + 16 * k
                gate_ref[at:at + 16, :] = gate[k]

    for lo, hi in zip(bounds[:-1], bounds[1:]):
        s_ref[lo:hi, :] = _dot(u_ref[lo:hi, :], ht).astype(BF16)
        gate_rows(lo, hi)
    ws = [gate_ref[lo:hi, :] * _gelu(s_ref[lo:hi, :])
          for lo, hi in zip(bounds[:-1], bounds[1:])]
    acc_ref[...] += _dot(vt_ref[...], jnp.concatenate(ws, axis=0))

    @pl.when(e == pl.num_programs(2) - 1)
    def _():
        o_ref[0] = x_ref[0] + mod_ref[0][5:6] * acc_ref[...].T


def _peer_dense(x, mod, h, u_tabs, vt_tabs, layer, n_e1, p1n, rank2, p2, per_batch_mod):
    b, l, d = x.shape
    tm = PEER_TOKENS
    n_exp = u_tabs.shape[1]
    mod_map = (lambda bi, i, e: (bi, 0, 0)) if per_batch_mod else (lambda bi, i, e: (0, 0, 0))
    tok = lambda: pl.BlockSpec((1, tm, d), lambda bi, i, e: (bi, i, 0))
    by_key = pl.BlockSpec((1, PEER_HEADS, PEER_NKEYS, tm), lambda bi, i, e: (bi, 0, 0, i))
    by_row = by_key
    return pl.pallas_call(
        _peer_dense_kernel,
        grid=(b, l // tm, n_exp // PEER_STEP),
        in_specs=[tok(), pl.BlockSpec((1, 6, d), mod_map), tok(),
                  pl.BlockSpec((None, PEER_STEP, d), lambda bi, i, e: (layer, e, 0)),
                  pl.BlockSpec((None, d, PEER_STEP), lambda bi, i, e: (layer, 0, e)),
                  by_row, by_row, by_key, by_key],
        out_specs=tok(),
        out_shape=jax.ShapeDtypeStruct((b, l, d), F32),
        scratch_shapes=[pltpu.VMEM((d, tm), BF16), pltpu.VMEM((d, tm), F32),
                        pltpu.VMEM((PEER_STEP, tm), BF16), pltpu.VMEM((PEER_STEP, tm), BF16)],
        compiler_params=_params("arbitrary", "arbitrary", "arbitrary"),
        name="peer_dense",
    )(x, mod, h, u_tabs, vt_tabs, n_e1, p1n, rank2, p2)


def _rope_tables(n_lat, width):
    q4 = DIFF_DK // 4
    n_rows = n_lat // GRID_W
    row = jnp.repeat(jnp.arange(n_rows), GRID_W).astype(F32)
    col = jnp.tile(jnp.arange(GRID_W), n_rows).astype(F32)
    freqs = ROPE_BASE ** (-jnp.arange(q4, dtype=F32) / q4)
    ang = jnp.stack([row[:, None] * freqs, col[:, None] * freqs], axis=1)
    cos, sin = jnp.cos(ang), jnp.sin(ang)
    cos64 = jnp.stack([cos, cos], axis=2).reshape(n_lat, 64)
    sin64 = jnp.stack([-sin, sin], axis=2).reshape(n_lat, 64)
    reps = width // 64
    return jnp.tile(cos64, (1, reps)), jnp.tile(sin64, (1, reps))


def _ones_block_diag(width, group):
    idx = jnp.arange(width) // group
    return jnp.where(idx[:, None] == idx[None, :], 1.0 / group, 0.0).astype(BF16)


def _peer_layer(x, mod, W, l, per_batch_mod):
    shape = x.shape
    if not per_batch_mod:
        x = x.reshape(1, -1, shape[-1])
    h, n_e1, p1n, rank2, p2 = _peer_topk(x, mod, W['norm_ffn'][l], W['peer_w_q'][l], W['peer_keys'][l],
                                         per_batch_mod)
    y = _peer_dense(x, mod, h, W['peer_u'], W['peer_vt'], l, n_e1, p1n, rank2, p2, per_batch_mod)
    return y.reshape(shape)


def _even_layer(x, mod, W, l, cache, per_batch_mod):
    b, seq, _ = x.shape
    i = l // 2
    lam_init = 0.8 - 0.6 * math.exp(-0.3 * l)
    rope_tabs = W['rope'] if cache is not None else None
    u_tm2, q, k_norm, k_att, v, v_att = _inproj_even(
        x, mod, W['norm_mix'][l], W['even_w_in'][i], W['ones_bd'], W['diff_q_norm'][i],
        W['diff_k_norm'][i], rope_tabs, per_batch_mod)
    if cache is None:
        h0 = jnp.zeros((2, b, S5_JB * S5_BLK), F32)
        k_all, v_all = k_att, v_att
    else:
        h0 = _s5_state_to_blocks(cache['s5'][:, i])
        k_all = jnp.concatenate([cache['diff_k'][:, i].reshape(b, -1, 512).astype(BF16), k_att], axis=1)
        v_all = jnp.concatenate([cache['diff_v'][:, i].reshape(b, -1, 512).astype(BF16), v_att], axis=1)
    yf, yb, fin = _s5_scan(u_tm2.reshape(seq, b, 512), W['s5_bd'][i], W['s5_cd'][i], W['s5_a'][i], h0)
    o_attn = _diff_attention(q, k_all, v_all, W['diff_lambda'][i], lam_init)
    x = _mixout_even(x, mod, u_tm2, yf.reshape(seq, b * 512), yb.reshape(seq, b * 512), o_attn,
                     W['s5_d'][i], W['s5_w_glu'][i], W['diff_sub_norm'][i], W['even_w_out'][i],
                     1.0 - lam_init, per_batch_mod)
    return x, (k_norm, v, fin)


def _odd_layer(x, mod, W, l, cache, per_batch_mod):
    b, seq, _ = x.shape
    i = l // 2
    rope_tabs = W['rope'] if cache is not None else None
    q, k_norm, k_att, v, v_att, u = _inproj_odd(
        x, mod, W['norm_mix'][l], W['odd_w_in'][i], W['ones_bd'], W['win_q_norm'][i],
        W['win_k_norm'][i], rope_tabs, per_batch_mod)
    if cache is None:
        o_c = _win_attention(q, k_att, v_att, W['win_sink'][i])
    else:
        pad = ((0, 0), (WINDOW, WINDOW), (0, 0))
        o_c = _win_attention(q, cache['win_k'][:, i].reshape(b, -1, 128).astype(BF16),
                             cache['win_v'][:, i].reshape(b, -1, 128).astype(BF16), W['win_sink'][i],
                             jnp.pad(k_att, pad), jnp.pad(v_att, pad))
    o_d = _pool_mixer(u, W['pool_w'][i], W['pool_scale'][i])
    x = _mixout_odd(x, mod, o_c, o_d, W['odd_w_out'][i], per_batch_mod)
    return x, (k_norm, v)


def kernel(x_prompt, x_sample, cache_diff_k, cache_diff_v, state_s5, cache_win_k, cache_win_v, c, c_ctx, ada_w, ada_b, norm_mix, norm_ffn, even_w_in, even_w_out, s5_lam_re, s5_lam_im, s5_log_step, s5_b_re, s5_b_im, s5_c_re, s5_c_im, s5_d, s5_w_glu, diff_q_norm, diff_k_norm, diff_lambda, diff_sub_norm, odd_w_in, odd_w_out, win_q_norm, win_k_norm, win_sink, pool_w, pool_scale, peer_w_q, peer_sub_keys, peer_u, peer_v):
    depth = ada_w.shape[0]
    n_even, n_odd = even_w_in.shape[0], odd_w_in.shape[0]
    bsz, seq, d = x_prompt.shape
    dec_b, dec_l, _ = x_sample.shape

    cvec = jnp.zeros((16, d), F32).at[0].set(c_ctx).at[1:1 + dec_b].set(c)
    mod = _modulation(cvec, ada_w, ada_b)

    s5 = [_s5_tables(s5_lam_re[i], s5_lam_im[i], s5_log_step[i], s5_b_re[i], s5_b_im[i],
                     s5_c_re[i], s5_c_im[i]) for i in range(n_even)]
    W = dict(
        norm_mix=norm_mix.reshape(depth, 1, d), norm_ffn=norm_ffn.reshape(depth, 1, d),
        even_w_in=even_w_in.astype(BF16), even_w_out=even_w_out.astype(BF16),
        odd_w_in=odd_w_in.astype(BF16), odd_w_out=odd_w_out.astype(BF16),
        ones_bd=_ones_block_diag(512, 64),
        diff_q_norm=jnp.tile(diff_q_norm, (1, 8)).reshape(n_even, 1, 512),
        diff_k_norm=jnp.tile(diff_k_norm, (1, 8)).reshape(n_even, 1, 512),
        diff_lambda=diff_lambda, diff_sub_norm=diff_sub_norm.reshape(n_even, 1, DIFF_DV),
        s5_bd=[t[0] for t in s5], s5_cd=[t[1] for t in s5], s5_a=[t[2] for t in s5],
        s5_d=s5_d.reshape(n_even, 1, 512), s5_w_glu=s5_w_glu.astype(BF16),
        win_q_norm=jnp.tile(win_q_norm, (1, 8)).reshape(n_odd, 1, 512),
        win_k_norm=jnp.tile(win_k_norm, (1, 2)).reshape(n_odd, 1, 128),
        win_sink=jnp.broadcast_to(win_sink[:, :, None], (n_odd, 8, 128)),
        pool_w=pool_w.astype(BF16), pool_scale=pool_scale.reshape(n_odd, 1, 512),
        peer_w_q=peer_w_q.astype(BF16),
        peer_keys=peer_sub_keys.astype(BF16).reshape(depth, 2 * PEER_HEADS, PEER_NKEYS, -1),
        peer_u=peer_u.astype(BF16),
        peer_vt=jnp.swapaxes(peer_v, 1, 2).astype(BF16),
        rope=_rope_tables(dec_l, 512),
    )
    cache = {'diff_k': cache_diff_k, 'diff_v': cache_diff_v, 's5': state_s5,
             'win_k': cache_win_k, 'win_v': cache_win_v}

    def run(x, mods, cch, per_batch_mod):
        states = []
        for l in range(depth):
            if l % 2 == 0:
                x, st = _even_layer(x, mods[l], W, l, cch, per_batch_mod)
            else:
                x, st = _odd_layer(x, mods[l], W, l, cch, per_batch_mod)
            states.append(st)
            x = _peer_layer(x, mods[l], W, l, per_batch_mod)
        return x, states

    ctx_mods = [mod[l, 0:1].reshape(1, 6, d) for l in range(depth)]
    dec_mods = [mod[l, 1:1 + dec_b].reshape(dec_b, 6, d) for l in range(depth)]
    y_prompt, states = run(x_prompt, ctx_mods, None, False)
    y_sample, _ = run(x_sample, dec_mods, cache, True)

    evens = [states[l] for l in range(depth) if l % 2 == 0]
    odds = [states[l] for l in range(depth) if l % 2 == 1]
    new_diff_k = jnp.stack([s[0].reshape(bsz, seq, 2, DIFF_HEADS, DIFF_DK) for s in evens], axis=1)
    new_diff_v = jnp.stack([s[1].reshape(bsz, seq, DIFF_HEADS, DIFF_DV) for s in evens], axis=1)
    new_s5 = jnp.stack([_s5_blocks_to_state(s[2]) for s in evens], axis=1)
    new_win_k = jnp.stack([s[0].reshape(bsz, seq, WIN_KV_HEADS, WIN_DH) for s in odds], axis=1)
    new_win_v = jnp.stack([s[1].reshape(bsz, seq, WIN_KV_HEADS, WIN_DH) for s in odds], axis=1)
    return (y_prompt, y_sample, new_diff_k, new_diff_v, new_s5, new_win_k, new_win_v)
```

```python
import functools
import math

import jax
import jax.numpy as jnp
from jax import lax
from jax.experimental import pallas as pl
from jax.experimental.pallas import tpu as pltpu

F32 = jnp.float32
BF16 = jnp.bfloat16

D_MODEL = 1024
MIX_HALF = 512
GRID_W = 64
EPS = 1e-6
NEG_INF = -1e30
ROPE_BASE = 10000.0

S5_GROUPS = 32
S5_GROUP = 16
S5_STATE = 64
S5_JB = 4
S5_BLK = 1024
S5_CHUNK = 32
S5_BATCH = 8

DIFF_HEADS = 4
DIFF_DK = 64
DIFF_DV = 128
WIN_KV_HEADS = 2
WIN_GROUP = 4
WIN_DH = 64
ATTN_SCALE = DIFF_DK ** -0.5
WINDOW = 128
POOL_WINDOWS = (2, 4, 8, 16)
POOL_PAD = 8

PEER_HEADS = 8
PEER_NKEYS = 128
PEER_TOPK = 16
RANK_CODE = 2.0 ** 100
PEER_SUBS = (128, 384, 512, 1024)
PEER_STEP = sum(PEER_SUBS)
PEER_TOKENS = 512

TOKEN_TILE = 256
WIDE_TILE = 512
VMEM_LIMIT = 56 * 1024 * 1024


def _params(*sem):
    return pltpu.CompilerParams(dimension_semantics=sem, vmem_limit_bytes=VMEM_LIMIT)


def _dot(a, b):
    return jnp.dot(a, b, preferred_element_type=F32)


def _dot_nt(a, b):
    return lax.dot_general(a, b, (((1,), (1,)), ((), ())), preferred_element_type=F32)


def _sigmoid(x):
    return 1.0 / (1.0 + jnp.exp(-x))


def _gelu(x):
    c = math.sqrt(2.0 / math.pi)
    hx = 0.5 * x
    return hx + hx * jnp.tanh(x * (c + (c * 0.044715) * (x * x)))


def _modulate(x, gain, shift, scale):
    ms = jnp.mean(x * x, axis=-1, keepdims=True)
    return (x * lax.rsqrt(ms + EPS) * gain) * (1.0 + scale) + shift


def _group_rms(z, ones_bd, gain):
    ms = _dot((z * z).astype(BF16), ones_bd)
    return z * lax.rsqrt(ms + EPS) * gain


def _rope(z, cos_t, sin_t):
    lane = lax.broadcasted_iota(jnp.int32, (1, 128), 1) % 32
    first = lane < 16
    parts = []
    for j in range(z.shape[1] // 128):
        c = z[:, 128 * j:128 * (j + 1)]
        parts.append(jnp.where(first, pltpu.roll(c, 112, axis=1), pltpu.roll(c, 16, axis=1)))
    swapped = parts[0] if len(parts) == 1 else jnp.concatenate(parts, axis=1)
    return z * cos_t + swapped * sin_t


def _mod_kernel(c_ref, w_ref, b_ref, o_ref):
    c = c_ref[...]
    s = c * _sigmoid(c)
    o_ref[0] = jnp.dot(s, w_ref[0], precision=lax.Precision.HIGHEST,
                       preferred_element_type=F32) + b_ref[0]


def _modulation(cvec, ada_w, ada_b):
    depth, d, n = ada_w.shape
    tn = 1536
    return pl.pallas_call(
        _mod_kernel,
        grid=(depth, n // tn),
        in_specs=[pl.BlockSpec((16, d), lambda l, j: (0, 0)),
                  pl.BlockSpec((1, d, tn), lambda l, j: (l, 0, j)),
                  pl.BlockSpec((1, 1, tn), lambda l, j: (l, 0, j))],
        out_specs=pl.BlockSpec((1, 16, tn), lambda l, j: (l, 0, j)),
        out_shape=jax.ShapeDtypeStruct((depth, 16, n), F32),
        compiler_params=_params("arbitrary", "arbitrary"),
        name="adaln_mod",
    )(cvec, ada_w, ada_b.reshape(depth, 1, n))


def _inproj_even_kernel(rope, *refs):
    if rope:
        (x_ref, mod_ref, g_ref, w_ref, bd_ref, gq_ref, gk_ref, cos_ref, sin_ref,
         u_ref, q_ref, ka_ref, va_ref) = refs
    else:
        (x_ref, mod_ref, g_ref, w_ref, bd_ref, gq_ref, gk_ref,
         u_ref, q_ref, ka_ref, va_ref, kn_ref, v_ref) = refs
    mod = mod_ref[0]
    h = _modulate(x_ref[0], g_ref[...], mod[0:1], mod[1:2])
    z = _dot(h.astype(BF16), w_ref[...])
    u = z[:, 0:512]
    q = _group_rms(z[:, 512:1024], bd_ref[...], gq_ref[...])
    k = _group_rms(z[:, 1024:1536], bd_ref[...], gk_ref[...])
    v = z[:, 1536:2048]
    u_ref[...] = u
    if not rope:
        kn_ref[0] = k
        v_ref[0] = v
    va_ref[0] = v.astype(BF16)
    if rope:
        q = _rope(q, cos_ref[...], sin_ref[...])
        k = _rope(k, cos_ref[...], sin_ref[...])
    q_ref[0] = (q * ATTN_SCALE).astype(BF16)
    ka_ref[0] = k.astype(BF16)


def _inproj_even(x, mod, gain, w, ones_bd, gq, gk, rope_tabs, per_batch_mod):
    b, l, d = x.shape
    tl = min(l, WIDE_TILE)
    rope = rope_tabs is not None
    mod_map = (lambda bi, i: (bi, 0, 0)) if per_batch_mod else (lambda bi, i: (0, 0, 0))
    const2 = lambda bi, i: (0, 0)
    tok = lambda w_: pl.BlockSpec((1, tl, w_), lambda bi, i: (bi, i, 0))
    in_specs = [tok(d),
                pl.BlockSpec((1, 6, d), mod_map),
                pl.BlockSpec((1, d), const2),
                pl.BlockSpec(w.shape, const2),
                pl.BlockSpec(ones_bd.shape, const2),
                pl.BlockSpec((1, 512), const2),
                pl.BlockSpec((1, 512), const2)]
    args = [x, mod, gain, w, ones_bd, gq, gk]
    if rope:
        in_specs += [pl.BlockSpec((tl, 512), lambda bi, i: (i, 0))] * 2
        args += list(rope_tabs)
    out_shape = [jax.ShapeDtypeStruct((l, b * 512), F32),
                 jax.ShapeDtypeStruct((b, l, 512), BF16),
                 jax.ShapeDtypeStruct((b, l, 512), BF16),
                 jax.ShapeDtypeStruct((b, l, 512), BF16)]
    out_specs = [pl.BlockSpec((tl, 512), lambda bi, i: (i, bi)), tok(512), tok(512), tok(512)]
    if not rope:
        out_shape += [jax.ShapeDtypeStruct((b, l, 512), F32)] * 2
        out_specs += [tok(512), tok(512)]
    outs = pl.pallas_call(
        functools.partial(_inproj_even_kernel, rope),
        grid=(b, l // tl), in_specs=in_specs, out_specs=out_specs, out_shape=out_shape,
        compiler_params=_params("arbitrary", "arbitrary"),
        name="inproj_even_rope" if rope else "inproj_even",
    )(*args)
    u_tm2, q, k_att, v_att = outs[:4]
    k_norm, v = (outs[4], outs[5]) if not rope else (None, None)
    return u_tm2, q, k_norm, k_att, v, v_att


def _inproj_odd_kernel(rope, *refs):
    if rope:
        (x_ref, mod_ref, g_ref, w_ref, bd_ref, gq_ref, gk_ref, cos_ref, sin_ref,
         q_ref, kn_ref, ka_ref, v_ref, va_ref, u_ref) = refs
    else:
        (x_ref, mod_ref, g_ref, w_ref, bd_ref, gq_ref, gk_ref,
         q_ref, kn_ref, ka_ref, v_ref, va_ref, u_ref) = refs
    mod = mod_ref[0]
    h = _modulate(x_ref[0], g_ref[...], mod[0:1], mod[1:2])
    z = _dot(h.astype(BF16), w_ref[...])
    bd = bd_ref[...]
    q = _group_rms(z[:, 0:512], bd, gq_ref[...])
    k = _group_rms(z[:, 512:640], bd[0:128, 0:128], gk_ref[...])
    v = z[:, 640:768]
    kn_ref[0] = k
    v_ref[0] = v
    va_ref[0] = v.astype(BF16)
    u_ref[0] = z[:, 768:1280]
    if rope:
        q = _rope(q, cos_ref[...], sin_ref[...])
        k = _rope(k, cos_ref[:, 0:128], sin_ref[:, 0:128])
    q_ref[0] = (q * ATTN_SCALE).astype(BF16)
    ka_ref[0] = k.astype(BF16)


def _inproj_odd(x, mod, gain, w, ones_bd, gq, gk, rope_tabs, per_batch_mod):
    b, l, d = x.shape
    tl = min(l, WIDE_TILE)
    rope = rope_tabs is not None
    mod_map = (lambda bi, i: (bi, 0, 0)) if per_batch_mod else (lambda bi, i: (0, 0, 0))
    const2 = lambda bi, i: (0, 0)
    tok = lambda w_: pl.BlockSpec((1, tl, w_), lambda bi, i: (bi, i, 0))
    in_specs = [tok(d),
                pl.BlockSpec((1, 6, d), mod_map),
                pl.BlockSpec((1, d), const2),
                pl.BlockSpec(w.shape, const2),
                pl.BlockSpec(ones_bd.shape, const2),
                pl.BlockSpec((1, 512), const2),
                pl.BlockSpec((1, 128), const2)]
    args = [x, mod, gain, w, ones_bd, gq, gk]
    if rope:
        in_specs += [pl.BlockSpec((tl, 512), lambda bi, i: (i, 0))] * 2
        args += list(rope_tabs)
    out_shape = [jax.ShapeDtypeStruct((b, l, 512), BF16),
                 jax.ShapeDtypeStruct((b, l, 128), F32),
                 jax.ShapeDtypeStruct((b, l, 128), BF16),
                 jax.ShapeDtypeStruct((b, l, 128), F32),
                 jax.ShapeDtypeStruct((b, l, 128), BF16),
                 jax.ShapeDtypeStruct((b, l, 512), F32)]
    out_specs = [tok(512), tok(128), tok(128), tok(128), tok(128), tok(512)]
    return pl.pallas_call(
        functools.partial(_inproj_odd_kernel, rope),
        grid=(b, l // tl), in_specs=in_specs, out_specs=out_specs, out_shape=out_shape,
        compiler_params=_params("arbitrary", "arbitrary"),
        name="inproj_odd_rope" if rope else "inproj_odd",
    )(*args)


def _s5_kernel(uf_ref, ub_ref, bd_ref, cd_ref, a_ref, h0_ref, yf_ref, yb_ref, fin_ref,
               bu_ref, carry_ref):
    c = pl.program_id(1)
    rows = S5_CHUNK * S5_BATCH

    @pl.when(c == 0)
    def _():
        carry_ref[...] = h0_ref[...]

    for d in range(2):
        u = (uf_ref if d == 0 else ub_ref)[...].reshape(rows, MIX_HALF).astype(BF16)
        for j in range(S5_JB):
            bu_ref[d, :, S5_BLK * j:S5_BLK * (j + 1)] = _dot(u[:, 128 * j:128 * (j + 1)], bd_ref[d, j])
        for j in range(S5_JB):
            lo = S5_BLK * j
            a_re = jnp.broadcast_to(a_ref[d, 0, j:j + 1, :], (S5_BATCH, 512))
            a_im = jnp.broadcast_to(a_ref[d, 1, j:j + 1, :], (S5_BATCH, 512))

            h_re, h_im = carry_ref[d, :, lo:lo + 512], carry_ref[d, :, lo + 512:lo + 1024]
            for s in range(S5_CHUNK):
                row = (s if d == 0 else S5_CHUNK - 1 - s) * S5_BATCH
                b_re = bu_ref[d, row:row + S5_BATCH, lo:lo + 512]
                b_im = bu_ref[d, row:row + S5_BATCH, lo + 512:lo + 1024]
                h_re, h_im = a_re * h_re - a_im * h_im + b_re, a_re * h_im + a_im * h_re + b_im
                bu_ref[d, row:row + S5_BATCH, lo:lo + 512] = h_re
                bu_ref[d, row:row + S5_BATCH, lo + 512:lo + 1024] = h_im
            carry_ref[d, :, lo:lo + 512] = h_re
            carry_ref[d, :, lo + 512:lo + 1024] = h_im
        ys = [_dot(bu_ref[d, :, S5_BLK * j:S5_BLK * (j + 1)].astype(BF16), cd_ref[d, j])
              for j in range(S5_JB)]
        y = jnp.concatenate(ys, axis=1).reshape(S5_CHUNK, S5_BATCH, MIX_HALF)
        if d == 0:
            yf_ref[...] = y
        else:
            yb_ref[...] = y

    @pl.when(c == pl.num_programs(1) - 1)
    def _():
        fin_ref[...] = carry_ref[...]


def _s5_scan(u_tm, bd, cd, acoef, h0):
    l, b, _ = u_tm.shape
    nc = l // S5_CHUNK
    ng = b // S5_BATCH
    state_w = S5_JB * S5_BLK
    blk = (S5_CHUNK, S5_BATCH, MIX_HALF)
    full = lambda a: pl.BlockSpec(a.shape, lambda g, c: (0,) * a.ndim)
    return pl.pallas_call(
        _s5_kernel,
        grid=(ng, nc),
        in_specs=[pl.BlockSpec(blk, lambda g, c: (c, g, 0)),
                  pl.BlockSpec(blk, lambda g, c: (nc - 1 - c, g, 0)),
                  full(bd), full(cd), full(acoef),
                  pl.BlockSpec((2, S5_BATCH, state_w), lambda g, c: (0, g, 0))],
        out_specs=[pl.BlockSpec(blk, lambda g, c: (c, g, 0)),
                   pl.BlockSpec(blk, lambda g, c: (nc - 1 - c, g, 0)),
                   pl.BlockSpec((2, S5_BATCH, state_w), lambda g, c: (0, g, 0))],
        out_shape=[jax.ShapeDtypeStruct((l, b, MIX_HALF), F32),
                   jax.ShapeDtypeStruct((l, b, MIX_HALF), F32),
                   jax.ShapeDtypeStruct((2, b, state_w), F32)],
        scratch_shapes=[pltpu.VMEM((2, S5_CHUNK * S5_BATCH, state_w), F32),
                        pltpu.VMEM((2, S5_BATCH, state_w), F32)],
        compiler_params=_params("arbitrary", "arbitrary"),
        name="s5_scan",
    )(u_tm, u_tm, bd, cd, acoef, h0)


def _s5_tables(lam_re, lam_im, log_step, b_re, b_im, c_re, c_im):
    step = jnp.exp(log_step.astype(F32))[..., None]
    lr, li = lam_re.astype(F32), lam_im.astype(F32)
    er = jnp.exp(lr * step)
    a_re, a_im = er * jnp.cos(li * step), er * jnp.sin(li * step)
    den = lr * lr + li * li
    q_re = ((a_re - 1.0) * lr + a_im * li) / den
    q_im = (a_im * lr - (a_re - 1.0) * li) / den
    bb_re = q_re[..., None] * b_re - q_im[..., None] * b_im
    bb_im = q_re[..., None] * b_im + q_im[..., None] * b_re
    eye = jnp.eye(8, dtype=F32)
    bb = jnp.stack([bb_re, bb_im], axis=1).reshape(2, 2, S5_JB, 8, S5_STATE, S5_GROUP)
    bd = jnp.einsum('drjgpc,gh->djgcrhp', bb, eye).reshape(2, S5_JB, 128, S5_BLK).astype(BF16)
    cc = jnp.stack([c_re, -c_im], axis=1).astype(F32).reshape(2, 2, S5_JB, 8, S5_GROUP, S5_STATE)
    cd = jnp.einsum('drjgcp,gh->djrgphc', cc, eye).reshape(2, S5_JB, S5_BLK, 128).astype(BF16)
    acoef = jnp.stack([a_re, a_im], axis=1).reshape(2, 2, S5_JB, 512)
    return bd, cd, acoef


def _s5_state_to_blocks(st):
    b = st.shape[0]
    st = st.astype(F32).reshape(b, 2, 2, S5_JB, 8, S5_STATE)
    return jnp.transpose(st, (1, 0, 3, 2, 4, 5)).reshape(2, b, S5_JB * S5_BLK)


def _s5_blocks_to_state(fin):
    b = fin.shape[1]
    fin = fin.reshape(2, b, S5_JB, 2, 8, S5_STATE)
    return jnp.transpose(fin, (1, 0, 3, 2, 4, 5)).reshape(b, 2, 2, S5_GROUPS, S5_STATE)


def _diff_attn_kernel(lam_init, q_ref, k_ref, v_ref, lv_ref, o_ref):
    lv = lv_ref[...]
    lam = (jnp.exp(jnp.sum(lv[0:1] * lv[1:2], axis=-1, keepdims=True))
           - jnp.exp(jnp.sum(lv[2:3] * lv[3:4], axis=-1, keepdims=True)) + lam_init)
    lane = lax.broadcasted_iota(jnp.int32, (1, 128), 1)
    def scores(h):
        out = []
        for m in range(2):
            j = m * 2 + h // 2
            keep = (lane < 64) if h % 2 == 0 else (lane >= 64)
            qb = jnp.where(keep, q_ref[0, :, 128 * j:128 * (j + 1)], jnp.zeros((), BF16))
            out.append(_dot_nt(qb, k_ref[0, :, 128 * j:128 * (j + 1)]))
        return out

    outs = []
    pending = [scores(0), scores(1)]
    for h in range(DIFF_HEADS):
        cur = pending.pop(0)
        es = [jnp.exp(s - jnp.max(s, axis=-1, keepdims=True)) for s in cur]
        sums = [jnp.sum(e, axis=-1, keepdims=True) for e in es]
        a = (es[0] - (lam * sums[0] / sums[1]) * es[1]).astype(BF16)
        if h + 2 < DIFF_HEADS:
            pending.append(scores(h + 2))
        outs.append(_dot(a, v_ref[0, :, 128 * h:128 * (h + 1)]) / sums[0])
    o_ref[0] = jnp.concatenate(outs, axis=1)


def _diff_attention(q, k, v, lv, lam_init):
    b, lq, _ = q.shape
    lk = k.shape[1]
    tq = TOKEN_TILE
    return pl.pallas_call(
        functools.partial(_diff_attn_kernel, lam_init),
        grid=(b, lq // tq),
        in_specs=[pl.BlockSpec((1, tq, 512), lambda bi, i: (bi, i, 0)),
                  pl.BlockSpec((1, lk, 512), lambda bi, i: (bi, 0, 0)),
                  pl.BlockSpec((1, lk, 512), lambda bi, i: (bi, 0, 0)),
                  pl.BlockSpec((4, DIFF_DK), lambda bi, i: (0, 0))],
        out_specs=pl.BlockSpec((1, tq, 512), lambda bi, i: (bi, i, 0)),
        out_shape=jax.ShapeDtypeStruct((b, lq, 512), F32),
        compiler_params=_params("arbitrary", "arbitrary"),
        name="diff_attention",
    )(q, k, v, lv)


def _win_attn_kernel(local, seq_len, *refs):
    if local:
        q_ref, kc_ref, vc_ref, kl_ref, vl_ref, sink_ref, o_ref = refs
    else:
        q_ref, kc_ref, vc_ref, sink_ref, o_ref = refs
    tq = q_ref.shape[1]
    lane = lax.broadcasted_iota(jnp.int32, (1, 128), 1)
    low = lane < 64
    kc = kc_ref[0]
    vc = vc_ref[0]
    n_ctx = kc.shape[0]
    if local:
        span = tq + 2 * WINDOW
        start = pl.multiple_of(pl.program_id(1) * tq, tq)
        kl = kl_ref[0, pl.ds(start, span), :]
        vl = vl_ref[0, pl.ds(start, span), :]
        qpos = start + lax.broadcasted_iota(jnp.int32, (tq, span), 0)
        kpos = start - WINDOW + lax.broadcasted_iota(jnp.int32, (tq, span), 1)
        ok = (jnp.abs(qpos - kpos) <= WINDOW) & (kpos >= 0) & (kpos < seq_len)
    def scores(hd):
        n = hd // WIN_GROUP
        j, half = hd // 2, hd % 2
        qb = q_ref[0, :, 128 * j:128 * (j + 1)]
        if half != n:
            qb = jnp.concatenate([qb[:, 64:128], qb[:, 0:64]], axis=1)
        qb = jnp.where(low if n == 0 else jnp.logical_not(low), qb, jnp.zeros((), BF16))
        s = _dot_nt(qb, kc)
        if local:
            s = jnp.concatenate([s, jnp.where(ok, _dot_nt(qb, kl), NEG_INF)], axis=1)
        return s

    heads = {}
    n_heads = WIN_KV_HEADS * WIN_GROUP
    nxt = scores(0)
    for hd in range(n_heads):
        s = nxt
        if hd + 1 < n_heads:
            nxt = scores(hd + 1)
        sink = sink_ref[hd:hd + 1, 0:1]
        mx = jnp.maximum(jnp.max(s, axis=-1, keepdims=True), sink)
        e = jnp.exp(s - mx)
        den = jnp.sum(e, axis=-1, keepdims=True) + jnp.exp(sink - mx)
        p = e.astype(BF16)
        o = _dot(p[:, 0:n_ctx], vc)
        if local:
            o = o + _dot(p[:, n_ctx:], vl)
        heads[hd] = o / den
    blocks = []
    for j in range(4):
        ev, od = heads[2 * j], heads[2 * j + 1]
        n = (2 * j) // WIN_GROUP
        if n == 1:
            ev = pltpu.roll(ev, 64, axis=1)
        else:
            od = pltpu.roll(od, 64, axis=1)
        blocks.append(jnp.where(low, ev, od))
    o_ref[0] = jnp.concatenate(blocks, axis=1).astype(BF16)


def _win_attention(q, kc, vc, sink, k_lat=None, v_lat=None):
    b, lq, _ = q.shape
    n_ctx = kc.shape[1]
    local = k_lat is not None
    tq = TOKEN_TILE
    in_specs = [pl.BlockSpec((1, tq, 512), lambda bi, i: (bi, i, 0)),
                pl.BlockSpec((1, n_ctx, 128), lambda bi, i: (bi, 0, 0)),
                pl.BlockSpec((1, n_ctx, 128), lambda bi, i: (bi, 0, 0))]
    args = [q, kc, vc]
    if local:
        lp = k_lat.shape[1]
        in_specs += [pl.BlockSpec((1, lp, 128), lambda bi, i: (bi, 0, 0))] * 2
        args += [k_lat, v_lat]
    in_specs.append(pl.BlockSpec((8, 128), lambda bi, i: (0, 0)))
    args.append(sink)
    return pl.pallas_call(
        functools.partial(_win_attn_kernel, local, lq),
        grid=(b, lq // tq), in_specs=in_specs,
        out_specs=pl.BlockSpec((1, tq, 512), lambda bi, i: (bi, i, 0)),
        out_shape=jax.ShapeDtypeStruct((b, lq, 512), BF16),
        compiler_params=_params("arbitrary", "arbitrary"),
        name="win_attention_local" if local else "win_attention",
    )(*args)


def _pool_kernel(u_ref, w_ref, scale_ref, o_ref):
    l = u_ref.shape[1]
    lp = l + 2 * POOL_PAD
    t = lax.broadcasted_iota(jnp.int32, (l, 128), 0)
    zpad = jnp.zeros((POOL_PAD, 128), F32)

    def shifted(a, s):
        return pltpu.roll(a, s, axis=0) + pltpu.roll(a, lp - s, axis=0)

    outs = []
    for gi, wl in enumerate(POOL_WINDOWS):
        x = u_ref[0, :, 128 * gi:128 * (gi + 1)]
        xp = jnp.concatenate([zpad, x, zpad], axis=0)
        acc = xp + pltpu.roll(xp, 1, axis=0)
        if wl >= 4:
            acc = shifted(acc, 1)
        if wl >= 8:
            acc = shifted(acc, 2)
        if wl >= 16:
            acc = shifted(acc, 4)
        win = acc[POOL_PAD:POOL_PAD + l]
        cnt = (jnp.minimum(t + wl // 2, l) - jnp.maximum(t - wl // 2, 0)).astype(F32)
        pooled = win / cnt - x
        outs.append(_dot(pooled.astype(BF16), w_ref[gi]))
    o_ref[0] = (jnp.concatenate(outs, axis=1) * scale_ref[...]).astype(BF16)


def _pool_mixer(u, w, scale):
    b, l, _ = u.shape
    return pl.pallas_call(
        _pool_kernel,
        grid=(b,),
        in_specs=[pl.BlockSpec((1, l, 512), lambda bi: (bi, 0, 0)),
                  pl.BlockSpec(w.shape, lambda bi: (0, 0, 0)),
                  pl.BlockSpec((1, 512), lambda bi: (0, 0))],
        out_specs=pl.BlockSpec((1, l, 512), lambda bi: (bi, 0, 0)),
        out_shape=jax.ShapeDtypeStruct((b, l, 512), BF16),
        compiler_params=_params("arbitrary"),
        name="pool_mixer",
    )(u, w, scale)


def _mixout_even_kernel(out_scale, x_ref, mod_ref, u_ref, yf_ref, yb_ref, oa_ref, sd_ref,
                        wglu_ref, gsub_ref, wout_ref, o_ref):
    mod = mod_ref[0]
    y = sd_ref[...] * u_ref[...] + yf_ref[...] + yb_ref[...]
    g = _gelu(y)
    o_a = g * _sigmoid(_dot(g.astype(BF16), wglu_ref[...]))
    parts = [o_a.astype(BF16)]
    for h in range(DIFF_HEADS):
        blk = oa_ref[0, :, 128 * h:128 * (h + 1)]
        ms = jnp.mean(blk * blk, axis=-1, keepdims=True)
        parts.append(((blk * lax.rsqrt(ms + EPS) * gsub_ref[...]) * out_scale).astype(BF16))
    cat = jnp.concatenate(parts, axis=1)
    o_ref[0] = x_ref[0] + mod[2:3] * _dot(cat, wout_ref[...])


def _mixout_even(x, mod, u_tm2, yf2, yb2, o_attn, s5_d, w_glu, g_sub, w_out, out_scale, per_batch_mod):
    b, l, d = x.shape
    tl = min(l, WIDE_TILE)
    mod_map = (lambda bi, i: (bi, 0, 0)) if per_batch_mod else (lambda bi, i: (0, 0, 0))
    const2 = lambda bi, i: (0, 0)
    tm = pl.BlockSpec((tl, 512), lambda bi, i: (i, bi))
    return pl.pallas_call(
        functools.partial(_mixout_even_kernel, out_scale),
        grid=(b, l // tl),
        in_specs=[pl.BlockSpec((1, tl, d), lambda bi, i: (bi, i, 0)),
                  pl.BlockSpec((1, 6, d), mod_map),
                  tm, tm, tm,
                  pl.BlockSpec((1, tl, 512), lambda bi, i: (bi, i, 0)),
                  pl.BlockSpec((1, 512), const2),
                  pl.BlockSpec((512, 512), const2),
                  pl.BlockSpec((1, 128), const2),
                  pl.BlockSpec((d, d), const2)],
        out_specs=pl.BlockSpec((1, tl, d), lambda bi, i: (bi, i, 0)),
        out_shape=jax.ShapeDtypeStruct((b, l, d), F32),
        compiler_params=_params("arbitrary", "arbitrary"),
        name="mixout_even",
    )(x, mod, u_tm2, yf2, yb2, o_attn, s5_d, w_glu, g_sub, w_out)


def _mixout_odd_kernel(x_ref, mod_ref, oc_ref, od_ref, wout_ref, o_ref):
    mod = mod_ref[0]
    cat = jnp.concatenate([oc_ref[0], od_ref[0]], axis=1)
    o_ref[0] = x_ref[0] + mod[2:3] * _dot(cat, wout_ref[...])


def _mixout_odd(x, mod, o_c, o_d, w_out, per_batch_mod):
    b, l, d = x.shape
    tl = min(l, WIDE_TILE)
    mod_map = (lambda bi, i: (bi, 0, 0)) if per_batch_mod else (lambda bi, i: (0, 0, 0))
    tok = lambda w_: pl.BlockSpec((1, tl, w_), lambda bi, i: (bi, i, 0))
    return pl.pallas_call(
        _mixout_odd_kernel,
        grid=(b, l // tl),
        in_specs=[tok(d), pl.BlockSpec((1, 6, d), mod_map), tok(512), tok(512),
                  pl.BlockSpec((d, d), lambda bi, i: (0, 0))],
        out_specs=tok(d),
        out_shape=jax.ShapeDtypeStruct((b, l, d), F32),
        compiler_params=_params("arbitrary", "arbitrary"),
        name="mixout_odd",
    )(x, mod, o_c, o_d, w_out)


def _rank16(s, kio, exact):
    n = s.shape[0]
    vals = []
    if exact:
        rank = jnp.full(s.shape, float(PEER_TOPK), F32)
        for r in range(PEER_TOPK):
            m = jnp.max(s, axis=0, keepdims=True)
            ix = jnp.min(jnp.where(s == m, kio, float(n)), axis=0, keepdims=True)
            hit = kio == ix
            rank = jnp.where(hit, float(r), rank)
            s = jnp.where(hit, -jnp.inf, s)
            vals.append(m)
        return vals, rank
    for r in range(PEER_TOPK):
        m = jnp.max(s, axis=0, keepdims=True)
        s = jnp.where(s == m, -RANK_CODE * (1.0 + r / 32.0), s)
        vals.append(m)
    code = s * (-1.0 / RANK_CODE)
    return vals, jnp.where(code >= 1.0, (code - 1.0) * 32.0, float(PEER_TOPK))


_SORT16 = ((0, 1), (2, 3), (0, 2), (1, 3), (1, 2), (4, 5), (6, 7), (4, 6), (5, 7), (5, 6), (0, 4), (2, 6), (2, 4),
           (1, 5), (3, 7), (3, 5), (1, 2), (3, 4), (5, 6), (8, 9), (10, 11), (8, 10), (9, 11), (9, 10), (12, 13),
           (14, 15), (12, 14), (13, 15), (13, 14), (8, 12), (10, 14), (10, 12), (9, 13), (11, 15), (11, 13), (9, 10),
           (11, 12), (13, 14), (0, 8), (4, 12), (4, 8), (2, 10), (6, 14), (6, 10), (2, 4), (6, 8), (10, 12), (1, 9),
           (5, 13), (5, 9), (3, 11), (7, 15), (7, 11), (3, 5), (7, 9), (11, 13), (1, 2), (3, 4), (5, 6), (7, 8),
           (9, 10), (11, 12), (13, 14))


def _sorted_top16(s):
    cols = [s[8 * i:8 * (i + 1), :] for i in range(16)]
    for i, j in _SORT16:
        cols[i], cols[j] = jnp.maximum(cols[i], cols[j]), jnp.minimum(cols[i], cols[j])
    vals = []
    for r in range(PEER_TOPK):
        m = cols[0]
        for shift in (4, 2, 1):
            m = jnp.maximum(m, pltpu.roll(m, shift, axis=0))
        vals.append(m[0:1])
        hit = cols[0] == m
        for i in range(PEER_TOPK - 1 - r):
            cols[i] = jnp.where(hit, cols[i + 1], cols[i])
    return vals


def _candidate_rows(v1, v2, j):
    v1hi = jnp.concatenate(v1[8:16], axis=0)
    v2lo = jnp.concatenate(v2[0:8], axis=0)
    v2hi = jnp.concatenate(v2[8:16], axis=0)
    g3_a = jnp.where(j < 5.0, v1[2], v1[4])
    g3_b = jnp.where(j < 5.0, v2lo, pltpu.roll(v2lo, 5, axis=0))
    g4_a = jnp.where(j < 4.0, v1[3], jnp.where(j < 6.0, v1[5], v1[6]))
    g4_b = jnp.where(j < 4.0, v2lo, jnp.where(j < 6.0, pltpu.roll(v2lo, 4, axis=0), pltpu.roll(v2lo, 6, axis=0)))
    return jnp.concatenate([v1[0] + v2lo, v1[0] + v2hi, v1[1] + v2lo, g3_a + g3_b, g4_a + g4_b,
                            jnp.where(j < 2.0, v1[7] + v2lo, -jnp.inf), v1hi + v2[0]], axis=0)


def _staircase_select(cand, j):
    rest = cand
    for r in range(PEER_TOPK):
        m = jnp.max(rest, axis=0, keepdims=True)
        if r == 0:
            best = m
        rest = jnp.where(rest == m, -jnp.inf, rest)
    picked = cand >= m
    z = jnp.sum(jnp.where(picked, jnp.exp(cand - best), 0.0), axis=0, keepdims=True)
    g = [jnp.where(picked[8 * i:8 * (i + 1)], 1.0, 0.0) for i in range(7)]
    tot = lambda v: jnp.sum(v, axis=0, keepdims=True)
    n = [None] * PEER_TOPK
    n[0], n[1], n[7] = tot(g[0] + g[1]), tot(g[2]), tot(g[5])
    n[2] = tot(jnp.where(j < 5.0, g[3], 0.0))
    n[4] = tot(g[3]) - n[2]
    n[3] = tot(jnp.where(j < 4.0, g[4], 0.0))
    n[5] = tot(jnp.where(j < 6.0, g[4], 0.0)) - n[3]
    n[6] = tot(g[4]) - n[3] - n[5]
    for k in range(8):
        n[8 + k] = g[6][k:k + 1]
    total = n[0]
    for a in range(1, PEER_TOPK):
        total = total + n[a]
    return z, n, total


def _candidate_positions(j):
    return jnp.concatenate([j, 8.0 + j, 16.0 + j,
                            jnp.where(j < 5.0, 32.0 + j, 59.0 + j),
                            jnp.where(j < 4.0, 48.0 + j, jnp.where(j < 6.0, 76.0 + j, 90.0 + j)),
                            jnp.where(j < 2.0, 112.0 + j, 999.0), 128.0 + 16.0 * j], axis=0)


def _peer_topk_kernel(x_ref, mod_ref, g_ref, wq_ref, keys_ref, h_ref, n_ref, p1_ref, r2_ref, p2_ref):
    mod = mod_ref[0]
    h = _modulate(x_ref[0], g_ref[...], mod[3:4], mod[4:5]).astype(BF16)
    h_ref[0] = h
    q = _dot(h, wq_ref[...])
    tl = q.shape[0]
    lanes = tl
    kio = lax.broadcasted_iota(jnp.int32, (PEER_NKEYS, lanes), 0).astype(F32)
    jrow = lax.broadcasted_iota(jnp.int32, (8, lanes), 0).astype(F32)
    pos = _candidate_positions(jrow)

    def head(hd, s1, s2, exact, at):
        s1, s2 = s1[:, at:at + lanes], s2[:, at:at + lanes]
        if exact:
            v1, rank1 = _rank16(s1, kio, True)
        else:
            v1 = _sorted_top16(s1)
        v2, rank2 = _rank16(s2, kio, exact)
        cand = _candidate_rows(v1, v2, jrow)
        if exact:
            sel_a = []
            for r in range(PEER_TOPK):
                m = jnp.max(cand, axis=0, keepdims=True)
                px = jnp.min(jnp.where(cand == m, pos, 999.0), axis=0, keepdims=True)
                cand = jnp.where(pos == px, -jnp.inf, cand)
                sel_a.append(jnp.floor(px * (1.0 / PEER_TOPK)))
                if r == 0:
                    best0, z = m, jnp.ones_like(m)
                else:
                    z = z + jnp.exp(m - best0)
            counts = []
            for a in range(PEER_TOPK):
                n_a = jnp.zeros_like(z)
                for r in range(a, PEER_TOPK):
                    n_a = n_a + jnp.where(sel_a[r] == float(a), 1.0, 0.0)
                counts.append(n_a)
        else:
            z, counts, picked = _staircase_select(cand, jrow)
        if exact:
            n_e1 = jnp.zeros((PEER_NKEYS, lanes), F32)
            for a in range(PEER_TOPK):
                n_e1 = jnp.where(rank1 == float(a), counts[a], n_e1)
        else:
            c8 = counts[8]
            for a in range(9, PEER_TOPK):
                c8 = c8 + counts[a]
            floor8 = v1[7]
            for k in range(1, 9):
                floor8 = jnp.where(c8 == float(k), v1[7 + k], floor8)
            n_e1 = jnp.where(s1 >= floor8, 1.0, 0.0) - jnp.where(s1 >= v1[7], 1.0, 0.0)
            for a in range(8):
                n_e1 = jnp.where(s1 == v1[a], counts[a], n_e1)
        n_ref[0, hd, :, at:at + lanes] = n_e1
        p1_ref[0, hd, :, at:at + lanes] = jnp.exp(s1 - v1[0]) * (1.0 / z)
        r2_ref[0, hd, :, at:at + lanes] = rank2.astype(BF16)
        p2_ref[0, hd, :, at:at + lanes] = jnp.exp(s2 - v2[0]).astype(BF16)
        if exact:
            return None
        total = (jnp.sum(jnp.where(s1 >= v1[PEER_TOPK - 1], 1.0, 0.0), axis=0, keepdims=True)
                 + jnp.sum(jnp.where(rank2 < float(PEER_TOPK), 1.0, 0.0), axis=0, keepdims=True) + picked)
        for r in range(PEER_TOPK - 1):
            total = total + jnp.where(v1[r] <= v1[r + 1], 1.0, 0.0)
        return jnp.max(jnp.abs(total - 3.0 * PEER_TOPK))

    def scores(hd):
        return (_dot_nt(keys_ref[2 * hd], q[:, 256 * hd:256 * hd + 128].astype(BF16)),
                _dot_nt(keys_ref[2 * hd + 1], q[:, 256 * hd + 128:256 * hd + 256].astype(BF16)))

    miscounts = []
    for hd in range(PEER_HEADS):
        s1, s2 = scores(hd)
        miscounts.append([head(hd, s1, s2, False, at) for at in range(0, tl, lanes)])
    for hd in range(PEER_HEADS):
        for at, miscount in zip(range(0, tl, lanes), miscounts[hd]):
            @pl.when(miscount > 0.0)
            def _(hd=hd, at=at):
                s1, s2 = scores(hd)
                head(hd, s1, s2, True, at)


def _peer_topk(x, mod, gain, w_q, keys, per_batch_mod):
    b, l, d = x.shape
    tl = TOKEN_TILE
    mod_map = (lambda bi, i: (bi, 0, 0)) if per_batch_mod else (lambda bi, i: (0, 0, 0))
    sel = pl.BlockSpec((1, PEER_HEADS, PEER_NKEYS, tl), lambda bi, i: (bi, 0, 0, i))
    sel_f32 = jax.ShapeDtypeStruct((b, PEER_HEADS, PEER_NKEYS, l), F32)
    sel_bf16 = jax.ShapeDtypeStruct((b, PEER_HEADS, PEER_NKEYS, l), BF16)
    return pl.pallas_call(
        _peer_topk_kernel,
        grid=(b, l // tl),
        in_specs=[pl.BlockSpec((1, tl, d), lambda bi, i: (bi, i, 0)),
                  pl.BlockSpec((1, 6, d), mod_map),
                  pl.BlockSpec((1, d), lambda bi, i: (0, 0)),
                  pl.BlockSpec(w_q.shape, lambda bi, i: (0, 0)),
                  pl.BlockSpec(keys.shape, lambda bi, i: (0, 0, 0))],
        out_specs=[pl.BlockSpec((1, tl, d), lambda bi, i: (bi, i, 0)), sel, sel, sel, sel],
        out_shape=[jax.ShapeDtypeStruct((b, l, d), BF16), sel_f32, sel_f32, sel_bf16, sel_bf16],
        compiler_params=_params("arbitrary", "arbitrary"),
        name="peer_topk",
    )(x, mod, gain, w_q, keys)


def _peer_dense_kernel(x_ref, mod_ref, h_ref, u_ref, vt_ref, n_ref, p1_ref, r2_ref, p2_ref, o_ref,
                       ht_ref, acc_ref, s_ref, gate_ref):
    e = pl.program_id(2)

    @pl.when(e == 0)
    def _():
        ht_ref[...] = h_ref[0].astype(F32).T.astype(BF16)
        acc_ref[...] = jnp.zeros_like(acc_ref)

    ht = ht_ref[...]
    tm = ht.shape[1]
    bounds = [sum(PEER_SUBS[:i]) for i in range(len(PEER_SUBS) + 1)]

    def gate_rows(lo, hi):
        for row in range(lo // PEER_NKEYS, hi // PEER_NKEYS):
            gate = [None] * (PEER_NKEYS // 16)
            e1 = e * (PEER_STEP // PEER_NKEYS) + row
            for hd in range(PEER_HEADS):
                n_b = jnp.broadcast_to(n_ref[0, hd, pl.ds(e1, 1), :], (16, tm)).astype(BF16)
                p1_b = jnp.broadcast_to(p1_ref[0, hd, pl.ds(e1, 1), :], (16, tm)).astype(BF16)
                for k in range(PEER_NKEYS // 16):
                    term = jnp.where(r2_ref[0, hd, 16 * k:16 * (k + 1), :] < n_b,
                                     p2_ref[0, hd, 16 * k:16 * (k + 1), :], jnp.zeros((), BF16)) * p1_b
                    gate[k] = term if gate[k] is None else gate[k] + term
            for k in range(PEER_NKEYS // 16):
                at = PEER_NKEYS * row + 16 * k
                gate_ref[at:at + 16, :] = gate[k]

    for lo, hi in zip(bounds[:-1], bounds[1:]):
        s_ref[lo:hi, :] = _dot(u_ref[lo:hi, :], ht).astype(BF16)
        gate_rows(lo, hi)
    ws = [gate_ref[lo:hi, :] * _gelu(s_ref[lo:hi, :])
          for lo, hi in zip(bounds[:-1], bounds[1:])]
    acc_ref[...] += _dot(vt_ref[...], jnp.concatenate(ws, axis=0))

    @pl.when(e == pl.num_programs(2) - 1)
    def _():
        o_ref[0] = x_ref[0] + mod_ref[0][5:6] * acc_ref[...].T


def _peer_dense(x, mod, h, u_tabs, vt_tabs, layer, n_e1, p1n, rank2, p2, per_batch_mod):
    b, l, d = x.shape
    tm = PEER_TOKENS
    n_exp = u_tabs.shape[1]
    mod_map = (lambda bi, i, e: (bi, 0, 0)) if per_batch_mod else (lambda bi, i, e: (0, 0, 0))
    tok = lambda: pl.BlockSpec((1, tm, d), lambda bi, i, e: (bi, i, 0))
    by_key = pl.BlockSpec((1, PEER_HEADS, PEER_NKEYS, tm), lambda bi, i, e: (bi, 0, 0, i))
    by_row = by_key
    return pl.pallas_call(
        _peer_dense_kernel,
        grid=(b, l // tm, n_exp // PEER_STEP),
        in_specs=[tok(), pl.BlockSpec((1, 6, d), mod_map), tok(),
                  pl.BlockSpec((None, PEER_STEP, d), lambda bi, i, e: (layer, e, 0)),
                  pl.BlockSpec((None, d, PEER_STEP), lambda bi, i, e: (layer, 0, e)),
                  by_row, by_row, by_key, by_key],
        out_specs=tok(),
        out_shape=jax.ShapeDtypeStruct((b, l, d), F32),
        scratch_shapes=[pltpu.VMEM((d, tm), BF16), pltpu.VMEM((d, tm), F32),
                        pltpu.VMEM((PEER_STEP, tm), BF16), pltpu.VMEM((PEER_STEP, tm), BF16)],
        compiler_params=_params("arbitrary", "arbitrary", "arbitrary"),
        name="peer_dense",
    )(x, mod, h, u_tabs, vt_tabs, n_e1, p1n, rank2, p2)


def _rope_tables(n_lat, width):
    q4 = DIFF_DK // 4
    n_rows = n_lat // GRID_W
    row = jnp.repeat(jnp.arange(n_rows), GRID_W).astype(F32)
    col = jnp.tile(jnp.arange(GRID_W), n_rows).astype(F32)
    freqs = ROPE_BASE ** (-jnp.arange(q4, dtype=F32) / q4)
    ang = jnp.stack([row[:, None] * freqs, col[:, None] * freqs], axis=1)
    cos, sin = jnp.cos(ang), jnp.sin(ang)
    cos64 = jnp.stack([cos, cos], axis=2).reshape(n_lat, 64)
    sin64 = jnp.stack([-sin, sin], axis=2).reshape(n_lat, 64)
    reps = width // 64
    return jnp.tile(cos64, (1, reps)), jnp.tile(sin64, (1, reps))


def _ones_block_diag(width, group):
    idx = jnp.arange(width) // group
    return jnp.where(idx[:, None] == idx[None, :], 1.0 / group, 0.0).astype(BF16)


def _peer_layer(x, mod, W, l, per_batch_mod):
    shape = x.shape
    if not per_batch_mod:
        x = x.reshape(1, -1, shape[-1])
    h, n_e1, p1n, rank2, p2 = _peer_topk(x, mod, W['norm_ffn'][l], W['peer_w_q'][l], W['peer_keys'][l],
                                         per_batch_mod)
    y = _peer_dense(x, mod, h, W['peer_u'], W['peer_vt'], l, n_e1, p1n, rank2, p2, per_batch_mod)
    return y.reshape(shape)


def _even_layer(x, mod, W, l, cache, per_batch_mod):
    b, seq, _ = x.shape
    i = l // 2
    lam_init = 0.8 - 0.6 * math.exp(-0.3 * l)
    rope_tabs = W['rope'] if cache is not None else None
    u_tm2, q, k_norm, k_att, v, v_att = _inproj_even(
        x, mod, W['norm_mix'][l], W['even_w_in'][i], W['ones_bd'], W['diff_q_norm'][i],
        W['diff_k_norm'][i], rope_tabs, per_batch_mod)
    if cache is None:
        h0 = jnp.zeros((2, b, S5_JB * S5_BLK), F32)
        k_all, v_all = k_att, v_att
    else:
        h0 = _s5_state_to_blocks(cache['s5'][:, i])
        k_all = jnp.concatenate([cache['diff_k'][:, i].reshape(b, -1, 512).astype(BF16), k_att], axis=1)
        v_all = jnp.concatenate([cache['diff_v'][:, i].reshape(b, -1, 512).astype(BF16), v_att], axis=1)
    yf, yb, fin = _s5_scan(u_tm2.reshape(seq, b, 512), W['s5_bd'][i], W['s5_cd'][i], W['s5_a'][i], h0)
    o_attn = _diff_attention(q, k_all, v_all, W['diff_lambda'][i], lam_init)
    x = _mixout_even(x, mod, u_tm2, yf.reshape(seq, b * 512), yb.reshape(seq, b * 512), o_attn,
                     W['s5_d'][i], W['s5_w_glu'][i], W['diff_sub_norm'][i], W['even_w_out'][i],
                     1.0 - lam_init, per_batch_mod)
    return x, (k_norm, v, fin)


def _odd_layer(x, mod, W, l, cache, per_batch_mod):
    b, seq, _ = x.shape
    i = l // 2
    rope_tabs = W['rope'] if cache is not None else None
    q, k_norm, k_att, v, v_att, u = _inproj_odd(
        x, mod, W['norm_mix'][l], W['odd_w_in'][i], W['ones_bd'], W['win_q_norm'][i],
        W['win_k_norm'][i], rope_tabs, per_batch_mod)
    if cache is None:
        o_c = _win_attention(q, k_att, v_att, W['win_sink'][i])
    else:
        pad = ((0, 0), (WINDOW, WINDOW), (0, 0))
        o_c = _win_attention(q, cache['win_k'][:, i].reshape(b, -1, 128).astype(BF16),
                             cache['win_v'][:, i].reshape(b, -1, 128).astype(BF16), W['win_sink'][i],
                             jnp.pad(k_att, pad), jnp.pad(v_att, pad))
    o_d = _pool_mixer(u, W['pool_w'][i], W['pool_scale'][i])
    x = _mixout_odd(x, mod, o_c, o_d, W['odd_w_out'][i], per_batch_mod)
    return x, (k_norm, v)


def kernel(x_prompt, x_sample, cache_diff_k, cache_diff_v, state_s5, cache_win_k, cache_win_v, c, c_ctx, ada_w, ada_b, norm_mix, norm_ffn, even_w_in, even_w_out, s5_lam_re, s5_lam_im, s5_log_step, s5_b_re, s5_b_im, s5_c_re, s5_c_im, s5_d, s5_w_glu, diff_q_norm, diff_k_norm, diff_lambda, diff_sub_norm, odd_w_in, odd_w_out, win_q_norm, win_k_norm, win_sink, pool_w, pool_scale, peer_w_q, peer_sub_keys, peer_u, peer_v):
    depth = ada_w.shape[0]
    n_even, n_odd = even_w_in.shape[0], odd_w_in.shape[0]
    bsz, seq, d = x_prompt.shape
    dec_b, dec_l, _ = x_sample.shape

    cvec = jnp.zeros((16, d), F32).at[0].set(c_ctx).at[1:1 + dec_b].set(c)
    mod = _modulation(cvec, ada_w, ada_b)

    s5 = [_s5_tables(s5_lam_re[i], s5_lam_im[i], s5_log_step[i], s5_b_re[i], s5_b_im[i],
                     s5_c_re[i], s5_c_im[i]) for i in range(n_even)]
    W = dict(
        norm_mix=norm_mix.reshape(depth, 1, d), norm_ffn=norm_ffn.reshape(depth, 1, d),
        even_w_in=even_w_in.astype(BF16), even_w_out=even_w_out.astype(BF16),
        odd_w_in=odd_w_in.astype(BF16), odd_w_out=odd_w_out.astype(BF16),
        ones_bd=_ones_block_diag(512, 64),
        diff_q_norm=jnp.tile(diff_q_norm, (1, 8)).reshape(n_even, 1, 512),
        diff_k_norm=jnp.tile(diff_k_norm, (1, 8)).reshape(n_even, 1, 512),
        diff_lambda=diff_lambda, diff_sub_norm=diff_sub_norm.reshape(n_even, 1, DIFF_DV),
        s5_bd=[t[0] for t in s5], s5_cd=[t[1] for t in s5], s5_a=[t[2] for t in s5],
        s5_d=s5_d.reshape(n_even, 1, 512), s5_w_glu=s5_w_glu.astype(BF16),
        win_q_norm=jnp.tile(win_q_norm, (1, 8)).reshape(n_odd, 1, 512),
        win_k_norm=jnp.tile(win_k_norm, (1, 2)).reshape(n_odd, 1, 128),
        win_sink=jnp.broadcast_to(win_sink[:, :, None], (n_odd, 8, 128)),
        pool_w=pool_w.astype(BF16), pool_scale=pool_scale.reshape(n_odd, 1, 512),
        peer_w_q=peer_w_q.astype(BF16),
        peer_keys=peer_sub_keys.astype(BF16).reshape(depth, 2 * PEER_HEADS, PEER_NKEYS, -1),
        peer_u=peer_u.astype(BF16),
        peer_vt=jnp.swapaxes(peer_v, 1, 2).astype(BF16),
        rope=_rope_tables(dec_l, 512),
    )
    cache = {'diff_k': cache_diff_k, 'diff_v': cache_diff_v, 's5': state_s5,
             'win_k': cache_win_k, 'win_v': cache_win_v}

    def run(x, mods, cch, per_batch_mod):
        states = []
        for l in range(depth):
            if l % 2 == 0:
                x, st = _even_layer(x, mods[l], W, l, cch, per_batch_mod)
            else:
                x, st = _odd_layer(x, mods[l], W, l, cch, per_batch_mod)
            states.append(st)
            x = _peer_layer(x, mods[l], W, l, per_batch_mod)
        return x, states

    ctx_mods = [mod[l, 0:1].reshape(1, 6, d) for l in range(depth)]
    dec_mods = [mod[l, 1:1 + dec_b].reshape(dec_b, 6, d) for l in range(depth)]
    y_prompt, states = run(x_prompt, ctx_mods, None, False)
    y_sample, _ = run(x_sample, dec_mods, cache, True)

    evens = [states[l] for l in range(depth) if l % 2 == 0]
    odds = [states[l] for l in range(depth) if l % 2 == 1]
    new_diff_k = jnp.stack([s[0].reshape(bsz, seq, 2, DIFF_HEADS, DIFF_DK) for s in evens], axis=1)
    new_diff_v = jnp.stack([s[1].reshape(bsz, seq, DIFF_HEADS, DIFF_DV) for s in evens], axis=1)
    new_s5 = jnp.stack([_s5_blocks_to_state(s[2]) for s in evens], axis=1)
    new_win_k = jnp.stack([s[0].reshape(bsz, seq, WIN_KV_HEADS, WIN_DH) for s in odds], axis=1)
    new_win_v = jnp.stack([s[1].reshape(bsz, seq, WIN_KV_HEADS, WIN_DH) for s in odds], axis=1)
    return (y_prompt, y_sample, new_diff_k, new_diff_v, new_s5, new_win_k, new_win_v)
```

```python
import functools
import math

import jax
import jax.numpy as jnp
from jax import lax
from jax.experimental import pallas as pl
from jax.experimental.pallas import tpu as pltpu

F32 = jnp.float32
BF16 = jnp.bfloat16

D_MODEL = 1024
MIX_HALF = 512
GRID_W = 64
EPS = 1e-6
NEG_INF = -1e30
ROPE_BASE = 10000.0

S5_GROUPS = 32
S5_GROUP = 16
S5_STATE = 64
S5_JB = 4
S5_BLK = 1024
S5_CHUNK = 32
S5_BATCH = 8

DIFF_HEADS = 4
DIFF_DK = 64
DIFF_DV = 128
WIN_KV_HEADS = 2
WIN_GROUP = 4
WIN_DH = 64
ATTN_SCALE = DIFF_DK ** -0.5
WINDOW = 128
POOL_WINDOWS = (2, 4, 8, 16)
POOL_PAD = 8

PEER_HEADS = 8
PEER_NKEYS = 128
PEER_TOPK = 16
RANK_CODE = 2.0 ** 100
PEER_SUBS = (128, 384, 512, 1024)
PEER_STEP = sum(PEER_SUBS)
PEER_TOKENS = 512

TOKEN_TILE = 256
WIDE_TILE = 512
VMEM_LIMIT = 56 * 1024 * 1024


def _params(*sem):
    return pltpu.CompilerParams(dimension_semantics=sem, vmem_limit_bytes=VMEM_LIMIT)


def _dot(a, b):
    return jnp.dot(a, b, preferred_element_type=F32)


def _dot_nt(a, b):
    return lax.dot_general(a, b, (((1,), (1,)), ((), ())), preferred_element_type=F32)


def _sigmoid(x):
    return 1.0 / (1.0 + jnp.exp(-x))


def _gelu(x):
    c = math.sqrt(2.0 / math.pi)
    hx = 0.5 * x
    return hx + hx * jnp.tanh(x * (c + (c * 0.044715) * (x * x)))


def _modulate(x, gain, shift, scale):
    ms = jnp.mean(x * x, axis=-1, keepdims=True)
    return (x * lax.rsqrt(ms + EPS) * gain) * (1.0 + scale) + shift


def _group_rms(z, ones_bd, gain):
    ms = _dot((z * z).astype(BF16), ones_bd)
    return z * lax.rsqrt(ms + EPS) * gain


def _rope(z, cos_t, sin_t):
    lane = lax.broadcasted_iota(jnp.int32, (1, 128), 1) % 32
    first = lane < 16
    parts = []
    for j in range(z.shape[1] // 128):
        c = z[:, 128 * j:128 * (j + 1)]
        parts.append(jnp.where(first, pltpu.roll(c, 112, axis=1), pltpu.roll(c, 16, axis=1)))
    swapped = parts[0] if len(parts) == 1 else jnp.concatenate(parts, axis=1)
    return z * cos_t + swapped * sin_t


def _mod_kernel(c_ref, w_ref, b_ref, o_ref):
    c = c_ref[...]
    s = c * _sigmoid(c)
    o_ref[0] = jnp.dot(s, w_ref[0], precision=lax.Precision.HIGHEST,
                       preferred_element_type=F32) + b_ref[0]


def _modulation(cvec, ada_w, ada_b):
    depth, d, n = ada_w.shape
    tn = 1536
    return pl.pallas_call(
        _mod_kernel,
        grid=(depth, n // tn),
        in_specs=[pl.BlockSpec((16, d), lambda l, j: (0, 0)),
                  pl.BlockSpec((1, d, tn), lambda l, j: (l, 0, j)),
                  pl.BlockSpec((1, 1, tn), lambda l, j: (l, 0, j))],
        out_specs=pl.BlockSpec((1, 16, tn), lambda l, j: (l, 0, j)),
        out_shape=jax.ShapeDtypeStruct((depth, 16, n), F32),
        compiler_params=_params("arbitrary", "arbitrary"),
        name="adaln_mod",
    )(cvec, ada_w, ada_b.reshape(depth, 1, n))


def _inproj_even_kernel(rope, *refs):
    if rope:
        (x_ref, mod_ref, g_ref, w_ref, bd_ref, gq_ref, gk_ref, cos_ref, sin_ref,
         u_ref, q_ref, ka_ref, va_ref) = refs
    else:
        (x_ref, mod_ref, g_ref, w_ref, bd_ref, gq_ref, gk_ref,
         u_ref, q_ref, ka_ref, va_ref, kn_ref, v_ref) = refs
    mod = mod_ref[0]
    h = _modulate(x_ref[0], g_ref[...], mod[0:1], mod[1:2])
    z = _dot(h.astype(BF16), w_ref[...])
    u = z[:, 0:512]
    q = _group_rms(z[:, 512:1024], bd_ref[...], gq_ref[...])
    k = _group_rms(z[:, 1024:1536], bd_ref[...], gk_ref[...])
    v = z[:, 1536:2048]
    u_ref[...] = u
    if not rope:
        kn_ref[0] = k
        v_ref[0] = v
    va_ref[0] = v.astype(BF16)
    if rope:
        q = _rope(q, cos_ref[...], sin_ref[...])
        k = _rope(k, cos_ref[...], sin_ref[...])
    q_ref[0] = (q * ATTN_SCALE).astype(BF16)
    ka_ref[0] = k.astype(BF16)


def _inproj_even(x, mod, gain, w, ones_bd, gq, gk, rope_tabs, per_batch_mod):
    b, l, d = x.shape
    tl = min(l, WIDE_TILE)
    rope = rope_tabs is not None
    mod_map = (lambda bi, i: (bi, 0, 0)) if per_batch_mod else (lambda bi, i: (0, 0, 0))
    const2 = lambda bi, i: (0, 0)
    tok = lambda w_: pl.BlockSpec((1, tl, w_), lambda bi, i: (bi, i, 0))
    in_specs = [tok(d),
                pl.BlockSpec((1, 6, d), mod_map),
                pl.BlockSpec((1, d), const2),
                pl.BlockSpec(w.shape, const2),
                pl.BlockSpec(ones_bd.shape, const2),
                pl.BlockSpec((1, 512), const2),
                pl.BlockSpec((1, 512), const2)]
    args = [x, mod, gain, w, ones_bd, gq, gk]
    if rope:
        in_specs += [pl.BlockSpec((tl, 512), lambda bi, i: (i, 0))] * 2
        args += list(rope_tabs)
    out_shape = [jax.ShapeDtypeStruct((l, b * 512), F32),
                 jax.ShapeDtypeStruct((b, l, 512), BF16),
                 jax.ShapeDtypeStruct((b, l, 512), BF16),
                 jax.ShapeDtypeStruct((b, l, 512), BF16)]
    out_specs = [pl.BlockSpec((tl, 512), lambda bi, i: (i, bi)), tok(512), tok(512), tok(512)]
    if not rope:
        out_shape += [jax.ShapeDtypeStruct((b, l, 512), F32)] * 2
        out_specs += [tok(512), tok(512)]
    outs = pl.pallas_call(
        functools.partial(_inproj_even_kernel, rope),
        grid=(b, l // tl), in_specs=in_specs, out_specs=out_specs, out_shape=out_shape,
        compiler_params=_params("arbitrary", "arbitrary"),
        name="inproj_even_rope" if rope else "inproj_even",
    )(*args)
    u_tm2, q, k_att, v_att = outs[:4]
    k_norm, v = (outs[4], outs[5]) if not rope else (None, None)
    return u_tm2, q, k_norm, k_att, v, v_att


def _inproj_odd_kernel(rope, *refs):
    if rope:
        (x_ref, mod_ref, g_ref, w_ref, bd_ref, gq_ref, gk_ref, cos_ref, sin_ref,
         q_ref, kn_ref, ka_ref, v_ref, va_ref, u_ref) = refs
    else:
        (x_ref, mod_ref, g_ref, w_ref, bd_ref, gq_ref, gk_ref,
         q_ref, kn_ref, ka_ref, v_ref, va_ref, u_ref) = refs
    mod = mod_ref[0]
    h = _modulate(x_ref[0], g_ref[...], mod[0:1], mod[1:2])
    z = _dot(h.astype(BF16), w_ref[...])
    bd = bd_ref[...]
    q = _group_rms(z[:, 0:512], bd, gq_ref[...])
    k = _group_rms(z[:, 512:640], bd[0:128, 0:128], gk_ref[...])
    v = z[:, 640:768]
    kn_ref[0] = k
    v_ref[0] = v
    va_ref[0] = v.astype(BF16)
    u_ref[0] = z[:, 768:1280]
    if rope:
        q = _rope(q, cos_ref[...], sin_ref[...])
        k = _rope(k, cos_ref[:, 0:128], sin_ref[:, 0:128])
    q_ref[0] = (q * ATTN_SCALE).astype(BF16)
    ka_ref[0] = k.astype(BF16)


def _inproj_odd(x, mod, gain, w, ones_bd, gq, gk, rope_tabs, per_batch_mod):
    b, l, d = x.shape
    tl = min(l, WIDE_TILE)
    rope = rope_tabs is not None
    mod_map = (lambda bi, i: (bi, 0, 0)) if per_batch_mod else (lambda bi, i: (0, 0, 0))
    const2 = lambda bi, i: (0, 0)
    tok = lambda w_: pl.BlockSpec((1, tl, w_), lambda bi, i: (bi, i, 0))
    in_specs = [tok(d),
                pl.BlockSpec((1, 6, d), mod_map),
                pl.BlockSpec((1, d), const2),
                pl.BlockSpec(w.shape, const2),
                pl.BlockSpec(ones_bd.shape, const2),
                pl.BlockSpec((1, 512), const2),
                pl.BlockSpec((1, 128), const2)]
    args = [x, mod, gain, w, ones_bd, gq, gk]
    if rope:
        in_specs += [pl.BlockSpec((tl, 512), lambda bi, i: (i, 0))] * 2
        args += list(rope_tabs)
    out_shape = [jax.ShapeDtypeStruct((b, l, 512), BF16),
                 jax.ShapeDtypeStruct((b, l, 128), F32),
                 jax.ShapeDtypeStruct((b, l, 128), BF16),
                 jax.ShapeDtypeStruct((b, l, 128), F32),
                 jax.ShapeDtypeStruct((b, l, 128), BF16),
                 jax.ShapeDtypeStruct((b, l, 512), F32)]
    out_specs = [tok(512), tok(128), tok(128), tok(128), tok(128), tok(512)]
    return pl.pallas_call(
        functools.partial(_inproj_odd_kernel, rope),
        grid=(b, l // tl), in_specs=in_specs, out_specs=out_specs, out_shape=out_shape,
        compiler_params=_params("arbitrary", "arbitrary"),
        name="inproj_odd_rope" if rope else "inproj_odd",
    )(*args)


def _s5_kernel(uf_ref, ub_ref, bd_ref, cd_ref, a_ref, h0_ref, yf_ref, yb_ref, fin_ref,
               bu_ref, carry_ref):
    c = pl.program_id(1)
    rows = S5_CHUNK * S5_BATCH

    @pl.when(c == 0)
    def _():
        carry_ref[...] = h0_ref[...]

    for d in range(2):
        u = (uf_ref if d == 0 else ub_ref)[...].reshape(rows, MIX_HALF).astype(BF16)
        for j in range(S5_JB):
            bu_ref[d, :, S5_BLK * j:S5_BLK * (j + 1)] = _dot(u[:, 128 * j:128 * (j + 1)], bd_ref[d, j])
        for j in range(S5_JB):
            lo = S5_BLK * j
            a_re = jnp.broadcast_to(a_ref[d, 0, j:j + 1, :], (S5_BATCH, 512))
            a_im = jnp.broadcast_to(a_ref[d, 1, j:j + 1, :], (S5_BATCH, 512))

            h_re, h_im = carry_ref[d, :, lo:lo + 512], carry_ref[d, :, lo + 512:lo + 1024]
            for s in range(S5_CHUNK):
                row = (s if d == 0 else S5_CHUNK - 1 - s) * S5_BATCH
                b_re = bu_ref[d, row:row + S5_BATCH, lo:lo + 512]
                b_im = bu_ref[d, row:row + S5_BATCH, lo + 512:lo + 1024]
                h_re, h_im = a_re * h_re - a_im * h_im + b_re, a_re * h_im + a_im * h_re + b_im
                bu_ref[d, row:row + S5_BATCH, lo:lo + 512] = h_re
                bu_ref[d, row:row + S5_BATCH, lo + 512:lo + 1024] = h_im
            carry_ref[d, :, lo:lo + 512] = h_re
            carry_ref[d, :, lo + 512:lo + 1024] = h_im
        ys = [_dot(bu_ref[d, :, S5_BLK * j:S5_BLK * (j + 1)].astype(BF16), cd_ref[d, j])
              for j in range(S5_JB)]
        y = jnp.concatenate(ys, axis=1).reshape(S5_CHUNK, S5_BATCH, MIX_HALF)
        if d == 0:
            yf_ref[...] = y
        else:
            yb_ref[...] = y

    @pl.when(c == pl.num_programs(1) - 1)
    def _():
        fin_ref[...] = carry_ref[...]


def _s5_scan(u_tm, bd, cd, acoef, h0):
    l, b, _ = u_tm.shape
    nc = l // S5_CHUNK
    ng = b // S5_BATCH
    state_w = S5_JB * S5_BLK
    blk = (S5_CHUNK, S5_BATCH, MIX_HALF)
    full = lambda a: pl.BlockSpec(a.shape, lambda g, c: (0,) * a.ndim)
    return pl.pallas_call(
        _s5_kernel,
        grid=(ng, nc),
        in_specs=[pl.BlockSpec(blk, lambda g, c: (c, g, 0)),
                  pl.BlockSpec(blk, lambda g, c: (nc - 1 - c, g, 0)),
                  full(bd), full(cd), full(acoef),
                  pl.BlockSpec((2, S5_BATCH, state_w), lambda g, c: (0, g, 0))],
        out_specs=[pl.BlockSpec(blk, lambda g, c: (c, g, 0)),
                   pl.BlockSpec(blk, lambda g, c: (nc - 1 - c, g, 0)),
                   pl.BlockSpec((2, S5_BATCH, state_w), lambda g, c: (0, g, 0))],
        out_shape=[jax.ShapeDtypeStruct((l, b, MIX_HALF), F32),
                   jax.ShapeDtypeStruct((l, b, MIX_HALF), F32),
                   jax.ShapeDtypeStruct((2, b, state_w), F32)],
        scratch_shapes=[pltpu.VMEM((2, S5_CHUNK * S5_BATCH, state_w), F32),
                        pltpu.VMEM((2, S5_BATCH, state_w), F32)],
        compiler_params=_params("arbitrary", "arbitrary"),
        name="s5_scan",
    )(u_tm, u_tm, bd, cd, acoef, h0)


def _s5_tables(lam_re, lam_im, log_step, b_re, b_im, c_re, c_im):
    step = jnp.exp(log_step.astype(F32))[..., None]
    lr, li = lam_re.astype(F32), lam_im.astype(F32)
    er = jnp.exp(lr * step)
    a_re, a_im = er * jnp.cos(li * step), er * jnp.sin(li * step)
    den = lr * lr + li * li
    q_re = ((a_re - 1.0) * lr + a_im * li) / den
    q_im = (a_im * lr - (a_re - 1.0) * li) / den
    bb_re = q_re[..., None] * b_re - q_im[..., None] * b_im
    bb_im = q_re[..., None] * b_im + q_im[..., None] * b_re
    eye = jnp.eye(8, dtype=F32)
    bb = jnp.stack([bb_re, bb_im], axis=1).reshape(2, 2, S5_JB, 8, S5_STATE, S5_GROUP)
    bd = jnp.einsum('drjgpc,gh->djgcrhp', bb, eye).reshape(2, S5_JB, 128, S5_BLK).astype(BF16)
    cc = jnp.stack([c_re, -c_im], axis=1).astype(F32).reshape(2, 2, S5_JB, 8, S5_GROUP, S5_STATE)
    cd = jnp.einsum('drjgcp,gh->djrgphc', cc, eye).reshape(2, S5_JB, S5_BLK, 128).astype(BF16)
    acoef = jnp.stack([a_re, a_im], axis=1).reshape(2, 2, S5_JB, 512)
    return bd, cd, acoef


def _s5_state_to_blocks(st):
    b = st.shape[0]
    st = st.astype(F32).reshape(b, 2, 2, S5_JB, 8, S5_STATE)
    return jnp.transpose(st, (1, 0, 3, 2, 4, 5)).reshape(2, b, S5_JB * S5_BLK)


def _s5_blocks_to_state(fin):
    b = fin.shape[1]
    fin = fin.reshape(2, b, S5_JB, 2, 8, S5_STATE)
    return jnp.transpose(fin, (1, 0, 3, 2, 4, 5)).reshape(b, 2, 2, S5_GROUPS, S5_STATE)


def _diff_attn_kernel(lam_init, q_ref, k_ref, v_ref, lv_ref, o_ref):
    lv = lv_ref[...]
    lam = (jnp.exp(jnp.sum(lv[0:1] * lv[1:2], axis=-1, keepdims=True))
           - jnp.exp(jnp.sum(lv[2:3] * lv[3:4], axis=-1, keepdims=True)) + lam_init)
    lane = lax.broadcasted_iota(jnp.int32, (1, 128), 1)
    def scores(h):
        out = []
        for m in range(2):
            j = m * 2 + h // 2
            keep = (lane < 64) if h % 2 == 0 else (lane >= 64)
            qb = jnp.where(keep, q_ref[0, :, 128 * j:128 * (j + 1)], jnp.zeros((), BF16))
            out.append(_dot_nt(qb, k_ref[0, :, 128 * j:128 * (j + 1)]))
        return out

    outs = []
    pending = [scores(0), scores(1)]
    for h in range(DIFF_HEADS):
        cur = pending.pop(0)
        es = [jnp.exp(s - jnp.max(s, axis=-1, keepdims=True)) for s in cur]
        sums = [jnp.sum(e, axis=-1, keepdims=True) for e in es]
        a = (es[0] - (lam * sums[0] / sums[1]) * es[1]).astype(BF16)
        if h + 2 < DIFF_HEADS:
            pending.append(scores(h + 2))
        outs.append(_dot(a, v_ref[0, :, 128 * h:128 * (h + 1)]) / sums[0])
    o_ref[0] = jnp.concatenate(outs, axis=1)


def _diff_attention(q, k, v, lv, lam_init):
    b, lq, _ = q.shape
    lk = k.shape[1]
    tq = TOKEN_TILE
    return pl.pallas_call(
        functools.partial(_diff_attn_kernel, lam_init),
        grid=(b, lq // tq),
        in_specs=[pl.BlockSpec((1, tq, 512), lambda bi, i: (bi, i, 0)),
                  pl.BlockSpec((1, lk, 512), lambda bi, i: (bi, 0, 0)),
                  pl.BlockSpec((1, lk, 512), lambda bi, i: (bi, 0, 0)),
                  pl.BlockSpec((4, DIFF_DK), lambda bi, i: (0, 0))],
        out_specs=pl.BlockSpec((1, tq, 512), lambda bi, i: (bi, i, 0)),
        out_shape=jax.ShapeDtypeStruct((b, lq, 512), F32),
        compiler_params=_params("arbitrary", "arbitrary"),
        name="diff_attention",
    )(q, k, v, lv)


def _win_attn_kernel(local, seq_len, *refs):
    if local:
        q_ref, kc_ref, vc_ref, kl_ref, vl_ref, sink_ref, o_ref = refs
    else:
        q_ref, kc_ref, vc_ref, sink_ref, o_ref = refs
    tq = q_ref.shape[1]
    lane = lax.broadcasted_iota(jnp.int32, (1, 128), 1)
    low = lane < 64
    kc = kc_ref[0]
    vc = vc_ref[0]
    n_ctx = kc.shape[0]
    if local:
        span = tq + 2 * WINDOW
        start = pl.multiple_of(pl.program_id(1) * tq, tq)
        kl = kl_ref[0, pl.ds(start, span), :]
        vl = vl_ref[0, pl.ds(start, span), :]
        qpos = start + lax.broadcasted_iota(jnp.int32, (tq, span), 0)
        kpos = start - WINDOW + lax.broadcasted_iota(jnp.int32, (tq, span), 1)
        ok = (jnp.abs(qpos - kpos) <= WINDOW) & (kpos >= 0) & (kpos < seq_len)
    def scores(hd):
        n = hd // WIN_GROUP
        j, half = hd // 2, hd % 2
        qb = q_ref[0, :, 128 * j:128 * (j + 1)]
        if half != n:
            qb = jnp.concatenate([qb[:, 64:128], qb[:, 0:64]], axis=1)
        qb = jnp.where(low if n == 0 else jnp.logical_not(low), qb, jnp.zeros((), BF16))
        s = _dot_nt(qb, kc)
        if local:
            s = jnp.concatenate([s, jnp.where(ok, _dot_nt(qb, kl), NEG_INF)], axis=1)
        return s

    heads = {}
    n_heads = WIN_KV_HEADS * WIN_GROUP
    ahead = 1 if local else 2
    pending = [scores(hd) for hd in range(ahead)]
    for hd in range(n_heads):
        s = pending.pop(0)
        if hd + ahead < n_heads:
            pending.append(scores(hd + ahead))
        sink = sink_ref[hd:hd + 1, 0:1]
        mx = jnp.maximum(jnp.max(s, axis=-1, keepdims=True), sink)
        e = jnp.exp(s - mx)
        den = jnp.sum(e, axis=-1, keepdims=True) + jnp.exp(sink - mx)
        p = e.astype(BF16)
        o = _dot(p[:, 0:n_ctx], vc)
        if local:
            o = o + _dot(p[:, n_ctx:], vl)
        heads[hd] = o / den
    blocks = []
    for j in range(4):
        ev, od = heads[2 * j], heads[2 * j + 1]
        n = (2 * j) // WIN_GROUP
        if n == 1:
            ev = pltpu.roll(ev, 64, axis=1)
        else:
            od = pltpu.roll(od, 64, axis=1)
        blocks.append(jnp.where(low, ev, od))
    o_ref[0] = jnp.concatenate(blocks, axis=1).astype(BF16)


def _win_attention(q, kc, vc, sink, k_lat=None, v_lat=None):
    b, lq, _ = q.shape
    n_ctx = kc.shape[1]
    local = k_lat is not None
    tq = TOKEN_TILE
    in_specs = [pl.BlockSpec((1, tq, 512), lambda bi, i: (bi, i, 0)),
                pl.BlockSpec((1, n_ctx, 128), lambda bi, i: (bi, 0, 0)),
                pl.BlockSpec((1, n_ctx, 128), lambda bi, i: (bi, 0, 0))]
    args = [q, kc, vc]
    if local:
        lp = k_lat.shape[1]
        in_specs += [pl.BlockSpec((1, lp, 128), lambda bi, i: (bi, 0, 0))] * 2
        args += [k_lat, v_lat]
    in_specs.append(pl.BlockSpec((8, 128), lambda bi, i: (0, 0)))
    args.append(sink)
    return pl.pallas_call(
        functools.partial(_win_attn_kernel, local, lq),
        grid=(b, lq // tq), in_specs=in_specs,
        out_specs=pl.BlockSpec((1, tq, 512), lambda bi, i: (bi, i, 0)),
        out_shape=jax.ShapeDtypeStruct((b, lq, 512), BF16),
        compiler_params=_params("arbitrary", "arbitrary"),
        name="win_attention_local" if local else "win_attention",
    )(*args)


def _pool_kernel(u_ref, w_ref, scale_ref, o_ref):
    l = u_ref.shape[1]
    lp = l + 2 * POOL_PAD
    t = lax.broadcasted_iota(jnp.int32, (l, 128), 0)
    zpad = jnp.zeros((POOL_PAD, 128), F32)

    def shifted(a, s):
        return pltpu.roll(a, s, axis=0) + pltpu.roll(a, lp - s, axis=0)

    outs = []
    for gi, wl in enumerate(POOL_WINDOWS):
        x = u_ref[0, :, 128 * gi:128 * (gi + 1)]
        xp = jnp.concatenate([zpad, x, zpad], axis=0)
        acc = xp + pltpu.roll(xp, 1, axis=0)
        if wl >= 4:
            acc = shifted(acc, 1)
        if wl >= 8:
            acc = shifted(acc, 2)
        if wl >= 16:
            acc = shifted(acc, 4)
        win = acc[POOL_PAD:POOL_PAD + l]
        cnt = (jnp.minimum(t + wl // 2, l) - jnp.maximum(t - wl // 2, 0)).astype(F32)
        pooled = win / cnt - x
        outs.append(_dot(pooled.astype(BF16), w_ref[gi]))
    o_ref[0] = (jnp.concatenate(outs, axis=1) * scale_ref[...]).astype(BF16)


def _pool_mixer(u, w, scale):
    b, l, _ = u.shape
    return pl.pallas_call(
        _pool_kernel,
        grid=(b,),
        in_specs=[pl.BlockSpec((1, l, 512), lambda bi: (bi, 0, 0)),
                  pl.BlockSpec(w.shape, lambda bi: (0, 0, 0)),
                  pl.BlockSpec((1, 512), lambda bi: (0, 0))],
        out_specs=pl.BlockSpec((1, l, 512), lambda bi: (bi, 0, 0)),
        out_shape=jax.ShapeDtypeStruct((b, l, 512), BF16),
        compiler_params=_params("arbitrary"),
        name="pool_mixer",
    )(u, w, scale)


def _mixout_even_kernel(out_scale, x_ref, mod_ref, u_ref, yf_ref, yb_ref, oa_ref, sd_ref,
                        wglu_ref, gsub_ref, wout_ref, o_ref):
    mod = mod_ref[0]
    y = sd_ref[...] * u_ref[...] + yf_ref[...] + yb_ref[...]
    g = _gelu(y)
    o_a = g * _sigmoid(_dot(g.astype(BF16), wglu_ref[...]))
    parts = [o_a.astype(BF16)]
    for h in range(DIFF_HEADS):
        blk = oa_ref[0, :, 128 * h:128 * (h + 1)]
        ms = jnp.mean(blk * blk, axis=-1, keepdims=True)
        parts.append(((blk * lax.rsqrt(ms + EPS) * gsub_ref[...]) * out_scale).astype(BF16))
    cat = jnp.concatenate(parts, axis=1)
    o_ref[0] = x_ref[0] + mod[2:3] * _dot(cat, wout_ref[...])


def _mixout_even(x, mod, u_tm2, yf2, yb2, o_attn, s5_d, w_glu, g_sub, w_out, out_scale, per_batch_mod):
    b, l, d = x.shape
    tl = min(l, WIDE_TILE)
    mod_map = (lambda bi, i: (bi, 0, 0)) if per_batch_mod else (lambda bi, i: (0, 0, 0))
    const2 = lambda bi, i: (0, 0)
    tm = pl.BlockSpec((tl, 512), lambda bi, i: (i, bi))
    return pl.pallas_call(
        functools.partial(_mixout_even_kernel, out_scale),
        grid=(b, l // tl),
        in_specs=[pl.BlockSpec((1, tl, d), lambda bi, i: (bi, i, 0)),
                  pl.BlockSpec((1, 6, d), mod_map),
                  tm, tm, tm,
                  pl.BlockSpec((1, tl, 512), lambda bi, i: (bi, i, 0)),
                  pl.BlockSpec((1, 512), const2),
                  pl.BlockSpec((512, 512), const2),
                  pl.BlockSpec((1, 128), const2),
                  pl.BlockSpec((d, d), const2)],
        out_specs=pl.BlockSpec((1, tl, d), lambda bi, i: (bi, i, 0)),
        out_shape=jax.ShapeDtypeStruct((b, l, d), F32),
        compiler_params=_params("arbitrary", "arbitrary"),
        name="mixout_even",
    )(x, mod, u_tm2, yf2, yb2, o_attn, s5_d, w_glu, g_sub, w_out)


def _mixout_odd_kernel(x_ref, mod_ref, oc_ref, od_ref, wout_ref, o_ref):
    mod = mod_ref[0]
    cat = jnp.concatenate([oc_ref[0], od_ref[0]], axis=1)
    o_ref[0] = x_ref[0] + mod[2:3] * _dot(cat, wout_ref[...])


def _mixout_odd(x, mod, o_c, o_d, w_out, per_batch_mod):
    b, l, d = x.shape
    tl = min(l, WIDE_TILE)
    mod_map = (lambda bi, i: (bi, 0, 0)) if per_batch_mod else (lambda bi, i: (0, 0, 0))
    tok = lambda w_: pl.BlockSpec((1, tl, w_), lambda bi, i: (bi, i, 0))
    return pl.pallas_call(
        _mixout_odd_kernel,
        grid=(b, l // tl),
        in_specs=[tok(d), pl.BlockSpec((1, 6, d), mod_map), tok(512), tok(512),
                  pl.BlockSpec((d, d), lambda bi, i: (0, 0))],
        out_specs=tok(d),
        out_shape=jax.ShapeDtypeStruct((b, l, d), F32),
        compiler_params=_params("arbitrary", "arbitrary"),
        name="mixout_odd",
    )(x, mod, o_c, o_d, w_out)


def _rank16(s, kio, exact):
    n = s.shape[0]
    vals = []
    if exact:
        rank = jnp.full(s.shape, float(PEER_TOPK), F32)
        for r in range(PEER_TOPK):
            m = jnp.max(s, axis=0, keepdims=True)
            ix = jnp.min(jnp.where(s == m, kio, float(n)), axis=0, keepdims=True)
            hit = kio == ix
            rank = jnp.where(hit, float(r), rank)
            s = jnp.where(hit, -jnp.inf, s)
            vals.append(m)
        return vals, rank
    for r in range(PEER_TOPK):
        m = jnp.max(s, axis=0, keepdims=True)
        s = jnp.where(s == m, -RANK_CODE * (1.0 + r / 32.0), s)
        vals.append(m)
    code = s * (-1.0 / RANK_CODE)
    return vals, jnp.where(code >= 1.0, (code - 1.0) * 32.0, float(PEER_TOPK))


_SORT16 = ((0, 1), (2, 3), (0, 2), (1, 3), (1, 2), (4, 5), (6, 7), (4, 6), (5, 7), (5, 6), (0, 4), (2, 6), (2, 4),
           (1, 5), (3, 7), (3, 5), (1, 2), (3, 4), (5, 6), (8, 9), (10, 11), (8, 10), (9, 11), (9, 10), (12, 13),
           (14, 15), (12, 14), (13, 15), (13, 14), (8, 12), (10, 14), (10, 12), (9, 13), (11, 15), (11, 13), (9, 10),
           (11, 12), (13, 14), (0, 8), (4, 12), (4, 8), (2, 10), (6, 14), (6, 10), (2, 4), (6, 8), (10, 12), (1, 9),
           (5, 13), (5, 9), (3, 11), (7, 15), (7, 11), (3, 5), (7, 9), (11, 13), (1, 2), (3, 4), (5, 6), (7, 8),
           (9, 10), (11, 12), (13, 14))


def _sorted_top16(s):
    cols = [s[8 * i:8 * (i + 1), :] for i in range(16)]
    for i, j in _SORT16:
        cols[i], cols[j] = jnp.maximum(cols[i], cols[j]), jnp.minimum(cols[i], cols[j])
    vals = []
    for r in range(PEER_TOPK):
        m = cols[0]
        for shift in (4, 2, 1):
            m = jnp.maximum(m, pltpu.roll(m, shift, axis=0))
        vals.append(m[0:1])
        hit = cols[0] == m
        for i in range(PEER_TOPK - 1 - r):
            cols[i] = jnp.where(hit, cols[i + 1], cols[i])
    return vals


def _candidate_rows(v1, v2, j):
    v1hi = jnp.concatenate(v1[8:16], axis=0)
    v2lo = jnp.concatenate(v2[0:8], axis=0)
    v2hi = jnp.concatenate(v2[8:16], axis=0)
    g3_a = jnp.where(j < 5.0, v1[2], v1[4])
    g3_b = jnp.where(j < 5.0, v2lo, pltpu.roll(v2lo, 5, axis=0))
    g4_a = jnp.where(j < 4.0, v1[3], jnp.where(j < 6.0, v1[5], v1[6]))
    g4_b = jnp.where(j < 4.0, v2lo, jnp.where(j < 6.0, pltpu.roll(v2lo, 4, axis=0), pltpu.roll(v2lo, 6, axis=0)))
    return jnp.concatenate([v1[0] + v2lo, v1[0] + v2hi, v1[1] + v2lo, g3_a + g3_b, g4_a + g4_b,
                            jnp.where(j < 2.0, v1[7] + v2lo, -jnp.inf), v1hi + v2[0]], axis=0)


def _staircase_select(cand, j):
    rest = cand
    for r in range(PEER_TOPK):
        m = jnp.max(rest, axis=0, keepdims=True)
        if r == 0:
            best = m
        rest = jnp.where(rest == m, -jnp.inf, rest)
    picked = cand >= m
    z = jnp.sum(jnp.where(picked, jnp.exp(cand - best), 0.0), axis=0, keepdims=True)
    g = [jnp.where(picked[8 * i:8 * (i + 1)], 1.0, 0.0) for i in range(7)]
    tot = lambda v: jnp.sum(v, axis=0, keepdims=True)
    n = [None] * PEER_TOPK
    n[0], n[1], n[7] = tot(g[0] + g[1]), tot(g[2]), tot(g[5])
    n[2] = tot(jnp.where(j < 5.0, g[3], 0.0))
    n[4] = tot(g[3]) - n[2]
    n[3] = tot(jnp.where(j < 4.0, g[4], 0.0))
    n[5] = tot(jnp.where(j < 6.0, g[4], 0.0)) - n[3]
    n[6] = tot(g[4]) - n[3] - n[5]
    for k in range(8):
        n[8 + k] = g[6][k:k + 1]
    total = n[0]
    for a in range(1, PEER_TOPK):
        total = total + n[a]
    return z, n, total


def _candidate_positions(j):
    return jnp.concatenate([j, 8.0 + j, 16.0 + j,
                            jnp.where(j < 5.0, 32.0 + j, 59.0 + j),
                            jnp.where(j < 4.0, 48.0 + j, jnp.where(j < 6.0, 76.0 + j, 90.0 + j)),
                            jnp.where(j < 2.0, 112.0 + j, 999.0), 128.0 + 16.0 * j], axis=0)


def _peer_topk_kernel(x_ref, mod_ref, g_ref, wq_ref, keys_ref, h_ref, n_ref, p1_ref, r2_ref, p2_ref):
    mod = mod_ref[0]
    h = _modulate(x_ref[0], g_ref[...], mod[3:4], mod[4:5]).astype(BF16)
    h_ref[0] = h
    q = _dot(h, wq_ref[...])
    tl = q.shape[0]
    lanes = tl
    kio = lax.broadcasted_iota(jnp.int32, (PEER_NKEYS, lanes), 0).astype(F32)
    jrow = lax.broadcasted_iota(jnp.int32, (8, lanes), 0).astype(F32)
    pos = _candidate_positions(jrow)

    def head(hd, s1, s2, exact, at):
        s1, s2 = s1[:, at:at + lanes], s2[:, at:at + lanes]
        if exact:
            v1, rank1 = _rank16(s1, kio, True)
        else:
            v1 = _sorted_top16(s1)
        v2, rank2 = _rank16(s2, kio, exact)
        cand = _candidate_rows(v1, v2, jrow)
        if exact:
            sel_a = []
            for r in range(PEER_TOPK):
                m = jnp.max(cand, axis=0, keepdims=True)
                px = jnp.min(jnp.where(cand == m, pos, 999.0), axis=0, keepdims=True)
                cand = jnp.where(pos == px, -jnp.inf, cand)
                sel_a.append(jnp.floor(px * (1.0 / PEER_TOPK)))
                if r == 0:
                    best0, z = m, jnp.ones_like(m)
                else:
                    z = z + jnp.exp(m - best0)
            counts = []
            for a in range(PEER_TOPK):
                n_a = jnp.zeros_like(z)
                for r in range(a, PEER_TOPK):
                    n_a = n_a + jnp.where(sel_a[r] == float(a), 1.0, 0.0)
                counts.append(n_a)
        else:
            z, counts, picked = _staircase_select(cand, jrow)
        if exact:
            n_e1 = jnp.zeros((PEER_NKEYS, lanes), F32)
            for a in range(PEER_TOPK):
                n_e1 = jnp.where(rank1 == float(a), counts[a], n_e1)
        else:
            c8 = counts[8]
            for a in range(9, PEER_TOPK):
                c8 = c8 + counts[a]
            floor8 = v1[7]
            for k in range(1, 9):
                floor8 = jnp.where(c8 == float(k), v1[7 + k], floor8)
            n_e1 = jnp.where(s1 >= floor8, 1.0, 0.0) - jnp.where(s1 >= v1[7], 1.0, 0.0)
            for a in range(8):
                n_e1 = jnp.where(s1 == v1[a], counts[a], n_e1)
        n_ref[0, hd, :, at:at + lanes] = n_e1
        p1_ref[0, hd, :, at:at + lanes] = jnp.exp(s1 - v1[0]) * (1.0 / z)
        r2_ref[0, hd, :, at:at + lanes] = rank2.astype(BF16)
        p2_ref[0, hd, :, at:at + lanes] = jnp.exp(s2 - v2[0]).astype(BF16)
        if exact:
            return None
        total = (jnp.sum(jnp.where(s1 >= v1[PEER_TOPK - 1], 1.0, 0.0), axis=0, keepdims=True)
                 + jnp.sum(jnp.where(rank2 < float(PEER_TOPK), 1.0, 0.0), axis=0, keepdims=True) + picked)
        for r in range(PEER_TOPK - 1):
            total = total + jnp.where(v1[r] <= v1[r + 1], 1.0, 0.0)
        return jnp.max(jnp.abs(total - 3.0 * PEER_TOPK))

    def scores(hd):
        return (_dot_nt(keys_ref[2 * hd], q[:, 256 * hd:256 * hd + 128].astype(BF16)),
                _dot_nt(keys_ref[2 * hd + 1], q[:, 256 * hd + 128:256 * hd + 256].astype(BF16)))

    miscounts = []
    for hd in range(PEER_HEADS):
        s1, s2 = scores(hd)
        miscounts.append([head(hd, s1, s2, False, at) for at in range(0, tl, lanes)])
    for hd in range(PEER_HEADS):
        for at, miscount in zip(range(0, tl, lanes), miscounts[hd]):
            @pl.when(miscount > 0.0)
            def _(hd=hd, at=at):
                s1, s2 = scores(hd)
                head(hd, s1, s2, True, at)


def _peer_topk(x, mod, gain, w_q, keys, per_batch_mod):
    b, l, d = x.shape
    tl = TOKEN_TILE
    mod_map = (lambda bi, i: (bi, 0, 0)) if per_batch_mod else (lambda bi, i: (0, 0, 0))
    sel = pl.BlockSpec((1, PEER_HEADS, PEER_NKEYS, tl), lambda bi, i: (bi, 0, 0, i))
    sel_f32 = jax.ShapeDtypeStruct((b, PEER_HEADS, PEER_NKEYS, l), F32)
    sel_bf16 = jax.ShapeDtypeStruct((b, PEER_HEADS, PEER_NKEYS, l), BF16)
    return pl.pallas_call(
        _peer_topk_kernel,
        grid=(b, l // tl),
        in_specs=[pl.BlockSpec((1, tl, d), lambda bi, i: (bi, i, 0)),
                  pl.BlockSpec((1, 6, d), mod_map),
                  pl.BlockSpec((1, d), lambda bi, i: (0, 0)),
                  pl.BlockSpec(w_q.shape, lambda bi, i: (0, 0)),
                  pl.BlockSpec(keys.shape, lambda bi, i: (0, 0, 0))],
        out_specs=[pl.BlockSpec((1, tl, d), lambda bi, i: (bi, i, 0)), sel, sel, sel, sel],
        out_shape=[jax.ShapeDtypeStruct((b, l, d), BF16), sel_f32, sel_f32, sel_bf16, sel_bf16],
        compiler_params=_params("arbitrary", "arbitrary"),
        name="peer_topk",
    )(x, mod, gain, w_q, keys)


def _peer_dense_kernel(x_ref, mod_ref, h_ref, u_ref, vt_ref, n_ref, p1_ref, r2_ref, p2_ref, o_ref,
                       ht_ref, acc_ref, s_ref, gate_ref):
    e = pl.program_id(2)

    @pl.when(e == 0)
    def _():
        ht_ref[...] = h_ref[0].astype(F32).T.astype(BF16)
        acc_ref[...] = jnp.zeros_like(acc_ref)

    ht = ht_ref[...]
    tm = ht.shape[1]
    bounds = [sum(PEER_SUBS[:i]) for i in range(len(PEER_SUBS) + 1)]

    def gate_rows(lo, hi):
        for row in range(lo // PEER_NKEYS, hi // PEER_NKEYS):
            gate = [None] * (PEER_NKEYS // 16)
            e1 = e * (PEER_STEP // PEER_NKEYS) + row
            for hd in range(PEER_HEADS):
                n_b = jnp.broadcast_to(n_ref[0, hd, pl.ds(e1, 1), :], (16, tm)).astype(BF16)
                p1_b = jnp.broadcast_to(p1_ref[0, hd, pl.ds(e1, 1), :], (16, tm)).astype(BF16)
                for k in range(PEER_NKEYS // 16):
                    term = jnp.where(r2_ref[0, hd, 16 * k:16 * (k + 1), :] < n_b,
                                     p2_ref[0, hd, 16 * k:16 * (k + 1), :], jnp.zeros((), BF16)) * p1_b
                    gate[k] = term if gate[k] is None else gate[k] + term
            for k in range(PEER_NKEYS // 16):
                at = PEER_NKEYS * row + 16 * k
                gate_ref[at:at + 16, :] = gate[k]

    for lo, hi in zip(bounds[:-1], bounds[1:]):
        s_ref[lo:hi, :] = _dot(u_ref[lo:hi, :], ht).astype(BF16)
        gate_rows(lo, hi)
    ws = [gate_ref[lo:hi, :] * _gelu(s_ref[lo:hi, :])
          for lo, hi in zip(bounds[:-1], bounds[1:])]
    acc_ref[...] += _dot(vt_ref[...], jnp.concatenate(ws, axis=0))

    @pl.when(e == pl.num_programs(2) - 1)
    def _():
        o_ref[0] = x_ref[0] + mod_ref[0][5:6] * acc_ref[...].T


def _peer_dense(x, mod, h, u_tabs, vt_tabs, layer, n_e1, p1n, rank2, p2, per_batch_mod):
    b, l, d = x.shape
    tm = PEER_TOKENS
    n_exp = u_tabs.shape[1]
    mod_map = (lambda bi, i, e: (bi, 0, 0)) if per_batch_mod else (lambda bi, i, e: (0, 0, 0))
    tok = lambda: pl.BlockSpec((1, tm, d), lambda bi, i, e: (bi, i, 0))
    by_key = pl.BlockSpec((1, PEER_HEADS, PEER_NKEYS, tm), lambda bi, i, e: (bi, 0, 0, i))
    by_row = by_key
    return pl.pallas_call(
        _peer_dense_kernel,
        grid=(b, l // tm, n_exp // PEER_STEP),
        in_specs=[tok(), pl.BlockSpec((1, 6, d), mod_map), tok(),
                  pl.BlockSpec((None, PEER_STEP, d), lambda bi, i, e: (layer, e, 0)),
                  pl.BlockSpec((None, d, PEER_STEP), lambda bi, i, e: (layer, 0, e)),
                  by_row, by_row, by_key, by_key],
        out_specs=tok(),
        out_shape=jax.ShapeDtypeStruct((b, l, d), F32),
        scratch_shapes=[pltpu.VMEM((d, tm), BF16), pltpu.VMEM((d, tm), F32),
                        pltpu.VMEM((PEER_STEP, tm), BF16), pltpu.VMEM((PEER_STEP, tm), BF16)],
        compiler_params=_params("arbitrary", "arbitrary", "arbitrary"),
        name="peer_dense",
    )(x, mod, h, u_tabs, vt_tabs, n_e1, p1n, rank2, p2)


def _rope_tables(n_lat, width):
    q4 = DIFF_DK // 4
    n_rows = n_lat // GRID_W
    row = jnp.repeat(jnp.arange(n_rows), GRID_W).astype(F32)
    col = jnp.tile(jnp.arange(GRID_W), n_rows).astype(F32)
    freqs = ROPE_BASE ** (-jnp.arange(q4, dtype=F32) / q4)
    ang = jnp.stack([row[:, None] * freqs, col[:, None] * freqs], axis=1)
    cos, sin = jnp.cos(ang), jnp.sin(ang)
    cos64 = jnp.stack([cos, cos], axis=2).reshape(n_lat, 64)
    sin64 = jnp.stack([-sin, sin], axis=2).reshape(n_lat, 64)
    reps = width // 64
    return jnp.tile(cos64, (1, reps)), jnp.tile(sin64, (1, reps))


def _ones_block_diag(width, group):
    idx = jnp.arange(width) // group
    return jnp.where(idx[:, None] == idx[None, :], 1.0 / group, 0.0).astype(BF16)


def _peer_layer(x, mod, W, l, per_batch_mod):
    shape = x.shape
    if not per_batch_mod:
        x = x.reshape(1, -1, shape[-1])
    h, n_e1, p1n, rank2, p2 = _peer_topk(x, mod, W['norm_ffn'][l], W['peer_w_q'][l], W['peer_keys'][l],
                                         per_batch_mod)
    y = _peer_dense(x, mod, h, W['peer_u'], W['peer_vt'], l, n_e1, p1n, rank2, p2, per_batch_mod)
    return y.reshape(shape)


def _even_layer(x, mod, W, l, cache, per_batch_mod):
    b, seq, _ = x.shape
    i = l // 2
    lam_init = 0.8 - 0.6 * math.exp(-0.3 * l)
    rope_tabs = W['rope'] if cache is not None else None
    u_tm2, q, k_norm, k_att, v, v_att = _inproj_even(
        x, mod, W['norm_mix'][l], W['even_w_in'][i], W['ones_bd'], W['diff_q_norm'][i],
        W['diff_k_norm'][i], rope_tabs, per_batch_mod)
    if cache is None:
        h0 = jnp.zeros((2, b, S5_JB * S5_BLK), F32)
        k_all, v_all = k_att, v_att
    else:
        h0 = _s5_state_to_blocks(cache['s5'][:, i])
        k_all = jnp.concatenate([cache['diff_k'][:, i].reshape(b, -1, 512).astype(BF16), k_att], axis=1)
        v_all = jnp.concatenate([cache['diff_v'][:, i].reshape(b, -1, 512).astype(BF16), v_att], axis=1)
    yf, yb, fin = _s5_scan(u_tm2.reshape(seq, b, 512), W['s5_bd'][i], W['s5_cd'][i], W['s5_a'][i], h0)
    o_attn = _diff_attention(q, k_all, v_all, W['diff_lambda'][i], lam_init)
    x = _mixout_even(x, mod, u_tm2, yf.reshape(seq, b * 512), yb.reshape(seq, b * 512), o_attn,
                     W['s5_d'][i], W['s5_w_glu'][i], W['diff_sub_norm'][i], W['even_w_out'][i],
                     1.0 - lam_init, per_batch_mod)
    return x, (k_norm, v, fin)


def _odd_layer(x, mod, W, l, cache, per_batch_mod):
    b, seq, _ = x.shape
    i = l // 2
    rope_tabs = W['rope'] if cache is not None else None
    q, k_norm, k_att, v, v_att, u = _inproj_odd(
        x, mod, W['norm_mix'][l], W['odd_w_in'][i], W['ones_bd'], W['win_q_norm'][i],
        W['win_k_norm'][i], rope_tabs, per_batch_mod)
    if cache is None:
        o_c = _win_attention(q, k_att, v_att, W['win_sink'][i])
    else:
        pad = ((0, 0), (WINDOW, WINDOW), (0, 0))
        o_c = _win_attention(q, cache['win_k'][:, i].reshape(b, -1, 128).astype(BF16),
                             cache['win_v'][:, i].reshape(b, -1, 128).astype(BF16), W['win_sink'][i],
                             jnp.pad(k_att, pad), jnp.pad(v_att, pad))
    o_d = _pool_mixer(u, W['pool_w'][i], W['pool_scale'][i])
    x = _mixout_odd(x, mod, o_c, o_d, W['odd_w_out'][i], per_batch_mod)
    return x, (k_norm, v)


def kernel(x_prompt, x_sample, cache_diff_k, cache_diff_v, state_s5, cache_win_k, cache_win_v, c, c_ctx, ada_w, ada_b, norm_mix, norm_ffn, even_w_in, even_w_out, s5_lam_re, s5_lam_im, s5_log_step, s5_b_re, s5_b_im, s5_c_re, s5_c_im, s5_d, s5_w_glu, diff_q_norm, diff_k_norm, diff_lambda, diff_sub_norm, odd_w_in, odd_w_out, win_q_norm, win_k_norm, win_sink, pool_w, pool_scale, peer_w_q, peer_sub_keys, peer_u, peer_v):
    depth = ada_w.shape[0]
    n_even, n_odd = even_w_in.shape[0], odd_w_in.shape[0]
    bsz, seq, d = x_prompt.shape
    dec_b, dec_l, _ = x_sample.shape

    cvec = jnp.zeros((16, d), F32).at[0].set(c_ctx).at[1:1 + dec_b].set(c)
    mod = _modulation(cvec, ada_w, ada_b)

    s5 = [_s5_tables(s5_lam_re[i], s5_lam_im[i], s5_log_step[i], s5_b_re[i], s5_b_im[i],
                     s5_c_re[i], s5_c_im[i]) for i in range(n_even)]
    W = dict(
        norm_mix=norm_mix.reshape(depth, 1, d), norm_ffn=norm_ffn.reshape(depth, 1, d),
        even_w_in=even_w_in.astype(BF16), even_w_out=even_w_out.astype(BF16),
        odd_w_in=odd_w_in.astype(BF16), odd_w_out=odd_w_out.astype(BF16),
        ones_bd=_ones_block_diag(512, 64),
        diff_q_norm=jnp.tile(diff_q_norm, (1, 8)).reshape(n_even, 1, 512),
        diff_k_norm=jnp.tile(diff_k_norm, (1, 8)).reshape(n_even, 1, 512),
        diff_lambda=diff_lambda, diff_sub_norm=diff_sub_norm.reshape(n_even, 1, DIFF_DV),
        s5_bd=[t[0] for t in s5], s5_cd=[t[1] for t in s5], s5_a=[t[2] for t in s5],
        s5_d=s5_d.reshape(n_even, 1, 512), s5_w_glu=s5_w_glu.astype(BF16),
        win_q_norm=jnp.tile(win_q_norm, (1, 8)).reshape(n_odd, 1, 512),
        win_k_norm=jnp.tile(win_k_norm, (1, 2)).reshape(n_odd, 1, 128),
        win_sink=jnp.broadcast_to(win_sink[:, :, None], (n_odd, 8, 128)),
        pool_w=pool_w.astype(BF16), pool_scale=pool_scale.reshape(n_odd, 1, 512),
        peer_w_q=peer_w_q.astype(BF16),
        peer_keys=peer_sub_keys.astype(BF16).reshape(depth, 2 * PEER_HEADS, PEER_NKEYS, -1),
        peer_u=peer_u.astype(BF16),
        peer_vt=jnp.swapaxes(peer_v, 1, 2).astype(BF16),
        rope=_rope_tables(dec_l, 512),
    )
    cache = {'diff_k': cache_diff_k, 'diff_v': cache_diff_v, 's5': state_s5,
             'win_k': cache_win_k, 'win_v': cache_win_v}

    def run(x, mods, cch, per_batch_mod):
        states = []
        for l in range(depth):
            if l % 2 == 0:
                x, st = _even_layer(x, mods[l], W, l, cch, per_batch_mod)
            else:
                x, st = _odd_layer(x, mods[l], W, l, cch, per_batch_mod)
            states.append(st)
            x = _peer_layer(x, mods[l], W, l, per_batch_mod)
        return x, states

    ctx_mods = [mod[l, 0:1].reshape(1, 6, d) for l in range(depth)]
    dec_mods = [mod[l, 1:1 + dec_b].reshape(dec_b, 6, d) for l in range(depth)]
    y_prompt, states = run(x_prompt, ctx_mods, None, False)
    y_sample, _ = run(x_sample, dec_mods, cache, True)

    evens = [states[l] for l in range(depth) if l % 2 == 0]
    odds = [states[l] for l in range(depth) if l % 2 == 1]
    new_diff_k = jnp.stack([s[0].reshape(bsz, seq, 2, DIFF_HEADS, DIFF_DK) for s in evens], axis=1)
    new_diff_v = jnp.stack([s[1].reshape(bsz, seq, DIFF_HEADS, DIFF_DV) for s in evens], axis=1)
    new_s5 = jnp.stack([_s5_blocks_to_state(s[2]) for s in evens], axis=1)
    new_win_k = jnp.stack([s[0].reshape(bsz, seq, WIN_KV_HEADS, WIN_DH) for s in odds], axis=1)
    new_win_v = jnp.stack([s[1].reshape(bsz, seq, WIN_KV_HEADS, WIN_DH) for s in odds], axis=1)
    return (y_prompt, y_sample, new_diff_k, new_diff_v, new_s5, new_win_k, new_win_v)
```
